```python
import math
import numpy as np
import jax
import jax.numpy as jnp
from jax import lax

D_MODEL = 2048
BATCH = 4
SEQ = 4096
DEPTH = 4

F32 = jnp.float32
N_MIXERS = 4
D_MIX = D_MODEL
D_GROUP = D_MIX // N_MIXERS
HEAD_DIM = 64
N_HEADS = D_GROUP // HEAD_DIM
ATTN_BLOCK = 128
NEG_INF = -1e30
LN_EPS = 1e-5
DEEPNORM_ALPHA = (2 * DEPTH) ** 0.25
DEEPNORM_BETA = (8 * DEPTH) ** -0.25
RW_LORA = 32
RW_GN_EPS = 64e-5
RW_SHIFT = 3 * D_GROUP + 2 * RW_LORA
RW_SPLITS = (D_GROUP, 2 * D_GROUP, 3 * D_GROUP, 3 * D_GROUP + RW_LORA)
SWA_WINDOW = 128
SWA_KV_HEADS = 2
NSA_KV_HEADS = 2
NSA_CMP_BLOCK = 32
NSA_CMP_STRIDE = 16
NSA_CMP_HIDDEN = 128
NSA_SEL_BLOCK = 64
NSA_TOPN = 16
NSA_WINDOW = 512
NSA_QCHUNK = 128
NSA_FORCE = 1e6
GDN_CONV = 4
GDN_CHUNK = 64
GDN_EPS = 1e-6

SWA_KV = SWA_KV_HEADS * HEAD_DIM
NSA_KV = NSA_KV_HEADS * HEAD_DIM
COLS = (
    ('rw_shift', RW_SHIFT), ('rw_g', D_GROUP),
    ('swa_q', D_GROUP), ('swa_k', SWA_KV), ('swa_v', SWA_KV), ('swa_g', D_GROUP),
    ('nsa_q', D_GROUP), ('nsa_kc', NSA_KV), ('nsa_vc', NSA_KV), ('nsa_ks', NSA_KV), ('nsa_vs', NSA_KV),
    ('nsa_kw', NSA_KV), ('nsa_vw', NSA_KV), ('nsa_gate', 3 * N_HEADS), ('nsa_g', D_GROUP),
    ('gdn_qkv', 3 * D_GROUP), ('gdn_beta', N_HEADS), ('gdn_a', N_HEADS), ('gdn_g', D_GROUP),
)
N_IN = sum(size for _, size in COLS)

kernel_name = 'hymba_style_rwkv7_swa_nsa_gdn_deepnorm'


def split_cols(p):
    out, off = {}, 0
    for name, size in COLS:
        out[name] = p[..., off:off + size]
        off += size
    return out


def layer_norm(x, g, b):
    xf = x.astype(F32)
    mu = jnp.mean(xf, -1, keepdims=True)
    var = jnp.mean(jnp.square(xf - mu), -1, keepdims=True)
    return ((xf - mu) * lax.rsqrt(var + LN_EPS)).astype(x.dtype) * g + b


def l2norm(t):
    tf = t.astype(F32)
    return tf * lax.rsqrt(jnp.sum(tf * tf, -1, keepdims=True) + 1e-6)


def token_shift(t):
    return jnp.pad(t[:, :-1], ((0, 0), (1, 0), (0, 0)))


def causal_conv(x, w):
    return lax.conv_general_dilated(x, w[:, None, :], window_strides=(1,), padding=((GDN_CONV - 1, 0),),
                                    dimension_numbers=('NWC', 'WIO', 'NWC'), feature_group_count=x.shape[-1])


def banded_attention(q, k, v, window, sinks=None):
    B_, T, G, HPG, Dh = q.shape
    nb = T // ATTN_BLOCK
    L = ATTN_BLOCK + window
    kidx = np.arange(nb)[:, None] * ATTN_BLOCK + np.arange(L)[None, :]
    pad = ((0, 0), (window, 0), (0, 0), (0, 0))
    kw = jnp.pad(k, pad)[:, kidx]
    vw = jnp.pad(v, pad)[:, kidx]
    qb = q.reshape(B_, nb, ATTN_BLOCK, G, HPG, Dh)
    s = jnp.einsum('bnqghd,bnlgd->bghnql', qb, kw).astype(F32) * (Dh ** -0.5)
    qpos = np.arange(nb)[:, None, None] * ATTN_BLOCK + np.arange(ATTN_BLOCK)[None, :, None]
    kpos = kidx[:, None, :] - window
    rel = qpos - kpos
    mask = (rel >= 0) & (rel < window) & (kpos >= 0)
    s = jnp.where(mask, s, NEG_INF)
    m = jnp.max(s, -1, keepdims=True)
    if sinks is None:
        e = jnp.exp(s - m)
        denom = jnp.sum(e, -1, keepdims=True)
    else:
        sk = sinks.astype(F32).reshape(G, HPG)[None, :, :, None, None, None]
        m = jnp.maximum(m, sk)
        e = jnp.exp(s - m)
        denom = jnp.sum(e, -1, keepdims=True) + jnp.exp(sk - m)
    p = (e / denom).astype(v.dtype)
    o = jnp.einsum('bghnql,bnlgd->bnqghd', p, vw)
    return o.reshape(B_, T, G * HPG * Dh)


def rwkv7_mixer(r, k, v, wd, ad, w0, w2, a0, a2, k_k, k_a, r_k, gn_g, gn_b):
    B_, T, _ = r.shape
    heads = lambda t: t.reshape(B_, T, N_HEADS, HEAD_DIM)
    w = w0 + jnp.tanh(wd) @ w2
    w = -jax.nn.softplus(-w) - 0.5
    decay = jnp.exp(-jnp.exp(w.astype(F32)))
    a = jax.nn.sigmoid(a0 + ad @ a2)
    kk = l2norm(heads(k * k_k))
    k = k * (1 + (a - 1) * k_a)
    r_h, k_h, v_h = heads(r), heads(k), heads(v)

    def step(S, inp):
        r_t, w_t, k_t, v_t, kk_t, a_t = inp
        sa = jnp.einsum('bhij,bhj->bhi', S, -kk_t)
        S = S * w_t[:, :, None, :] + sa[..., None] * (kk_t * a_t)[:, :, None, :] + v_t[..., None] * k_t[:, :, None, :]
        return S, jnp.einsum('bhij,bhj->bhi', S, r_t)

    xs = tuple(jnp.moveaxis(t.astype(F32), 1, 0) for t in (r_h, heads(decay), k_h, v_h, kk, heads(a)))
    S0 = jnp.zeros((B_, N_HEADS, HEAD_DIM, HEAD_DIM), F32)
    _, y = lax.scan(step, S0, xs)
    y = jnp.moveaxis(y, 0, 1)
    mu = jnp.mean(y, -1, keepdims=True)
    var = jnp.mean(jnp.square(y - mu), -1, keepdims=True)
    y = ((y - mu) * lax.rsqrt(var + RW_GN_EPS)).reshape(B_, T, D_GROUP) * gn_g.astype(F32) + gn_b.astype(F32)
    bonus = jnp.sum((r_h * k_h * r_k).astype(F32), -1, keepdims=True) * v_h.astype(F32)
    return (y + bonus.reshape(B_, T, D_GROUP)).astype(r.dtype)


def nsa_mixer(q, kc, vc, ks, vs, kw, vw, gate_logits, pe_k, pe_v, k_w1, k_w2, v_w1, v_w2):
    B_, T, _ = q.shape
    G, HPG, Dh = NSA_KV_HEADS, N_HEADS // NSA_KV_HEADS, HEAD_DIM
    scale = Dh ** -0.5
    qh = q.reshape(B_, T, G, HPG, Dh)
    kvh = lambda t: t.reshape(B_, T, G, Dh)
    kc, vc, ks, vs, kw, vw = (kvh(t) for t in (kc, vc, ks, vs, kw, vw))
    tpos = np.arange(T)

    nc = (T - NSA_CMP_BLOCK) // NSA_CMP_STRIDE + 1
    cidx = np.arange(nc)[:, None] * NSA_CMP_STRIDE + np.arange(NSA_CMP_BLOCK)[None, :]

    def compress(t, pe, w1, w2):
        blk = t[:, cidx] + pe[None, None, :, None, :]
        blk = jnp.moveaxis(blk, 3, 2).reshape(B_, nc, G, NSA_CMP_BLOCK * Dh)
        return jax.nn.gelu(blk @ w1) @ w2

    k_cmp = compress(kc, pe_k, k_w1, k_w2)
    v_cmp = compress(vc, pe_v, v_w1, v_w2)
    cmask = cidx[:, -1][None, :] <= tpos[:, None]
    s = jnp.where(cmask, jnp.einsum('btghd,bcgd->bghtc', qh, k_cmp).astype(F32) * scale, NEG_INF)
    p_cmp = jnp.where(cmask, jax.nn.softmax(s, axis=-1), 0.0)
    o_cmp = jnp.einsum('bghtc,bcgd->btghd', p_cmp.astype(v_cmp.dtype), v_cmp)

    ns = T // NSA_SEL_BLOCK
    n_sel = min(NSA_TOPN, ns)
    cst = np.arange(nc) * NSA_CMP_STRIDE
    jst = np.arange(ns) * NSA_SEL_BLOCK
    ov = np.clip(np.minimum(cst[:, None] + NSA_CMP_BLOCK, jst[None, :] + NSA_SEL_BLOCK)
                 - np.maximum(cst[:, None], jst[None, :]), 0, None).astype(np.float32) / NSA_CMP_BLOCK
    imp = jnp.einsum('bghtc,cj->bgtj', p_cmp, ov)
    tblk = tpos // NSA_SEL_BLOCK
    jj = np.arange(ns)
    forced = (jj[None, :] == 0) | (jj[None, :] == tblk[:, None]) | (jj[None, :] == tblk[:, None] - 1)
    causal = jj[None, :] <= tblk[:, None]
    imp = jnp.where(causal, jnp.where(forced, NSA_FORCE, imp), NEG_INF)
    top_val, top_idx = lax.top_k(imp, n_sel)
    top_ok = top_val > NEG_INF / 2

    kb = jnp.moveaxis(ks.reshape(B_, ns, NSA_SEL_BLOCK, G, Dh), 3, 1)
    vb = jnp.moveaxis(vs.reshape(B_, ns, NSA_SEL_BLOCK, G, Dh), 3, 1)
    gather = jax.vmap(jax.vmap(lambda blocks, idx: blocks[idx]))

    def sel_chunk(args):
        q_c, idx_c, ok_c, t_c = args
        kg = gather(kb, idx_c)
        vg = gather(vb, idx_c)
        sc = jnp.einsum('bcghd,bgcnsd->bghcns', q_c, kg).astype(F32) * scale
        kpos = idx_c[..., None] * NSA_SEL_BLOCK + jnp.arange(NSA_SEL_BLOCK, dtype=idx_c.dtype)
        ok = ok_c[..., None] & (kpos <= t_c[:, None, None])
        sc = jnp.where(ok[:, :, None], sc, NEG_INF)
        b_, g_, h_, tc_, n_, sb_ = sc.shape
        p = jax.nn.softmax(sc.reshape(b_, g_, h_, tc_, n_ * sb_), axis=-1).reshape(sc.shape)
        return jnp.einsum('bghcns,bgcnsd->bcghd', p.astype(vg.dtype), vg)

    nq = T // NSA_QCHUNK
    q_chunks = jnp.moveaxis(qh.reshape(B_, nq, NSA_QCHUNK, G, HPG, Dh), 1, 0)
    idx_chunks = jnp.moveaxis(top_idx.reshape(B_, G, nq, NSA_QCHUNK, n_sel), 2, 0)
    ok_chunks = jnp.moveaxis(top_ok.reshape(B_, G, nq, NSA_QCHUNK, n_sel), 2, 0)
    t_chunks = jnp.arange(T, dtype=jnp.int32).reshape(nq, NSA_QCHUNK)
    o_sel = lax.map(sel_chunk, (q_chunks, idx_chunks, ok_chunks, t_chunks))
    o_sel = jnp.moveaxis(o_sel, 0, 1).reshape(B_, T, G, HPG, Dh)

    o_win = banded_attention(qh, kw, vw, NSA_WINDOW).reshape(B_, T, G, HPG, Dh)

    gate = jax.nn.sigmoid(gate_logits).reshape(B_, T, 3, G, HPG)[..., None]
    o = gate[:, :, 0] * o_cmp + gate[:, :, 1] * o_sel + gate[:, :, 2] * o_win
    return o.reshape(B_, T, D_GROUP)


def gdn_mixer(qkv, beta_in, a_in, z, conv_w, A_log, dt_bias, norm_g):
    B_, T, _ = qkv.shape
    dtype = qkv.dtype
    qkv = jax.nn.silu(causal_conv(qkv, conv_w)).astype(F32)
    q, k, v = jnp.split(qkv, 3, axis=-1)
    heads = lambda t: t.reshape(B_, T, N_HEADS, HEAD_DIM)
    q = l2norm(heads(q)) * (HEAD_DIM ** -0.5)
    k = l2norm(heads(k))
    v = heads(v)
    beta = jax.nn.sigmoid(beta_in.astype(F32))
    g = -jnp.exp(A_log.astype(F32)) * jax.nn.softplus(a_in.astype(F32) + dt_bias.astype(F32))
    C = GDN_CHUNK
    nc = T // C
    chunk_v = lambda t: jnp.transpose(t.reshape(B_, nc, C, N_HEADS, HEAD_DIM), (0, 3, 1, 2, 4))
    chunk_s = lambda t: jnp.transpose(t.reshape(B_, nc, C, N_HEADS), (0, 3, 1, 2))
    qc, kc, vc = chunk_v(q), chunk_v(k), chunk_v(v)
    bc = chunk_s(beta)
    gam = jnp.cumsum(chunk_s(g), axis=-1)
    tri = np.tril(np.ones((C, C), bool))
    strict = np.tril(np.ones((C, C), bool), -1)
    decay = jnp.exp(jnp.where(tri, gam[..., :, None] - gam[..., None, :], NEG_INF))
    kbeta = kc * bc[..., None]
    Lm = jnp.where(strict, jnp.einsum('bhnid,bhnjd->bhnij', kbeta, kc) * decay, 0.0)
    eye = jnp.eye(C, dtype=F32)
    Tm = lax.linalg.triangular_solve(eye + Lm, jnp.broadcast_to(eye, Lm.shape), left_side=True, lower=True)
    u = Tm @ (vc * bc[..., None])
    w = Tm @ (kbeta * jnp.exp(gam)[..., None])
    qk = jnp.einsum('bhnid,bhnjd->bhnij', qc, kc) * decay

    def step(S, inp):
        q_c, k_c, u_c, w_c, g_c, qk_c = inp
        v_new = u_c - w_c @ S
        o = (q_c * jnp.exp(g_c)[..., None]) @ S + qk_c @ v_new
        g_last = g_c[..., -1]
        S = S * jnp.exp(g_last)[..., None, None] + jnp.einsum(
            'bhcd,bhce->bhde', k_c * jnp.exp(g_last[..., None] - g_c)[..., None], v_new)
        return S, o

    xs = tuple(jnp.moveaxis(t, 2, 0) for t in (qc, kc, u, w, gam, qk))
    S0 = jnp.zeros((B_, N_HEADS, HEAD_DIM, HEAD_DIM), F32)
    _, o = lax.scan(step, S0, xs)
    o = jnp.transpose(o, (1, 0, 3, 2, 4)).reshape(B_, T, N_HEADS, HEAD_DIM)
    o = o * lax.rsqrt(jnp.mean(o * o, -1, keepdims=True) + GDN_EPS) * norm_g.astype(F32)
    return o.reshape(B_, T, D_GROUP).astype(dtype) * jax.nn.silu(z)


def hybrid_layer(x, w_in, w_out, ln_g, ln_b, rw_mu, rw_w0, rw_w2, rw_a0, rw_a2, rw_kk, rw_ka, rw_rk,
                 rw_gn_g, rw_gn_b, swa_sinks, nsa_pe_k, nsa_pe_v, nsa_k_w1, nsa_k_w2, nsa_v_w1, nsa_v_w2,
                 gdn_conv, gdn_A_log, gdn_dt_bias, gdn_norm_g):
    B_, T, _ = x.shape
    c = split_cols(x @ w_in)
    rw = c['rw_shift']
    rw = rw + (token_shift(rw) - rw) * rw_mu
    r, k, v, wd, ad = jnp.split(rw, RW_SPLITS, axis=-1)
    y_a = rwkv7_mixer(r, k, v, wd, ad, rw_w0, rw_w2, rw_a0, rw_a2, rw_kk, rw_ka, rw_rk, rw_gn_g, rw_gn_b)
    y_a = y_a * jax.nn.silu(c['rw_g'])
    hpg = N_HEADS // SWA_KV_HEADS
    y_b = banded_attention(c['swa_q'].reshape(B_, T, SWA_KV_HEADS, hpg, HEAD_DIM),
                           c['swa_k'].reshape(B_, T, SWA_KV_HEADS, HEAD_DIM),
                           c['swa_v'].reshape(B_, T, SWA_KV_HEADS, HEAD_DIM), SWA_WINDOW, swa_sinks)
    y_b = y_b * jax.nn.silu(c['swa_g'])
    y_c = nsa_mixer(c['nsa_q'], c['nsa_kc'], c['nsa_vc'], c['nsa_ks'], c['nsa_vs'], c['nsa_kw'], c['nsa_vw'],
                    c['nsa_gate'], nsa_pe_k, nsa_pe_v, nsa_k_w1, nsa_k_w2, nsa_v_w1, nsa_v_w2)
    y_c = y_c * jax.nn.silu(c['nsa_g'])
    y_d = gdn_mixer(c['gdn_qkv'], c['gdn_beta'], c['gdn_a'], c['gdn_g'], gdn_conv, gdn_A_log, gdn_dt_bias, gdn_norm_g)
    y = jnp.concatenate([y_a, y_b, y_c, y_d], axis=-1) @ w_out
    return layer_norm(DEEPNORM_ALPHA * x + y, ln_g, ln_b)


def setup_inputs(seed: int = 0) -> dict:
    key = jax.random.key(seed)
    ks = jax.random.split(key, 32)
    L = DEPTH
    nrm = lambda k, shape, scale: scale * jax.random.normal(k, shape, F32)
    unif = lambda k, shape, lo, hi: jax.random.uniform(k, shape, F32, lo, hi)
    cmp_in = NSA_CMP_BLOCK * HEAD_DIM
    dt = jnp.exp(unif(ks[24], (L, N_HEADS), math.log(1e-3), math.log(1e-1)))
    return {
        'x': nrm(ks[0], (BATCH, SEQ, D_MODEL), 1.0),
        'w_in': nrm(ks[1], (L, D_MODEL, N_IN), D_MODEL ** -0.5),
        'w_out': nrm(ks[2], (L, D_MIX, D_MODEL), DEEPNORM_BETA * D_MIX ** -0.5),
        'ln_g': 1.0 + nrm(ks[3], (L, D_MODEL), 0.02),
        'ln_b': nrm(ks[4], (L, D_MODEL), 0.02),
        'rw_mu': unif(ks[5], (L, RW_SHIFT), 0.0, 1.0),
        'rw_w0': unif(ks[6], (L, D_GROUP), -5.0, 1.0),
        'rw_w2': nrm(ks[7], (L, RW_LORA, D_GROUP), 0.1),
        'rw_a0': nrm(ks[8], (L, D_GROUP), 0.1),
        'rw_a2': nrm(ks[9], (L, RW_LORA, D_GROUP), 0.1),
        'rw_kk': 0.85 + nrm(ks[10], (L, D_GROUP), 0.02),
        'rw_ka': 1.0 + nrm(ks[11], (L, D_GROUP), 0.02),
        'rw_rk': nrm(ks[12], (L, N_HEADS, HEAD_DIM), 0.1),
        'rw_gn_g': 1.0 + nrm(ks[13], (L, D_GROUP), 0.02),
        'rw_gn_b': nrm(ks[14], (L, D_GROUP), 0.02),
        'swa_sinks': nrm(ks[15], (L, N_HEADS), 0.5),
        'nsa_pe_k': nrm(ks[16], (L, NSA_CMP_BLOCK, HEAD_DIM), 0.02),
        'nsa_pe_v': nrm(ks[17], (L, NSA_CMP_BLOCK, HEAD_DIM), 0.02),
        'nsa_k_w1': nrm(ks[18], (L, cmp_in, NSA_CMP_HIDDEN), cmp_in ** -0.5),
        'nsa_k_w2': nrm(ks[19], (L, NSA_CMP_HIDDEN, HEAD_DIM), NSA_CMP_HIDDEN ** -0.5),
        'nsa_v_w1': nrm(ks[20], (L, cmp_in, NSA_CMP_HIDDEN), cmp_in ** -0.5),
        'nsa_v_w2': nrm(ks[21], (L, NSA_CMP_HIDDEN, HEAD_DIM), NSA_CMP_HIDDEN ** -0.5),
        'gdn_conv': nrm(ks[22], (L, GDN_CONV, 3 * D_GROUP), GDN_CONV ** -0.5),
        'gdn_A_log': jnp.log(unif(ks[23], (L, N_HEADS), 1.0, 16.0)),
        'gdn_dt_bias': jnp.log(jnp.expm1(dt)),
        'gdn_norm_g': 1.0 + nrm(ks[25], (L, HEAD_DIM), 0.02),
    }


def reference(x, w_in, w_out, ln_g, ln_b, rw_mu, rw_w0, rw_w2, rw_a0, rw_a2, rw_kk, rw_ka, rw_rk,
              rw_gn_g, rw_gn_b, swa_sinks, nsa_pe_k, nsa_pe_v, nsa_k_w1, nsa_k_w2, nsa_v_w1, nsa_v_w2,
              gdn_conv, gdn_A_log, gdn_dt_bias, gdn_norm_g):
    for i in range(DEPTH):
        x = hybrid_layer(x, w_in[i], w_out[i], ln_g[i], ln_b[i], rw_mu[i], rw_w0[i], rw_w2[i], rw_a0[i],
                         rw_a2[i], rw_kk[i], rw_ka[i], rw_rk[i], rw_gn_g[i], rw_gn_b[i], swa_sinks[i],
                         nsa_pe_k[i], nsa_pe_v[i], nsa_k_w1[i], nsa_k_w2[i], nsa_v_w1[i], nsa_v_w2[i],
                         gdn_conv[i], gdn_A_log[i], gdn_dt_bias[i], gdn_norm_g[i])
    return x
```

```python
import functools
import math

import numpy as np
import jax
import jax.numpy as jnp
from jax import lax
from jax.experimental import pallas as pl
from jax.experimental.pallas import tpu as pltpu

F32 = jnp.float32
BF16 = jnp.bfloat16
HI = lax.Precision.HIGHEST

D_MODEL = 2048
DEPTH = 4
D_GROUP = 512
HEAD_DIM = 64
N_HEADS = 8
KV_HEADS = 2
HPG = N_HEADS // KV_HEADS
KV_W = KV_HEADS * HEAD_DIM
NEG_INF = -1e30
LN_EPS = 1e-5
DEEPNORM_ALPHA = (2 * DEPTH) ** 0.25
RW_LORA = 32
RW_GN_EPS = 64e-5
SWA_WINDOW = 128
NSA_CMP_BLOCK = 32
NSA_CMP_STRIDE = 16
NSA_CMP_HIDDEN = 128
NSA_SEL_BLOCK = 64
NSA_TOPN = 16
NSA_WINDOW = 512
NSA_FORCE = 1e6
GDN_CONV = 4
GDN_EPS = 1e-6
SCALE = HEAD_DIM ** -0.5

CHUNK = 64
TQ = 128
TAIL = 8
VMEM_LIMIT = 56 * 1024 * 1024

_COLS = (
    ('rw_r', 512), ('rw_k', 512), ('rw_v', 512), ('rw_wd', 32), ('rw_ad', 32), ('rw_g', 512),
    ('swa_q', 512), ('swa_k', 128), ('swa_v', 128), ('swa_g', 512),
    ('nsa_q', 512), ('nsa_kc', 128), ('nsa_vc', 128), ('nsa_ks', 128), ('nsa_vs', 128),
    ('nsa_kw', 128), ('nsa_vw', 128), ('nsa_gate', 24), ('nsa_g', 512),
    ('gdn_q', 512), ('gdn_k', 512), ('gdn_v', 512), ('gdn_beta', 8), ('gdn_a', 8), ('gdn_g', 512),
)
_OFF = {}
_o = 0
for _n, _s in _COLS:
    _OFF[_n] = (_o, _s)
    _o += _s
N_IN = _o

F32_GROUPS = ('rw_r', 'rw_k', 'rw_v', 'rw_g', 'gdn_q', 'gdn_k', 'gdn_v', 'gdn_g', 'swa_g', 'nsa_g')
FCOL = {n: i for i, n in enumerate(F32_GROUPS)}
SMALL_W = 256
SMALL_BLK = len(F32_GROUPS) * 512 // SMALL_W
SM_WD, SM_AD, SM_GATE, SM_BETA, SM_A = 0, 32, 64, 88, 96
N_F32 = len(F32_GROUPS) * 512 + SMALL_W
BF_Q = {'swa_q': 0, 'nsa_q': 1}
BF_KV = {n: 8 + i for i, n in enumerate(('swa_k', 'swa_v', 'nsa_ks', 'nsa_vs', 'nsa_kw', 'nsa_vw'))}
N_BF = 1024 + 6 * 128


def _dot(a, b, prec=None):
    return lax.dot_general(a, b, (((1,), (0,)), ((), ())), precision=prec, preferred_element_type=F32)


def _dot_nt(a, b, prec=None):
    return lax.dot_general(a, b, (((1,), (1,)), ((), ())), precision=prec, preferred_element_type=F32)


def _dot_tn(a, b, prec=None):
    return lax.dot_general(a, b, (((0,), (0,)), ((), ())), precision=prec, preferred_element_type=F32)


def _sigmoid(x):
    return 1.0 / (1.0 + jnp.exp(-x))


def _silu(x):
    return x * _sigmoid(x)


def _softplus(x):
    return jnp.maximum(x, 0.0) + jnp.log(1.0 + jnp.exp(-jnp.abs(x)))


def _iota2(shape, dim):
    return lax.broadcasted_iota(jnp.int32, shape, dim)


def _inv_unit_lower(x):
    n = x.shape[0]
    eye = (_iota2((n, n), 0) == _iota2((n, n), 1)).astype(F32)
    p = eye + x
    for _ in range(int(math.log2(n)) - 1):
        x = _dot(x, x, HI)
        p = p + _dot(x, p, HI)
    return p


def _params(sem):
    return pltpu.CompilerParams(dimension_semantics=sem, vmem_limit_bytes=VMEM_LIMIT)


def _mm_kernel(x_ref, w_ref, o_ref):
    o_ref[...] = jnp.dot(x_ref[...], w_ref[...], preferred_element_type=F32).astype(o_ref.dtype)


def _matmul(x, w, out_dtype, tm, tn):
    m, k = x.shape
    n = w.shape[1]
    return pl.pallas_call(
        _mm_kernel,
        grid=(m // tm, n // tn),
        in_specs=[pl.BlockSpec((tm, k), lambda i, j: (i, 0)),
                  pl.BlockSpec((k, tn), lambda i, j: (0, j))],
        out_specs=pl.BlockSpec((tm, tn), lambda i, j: (i, j)),
        out_shape=jax.ShapeDtypeStruct((m, n), out_dtype),
        compiler_params=_params(("parallel", "arbitrary")),
        name="proj_in",
    )(x, w)


def _rwkv_kernel(r_ref, k_ref, v_ref, g_ref, sm_ref, mu_ref, musm_ref, w0_ref, w2_ref, a0_ref, a2_ref,
                 kk_ref, ka_ref, rk_ref, gng_ref, gnb_ref, o_ref, buf_ref, bufsm_ref, s_ref):
    c = pl.program_id(1)
    C = CHUNK

    @pl.when(c == 0)
    def _():
        buf_ref[0:TAIL, :] = jnp.zeros((TAIL, 3 * D_GROUP), F32)
        bufsm_ref[0:TAIL, :] = jnp.zeros((TAIL, SMALL_W), F32)
        s_ref[...] = jnp.zeros(s_ref.shape, F32)

    buf_ref[TAIL:TAIL + C, 0:512] = r_ref[...]
    buf_ref[TAIL:TAIL + C, 512:1024] = k_ref[...]
    buf_ref[TAIL:TAIL + C, 1024:1536] = v_ref[...]
    bufsm_ref[TAIL:TAIL + C, :] = sm_ref[...]
    cur = buf_ref[TAIL:TAIL + C, :]
    prev = buf_ref[TAIL - 1:TAIL - 1 + C, :]
    mixed = cur + (prev - cur) * mu_ref[...]
    cur_sm = bufsm_ref[TAIL:TAIL + C, :]
    prev_sm = bufsm_ref[TAIL - 1:TAIL - 1 + C, :]
    smix = cur_sm + (prev_sm - cur_sm) * musm_ref[...]
    buf_ref[0:TAIL, :] = buf_ref[C:C + TAIL, :]
    bufsm_ref[0:TAIL, :] = bufsm_ref[C:C + TAIL, :]

    r = mixed[:, 0:512]
    k = mixed[:, 512:1024]
    v = mixed[:, 1024:1536]
    wd = smix[:, SM_WD:SM_WD + RW_LORA]
    ad = smix[:, SM_AD:SM_AD + RW_LORA]
    wl = w0_ref[...] + _dot(jnp.tanh(wd), w2_ref[...], HI)
    logw = -jnp.exp(-_softplus(-wl) - 0.5)
    alpha = _sigmoid(a0_ref[...] + _dot(ad, a2_ref[...], HI))
    kkraw = k * kk_ref[...]
    k2 = k * (1.0 + (alpha - 1.0) * ka_ref[...])
    ri = _iota2((C, C), 0)
    ci = _iota2((C, C), 1)
    tril = ri >= ci
    strict = ri > ci
    gcum = _dot(tril.astype(F32), logw, HI)
    gate = _silu(g_ref[...])
    rk = rk_ref[...]
    gng = gng_ref[...]
    gnb = gnb_ref[...]

    for h in range(N_HEADS):
        sl = slice(h * HEAD_DIM, (h + 1) * HEAD_DIM)
        r_h, k_h, v_h, a_h = r[:, sl], k2[:, sl], v[:, sl], alpha[:, sl]
        kkr = kkraw[:, sl]
        kk_h = kkr * lax.rsqrt(jnp.sum(kkr * kkr, -1, keepdims=True) + 1e-6)
        lw_h, g_h = logw[:, sl], gcum[:, sl]
        gm = g_h[C // 2 - 1:C // 2, :]
        ge = g_h[C - 1:C, :]
        b_h = kk_h * a_h
        einv = jnp.exp(gm - g_h)
        eend = jnp.exp(ge - g_h)
        left = jnp.concatenate([-kk_h * jnp.exp(g_h - lw_h - gm), r_h * jnp.exp(g_h - gm)], axis=0)
        ab = _dot_nt(left, b_h * einv, HI)
        ak = _dot_nt(left, k_h * einv, HI)
        a_ab = jnp.where(strict, ab[:C], 0.0)
        a_rb = jnp.where(tril, ab[C:], 0.0)
        a_k = jnp.concatenate([jnp.where(strict, ak[:C], 0.0), jnp.where(tril, ak[C:], 0.0)], axis=0)
        tinv = _inv_unit_lower(a_ab)
        s_h = s_ref[h]
        ls = _dot_nt(left, s_h * jnp.exp(gm), HI)
        av = _dot(a_k, v_h, HI)
        u = _dot(tinv, ls[:C] + av[:C], HI)
        o = ls[C:] + _dot(a_rb, u, HI) + av[C:]
        s_ref[h] = s_h * jnp.exp(ge) + _dot_tn(u, b_h * eend, HI) + _dot_tn(v_h, k_h * eend, HI)
        mu = jnp.mean(o, -1, keepdims=True)
        var = jnp.mean(jnp.square(o - mu), -1, keepdims=True)
        y = (o - mu) * lax.rsqrt(var + RW_GN_EPS) * gng[:, sl] + gnb[:, sl]
        bonus = jnp.sum(r_h * k_h * rk[:, sl], -1, keepdims=True) * v_h
        o_ref[:, sl] = (y + bonus) * gate[:, sl]


def _rwkv(cf, B, T, mu3, musm, w0, w2, a0, a2, kk, ka, rk, gng, gnb):
    nch = T // CHUNK
    row = lambda j: pl.BlockSpec((CHUNK, 512), lambda b, c, j=j: (b * nch + c, j))
    full = lambda a: pl.BlockSpec(a.shape, lambda b, c: (0,) * a.ndim)
    ps = (mu3, musm, w0, w2, a0, a2, kk, ka, rk, gng, gnb)
    return pl.pallas_call(
        _rwkv_kernel,
        grid=(B, nch),
        in_specs=[row(FCOL['rw_r']), row(FCOL['rw_k']), row(FCOL['rw_v']), row(FCOL['rw_g']),
                  pl.BlockSpec((CHUNK, SMALL_W), lambda b, c: (b * nch + c, SMALL_BLK))] + [full(a) for a in ps],
        out_specs=pl.BlockSpec((CHUNK, 512), lambda b, c: (b * nch + c, 0)),
        out_shape=jax.ShapeDtypeStruct((B * T, D_GROUP), F32),
        scratch_shapes=[pltpu.VMEM((TAIL + CHUNK, 3 * D_GROUP), F32),
                        pltpu.VMEM((TAIL + CHUNK, SMALL_W), F32),
                        pltpu.VMEM((N_HEADS, HEAD_DIM, HEAD_DIM), F32)],
        compiler_params=_params(("parallel", "arbitrary")),
        name="rwkv7",
    )(cf, cf, cf, cf, cf, *ps)


def _gdn_kernel(q_ref, k_ref, v_ref, z_ref, sm_ref, conv_ref, alog_ref, dtb_ref, ng_ref, o_ref,
                buf_ref, s_ref):
    c = pl.program_id(1)
    C = CHUNK

    @pl.when(c == 0)
    def _():
        buf_ref[0:TAIL, :] = jnp.zeros((TAIL, 3 * D_GROUP), F32)
        s_ref[...] = jnp.zeros(s_ref.shape, F32)

    buf_ref[TAIL:TAIL + C, 0:512] = q_ref[...]
    buf_ref[TAIL:TAIL + C, 512:1024] = k_ref[...]
    buf_ref[TAIL:TAIL + C, 1024:1536] = v_ref[...]
    conv = conv_ref[...]
    acc = buf_ref[TAIL:TAIL + C, :] * conv[GDN_CONV - 1:GDN_CONV, :]
    for i in range(GDN_CONV - 1):
        sh = GDN_CONV - 1 - i
        acc = acc + buf_ref[TAIL - sh:TAIL - sh + C, :] * conv[i:i + 1, :]
    buf_ref[0:TAIL, :] = buf_ref[C:C + TAIL, :]
    qkv = _silu(acc)
    q, k, v = qkv[:, 0:512], qkv[:, 512:1024], qkv[:, 1024:1536]

    sm = sm_ref[...]
    lane = _iota2((C, SMALL_W), 1)
    beta_all = _sigmoid(sm)
    g_all = jnp.where((lane >= SM_A) & (lane < SM_A + N_HEADS),
                      -jnp.exp(alog_ref[...]) * _softplus(sm + dtb_ref[...]), 0.0)
    ri = _iota2((C, C), 0)
    ci = _iota2((C, C), 1)
    tril = ri >= ci
    strict = ri > ci
    gam_all = _dot(tril.astype(F32), g_all, HI)
    zg = _silu(z_ref[...])
    ng = ng_ref[...]
    selrow = _iota2((C, SMALL_W), 1)

    for h in range(N_HEADS):
        sl = slice(h * HEAD_DIM, (h + 1) * HEAD_DIM)
        q_h, k_h, v_h = q[:, sl], k[:, sl], v[:, sl]
        q_h = q_h * lax.rsqrt(jnp.sum(q_h * q_h, -1, keepdims=True) + 1e-6) * SCALE
        k_h = k_h * lax.rsqrt(jnp.sum(k_h * k_h, -1, keepdims=True) + 1e-6)
        beta = beta_all[:, SM_BETA + h:SM_BETA + h + 1]
        gam = gam_all[:, SM_A + h:SM_A + h + 1]
        gam_row = _dot_nt((selrow == SM_A + h).astype(F32), gam_all, HI)
        decay = jnp.exp(jnp.where(tril, gam - gam_row, NEG_INF))
        kbeta = k_h * beta
        kkq = _dot_nt(jnp.concatenate([kbeta, q_h], axis=0), k_h, HI)
        lm = jnp.where(strict, kkq[:C] * decay, 0.0)
        qk = kkq[C:] * decay
        tm = _inv_unit_lower(-lm)
        u = _dot(tm, v_h * beta, HI)
        w = _dot(tm, kbeta * jnp.exp(gam), HI)
        s_h = s_ref[h]
        ws = _dot(jnp.concatenate([w, q_h * jnp.exp(gam)], axis=0), s_h, HI)
        v_new = u - ws[:C]
        o = ws[C:] + _dot(qk, v_new, HI)
        g_last = gam[C - 1:C, :]
        s_ref[h] = s_h * jnp.exp(g_last) + _dot_tn(k_h * jnp.exp(g_last - gam), v_new, HI)
        o = o * lax.rsqrt(jnp.mean(o * o, -1, keepdims=True) + GDN_EPS) * ng
        o_ref[:, sl] = o * zg[:, sl]


def _gdn(cf, B, T, conv, alog_sm, dtb_sm, ng):
    nch = T // CHUNK
    row = lambda j: pl.BlockSpec((CHUNK, 512), lambda b, c, j=j: (b * nch + c, j))
    full = lambda a: pl.BlockSpec(a.shape, lambda b, c: (0,) * a.ndim)
    ps = (conv, alog_sm, dtb_sm, ng)
    return pl.pallas_call(
        _gdn_kernel,
        grid=(B, nch),
        in_specs=[row(FCOL['gdn_q']), row(FCOL['gdn_k']), row(FCOL['gdn_v']), row(FCOL['gdn_g']),
                  pl.BlockSpec((CHUNK, SMALL_W), lambda b, c: (b * nch + c, SMALL_BLK))] + [full(a) for a in ps],
        out_specs=pl.BlockSpec((CHUNK, 512), lambda b, c: (b * nch + c, 0)),
        out_shape=jax.ShapeDtypeStruct((B * T, D_GROUP), F32),
        scratch_shapes=[pltpu.VMEM((TAIL + CHUNK, 3 * D_GROUP), F32),
                        pltpu.VMEM((N_HEADS, HEAD_DIM, HEAD_DIM), F32)],
        compiler_params=_params(("parallel", "arbitrary")),
        name="gdn",
    )(cf, cf, cf, cf, cf, *ps)


def _attn_kernel(*refs, window, use_sel, use_sink, gate_col):
    it = iter(refs)
    q_ref, k_ref, v_ref, g_ref = next(it), next(it), next(it), next(it)
    sm_ref = next(it) if gate_col is not None else None
    sel_ref = next(it) if use_sel else None
    sink_ref = next(it) if use_sink else None
    o_ref = next(it)
    i = pl.program_id(1)
    lo = 0 if window is None else jnp.maximum(i - window // TQ, 0)
    qpos = i * TQ + _iota2((TQ, TQ), 0)
    lane = _iota2((TQ, TQ), 1)
    gate = _silu(g_ref[...])
    if gate_col is not None:
        bgate = _sigmoid(sm_ref[...])

    for head in range(N_HEADS):
        g = head // HPG
        sl = slice(head * HEAD_DIM, (head + 1) * HEAD_DIM)
        ksl = slice(g * HEAD_DIM, (g + 1) * HEAD_DIM)
        q = q_ref[:, sl]
        if use_sel:
            sel = sel_ref[0, g].astype(BF16)
            nblk = sel.shape[1]

        def body(kt, carry, q=q, ksl=ksl):
            m, l, acc = carry
            off = pl.multiple_of(kt * TQ, TQ)
            kb = k_ref[pl.ds(off, TQ), ksl]
            vb = v_ref[pl.ds(off, TQ), ksl]
            s = _dot_nt(q, kb) * SCALE
            kpos = kt * TQ + lane
            mask = kpos <= qpos
            if window is not None:
                mask = mask & (qpos - kpos < window)
            if use_sel:
                expand = (_iota2((nblk, TQ), 0) ==
                          kt * (TQ // NSA_SEL_BLOCK) + _iota2((nblk, TQ), 1) // NSA_SEL_BLOCK)
                mask = mask & (_dot(sel, expand.astype(BF16)) > 0.5)
            s = jnp.where(mask, s, NEG_INF)
            m_new = jnp.maximum(m, jnp.max(s, -1, keepdims=True))
            a = jnp.exp(m - m_new)
            p = jnp.where(mask, jnp.exp(s - m_new), 0.0)
            l = a * l + jnp.sum(p, -1, keepdims=True)
            acc = a * acc + _dot(p.astype(BF16), vb)
            return m_new, l, acc

        if use_sink:
            m0 = jnp.zeros((TQ, 1), F32) + sink_ref[0:1, head:head + 1]
            l0 = jnp.ones((TQ, 1), F32)
        else:
            m0 = jnp.full((TQ, 1), NEG_INF, F32)
            l0 = jnp.zeros((TQ, 1), F32)
        m, l, acc = lax.fori_loop(lo, i + 1, body, (m0, l0, jnp.zeros((TQ, HEAD_DIM), F32)))
        o = acc / l
        if gate_col is not None:
            o = o * bgate[:, gate_col + head:gate_col + head + 1]
        o_ref[:, sl] = o * gate[:, sl]


def _attn(cb, cf, B, T, qname, kname, vname, gname, *, window=None, sel=None, sinks=None, gate_col=None):
    nq = T // TQ
    in_specs = [pl.BlockSpec((TQ, 512), lambda b, i, j=BF_Q[qname]: (b * nq + i, j)),
                pl.BlockSpec((T, KV_W), lambda b, i, j=BF_KV[kname]: (b, j)),
                pl.BlockSpec((T, KV_W), lambda b, i, j=BF_KV[vname]: (b, j)),
                pl.BlockSpec((TQ, 512), lambda b, i, j=FCOL[gname]: (b * nq + i, j))]
    args = [cb, cb, cb, cf]
    if gate_col is not None:
        in_specs.append(pl.BlockSpec((TQ, SMALL_W), lambda b, i: (b * nq + i, SMALL_BLK)))
        args.append(cf)
    if sel is not None:
        in_specs.append(pl.BlockSpec((1, KV_HEADS, TQ, sel.shape[-1]), lambda b, i: (b, 0, i, 0)))
        args.append(sel)
    if sinks is not None:
        in_specs.append(pl.BlockSpec(sinks.shape, lambda b, i: (0, 0)))
        args.append(sinks)
    return pl.pallas_call(
        functools.partial(_attn_kernel, window=window, use_sel=sel is not None,
                          use_sink=sinks is not None, gate_col=gate_col),
        grid=(B, nq),
        in_specs=in_specs,
        out_specs=pl.BlockSpec((TQ, 512), lambda b, i: (b * nq + i, 0)),
        out_shape=jax.ShapeDtypeStruct((B * T, D_GROUP), F32),
        compiler_params=_params(("parallel", "arbitrary")),
        name="attn_" + qname + ("_sel" if sel is not None else "_w%d" % window),
    )(*args)


def _gelu_tanh(x):
    return 0.5 * x * (1.0 + jnp.tanh(math.sqrt(2.0 / math.pi) * (x + 0.044715 * x * x * x)))


def _cmp_kernel(r_ref, pe1_ref, pe2_ref, w1a_ref, w1b_ref, w2_ref, o_ref):
    rows = r_ref[0]
    p1 = _dot((rows + pe1_ref[...]).astype(BF16), w1a_ref[...])
    p2 = _dot((rows + pe2_ref[...]).astype(BF16), w1b_ref[...])
    nr = rows.shape[0]
    hid = _gelu_tanh(p1 + pltpu.roll(p2, nr - 1, 0))
    o_ref[0] = _dot(hid.astype(BF16), w2_ref[...])


def _nsa_compress(ccmp, B, T, pe1, pe2, w1a, w1b, w2e):
    nr = T // NSA_CMP_STRIDE
    wide = NSA_CMP_STRIDE * 2 * KV_W
    rows = ccmp.reshape(B, nr, wide)
    full = lambda a: pl.BlockSpec(a.shape, lambda b: (0,) * a.ndim)
    return pl.pallas_call(
        _cmp_kernel,
        grid=(B,),
        in_specs=[pl.BlockSpec((1, nr, wide), lambda b: (b, 0, 0))] + [full(a) for a in (pe1, pe2, w1a, w1b, w2e)],
        out_specs=pl.BlockSpec((1, nr, 2 * KV_W), lambda b: (b, 0, 0)),
        out_shape=jax.ShapeDtypeStruct((B, nr, 2 * KV_W), F32),
        compiler_params=_params(("arbitrary",)),
        name="nsa_compress",
    )(rows, pe1, pe2, w1a, w1b, w2e)


def _cmpattn_kernel(q_ref, cmp_ref, g_ref, sm_ref, ov_ref, o_ref, sel_ref):
    i = pl.program_id(1)
    kv = cmp_ref[0]
    nr = kv.shape[0]
    ns = ov_ref.shape[1]
    tpos = i * TQ + _iota2((TQ, nr), 0)
    cmask = _iota2((TQ, nr), 1) * NSA_CMP_STRIDE + (NSA_CMP_BLOCK - 1) <= tpos
    gate = _silu(g_ref[...])
    bgate = _sigmoid(sm_ref[...])
    tblk = (i * TQ + _iota2((TQ, ns), 0)) // NSA_SEL_BLOCK
    jj = _iota2((TQ, ns), 1)
    forced = (jj == 0) | (jj == tblk) | (jj == tblk - 1)
    causal = jj <= tblk
    for g in range(KV_HEADS):
        kc = kv[:, g * HEAD_DIM:(g + 1) * HEAD_DIM].astype(BF16)
        vc = kv[:, KV_W + g * HEAD_DIM:KV_W + (g + 1) * HEAD_DIM].astype(BF16)
        psum = jnp.zeros((TQ, nr), F32)
        for h in range(HPG):
            head = g * HPG + h
            sl = slice(head * HEAD_DIM, (head + 1) * HEAD_DIM)
            s = jnp.where(cmask, _dot_nt(q_ref[:, sl], kc) * SCALE, NEG_INF)
            e = jnp.exp(s - jnp.max(s, -1, keepdims=True))
            p = jnp.where(cmask, e / jnp.sum(e, -1, keepdims=True), 0.0)
            psum = psum + p
            o = _dot(p.astype(BF16), vc)
            o_ref[:, sl] = o * bgate[:, SM_GATE + head:SM_GATE + head + 1] * gate[:, sl]
        imp = _dot(psum, ov_ref[...], HI)
        imp = jnp.where(causal, jnp.where(forced, NSA_FORCE, imp), NEG_INF)
        rank = jnp.zeros((TQ, ns), jnp.int32)
        for j in range(ns):
            col = imp[:, j:j + 1]
            rank = rank + ((col > imp) | ((col == imp) & (j < jj))).astype(jnp.int32)
        sel_ref[0, g] = ((rank < NSA_TOPN) & causal).astype(F32)


def _nsa_cmpattn(cb, cmp, cf, B, T, ov):
    nq = T // TQ
    nr = cmp.shape[1]
    ns = T // NSA_SEL_BLOCK
    return pl.pallas_call(
        _cmpattn_kernel,
        grid=(B, nq),
        in_specs=[pl.BlockSpec((TQ, 512), lambda b, i: (b * nq + i, BF_Q['nsa_q'])),
                  pl.BlockSpec((1, nr, 2 * KV_W), lambda b, i: (b, 0, 0)),
                  pl.BlockSpec((TQ, 512), lambda b, i: (b * nq + i, FCOL['nsa_g'])),
                  pl.BlockSpec((TQ, SMALL_W), lambda b, i: (b * nq + i, SMALL_BLK)),
                  pl.BlockSpec(ov.shape, lambda b, i: (0, 0))],
        out_specs=[pl.BlockSpec((TQ, 512), lambda b, i: (b * nq + i, 0)),
                   pl.BlockSpec((1, KV_HEADS, TQ, ns), lambda b, i: (b, 0, i, 0))],
        out_shape=[jax.ShapeDtypeStruct((B * T, D_GROUP), F32),
                   jax.ShapeDtypeStruct((B, KV_HEADS, T, ns), F32)],
        compiler_params=_params(("parallel", "arbitrary")),
        name="nsa_cmpattn",
    )(cb, cmp, cf, cf, ov)


def _out_kernel(ya_ref, yb_ref, c1_ref, c2_ref, c3_ref, yd_ref, x_ref, w_ref, lg_ref, lb_ref, o_ref, ob_ref):
    yc = c1_ref[...] + c2_ref[...] + c3_ref[...]
    acc = _dot(ya_ref[...].astype(BF16), w_ref[0:512, :])
    acc = acc + _dot(yb_ref[...].astype(BF16), w_ref[512:1024, :])
    acc = acc + _dot(yc.astype(BF16), w_ref[1024:1536, :])
    acc = acc + _dot(yd_ref[...].astype(BF16), w_ref[1536:2048, :])
    z = DEEPNORM_ALPHA * x_ref[...] + acc
    mu = jnp.mean(z, -1, keepdims=True)
    var = jnp.mean(jnp.square(z - mu), -1, keepdims=True)
    out = (z - mu) * lax.rsqrt(var + LN_EPS) * lg_ref[...] + lb_ref[...]
    o_ref[...] = out
    ob_ref[...] = out.astype(BF16)


def _out_proj(ys, x, w, lg, lb, tm=256):
    m = x.shape[0]
    yspec = pl.BlockSpec((tm, 512), lambda i: (i, 0))
    xspec = pl.BlockSpec((tm, D_MODEL), lambda i: (i, 0))
    return pl.pallas_call(
        _out_kernel,
        grid=(m // tm,),
        in_specs=[yspec] * 6 + [xspec, pl.BlockSpec(w.shape, lambda i: (0, 0)),
                                pl.BlockSpec(lg.shape, lambda i: (0, 0)), pl.BlockSpec(lb.shape, lambda i: (0, 0))],
        out_specs=[xspec, xspec],
        out_shape=[jax.ShapeDtypeStruct((m, D_MODEL), F32), jax.ShapeDtypeStruct((m, D_MODEL), BF16)],
        compiler_params=_params(("parallel",)),
        name="out_proj_ln",
    )(*ys, x, w, lg, lb)


def _cols(w, name):
    o, s = _OFF[name]
    return w[..., o:o + s]


def _small_row(pieces):
    row = jnp.zeros((1, SMALL_W), F32)
    for off, vals in pieces:
        row = row.at[0, off:off + vals.shape[-1]].set(vals.astype(F32))
    return row


def _overlap_matrix(T):
    nc = T // NSA_CMP_STRIDE
    ns = T // NSA_SEL_BLOCK
    cst = np.arange(nc) * NSA_CMP_STRIDE
    jst = np.arange(ns) * NSA_SEL_BLOCK
    ov = np.clip(np.minimum(cst[:, None] + NSA_CMP_BLOCK, jst[None, :] + NSA_SEL_BLOCK)
                 - np.maximum(cst[:, None], jst[None, :]), 0, None).astype(np.float32) / NSA_CMP_BLOCK
    return jnp.asarray(ov)


def _cmp_weights(pe_k, pe_v, k_w1, k_w2, v_w1, v_w2):
    half = NSA_CMP_STRIDE
    hid = NSA_CMP_HIDDEN

    def w1_half(lo):
        blocks = jnp.zeros((half, 4, HEAD_DIM, 4, hid), F32)
        kw = k_w1.reshape(NSA_CMP_BLOCK, HEAD_DIM, hid)[lo:lo + half]
        vw = v_w1.reshape(NSA_CMP_BLOCK, HEAD_DIM, hid)[lo:lo + half]
        for s, wsrc in ((0, kw), (1, kw), (2, vw), (3, vw)):
            blocks = blocks.at[:, s, :, s, :].set(wsrc)
        return blocks.reshape(half * 4 * HEAD_DIM, 4 * hid).astype(BF16)

    def pe_half(lo):
        pk, pv = pe_k[lo:lo + half], pe_v[lo:lo + half]
        return jnp.stack([pk, pk, pv, pv], axis=1).reshape(1, half * 4 * HEAD_DIM)

    w2 = jnp.zeros((4, hid, 4, HEAD_DIM), F32)
    for s, wsrc in ((0, k_w2), (1, k_w2), (2, v_w2), (3, v_w2)):
        w2 = w2.at[s, :, s, :].set(wsrc)
    return pe_half(0), pe_half(half), w1_half(0), w1_half(half), w2.reshape(4 * hid, 4 * HEAD_DIM).astype(BF16)


def _layer(x, xb, B, T, w_in, w_out, ln_g, ln_b, rw_mu, rw_w0, rw_w2, rw_a0, rw_a2, rw_kk, rw_ka, rw_rk,
           rw_gn_g, rw_gn_b, swa_sinks, nsa_pe_k, nsa_pe_v, nsa_k_w1, nsa_k_w2, nsa_v_w1, nsa_v_w2,
           gdn_conv, gdn_A_log, gdn_dt_bias, gdn_norm_g):
    small = jnp.concatenate([_cols(w_in, n) for n in ('rw_wd', 'rw_ad', 'nsa_gate', 'gdn_beta', 'gdn_a')], axis=1)
    small = jnp.pad(small, ((0, 0), (0, SMALL_W - small.shape[1])))
    w_f32 = jnp.concatenate([_cols(w_in, n) for n in F32_GROUPS] + [small], axis=1).astype(BF16)
    w_bf = jnp.concatenate([_cols(w_in, n) for n in ('swa_q', 'nsa_q', 'swa_k', 'swa_v', 'nsa_ks', 'nsa_vs',
                                                     'nsa_kw', 'nsa_vw')], axis=1).astype(BF16)
    w_cmp = jnp.concatenate([_cols(w_in, 'nsa_kc'), _cols(w_in, 'nsa_vc')], axis=1).astype(BF16)
    tm = min(1024, B * T)
    cf = _matmul(xb, w_f32, F32, tm, 256)
    cb = _matmul(xb, w_bf, BF16, tm, 256)
    ccmp = _matmul(xb, w_cmp, F32, tm, 256)

    row = lambda a: a.reshape(1, -1).astype(F32)
    mu3 = row(rw_mu[:3 * D_GROUP])
    musm = _small_row([(SM_WD, rw_mu[3 * D_GROUP:3 * D_GROUP + RW_LORA]),
                       (SM_AD, rw_mu[3 * D_GROUP + RW_LORA:])])
    y_a = _rwkv(cf, B, T, mu3, musm, row(rw_w0), rw_w2, row(rw_a0), rw_a2, row(rw_kk), row(rw_ka),
                row(rw_rk), row(rw_gn_g), row(rw_gn_b))
    y_d = _gdn(cf, B, T, gdn_conv, _small_row([(SM_A, gdn_A_log)]), _small_row([(SM_A, gdn_dt_bias)]),
               row(gdn_norm_g))
    y_b = _attn(cb, cf, B, T, 'swa_q', 'swa_k', 'swa_v', 'swa_g', window=SWA_WINDOW, sinks=row(swa_sinks))
    cmp = _nsa_compress(ccmp, B, T, *_cmp_weights(nsa_pe_k, nsa_pe_v, nsa_k_w1, nsa_k_w2, nsa_v_w1, nsa_v_w2))
    c1, sel = _nsa_cmpattn(cb, cmp, cf, B, T, _overlap_matrix(T))
    c2 = _attn(cb, cf, B, T, 'nsa_q', 'nsa_ks', 'nsa_vs', 'nsa_g', sel=sel, gate_col=SM_GATE + N_HEADS)
    c3 = _attn(cb, cf, B, T, 'nsa_q', 'nsa_kw', 'nsa_vw', 'nsa_g', window=NSA_WINDOW,
               gate_col=SM_GATE + 2 * N_HEADS)
    return _out_proj((y_a, y_b, c1, c2, c3, y_d), x, w_out.astype(BF16), row(ln_g), row(ln_b))


def kernel(x, w_in, w_out, ln_g, ln_b, rw_mu, rw_w0, rw_w2, rw_a0, rw_a2, rw_kk, rw_ka, rw_rk, rw_gn_g, rw_gn_b, swa_sinks, nsa_pe_k, nsa_pe_v, nsa_k_w1, nsa_k_w2, nsa_v_w1, nsa_v_w2, gdn_conv, gdn_A_log, gdn_dt_bias, gdn_norm_g):
    B, T, D = x.shape
    params = (w_in, w_out, ln_g, ln_b, rw_mu, rw_w0, rw_w2, rw_a0, rw_a2, rw_kk, rw_ka, rw_rk, rw_gn_g, rw_gn_b,
              swa_sinks, nsa_pe_k, nsa_pe_v, nsa_k_w1, nsa_k_w2, nsa_v_w1, nsa_v_w2, gdn_conv, gdn_A_log,
              gdn_dt_bias, gdn_norm_g)
    xf = x.reshape(B * T, D)
    xb = xf.astype(BF16)
    for i in range(w_in.shape[0]):
        xf, xb = _layer(xf, xb, B, T, *(p[i] for p in params))
    return xf.reshape(B, T, D)
```

```python
import functools
import math

import numpy as np
import jax
import jax.numpy as jnp
from jax import lax
from jax.experimental import pallas as pl
from jax.experimental.pallas import tpu as pltpu

F32 = jnp.float32
BF16 = jnp.bfloat16
HI = lax.Precision.HIGHEST

D_MODEL = 2048
DEPTH = 4
D_GROUP = 512
HEAD_DIM = 64
N_HEADS = 8
KV_HEADS = 2
HPG = N_HEADS // KV_HEADS
KV_W = KV_HEADS * HEAD_DIM
NEG_INF = -1e30
LN_EPS = 1e-5
DEEPNORM_ALPHA = (2 * DEPTH) ** 0.25
RW_LORA = 32
RW_GN_EPS = 64e-5
SWA_WINDOW = 128
NSA_CMP_BLOCK = 32
NSA_CMP_STRIDE = 16
NSA_CMP_HIDDEN = 128
NSA_SEL_BLOCK = 64
NSA_TOPN = 16
NSA_WINDOW = 512
NSA_FORCE = 1e6
GDN_CONV = 4
GDN_EPS = 1e-6
SCALE = HEAD_DIM ** -0.5

CHUNK = 64
TQ = 128
TAIL = 8
VMEM_LIMIT = 56 * 1024 * 1024

_COLS = (
    ('rw_r', 512), ('rw_k', 512), ('rw_v', 512), ('rw_wd', 32), ('rw_ad', 32), ('rw_g', 512),
    ('swa_q', 512), ('swa_k', 128), ('swa_v', 128), ('swa_g', 512),
    ('nsa_q', 512), ('nsa_kc', 128), ('nsa_vc', 128), ('nsa_ks', 128), ('nsa_vs', 128),
    ('nsa_kw', 128), ('nsa_vw', 128), ('nsa_gate', 24), ('nsa_g', 512),
    ('gdn_q', 512), ('gdn_k', 512), ('gdn_v', 512), ('gdn_beta', 8), ('gdn_a', 8), ('gdn_g', 512),
)
_OFF = {}
_o = 0
for _n, _s in _COLS:
    _OFF[_n] = (_o, _s)
    _o += _s
N_IN = _o

F32_GROUPS = ('rw_r', 'rw_k', 'rw_v', 'rw_g', 'gdn_q', 'gdn_k', 'gdn_v', 'gdn_g', 'swa_g', 'nsa_g')
FCOL = {n: i for i, n in enumerate(F32_GROUPS)}
SMALL_W = 256
SMALL_BLK = len(F32_GROUPS) * 512 // SMALL_W
SM_WD, SM_AD, SM_GATE, SM_BETA, SM_A = 0, 32, 64, 88, 96
N_F32 = len(F32_GROUPS) * 512 + SMALL_W
BF_Q = {'swa_q': 0, 'nsa_q': 1}
BF_KV = {n: 8 + i for i, n in enumerate(('swa_k', 'swa_v', 'nsa_ks', 'nsa_vs', 'nsa_kw', 'nsa_vw'))}
N_BF = 1024 + 6 * 128


def _dot(a, b, prec=None):
    return lax.dot_general(a, b, (((1,), (0,)), ((), ())), precision=prec, preferred_element_type=F32)


def _dot_nt(a, b, prec=None):
    return lax.dot_general(a, b, (((1,), (1,)), ((), ())), precision=prec, preferred_element_type=F32)


def _dot_tn(a, b, prec=None):
    return lax.dot_general(a, b, (((0,), (0,)), ((), ())), precision=prec, preferred_element_type=F32)


def _rdot(a, b):
    return _dot(a.astype(BF16), b.astype(BF16))


def _rdot_nt(a, b):
    return _dot_nt(a.astype(BF16), b.astype(BF16))


def _rdot_tn(a, b):
    return _dot_tn(a.astype(BF16), b.astype(BF16))


def _sigmoid(x):
    return 1.0 / (1.0 + jnp.exp(-x))


def _silu(x):
    return x * _sigmoid(x)


def _softplus(x):
    return jnp.maximum(x, 0.0) + jnp.log(1.0 + jnp.exp(-jnp.abs(x)))


def _iota2(shape, dim):
    return lax.broadcasted_iota(jnp.int32, shape, dim)


def _inv_unit_lower(x):
    n = x.shape[0]
    eye = (_iota2((n, n), 0) == _iota2((n, n), 1)).astype(F32)
    p = eye + x
    for _ in range(int(math.log2(n)) - 1):
        x = _rdot(x, x)
        p = p + _rdot(x, p)
    return p


def _params(sem):
    return pltpu.CompilerParams(dimension_semantics=sem, vmem_limit_bytes=VMEM_LIMIT)


def _mm_kernel(x_ref, w_ref, o_ref):
    o_ref[...] = jnp.dot(x_ref[...], w_ref[...], preferred_element_type=F32).astype(o_ref.dtype)


def _matmul(x, w, out_dtype, tm, tn):
    m, k = x.shape
    n = w.shape[1]
    return pl.pallas_call(
        _mm_kernel,
        grid=(m // tm, n // tn),
        in_specs=[pl.BlockSpec((tm, k), lambda i, j: (i, 0)),
                  pl.BlockSpec((k, tn), lambda i, j: (0, j))],
        out_specs=pl.BlockSpec((tm, tn), lambda i, j: (i, j)),
        out_shape=jax.ShapeDtypeStruct((m, n), out_dtype),
        compiler_params=_params(("parallel", "arbitrary")),
        name="proj_in",
    )(x, w)


def _rwkv_kernel(r_ref, k_ref, v_ref, g_ref, sm_ref, mu_ref, musm_ref, w0_ref, w2_ref, a0_ref, a2_ref,
                 kk_ref, ka_ref, rk_ref, gng_ref, gnb_ref, o_ref, buf_ref, bufsm_ref, s_ref):
    c = pl.program_id(1)
    C = CHUNK

    @pl.when(c == 0)
    def _():
        buf_ref[0:TAIL, :] = jnp.zeros((TAIL, 3 * D_GROUP), F32)
        bufsm_ref[0:TAIL, :] = jnp.zeros((TAIL, SMALL_W), F32)
        s_ref[...] = jnp.zeros(s_ref.shape, F32)

    buf_ref[TAIL:TAIL + C, 0:512] = r_ref[...]
    buf_ref[TAIL:TAIL + C, 512:1024] = k_ref[...]
    buf_ref[TAIL:TAIL + C, 1024:1536] = v_ref[...]
    bufsm_ref[TAIL:TAIL + C, :] = sm_ref[...]
    cur = buf_ref[TAIL:TAIL + C, :]
    prev = buf_ref[TAIL - 1:TAIL - 1 + C, :]
    mixed = cur + (prev - cur) * mu_ref[...]
    cur_sm = bufsm_ref[TAIL:TAIL + C, :]
    prev_sm = bufsm_ref[TAIL - 1:TAIL - 1 + C, :]
    smix = cur_sm + (prev_sm - cur_sm) * musm_ref[...]
    buf_ref[0:TAIL, :] = buf_ref[C:C + TAIL, :]
    bufsm_ref[0:TAIL, :] = bufsm_ref[C:C + TAIL, :]

    r = mixed[:, 0:512]
    k = mixed[:, 512:1024]
    v = mixed[:, 1024:1536]
    wd = smix[:, SM_WD:SM_WD + RW_LORA]
    ad = smix[:, SM_AD:SM_AD + RW_LORA]
    wl = w0_ref[...] + _dot(jnp.tanh(wd), w2_ref[...], HI)
    logw = -jnp.exp(-_softplus(-wl) - 0.5)
    alpha = _sigmoid(a0_ref[...] + _dot(ad, a2_ref[...], HI))
    kkraw = k * kk_ref[...]
    k2 = k * (1.0 + (alpha - 1.0) * ka_ref[...])
    ri = _iota2((C, C), 0)
    ci = _iota2((C, C), 1)
    tril = ri >= ci
    strict = ri > ci
    gcum = _dot(tril.astype(F32), logw, HI)
    gate = _silu(g_ref[...])
    rk = rk_ref[...]
    gng = gng_ref[...]
    gnb = gnb_ref[...]

    for h in range(N_HEADS):
        sl = slice(h * HEAD_DIM, (h + 1) * HEAD_DIM)
        r_h, k_h, v_h, a_h = r[:, sl], k2[:, sl], v[:, sl], alpha[:, sl]
        kkr = kkraw[:, sl]
        kk_h = kkr * lax.rsqrt(jnp.sum(kkr * kkr, -1, keepdims=True) + 1e-6)
        lw_h, g_h = logw[:, sl], gcum[:, sl]
        gm = g_h[C // 2 - 1:C // 2, :]
        ge = g_h[C - 1:C, :]
        b_h = kk_h * a_h
        einv = jnp.exp(gm - g_h)
        eend = jnp.exp(ge - g_h)
        left = jnp.concatenate([-kk_h * jnp.exp(g_h - lw_h - gm), r_h * jnp.exp(g_h - gm)], axis=0)
        ab = _rdot_nt(left, b_h * einv)
        ak = _rdot_nt(left, k_h * einv)
        a_ab = jnp.where(strict, ab[:C], 0.0)
        a_rb = jnp.where(tril, ab[C:], 0.0)
        a_k = jnp.concatenate([jnp.where(strict, ak[:C], 0.0), jnp.where(tril, ak[C:], 0.0)], axis=0)
        tinv = _inv_unit_lower(a_ab)
        s_h = s_ref[h]
        ls = _rdot_nt(left, s_h * jnp.exp(gm))
        av = _rdot(a_k, v_h)
        u = _rdot(tinv, ls[:C] + av[:C])
        o = ls[C:] + _rdot(a_rb, u) + av[C:]
        s_ref[h] = s_h * jnp.exp(ge) + _rdot_tn(u, b_h * eend) + _rdot_tn(v_h, k_h * eend)
        mu = jnp.mean(o, -1, keepdims=True)
        var = jnp.mean(jnp.square(o - mu), -1, keepdims=True)
        y = (o - mu) * lax.rsqrt(var + RW_GN_EPS) * gng[:, sl] + gnb[:, sl]
        bonus = jnp.sum(r_h * k_h * rk[:, sl], -1, keepdims=True) * v_h
        o_ref[:, sl] = (y + bonus) * gate[:, sl]


def _rwkv(cf, B, T, mu3, musm, w0, w2, a0, a2, kk, ka, rk, gng, gnb):
    nch = T // CHUNK
    row = lambda j: pl.BlockSpec((CHUNK, 512), lambda b, c, j=j: (b * nch + c, j))
    full = lambda a: pl.BlockSpec(a.shape, lambda b, c: (0,) * a.ndim)
    ps = (mu3, musm, w0, w2, a0, a2, kk, ka, rk, gng, gnb)
    return pl.pallas_call(
        _rwkv_kernel,
        grid=(B, nch),
        in_specs=[row(FCOL['rw_r']), row(FCOL['rw_k']), row(FCOL['rw_v']), row(FCOL['rw_g']),
                  pl.BlockSpec((CHUNK, SMALL_W), lambda b, c: (b * nch + c, SMALL_BLK))] + [full(a) for a in ps],
        out_specs=pl.BlockSpec((CHUNK, 512), lambda b, c: (b * nch + c, 0)),
        out_shape=jax.ShapeDtypeStruct((B * T, D_GROUP), F32),
        scratch_shapes=[pltpu.VMEM((TAIL + CHUNK, 3 * D_GROUP), F32),
                        pltpu.VMEM((TAIL + CHUNK, SMALL_W), F32),
                        pltpu.VMEM((N_HEADS, HEAD_DIM, HEAD_DIM), F32)],
        compiler_params=_params(("parallel", "arbitrary")),
        name="rwkv7",
    )(cf, cf, cf, cf, cf, *ps)


def _gdn_kernel(q_ref, k_ref, v_ref, z_ref, sm_ref, conv_ref, alog_ref, dtb_ref, ng_ref, o_ref,
                buf_ref, s_ref):
    c = pl.program_id(1)
    C = CHUNK

    @pl.when(c == 0)
    def _():
        buf_ref[0:TAIL, :] = jnp.zeros((TAIL, 3 * D_GROUP), F32)
        s_ref[...] = jnp.zeros(s_ref.shape, F32)

    buf_ref[TAIL:TAIL + C, 0:512] = q_ref[...]
    buf_ref[TAIL:TAIL + C, 512:1024] = k_ref[...]
    buf_ref[TAIL:TAIL + C, 1024:1536] = v_ref[...]
    conv = conv_ref[...]
    acc = buf_ref[TAIL:TAIL + C, :] * conv[GDN_CONV - 1:GDN_CONV, :]
    for i in range(GDN_CONV - 1):
        sh = GDN_CONV - 1 - i
        acc = acc + buf_ref[TAIL - sh:TAIL - sh + C, :] * conv[i:i + 1, :]
    buf_ref[0:TAIL, :] = buf_ref[C:C + TAIL, :]
    qkv = _silu(acc)
    q, k, v = qkv[:, 0:512], qkv[:, 512:1024], qkv[:, 1024:1536]

    sm = sm_ref[...]
    lane = _iota2((C, SMALL_W), 1)
    beta_all = _sigmoid(sm)
    g_all = jnp.where((lane >= SM_A) & (lane < SM_A + N_HEADS),
                      -jnp.exp(alog_ref[...]) * _softplus(sm + dtb_ref[...]), 0.0)
    ri = _iota2((C, C), 0)
    ci = _iota2((C, C), 1)
    tril = ri >= ci
    strict = ri > ci
    gam_all = _dot(tril.astype(F32), g_all, HI)
    zg = _silu(z_ref[...])
    ng = ng_ref[...]
    pick = (_iota2((N_HEADS, SMALL_W), 1) == SM_A + _iota2((N_HEADS, SMALL_W), 0)).astype(F32)
    gam_rows = _dot_nt(pick, gam_all, HI)

    for h in range(N_HEADS):
        sl = slice(h * HEAD_DIM, (h + 1) * HEAD_DIM)
        q_h, k_h, v_h = q[:, sl], k[:, sl], v[:, sl]
        q_h = q_h * lax.rsqrt(jnp.sum(q_h * q_h, -1, keepdims=True) + 1e-6) * SCALE
        k_h = k_h * lax.rsqrt(jnp.sum(k_h * k_h, -1, keepdims=True) + 1e-6)
        beta = beta_all[:, SM_BETA + h:SM_BETA + h + 1]
        gam = gam_all[:, SM_A + h:SM_A + h + 1]
        decay = jnp.exp(jnp.where(tril, gam - gam_rows[h:h + 1, :], NEG_INF))
        kbeta = k_h * beta
        kkq = _rdot_nt(jnp.concatenate([kbeta, q_h], axis=0), k_h)
        lm = jnp.where(strict, kkq[:C] * decay, 0.0)
        qk = kkq[C:] * decay
        tm = _inv_unit_lower(-lm)
        u = _rdot(tm, v_h * beta)
        w = _rdot(tm, kbeta * jnp.exp(gam))
        s_h = s_ref[h]
        ws = _rdot(jnp.concatenate([w, q_h * jnp.exp(gam)], axis=0), s_h)
        v_new = u - ws[:C]
        o = ws[C:] + _rdot(qk, v_new)
        g_last = gam[C - 1:C, :]
        s_ref[h] = s_h * jnp.exp(g_last) + _rdot_tn(k_h * jnp.exp(g_last - gam), v_new)
        o = o * lax.rsqrt(jnp.mean(o * o, -1, keepdims=True) + GDN_EPS) * ng
        o_ref[:, sl] = o * zg[:, sl]


def _gdn(cf, B, T, conv, alog_sm, dtb_sm, ng):
    nch = T // CHUNK
    row = lambda j: pl.BlockSpec((CHUNK, 512), lambda b, c, j=j: (b * nch + c, j))
    full = lambda a: pl.BlockSpec(a.shape, lambda b, c: (0,) * a.ndim)
    ps = (conv, alog_sm, dtb_sm, ng)
    return pl.pallas_call(
        _gdn_kernel,
        grid=(B, nch),
        in_specs=[row(FCOL['gdn_q']), row(FCOL['gdn_k']), row(FCOL['gdn_v']), row(FCOL['gdn_g']),
                  pl.BlockSpec((CHUNK, SMALL_W), lambda b, c: (b * nch + c, SMALL_BLK))] + [full(a) for a in ps],
        out_specs=pl.BlockSpec((CHUNK, 512), lambda b, c: (b * nch + c, 0)),
        out_shape=jax.ShapeDtypeStruct((B * T, D_GROUP), F32),
        scratch_shapes=[pltpu.VMEM((TAIL + CHUNK, 3 * D_GROUP), F32),
                        pltpu.VMEM((N_HEADS, HEAD_DIM, HEAD_DIM), F32)],
        compiler_params=_params(("parallel", "arbitrary")),
        name="gdn",
    )(cf, cf, cf, cf, cf, *ps)


def _attn_kernel(*refs, window, use_sel, use_sink, gate_col, tk):
    it = iter(refs)
    q_ref, k_ref, v_ref, g_ref = next(it), next(it), next(it), next(it)
    sm_ref = next(it) if gate_col is not None else None
    sel_ref = next(it) if use_sel else None
    sink_ref = next(it) if use_sink else None
    o_ref = next(it)
    i = pl.program_id(1)
    hi = (i * TQ + TQ + tk - 1) // tk
    lo = 0 if window is None else jnp.maximum(i * TQ - window + 1, 0) // tk
    qpos = i * TQ + _iota2((TQ, tk), 0)
    lane = _iota2((TQ, tk), 1)
    gate = _silu(g_ref[...])
    if gate_col is not None:
        bgate = _sigmoid(sm_ref[...])

    for g in range(KV_HEADS):
        ksl = slice(g * HEAD_DIM, (g + 1) * HEAD_DIM)
        heads = range(g * HPG, (g + 1) * HPG)
        qg = jnp.concatenate([q_ref[:, h * HEAD_DIM:(h + 1) * HEAD_DIM] for h in heads], axis=0) * SCALE
        if use_sel:
            sel = sel_ref[0, g].astype(BF16)
            nblk = sel.shape[1]

        def body(kt, carry, qg=qg, ksl=ksl):
            m, l, acc = carry
            off = pl.multiple_of(kt * tk, tk)
            kb = k_ref[pl.ds(off, tk), ksl]
            vb = v_ref[pl.ds(off, tk), ksl]
            kpos = kt * tk + lane
            mask = kpos <= qpos
            if window is not None:
                mask = mask & (qpos - kpos < window)
            if use_sel:
                expand = (_iota2((nblk, tk), 0) ==
                          kt * (tk // NSA_SEL_BLOCK) + _iota2((nblk, tk), 1) // NSA_SEL_BLOCK)
                mask = mask & (_dot(sel, expand.astype(BF16)) > 0.5)
            bias = jnp.where(mask, 0.0, NEG_INF)
            s = _dot_nt(qg, kb)
            s = (s.reshape(HPG, TQ, tk) + bias[None]).reshape(HPG * TQ, tk)
            m_new = jnp.maximum(m, jnp.max(s, -1, keepdims=True))
            a = jnp.exp(m - m_new)
            p = jnp.exp(s - m_new)
            l = a * l + jnp.sum(p, -1, keepdims=True)
            acc = a * acc + _dot(p.astype(BF16), vb)
            return m_new, l, acc

        if use_sink:
            m0 = jnp.concatenate([jnp.zeros((TQ, 1), F32) + sink_ref[0:1, h:h + 1] for h in heads], axis=0)
            l0 = jnp.ones((HPG * TQ, 1), F32)
        else:
            m0 = jnp.full((HPG * TQ, 1), NEG_INF, F32)
            l0 = jnp.zeros((HPG * TQ, 1), F32)
        m, l, acc = lax.fori_loop(lo, hi, body, (m0, l0, jnp.zeros((HPG * TQ, HEAD_DIM), F32)))
        o = acc / l
        for j, h in enumerate(heads):
            sl = slice(h * HEAD_DIM, (h + 1) * HEAD_DIM)
            oh = o[j * TQ:(j + 1) * TQ]
            if gate_col is not None:
                oh = oh * bgate[:, gate_col + h:gate_col + h + 1]
            o_ref[:, sl] = oh * gate[:, sl]


def _attn(cb, cf, B, T, qname, kname, vname, gname, *, window=None, sel=None, sinks=None, gate_col=None, tk=TQ):
    nq = T // TQ
    in_specs = [pl.BlockSpec((TQ, 512), lambda b, i, j=BF_Q[qname]: (b * nq + i, j)),
                pl.BlockSpec((T, KV_W), lambda b, i, j=BF_KV[kname]: (b, j)),
                pl.BlockSpec((T, KV_W), lambda b, i, j=BF_KV[vname]: (b, j)),
                pl.BlockSpec((TQ, 512), lambda b, i, j=FCOL[gname]: (b * nq + i, j))]
    args = [cb, cb, cb, cf]
    if gate_col is not None:
        in_specs.append(pl.BlockSpec((TQ, SMALL_W), lambda b, i: (b * nq + i, SMALL_BLK)))
        args.append(cf)
    if sel is not None:
        in_specs.append(pl.BlockSpec((1, KV_HEADS, TQ, sel.shape[-1]), lambda b, i: (b, 0, i, 0)))
        args.append(sel)
    if sinks is not None:
        in_specs.append(pl.BlockSpec(sinks.shape, lambda b, i: (0, 0)))
        args.append(sinks)
    return pl.pallas_call(
        functools.partial(_attn_kernel, window=window, use_sel=sel is not None,
                          use_sink=sinks is not None, gate_col=gate_col, tk=tk),
        grid=(B, nq),
        in_specs=in_specs,
        out_specs=pl.BlockSpec((TQ, 512), lambda b, i: (b * nq + i, 0)),
        out_shape=jax.ShapeDtypeStruct((B * T, D_GROUP), F32),
        compiler_params=_params(("parallel", "arbitrary")),
        name="attn_" + qname + ("_sel" if sel is not None else "_w%d" % window),
    )(*args)


def _gelu_tanh(x):
    return 0.5 * x * (1.0 + jnp.tanh(math.sqrt(2.0 / math.pi) * (x + 0.044715 * x * x * x)))


def _cmp_kernel(r_ref, pe1_ref, pe2_ref, w1a_ref, w1b_ref, w2_ref, o_ref):
    rows = r_ref[0]
    p1 = _dot((rows + pe1_ref[...]).astype(BF16), w1a_ref[...])
    p2 = _dot((rows + pe2_ref[...]).astype(BF16), w1b_ref[...])
    nr = rows.shape[0]
    hid = _gelu_tanh(p1 + pltpu.roll(p2, nr - 1, 0))
    o_ref[0] = _dot(hid.astype(BF16), w2_ref[...])


def _nsa_compress(ccmp, B, T, pe1, pe2, w1a, w1b, w2e):
    nr = T // NSA_CMP_STRIDE
    wide = NSA_CMP_STRIDE * 2 * KV_W
    rows = ccmp.reshape(B, nr, wide)
    full = lambda a: pl.BlockSpec(a.shape, lambda b: (0,) * a.ndim)
    return pl.pallas_call(
        _cmp_kernel,
        grid=(B,),
        in_specs=[pl.BlockSpec((1, nr, wide), lambda b: (b, 0, 0))] + [full(a) for a in (pe1, pe2, w1a, w1b, w2e)],
        out_specs=pl.BlockSpec((1, nr, 2 * KV_W), lambda b: (b, 0, 0)),
        out_shape=jax.ShapeDtypeStruct((B, nr, 2 * KV_W), F32),
        compiler_params=_params(("arbitrary",)),
        name="nsa_compress",
    )(rows, pe1, pe2, w1a, w1b, w2e)


def _cmpattn_kernel(q_ref, cmp_ref, g_ref, sm_ref, ov_ref, o_ref, sel_ref):
    i = pl.program_id(1)
    kv = cmp_ref[0]
    nr = kv.shape[0]
    ns = ov_ref.shape[1]
    tpos = i * TQ + _iota2((TQ, nr), 0)
    cmask = _iota2((TQ, nr), 1) * NSA_CMP_STRIDE + (NSA_CMP_BLOCK - 1) <= tpos
    gate = _silu(g_ref[...])
    bgate = _sigmoid(sm_ref[...])
    tblk = (i * TQ + _iota2((TQ, ns), 0)) // NSA_SEL_BLOCK
    jj = _iota2((TQ, ns), 1)
    forced = (jj == 0) | (jj == tblk) | (jj == tblk - 1)
    causal = jj <= tblk
    cbias = jnp.where(cmask, 0.0, NEG_INF)
    for g in range(KV_HEADS):
        kc = kv[:, g * HEAD_DIM:(g + 1) * HEAD_DIM].astype(BF16)
        vc = kv[:, KV_W + g * HEAD_DIM:KV_W + (g + 1) * HEAD_DIM].astype(BF16)
        heads = range(g * HPG, (g + 1) * HPG)
        qg = jnp.concatenate([q_ref[:, h * HEAD_DIM:(h + 1) * HEAD_DIM] for h in heads], axis=0) * SCALE
        s = _dot_nt(qg, kc).reshape(HPG, TQ, nr) + cbias[None]
        e = jnp.exp(s - jnp.max(s, -1, keepdims=True))
        p = jnp.where(cmask[None], e / jnp.sum(e, -1, keepdims=True), 0.0)
        psum = jnp.sum(p, axis=0)
        o = _dot(p.reshape(HPG * TQ, nr).astype(BF16), vc)
        for j, h in enumerate(heads):
            sl = slice(h * HEAD_DIM, (h + 1) * HEAD_DIM)
            o_ref[:, sl] = o[j * TQ:(j + 1) * TQ] * bgate[:, SM_GATE + h:SM_GATE + h + 1] * gate[:, sl]
        imp = _dot(psum, ov_ref[...], HI)
        imp = jnp.where(causal, jnp.where(forced, NSA_FORCE, imp), NEG_INF)
        rank = jnp.zeros((TQ, ns), jnp.int32)
        for j in range(ns):
            col = imp[:, j:j + 1]
            rank = rank + ((col > imp) | ((col == imp) & (j < jj))).astype(jnp.int32)
        sel_ref[0, g] = ((rank < NSA_TOPN) & causal).astype(F32)


def _nsa_cmpattn(cb, cmp, cf, B, T, ov):
    nq = T // TQ
    nr = cmp.shape[1]
    ns = T // NSA_SEL_BLOCK
    return pl.pallas_call(
        _cmpattn_kernel,
        grid=(B, nq),
        in_specs=[pl.BlockSpec((TQ, 512), lambda b, i: (b * nq + i, BF_Q['nsa_q'])),
                  pl.BlockSpec((1, nr, 2 * KV_W), lambda b, i: (b, 0, 0)),
                  pl.BlockSpec((TQ, 512), lambda b, i: (b * nq + i, FCOL['nsa_g'])),
                  pl.BlockSpec((TQ, SMALL_W), lambda b, i: (b * nq + i, SMALL_BLK)),
                  pl.BlockSpec(ov.shape, lambda b, i: (0, 0))],
        out_specs=[pl.BlockSpec((TQ, 512), lambda b, i: (b * nq + i, 0)),
                   pl.BlockSpec((1, KV_HEADS, TQ, ns), lambda b, i: (b, 0, i, 0))],
        out_shape=[jax.ShapeDtypeStruct((B * T, D_GROUP), F32),
                   jax.ShapeDtypeStruct((B, KV_HEADS, T, ns), F32)],
        compiler_params=_params(("parallel", "arbitrary")),
        name="nsa_cmpattn",
    )(cb, cmp, cf, cf, ov)


def _out_kernel(ya_ref, yb_ref, c1_ref, c2_ref, c3_ref, yd_ref, x_ref, w_ref, lg_ref, lb_ref, o_ref, ob_ref):
    yc = c1_ref[...] + c2_ref[...] + c3_ref[...]
    acc = _dot(ya_ref[...].astype(BF16), w_ref[0:512, :])
    acc = acc + _dot(yb_ref[...].astype(BF16), w_ref[512:1024, :])
    acc = acc + _dot(yc.astype(BF16), w_ref[1024:1536, :])
    acc = acc + _dot(yd_ref[...].astype(BF16), w_ref[1536:2048, :])
    z = DEEPNORM_ALPHA * x_ref[...] + acc
    mu = jnp.mean(z, -1, keepdims=True)
    var = jnp.mean(jnp.square(z - mu), -1, keepdims=True)
    out = (z - mu) * lax.rsqrt(var + LN_EPS) * lg_ref[...] + lb_ref[...]
    o_ref[...] = out
    ob_ref[...] = out.astype(BF16)


def _out_proj(ys, x, w, lg, lb, tm=256):
    m = x.shape[0]
    yspec = pl.BlockSpec((tm, 512), lambda i: (i, 0))
    xspec = pl.BlockSpec((tm, D_MODEL), lambda i: (i, 0))
    return pl.pallas_call(
        _out_kernel,
        grid=(m // tm,),
        in_specs=[yspec] * 6 + [xspec, pl.BlockSpec(w.shape, lambda i: (0, 0)),
                                pl.BlockSpec(lg.shape, lambda i: (0, 0)), pl.BlockSpec(lb.shape, lambda i: (0, 0))],
        out_specs=[xspec, xspec],
        out_shape=[jax.ShapeDtypeStruct((m, D_MODEL), F32), jax.ShapeDtypeStruct((m, D_MODEL), BF16)],
        compiler_params=_params(("parallel",)),
        name="out_proj_ln",
    )(*ys, x, w, lg, lb)


def _cols(w, name):
    o, s = _OFF[name]
    return w[..., o:o + s]


def _small_row(pieces):
    row = jnp.zeros((1, SMALL_W), F32)
    for off, vals in pieces:
        row = row.at[0, off:off + vals.shape[-1]].set(vals.astype(F32))
    return row


def _overlap_matrix(T):
    nc = T // NSA_CMP_STRIDE
    ns = T // NSA_SEL_BLOCK
    cst = np.arange(nc) * NSA_CMP_STRIDE
    jst = np.arange(ns) * NSA_SEL_BLOCK
    ov = np.clip(np.minimum(cst[:, None] + NSA_CMP_BLOCK, jst[None, :] + NSA_SEL_BLOCK)
                 - np.maximum(cst[:, None], jst[None, :]), 0, None).astype(np.float32) / NSA_CMP_BLOCK
    return jnp.asarray(ov)


def _cmp_weights(pe_k, pe_v, k_w1, k_w2, v_w1, v_w2):
    half = NSA_CMP_STRIDE
    hid = NSA_CMP_HIDDEN

    def w1_half(lo):
        blocks = jnp.zeros((half, 4, HEAD_DIM, 4, hid), F32)
        kw = k_w1.reshape(NSA_CMP_BLOCK, HEAD_DIM, hid)[lo:lo + half]
        vw = v_w1.reshape(NSA_CMP_BLOCK, HEAD_DIM, hid)[lo:lo + half]
        for s, wsrc in ((0, kw), (1, kw), (2, vw), (3, vw)):
            blocks = blocks.at[:, s, :, s, :].set(wsrc)
        return blocks.reshape(half * 4 * HEAD_DIM, 4 * hid).astype(BF16)

    def pe_half(lo):
        pk, pv = pe_k[lo:lo + half], pe_v[lo:lo + half]
        return jnp.stack([pk, pk, pv, pv], axis=1).reshape(1, half * 4 * HEAD_DIM)

    w2 = jnp.zeros((4, hid, 4, HEAD_DIM), F32)
    for s, wsrc in ((0, k_w2), (1, k_w2), (2, v_w2), (3, v_w2)):
        w2 = w2.at[s, :, s, :].set(wsrc)
    return pe_half(0), pe_half(half), w1_half(0), w1_half(half), w2.reshape(4 * hid, 4 * HEAD_DIM).astype(BF16)


def _layer(x, xb, B, T, w_in, w_out, ln_g, ln_b, rw_mu, rw_w0, rw_w2, rw_a0, rw_a2, rw_kk, rw_ka, rw_rk,
           rw_gn_g, rw_gn_b, swa_sinks, nsa_pe_k, nsa_pe_v, nsa_k_w1, nsa_k_w2, nsa_v_w1, nsa_v_w2,
           gdn_conv, gdn_A_log, gdn_dt_bias, gdn_norm_g):
    small = jnp.concatenate([_cols(w_in, n) for n in ('rw_wd', 'rw_ad', 'nsa_gate', 'gdn_beta', 'gdn_a')], axis=1)
    small = jnp.pad(small, ((0, 0), (0, SMALL_W - small.shape[1])))
    w_f32 = jnp.concatenate([_cols(w_in, n) for n in F32_GROUPS] + [small], axis=1).astype(BF16)
    w_bf = jnp.concatenate([_cols(w_in, n) for n in ('swa_q', 'nsa_q', 'swa_k', 'swa_v', 'nsa_ks', 'nsa_vs',
                                                     'nsa_kw', 'nsa_vw')], axis=1).astype(BF16)
    w_cmp = jnp.concatenate([_cols(w_in, 'nsa_kc'), _cols(w_in, 'nsa_vc')], axis=1).astype(BF16)
    tm = min(1024, B * T)
    cf = _matmul(xb, w_f32, F32, tm, 256)
    cb = _matmul(xb, w_bf, BF16, tm, 256)
    ccmp = _matmul(xb, w_cmp, F32, tm, 256)

    row = lambda a: a.reshape(1, -1).astype(F32)
    mu3 = row(rw_mu[:3 * D_GROUP])
    musm = _small_row([(SM_WD, rw_mu[3 * D_GROUP:3 * D_GROUP + RW_LORA]),
                       (SM_AD, rw_mu[3 * D_GROUP + RW_LORA:])])
    y_a = _rwkv(cf, B, T, mu3, musm, row(rw_w0), rw_w2, row(rw_a0), rw_a2, row(rw_kk), row(rw_ka),
                row(rw_rk), row(rw_gn_g), row(rw_gn_b))
    y_d = _gdn(cf, B, T, gdn_conv, _small_row([(SM_A, gdn_A_log)]), _small_row([(SM_A, gdn_dt_bias)]),
               row(gdn_norm_g))
    y_b = _attn(cb, cf, B, T, 'swa_q', 'swa_k', 'swa_v', 'swa_g', window=SWA_WINDOW, sinks=row(swa_sinks))
    cmp = _nsa_compress(ccmp, B, T, *_cmp_weights(nsa_pe_k, nsa_pe_v, nsa_k_w1, nsa_k_w2, nsa_v_w1, nsa_v_w2))
    c1, sel = _nsa_cmpattn(cb, cmp, cf, B, T, _overlap_matrix(T))
    c2 = _attn(cb, cf, B, T, 'nsa_q', 'nsa_ks', 'nsa_vs', 'nsa_g', sel=sel, gate_col=SM_GATE + N_HEADS, tk=2 * TQ)
    c3 = _attn(cb, cf, B, T, 'nsa_q', 'nsa_kw', 'nsa_vw', 'nsa_g', window=NSA_WINDOW,
               gate_col=SM_GATE + 2 * N_HEADS)
    return _out_proj((y_a, y_b, c1, c2, c3, y_d), x, w_out.astype(BF16), row(ln_g), row(ln_b))


def kernel(x, w_in, w_out, ln_g, ln_b, rw_mu, rw_w0, rw_w2, rw_a0, rw_a2, rw_kk, rw_ka, rw_rk, rw_gn_g, rw_gn_b, swa_sinks, nsa_pe_k, nsa_pe_v, nsa_k_w1, nsa_k_w2, nsa_v_w1, nsa_v_w2, gdn_conv, gdn_A_log, gdn_dt_bias, gdn_norm_g):
    B, T, D = x.shape
    params = (w_in, w_out, ln_g, ln_b, rw_mu, rw_w0, rw_w2, rw_a0, rw_a2, rw_kk, rw_ka, rw_rk, rw_gn_g, rw_gn_b,
              swa_sinks, nsa_pe_k, nsa_pe_v, nsa_k_w1, nsa_k_w2, nsa_v_w1, nsa_v_w2, gdn_conv, gdn_A_log,
              gdn_dt_bias, gdn_norm_g)
    xf = x.reshape(B * T, D)
    xb = xf.astype(BF16)
    for i in range(w_in.shape[0]):
        xf, xb = _layer(xf, xb, B, T, *(p[i] for p in params))
    return xf.reshape(B, T, D)
```

```python
import functools
import math

import numpy as np
import jax
import jax.numpy as jnp
from jax import lax
from jax.experimental import pallas as pl
from jax.experimental.pallas import tpu as pltpu

F32 = jnp.float32
BF16 = jnp.bfloat16
HI = lax.Precision.HIGHEST

D_MODEL = 2048
DEPTH = 4
D_GROUP = 512
HEAD_DIM = 64
N_HEADS = 8
KV_HEADS = 2
HPG = N_HEADS // KV_HEADS
KV_W = KV_HEADS * HEAD_DIM
NEG_INF = -1e30
LN_EPS = 1e-5
DEEPNORM_ALPHA = (2 * DEPTH) ** 0.25
RW_LORA = 32
RW_GN_EPS = 64e-5
SWA_WINDOW = 128
NSA_CMP_BLOCK = 32
NSA_CMP_STRIDE = 16
NSA_CMP_HIDDEN = 128
NSA_SEL_BLOCK = 64
NSA_TOPN = 16
NSA_WINDOW = 512
NSA_FORCE = 1e6
GDN_CONV = 4
GDN_EPS = 1e-6
SCALE = HEAD_DIM ** -0.5

CHUNK = 64
TQ = 128
TAIL = 8
VMEM_LIMIT = 56 * 1024 * 1024

_COLS = (
    ('rw_r', 512), ('rw_k', 512), ('rw_v', 512), ('rw_wd', 32), ('rw_ad', 32), ('rw_g', 512),
    ('swa_q', 512), ('swa_k', 128), ('swa_v', 128), ('swa_g', 512),
    ('nsa_q', 512), ('nsa_kc', 128), ('nsa_vc', 128), ('nsa_ks', 128), ('nsa_vs', 128),
    ('nsa_kw', 128), ('nsa_vw', 128), ('nsa_gate', 24), ('nsa_g', 512),
    ('gdn_q', 512), ('gdn_k', 512), ('gdn_v', 512), ('gdn_beta', 8), ('gdn_a', 8), ('gdn_g', 512),
)
_OFF = {}
_o = 0
for _n, _s in _COLS:
    _OFF[_n] = (_o, _s)
    _o += _s
N_IN = _o

F32_GROUPS = ('rw_r', 'rw_k', 'rw_v', 'rw_g', 'gdn_q', 'gdn_k', 'gdn_v', 'gdn_g', 'swa_g', 'nsa_g')
FCOL = {n: i for i, n in enumerate(F32_GROUPS)}
SMALL_W = 256
SMALL_BLK = len(F32_GROUPS) * 512 // SMALL_W
SM_WD, SM_AD, SM_GATE, SM_BETA, SM_A = 0, 32, 64, 88, 96
N_F32 = len(F32_GROUPS) * 512 + SMALL_W
BF_Q = {'swa_q': 0, 'nsa_q': 1}
BF_KV = {n: 8 + i for i, n in enumerate(('swa_k', 'swa_v', 'nsa_ks', 'nsa_vs', 'nsa_kw', 'nsa_vw'))}
N_BF = 1024 + 6 * 128


def _dot(a, b, prec=None):
    return lax.dot_general(a, b, (((1,), (0,)), ((), ())), precision=prec, preferred_element_type=F32)


def _dot_nt(a, b, prec=None):
    return lax.dot_general(a, b, (((1,), (1,)), ((), ())), precision=prec, preferred_element_type=F32)


def _dot_tn(a, b, prec=None):
    return lax.dot_general(a, b, (((0,), (0,)), ((), ())), precision=prec, preferred_element_type=F32)


def _rdot(a, b):
    return _dot(a.astype(BF16), b.astype(BF16))


def _rdot_nt(a, b):
    return _dot_nt(a.astype(BF16), b.astype(BF16))


def _rdot_tn(a, b):
    return _dot_tn(a.astype(BF16), b.astype(BF16))


def _sigmoid(x):
    return 1.0 / (1.0 + jnp.exp(-x))


def _silu(x):
    return x * _sigmoid(x)


def _softplus(x):
    return jnp.maximum(x, 0.0) + jnp.log(1.0 + jnp.exp(-jnp.abs(x)))


def _iota2(shape, dim):
    return lax.broadcasted_iota(jnp.int32, shape, dim)


def _heads(t):
    return [t[:, h * HEAD_DIM:(h + 1) * HEAD_DIM] for h in range(N_HEADS)]


def _inv_unit_lower(xs):
    n = xs[0].shape[0]
    eye = (_iota2((n, n), 0) == _iota2((n, n), 1)).astype(F32)
    ps = [eye + x for x in xs]
    for _ in range(int(math.log2(n)) - 1):
        xs = [_rdot(x, x) for x in xs]
        ps = [p + _rdot(x, p) for x, p in zip(xs, ps)]
    return ps


def _params(sem):
    return pltpu.CompilerParams(dimension_semantics=sem, vmem_limit_bytes=VMEM_LIMIT)


def _mm_kernel(x_ref, w_ref, o_ref):
    o_ref[...] = jnp.dot(x_ref[...], w_ref[...], preferred_element_type=F32).astype(o_ref.dtype)


def _matmul(x, w, out_dtype, tm, tn):
    m, k = x.shape
    n = w.shape[1]
    return pl.pallas_call(
        _mm_kernel,
        grid=(m // tm, n // tn),
        in_specs=[pl.BlockSpec((tm, k), lambda i, j: (i, 0)),
                  pl.BlockSpec((k, tn), lambda i, j: (0, j))],
        out_specs=pl.BlockSpec((tm, tn), lambda i, j: (i, j)),
        out_shape=jax.ShapeDtypeStruct((m, n), out_dtype),
        compiler_params=_params(("parallel", "arbitrary")),
        name="proj_in",
    )(x, w)


def _rwkv_kernel(r_ref, k_ref, v_ref, g_ref, sm_ref, mu_ref, musm_ref, w0_ref, w2_ref, a0_ref, a2_ref,
                 kk_ref, ka_ref, rk_ref, gng_ref, gnb_ref, o_ref, buf_ref, bufsm_ref, s_ref):
    c = pl.program_id(1)
    C = CHUNK

    @pl.when(c == 0)
    def _():
        buf_ref[0:TAIL, :] = jnp.zeros((TAIL, 3 * D_GROUP), F32)
        bufsm_ref[0:TAIL, :] = jnp.zeros((TAIL, SMALL_W), F32)
        s_ref[...] = jnp.zeros(s_ref.shape, F32)

    buf_ref[TAIL:TAIL + C, 0:512] = r_ref[...]
    buf_ref[TAIL:TAIL + C, 512:1024] = k_ref[...]
    buf_ref[TAIL:TAIL + C, 1024:1536] = v_ref[...]
    bufsm_ref[TAIL:TAIL + C, :] = sm_ref[...]
    cur = buf_ref[TAIL:TAIL + C, :]
    prev = buf_ref[TAIL - 1:TAIL - 1 + C, :]
    mixed = cur + (prev - cur) * mu_ref[...]
    cur_sm = bufsm_ref[TAIL:TAIL + C, :]
    prev_sm = bufsm_ref[TAIL - 1:TAIL - 1 + C, :]
    smix = cur_sm + (prev_sm - cur_sm) * musm_ref[...]
    buf_ref[0:TAIL, :] = buf_ref[C:C + TAIL, :]
    bufsm_ref[0:TAIL, :] = bufsm_ref[C:C + TAIL, :]

    r = mixed[:, 0:512]
    k = mixed[:, 512:1024]
    v = mixed[:, 1024:1536]
    wd = smix[:, SM_WD:SM_WD + RW_LORA]
    ad = smix[:, SM_AD:SM_AD + RW_LORA]
    wl = w0_ref[...] + _dot(jnp.tanh(wd), w2_ref[...], HI)
    logw = -jnp.exp(-_softplus(-wl) - 0.5)
    alpha = _sigmoid(a0_ref[...] + _dot(ad, a2_ref[...], HI))
    kkraw = k * kk_ref[...]
    k2 = k * (1.0 + (alpha - 1.0) * ka_ref[...])
    ri = _iota2((C, C), 0)
    ci = _iota2((C, C), 1)
    tril = ri >= ci
    strict = ri > ci
    gcum = _dot(tril.astype(F32), logw, HI)
    gate = _silu(g_ref[...])
    rk = rk_ref[...]
    gng = gng_ref[...]
    gnb = gnb_ref[...]

    H = range(N_HEADS)
    R, K2, V, AL, KKR, LW, G = (_heads(t) for t in (r, k2, v, alpha, kkraw, logw, gcum))
    KK = [t * lax.rsqrt(jnp.sum(t * t, -1, keepdims=True) + 1e-6) for t in KKR]
    GM = [t[C // 2 - 1:C // 2, :] for t in G]
    GE = [t[C - 1:C, :] for t in G]
    BV = [KK[h] * AL[h] for h in H]
    EINV = [jnp.exp(GM[h] - G[h]) for h in H]
    EEND = [jnp.exp(GE[h] - G[h]) for h in H]
    LEFT = [jnp.concatenate([-KK[h] * jnp.exp(G[h] - LW[h] - GM[h]), R[h] * jnp.exp(G[h] - GM[h])], axis=0)
            for h in H]
    AB = [_rdot_nt(LEFT[h], BV[h] * EINV[h]) for h in H]
    AK = [_rdot_nt(LEFT[h], K2[h] * EINV[h]) for h in H]
    A_RB = [jnp.where(tril, t[C:], 0.0) for t in AB]
    A_K = [jnp.concatenate([jnp.where(strict, t[:C], 0.0), jnp.where(tril, t[C:], 0.0)], axis=0) for t in AK]
    TINV = _inv_unit_lower([jnp.where(strict, t[:C], 0.0) for t in AB])
    S = [s_ref[h] for h in H]
    LS = [_rdot_nt(LEFT[h], S[h] * jnp.exp(GM[h])) for h in H]
    AV = [_rdot(A_K[h], V[h]) for h in H]
    U = [_rdot(TINV[h], LS[h][:C] + AV[h][:C]) for h in H]
    O = [LS[h][C:] + _rdot(A_RB[h], U[h]) + AV[h][C:] for h in H]
    for h in H:
        s_ref[h] = S[h] * jnp.exp(GE[h]) + _rdot_tn(U[h], BV[h] * EEND[h]) + _rdot_tn(V[h], K2[h] * EEND[h])
    for h in H:
        sl = slice(h * HEAD_DIM, (h + 1) * HEAD_DIM)
        o = O[h]
        mu = jnp.mean(o, -1, keepdims=True)
        var = jnp.mean(jnp.square(o - mu), -1, keepdims=True)
        y = (o - mu) * lax.rsqrt(var + RW_GN_EPS) * gng[:, sl] + gnb[:, sl]
        bonus = jnp.sum(R[h] * K2[h] * rk[:, sl], -1, keepdims=True) * V[h]
        o_ref[:, sl] = (y + bonus) * gate[:, sl]


def _rwkv(cf, B, T, mu3, musm, w0, w2, a0, a2, kk, ka, rk, gng, gnb):
    nch = T // CHUNK
    row = lambda j: pl.BlockSpec((CHUNK, 512), lambda b, c, j=j: (b * nch + c, j))
    full = lambda a: pl.BlockSpec(a.shape, lambda b, c: (0,) * a.ndim)
    ps = (mu3, musm, w0, w2, a0, a2, kk, ka, rk, gng, gnb)
    return pl.pallas_call(
        _rwkv_kernel,
        grid=(B, nch),
        in_specs=[row(FCOL['rw_r']), row(FCOL['rw_k']), row(FCOL['rw_v']), row(FCOL['rw_g']),
                  pl.BlockSpec((CHUNK, SMALL_W), lambda b, c: (b * nch + c, SMALL_BLK))] + [full(a) for a in ps],
        out_specs=pl.BlockSpec((CHUNK, 512), lambda b, c: (b * nch + c, 0)),
        out_shape=jax.ShapeDtypeStruct((B * T, D_GROUP), F32),
        scratch_shapes=[pltpu.VMEM((TAIL + CHUNK, 3 * D_GROUP), F32),
                        pltpu.VMEM((TAIL + CHUNK, SMALL_W), F32),
                        pltpu.VMEM((N_HEADS, HEAD_DIM, HEAD_DIM), F32)],
        compiler_params=_params(("parallel", "arbitrary")),
        name="rwkv7",
    )(cf, cf, cf, cf, cf, *ps)


def _gdn_kernel(q_ref, k_ref, v_ref, z_ref, sm_ref, conv_ref, alog_ref, dtb_ref, ng_ref, o_ref,
                buf_ref, s_ref):
    c = pl.program_id(1)
    C = CHUNK

    @pl.when(c == 0)
    def _():
        buf_ref[0:TAIL, :] = jnp.zeros((TAIL, 3 * D_GROUP), F32)
        s_ref[...] = jnp.zeros(s_ref.shape, F32)

    buf_ref[TAIL:TAIL + C, 0:512] = q_ref[...]
    buf_ref[TAIL:TAIL + C, 512:1024] = k_ref[...]
    buf_ref[TAIL:TAIL + C, 1024:1536] = v_ref[...]
    conv = conv_ref[...]
    acc = buf_ref[TAIL:TAIL + C, :] * conv[GDN_CONV - 1:GDN_CONV, :]
    for i in range(GDN_CONV - 1):
        sh = GDN_CONV - 1 - i
        acc = acc + buf_ref[TAIL - sh:TAIL - sh + C, :] * conv[i:i + 1, :]
    buf_ref[0:TAIL, :] = buf_ref[C:C + TAIL, :]
    qkv = _silu(acc)
    q, k, v = qkv[:, 0:512], qkv[:, 512:1024], qkv[:, 1024:1536]

    sm = sm_ref[...]
    lane = _iota2((C, SMALL_W), 1)
    beta_all = _sigmoid(sm)
    g_all = jnp.where((lane >= SM_A) & (lane < SM_A + N_HEADS),
                      -jnp.exp(alog_ref[...]) * _softplus(sm + dtb_ref[...]), 0.0)
    ri = _iota2((C, C), 0)
    ci = _iota2((C, C), 1)
    tril = ri >= ci
    strict = ri > ci
    gam_all = _dot(tril.astype(F32), g_all, HI)
    zg = _silu(z_ref[...])
    ng = ng_ref[...]
    pick = (_iota2((N_HEADS, SMALL_W), 1) == SM_A + _iota2((N_HEADS, SMALL_W), 0)).astype(F32)
    gam_rows = _dot_nt(pick, gam_all, HI)

    H = range(N_HEADS)
    Q = [t * lax.rsqrt(jnp.sum(t * t, -1, keepdims=True) + 1e-6) * SCALE for t in _heads(q)]
    K = [t * lax.rsqrt(jnp.sum(t * t, -1, keepdims=True) + 1e-6) for t in _heads(k)]
    V = _heads(v)
    BETA = [beta_all[:, SM_BETA + h:SM_BETA + h + 1] for h in H]
    GAM = [gam_all[:, SM_A + h:SM_A + h + 1] for h in H]
    DECAY = [jnp.exp(jnp.where(tril, GAM[h] - gam_rows[h:h + 1, :], NEG_INF)) for h in H]
    KB = [K[h] * BETA[h] for h in H]
    KKQ = [_rdot_nt(jnp.concatenate([KB[h], Q[h]], axis=0), K[h]) for h in H]
    QK = [KKQ[h][C:] * DECAY[h] for h in H]
    TM = _inv_unit_lower([-jnp.where(strict, KKQ[h][:C] * DECAY[h], 0.0) for h in H])
    UU = [_rdot(TM[h], V[h] * BETA[h]) for h in H]
    W = [_rdot(TM[h], KB[h] * jnp.exp(GAM[h])) for h in H]
    S = [s_ref[h] for h in H]
    WS = [_rdot(jnp.concatenate([W[h], Q[h] * jnp.exp(GAM[h])], axis=0), S[h]) for h in H]
    VN = [UU[h] - WS[h][:C] for h in H]
    O = [WS[h][C:] + _rdot(QK[h], VN[h]) for h in H]
    for h in H:
        g_last = GAM[h][C - 1:C, :]
        s_ref[h] = S[h] * jnp.exp(g_last) + _rdot_tn(K[h] * jnp.exp(g_last - GAM[h]), VN[h])
    for h in H:
        sl = slice(h * HEAD_DIM, (h + 1) * HEAD_DIM)
        o = O[h] * lax.rsqrt(jnp.mean(O[h] * O[h], -1, keepdims=True) + GDN_EPS) * ng
        o_ref[:, sl] = o * zg[:, sl]


def _gdn(cf, B, T, conv, alog_sm, dtb_sm, ng):
    nch = T // CHUNK
    row = lambda j: pl.BlockSpec((CHUNK, 512), lambda b, c, j=j: (b * nch + c, j))
    full = lambda a: pl.BlockSpec(a.shape, lambda b, c: (0,) * a.ndim)
    ps = (conv, alog_sm, dtb_sm, ng)
    return pl.pallas_call(
        _gdn_kernel,
        grid=(B, nch),
        in_specs=[row(FCOL['gdn_q']), row(FCOL['gdn_k']), row(FCOL['gdn_v']), row(FCOL['gdn_g']),
                  pl.BlockSpec((CHUNK, SMALL_W), lambda b, c: (b * nch + c, SMALL_BLK))] + [full(a) for a in ps],
        out_specs=pl.BlockSpec((CHUNK, 512), lambda b, c: (b * nch + c, 0)),
        out_shape=jax.ShapeDtypeStruct((B * T, D_GROUP), F32),
        scratch_shapes=[pltpu.VMEM((TAIL + CHUNK, 3 * D_GROUP), F32),
                        pltpu.VMEM((N_HEADS, HEAD_DIM, HEAD_DIM), F32)],
        compiler_params=_params(("parallel", "arbitrary")),
        name="gdn",
    )(cf, cf, cf, cf, cf, *ps)


def _attn_kernel(*refs, window, use_sel, use_sink, gate_col, tk):
    it = iter(refs)
    q_ref, k_ref, v_ref, g_ref = next(it), next(it), next(it), next(it)
    sm_ref = next(it) if gate_col is not None else None
    sel_ref = next(it) if use_sel else None
    sink_ref = next(it) if use_sink else None
    o_ref = next(it)
    vt_ref = next(it)
    selx_ref = next(it) if use_sel else None
    i = pl.program_id(1)
    W = HPG * TQ

    @pl.when(i == 0)
    def _():
        eye = (_iota2((KV_W, KV_W), 0) == _iota2((KV_W, KV_W), 1)).astype(BF16)
        for c in range(vt_ref.shape[0]):
            vt_ref[c] = _dot_nt(eye, v_ref[c * tk:(c + 1) * tk, :]).astype(BF16)

    hi = (i * TQ + TQ + tk - 1) // tk
    lo = 0 if window is None else jnp.maximum(i * TQ - window + 1, 0) // tk
    qpos = i * TQ + (_iota2((tk, W), 1) & (TQ - 1))
    krow = _iota2((tk, W), 0)
    gate = _silu(g_ref[...])
    if gate_col is not None:
        bgate = _sigmoid(sm_ref[...])

    G = range(KV_HEADS)
    QG = [jnp.concatenate([q_ref[:, h * HEAD_DIM:(h + 1) * HEAD_DIM] for h in range(g * HPG, (g + 1) * HPG)],
                          axis=0) * SCALE for g in G]
    if use_sel:
        for g in G:
            selx_ref[g] = jnp.concatenate([sel_ref[0, g]] * HPG, axis=1)
        nb = tk // NSA_SEL_BLOCK

    def body(kt, carry, masked):
        off = pl.multiple_of(kt * tk, tk)
        if masked:
            rel = qpos - (kt * tk + krow)
            ok = rel >= 0
            if window is not None:
                ok = ok & (rel < window)
            pbias = jnp.where(ok, 0.0, NEG_INF)
        out = []
        for g in G:
            m, l, acc = carry[3 * g:3 * g + 3]
            ksl = slice(g * HEAD_DIM, (g + 1) * HEAD_DIM)
            s = _dot_nt(k_ref[pl.ds(off, tk), ksl], QG[g])
            if masked:
                s = s + pbias
            if use_sel:
                s = s + jnp.concatenate(
                    [jnp.broadcast_to(selx_ref[g, pl.ds(kt * nb + j, 1), :], (NSA_SEL_BLOCK, W))
                     for j in range(nb)], axis=0)
            m_new = jnp.maximum(m, jnp.max(s, 0, keepdims=True))
            a = jnp.exp(m - m_new)
            p = jnp.exp(s - m_new)
            out += [m_new, a * l + jnp.sum(p, 0, keepdims=True),
                    a * acc + _dot(vt_ref[kt, ksl, :], p.astype(BF16))]
        return tuple(out)

    init = []
    for g in G:
        if use_sink:
            init += [jnp.concatenate([jnp.zeros((1, TQ), F32) + sink_ref[0:1, h:h + 1]
                                      for h in range(g * HPG, (g + 1) * HPG)], axis=1),
                     jnp.ones((1, W), F32)]
        else:
            init += [jnp.full((1, W), NEG_INF, F32), jnp.zeros((1, W), F32)]
        init.append(jnp.zeros((HEAD_DIM, W), F32))
    if window is None:
        res = lax.fori_loop(lo, hi - 1, functools.partial(body, masked=False), tuple(init))
        res = body(hi - 1, res, True)
    else:
        res = lax.fori_loop(lo, hi, functools.partial(body, masked=True), tuple(init))
    for g in G:
        ot = res[3 * g + 2] / res[3 * g + 1]
        for j in range(HPG):
            h = g * HPG + j
            sl = slice(h * HEAD_DIM, (h + 1) * HEAD_DIM)
            oh = ot[:, j * TQ:(j + 1) * TQ].T
            if gate_col is not None:
                oh = oh * bgate[:, gate_col + h:gate_col + h + 1]
            o_ref[:, sl] = oh * gate[:, sl]


def _attn(cb, cf, B, T, qname, kname, vname, gname, *, window=None, sel=None, sinks=None, gate_col=None, tk=TQ):
    nq = T // TQ
    in_specs = [pl.BlockSpec((TQ, 512), lambda b, i, j=BF_Q[qname]: (b * nq + i, j)),
                pl.BlockSpec((T, KV_W), lambda b, i, j=BF_KV[kname]: (b, j)),
                pl.BlockSpec((T, KV_W), lambda b, i, j=BF_KV[vname]: (b, j)),
                pl.BlockSpec((TQ, 512), lambda b, i, j=FCOL[gname]: (b * nq + i, j))]
    args = [cb, cb, cb, cf]
    if gate_col is not None:
        in_specs.append(pl.BlockSpec((TQ, SMALL_W), lambda b, i: (b * nq + i, SMALL_BLK)))
        args.append(cf)
    scratch = [pltpu.VMEM((T // tk, KV_W, tk), BF16)]
    if sel is not None:
        in_specs.append(pl.BlockSpec((1, KV_HEADS, sel.shape[2], TQ), lambda b, i: (b, 0, 0, i)))
        args.append(sel)
        scratch.append(pltpu.VMEM((KV_HEADS, sel.shape[2], HPG * TQ), F32))
    if sinks is not None:
        in_specs.append(pl.BlockSpec(sinks.shape, lambda b, i: (0, 0)))
        args.append(sinks)
    return pl.pallas_call(
        functools.partial(_attn_kernel, window=window, use_sel=sel is not None,
                          use_sink=sinks is not None, gate_col=gate_col, tk=tk),
        grid=(B, nq),
        in_specs=in_specs,
        out_specs=pl.BlockSpec((TQ, 512), lambda b, i: (b * nq + i, 0)),
        out_shape=jax.ShapeDtypeStruct((B * T, D_GROUP), F32),
        scratch_shapes=scratch,
        compiler_params=_params(("parallel", "arbitrary")),
        name="attn_" + qname + ("_sel" if sel is not None else "_w%d" % window),
    )(*args)


def _gelu_tanh(x):
    return 0.5 * x * (1.0 + jnp.tanh(math.sqrt(2.0 / math.pi) * (x + 0.044715 * x * x * x)))


def _cmp_kernel(r_ref, pe1_ref, pe2_ref, w1a_ref, w1b_ref, w2_ref, o_ref):
    rows = r_ref[0]
    p1 = _dot((rows + pe1_ref[...]).astype(BF16), w1a_ref[...])
    p2 = _dot((rows + pe2_ref[...]).astype(BF16), w1b_ref[...])
    nr = rows.shape[0]
    hid = _gelu_tanh(p1 + pltpu.roll(p2, nr - 1, 0))
    o_ref[0] = _dot(hid.astype(BF16), w2_ref[...])


def _nsa_compress(ccmp, B, T, pe1, pe2, w1a, w1b, w2e):
    nr = T // NSA_CMP_STRIDE
    wide = NSA_CMP_STRIDE * 2 * KV_W
    rows = ccmp.reshape(B, nr, wide)
    full = lambda a: pl.BlockSpec(a.shape, lambda b: (0,) * a.ndim)
    return pl.pallas_call(
        _cmp_kernel,
        grid=(B,),
        in_specs=[pl.BlockSpec((1, nr, wide), lambda b: (b, 0, 0))] + [full(a) for a in (pe1, pe2, w1a, w1b, w2e)],
        out_specs=pl.BlockSpec((1, nr, 2 * KV_W), lambda b: (b, 0, 0)),
        out_shape=jax.ShapeDtypeStruct((B, nr, 2 * KV_W), F32),
        compiler_params=_params(("arbitrary",)),
        name="nsa_compress",
    )(rows, pe1, pe2, w1a, w1b, w2e)


def _cmpattn_kernel(q_ref, cmp_ref, g_ref, sm_ref, ov_ref, o_ref, sel_ref):
    i = pl.program_id(1)
    kv = cmp_ref[0]
    nr = kv.shape[0]
    ns = ov_ref.shape[0]
    tpos = i * TQ + _iota2((TQ, nr), 0)
    cmask = _iota2((TQ, nr), 1) * NSA_CMP_STRIDE + (NSA_CMP_BLOCK - 1) <= tpos
    gate = _silu(g_ref[...])
    bgate = _sigmoid(sm_ref[...])
    tblk = (i * TQ + _iota2((ns, TQ), 1)) // NSA_SEL_BLOCK
    jj = _iota2((ns, TQ), 0)
    forced = (jj == 0) | (jj == tblk) | (jj == tblk - 1)
    causal = jj <= tblk
    cbias = jnp.where(cmask, 0.0, NEG_INF)
    for g in range(KV_HEADS):
        kc = kv[:, g * HEAD_DIM:(g + 1) * HEAD_DIM].astype(BF16)
        vc = kv[:, KV_W + g * HEAD_DIM:KV_W + (g + 1) * HEAD_DIM].astype(BF16)
        heads = range(g * HPG, (g + 1) * HPG)
        qg = jnp.concatenate([q_ref[:, h * HEAD_DIM:(h + 1) * HEAD_DIM] for h in heads], axis=0) * SCALE
        s = _dot_nt(qg, kc).reshape(HPG, TQ, nr) + cbias[None]
        e = jnp.exp(s - jnp.max(s, -1, keepdims=True))
        p = jnp.where(cmask[None], e / jnp.sum(e, -1, keepdims=True), 0.0)
        psum = jnp.sum(p, axis=0)
        o = _dot(p.reshape(HPG * TQ, nr).astype(BF16), vc)
        for j, h in enumerate(heads):
            sl = slice(h * HEAD_DIM, (h + 1) * HEAD_DIM)
            o_ref[:, sl] = o[j * TQ:(j + 1) * TQ] * bgate[:, SM_GATE + h:SM_GATE + h + 1] * gate[:, sl]
        imp = _dot_nt(ov_ref[...], psum, HI)
        imp = jnp.where(causal, jnp.where(forced, NSA_FORCE, imp), NEG_INF)
        rank = jnp.zeros((ns, TQ), jnp.int32)
        for j in range(ns):
            row = imp[j:j + 1, :]
            rank = rank + ((row > imp) | ((row == imp) & (j < jj))).astype(jnp.int32)
        sel_ref[0, g] = jnp.where((rank < NSA_TOPN) & causal, 0.0, NEG_INF)


def _nsa_cmpattn(cb, cmp, cf, B, T, ov):
    nq = T // TQ
    nr = cmp.shape[1]
    ns = T // NSA_SEL_BLOCK
    return pl.pallas_call(
        _cmpattn_kernel,
        grid=(B, nq),
        in_specs=[pl.BlockSpec((TQ, 512), lambda b, i: (b * nq + i, BF_Q['nsa_q'])),
                  pl.BlockSpec((1, nr, 2 * KV_W), lambda b, i: (b, 0, 0)),
                  pl.BlockSpec((TQ, 512), lambda b, i: (b * nq + i, FCOL['nsa_g'])),
                  pl.BlockSpec((TQ, SMALL_W), lambda b, i: (b * nq + i, SMALL_BLK)),
                  pl.BlockSpec(ov.shape, lambda b, i: (0, 0))],
        out_specs=[pl.BlockSpec((TQ, 512), lambda b, i: (b * nq + i, 0)),
                   pl.BlockSpec((1, KV_HEADS, ns, TQ), lambda b, i: (b, 0, 0, i))],
        out_shape=[jax.ShapeDtypeStruct((B * T, D_GROUP), F32),
                   jax.ShapeDtypeStruct((B, KV_HEADS, ns, T), F32)],
        compiler_params=_params(("parallel", "arbitrary")),
        name="nsa_cmpattn",
    )(cb, cmp, cf, cf, ov)


def _out_kernel(ya_ref, yb_ref, c1_ref, c2_ref, c3_ref, yd_ref, x_ref, w_ref, lg_ref, lb_ref, o_ref, ob_ref):
    yc = c1_ref[...] + c2_ref[...] + c3_ref[...]
    acc = _dot(ya_ref[...].astype(BF16), w_ref[0:512, :])
    acc = acc + _dot(yb_ref[...].astype(BF16), w_ref[512:1024, :])
    acc = acc + _dot(yc.astype(BF16), w_ref[1024:1536, :])
    acc = acc + _dot(yd_ref[...].astype(BF16), w_ref[1536:2048, :])
    z = DEEPNORM_ALPHA * x_ref[...] + acc
    mu = jnp.mean(z, -1, keepdims=True)
    var = jnp.mean(jnp.square(z - mu), -1, keepdims=True)
    out = (z - mu) * lax.rsqrt(var + LN_EPS) * lg_ref[...] + lb_ref[...]
    o_ref[...] = out
    ob_ref[...] = out.astype(BF16)


def _out_proj(ys, x, w, lg, lb, tm=256):
    m = x.shape[0]
    yspec = pl.BlockSpec((tm, 512), lambda i: (i, 0))
    xspec = pl.BlockSpec((tm, D_MODEL), lambda i: (i, 0))
    return pl.pallas_call(
        _out_kernel,
        grid=(m // tm,),
        in_specs=[yspec] * 6 + [xspec, pl.BlockSpec(w.shape, lambda i: (0, 0)),
                                pl.BlockSpec(lg.shape, lambda i: (0, 0)), pl.BlockSpec(lb.shape, lambda i: (0, 0))],
        out_specs=[xspec, xspec],
        out_shape=[jax.ShapeDtypeStruct((m, D_MODEL), F32), jax.ShapeDtypeStruct((m, D_MODEL), BF16)],
        compiler_params=_params(("parallel",)),
        name="out_proj_ln",
    )(*ys, x, w, lg, lb)


def _cols(w, name):
    o, s = _OFF[name]
    return w[..., o:o + s]


def _small_row(pieces):
    row = jnp.zeros((1, SMALL_W), F32)
    for off, vals in pieces:
        row = row.at[0, off:off + vals.shape[-1]].set(vals.astype(F32))
    return row


def _overlap_matrix(T):
    nc = T // NSA_CMP_STRIDE
    ns = T // NSA_SEL_BLOCK
    cst = np.arange(nc) * NSA_CMP_STRIDE
    jst = np.arange(ns) * NSA_SEL_BLOCK
    ov = np.clip(np.minimum(cst[:, None] + NSA_CMP_BLOCK, jst[None, :] + NSA_SEL_BLOCK)
                 - np.maximum(cst[:, None], jst[None, :]), 0, None).astype(np.float32) / NSA_CMP_BLOCK
    return jnp.asarray(ov.T)


def _cmp_weights(pe_k, pe_v, k_w1, k_w2, v_w1, v_w2):
    half = NSA_CMP_STRIDE
    hid = NSA_CMP_HIDDEN

    def w1_half(lo):
        blocks = jnp.zeros((half, 4, HEAD_DIM, 4, hid), F32)
        kw = k_w1.reshape(NSA_CMP_BLOCK, HEAD_DIM, hid)[lo:lo + half]
        vw = v_w1.reshape(NSA_CMP_BLOCK, HEAD_DIM, hid)[lo:lo + half]
        for s, wsrc in ((0, kw), (1, kw), (2, vw), (3, vw)):
            blocks = blocks.at[:, s, :, s, :].set(wsrc)
        return blocks.reshape(half * 4 * HEAD_DIM, 4 * hid).astype(BF16)

    def pe_half(lo):
        pk, pv = pe_k[lo:lo + half], pe_v[lo:lo + half]
        return jnp.stack([pk, pk, pv, pv], axis=1).reshape(1, half * 4 * HEAD_DIM)

    w2 = jnp.zeros((4, hid, 4, HEAD_DIM), F32)
    for s, wsrc in ((0, k_w2), (1, k_w2), (2, v_w2), (3, v_w2)):
        w2 = w2.at[s, :, s, :].set(wsrc)
    return pe_half(0), pe_half(half), w1_half(0), w1_half(half), w2.reshape(4 * hid, 4 * HEAD_DIM).astype(BF16)


def _layer(x, xb, B, T, w_in, w_out, ln_g, ln_b, rw_mu, rw_w0, rw_w2, rw_a0, rw_a2, rw_kk, rw_ka, rw_rk,
           rw_gn_g, rw_gn_b, swa_sinks, nsa_pe_k, nsa_pe_v, nsa_k_w1, nsa_k_w2, nsa_v_w1, nsa_v_w2,
           gdn_conv, gdn_A_log, gdn_dt_bias, gdn_norm_g):
    small = jnp.concatenate([_cols(w_in, n) for n in ('rw_wd', 'rw_ad', 'nsa_gate', 'gdn_beta', 'gdn_a')], axis=1)
    small = jnp.pad(small, ((0, 0), (0, SMALL_W - small.shape[1])))
    w_f32 = jnp.concatenate([_cols(w_in, n) for n in F32_GROUPS] + [small], axis=1).astype(BF16)
    w_bf = jnp.concatenate([_cols(w_in, n) for n in ('swa_q', 'nsa_q', 'swa_k', 'swa_v', 'nsa_ks', 'nsa_vs',
                                                     'nsa_kw', 'nsa_vw')], axis=1).astype(BF16)
    w_cmp = jnp.concatenate([_cols(w_in, 'nsa_kc'), _cols(w_in, 'nsa_vc')], axis=1).astype(BF16)
    tm = min(1024, B * T)
    cf = _matmul(xb, w_f32, F32, tm, 256)
    cb = _matmul(xb, w_bf, BF16, tm, 256)
    ccmp = _matmul(xb, w_cmp, F32, tm, 256)

    row = lambda a: a.reshape(1, -1).astype(F32)
    mu3 = row(rw_mu[:3 * D_GROUP])
    musm = _small_row([(SM_WD, rw_mu[3 * D_GROUP:3 * D_GROUP + RW_LORA]),
                       (SM_AD, rw_mu[3 * D_GROUP + RW_LORA:])])
    y_a = _rwkv(cf, B, T, mu3, musm, row(rw_w0), rw_w2, row(rw_a0), rw_a2, row(rw_kk), row(rw_ka),
                row(rw_rk), row(rw_gn_g), row(rw_gn_b))
    y_d = _gdn(cf, B, T, gdn_conv, _small_row([(SM_A, gdn_A_log)]), _small_row([(SM_A, gdn_dt_bias)]),
               row(gdn_norm_g))
    y_b = _attn(cb, cf, B, T, 'swa_q', 'swa_k', 'swa_v', 'swa_g', window=SWA_WINDOW, sinks=row(swa_sinks))
    cmp = _nsa_compress(ccmp, B, T, *_cmp_weights(nsa_pe_k, nsa_pe_v, nsa_k_w1, nsa_k_w2, nsa_v_w1, nsa_v_w2))
    c1, sel = _nsa_cmpattn(cb, cmp, cf, B, T, _overlap_matrix(T))
    c2 = _attn(cb, cf, B, T, 'nsa_q', 'nsa_ks', 'nsa_vs', 'nsa_g', sel=sel, gate_col=SM_GATE + N_HEADS, tk=2 * TQ)
    c3 = _attn(cb, cf, B, T, 'nsa_q', 'nsa_kw', 'nsa_vw', 'nsa_g', window=NSA_WINDOW,
               gate_col=SM_GATE + 2 * N_HEADS)
    return _out_proj((y_a, y_b, c1, c2, c3, y_d), x, w_out.astype(BF16), row(ln_g), row(ln_b))


def kernel(x, w_in, w_out, ln_g, ln_b, rw_mu, rw_w0, rw_w2, rw_a0, rw_a2, rw_kk, rw_ka, rw_rk, rw_gn_g, rw_gn_b, swa_sinks, nsa_pe_k, nsa_pe_v, nsa_k_w1, nsa_k_w2, nsa_v_w1, nsa_v_w2, gdn_conv, gdn_A_log, gdn_dt_bias, gdn_norm_g):
    B, T, D = x.shape
    params = (w_in, w_out, ln_g, ln_b, rw_mu, rw_w0, rw_w2, rw_a0, rw_a2, rw_kk, rw_ka, rw_rk, rw_gn_g, rw_gn_b,
              swa_sinks, nsa_pe_k, nsa_pe_v, nsa_k_w1, nsa_k_w2, nsa_v_w1, nsa_v_w2, gdn_conv, gdn_A_log,
              gdn_dt_bias, gdn_norm_g)
    xf = x.reshape(B * T, D)
    xb = xf.astype(BF16)
    for i in range(w_in.shape[0]):
        xf, xb = _layer(xf, xb, B, T, *(p[i] for p in params))
    return xf.reshape(B, T, D)
```

```python
import functools
import math

import numpy as np
import jax
import jax.numpy as jnp
from jax import lax
from jax.experimental import pallas as pl
from jax.experimental.pallas import tpu as pltpu

F32 = jnp.float32
BF16 = jnp.bfloat16
HI = lax.Precision.HIGHEST

D_MODEL = 2048
DEPTH = 4
D_GROUP = 512
HEAD_DIM = 64
N_HEADS = 8
KV_HEADS = 2
HPG = N_HEADS // KV_HEADS
KV_W = KV_HEADS * HEAD_DIM
NEG_INF = -1e30
LN_EPS = 1e-5
DEEPNORM_ALPHA = (2 * DEPTH) ** 0.25
RW_LORA = 32
RW_GN_EPS = 64e-5
SWA_WINDOW = 128
NSA_CMP_BLOCK = 32
NSA_CMP_STRIDE = 16
NSA_CMP_HIDDEN = 128
NSA_SEL_BLOCK = 64
NSA_TOPN = 16
NSA_WINDOW = 512
NSA_FORCE = 1e6
GDN_CONV = 4
GDN_EPS = 1e-6
SCALE = HEAD_DIM ** -0.5

CHUNK = 64
RW_SEQS = 4
GDN_SEQS = 2
TQ = 128
TAIL = 8
VMEM_LIMIT = 56 * 1024 * 1024

_COLS = (
    ('rw_r', 512), ('rw_k', 512), ('rw_v', 512), ('rw_wd', 32), ('rw_ad', 32), ('rw_g', 512),
    ('swa_q', 512), ('swa_k', 128), ('swa_v', 128), ('swa_g', 512),
    ('nsa_q', 512), ('nsa_kc', 128), ('nsa_vc', 128), ('nsa_ks', 128), ('nsa_vs', 128),
    ('nsa_kw', 128), ('nsa_vw', 128), ('nsa_gate', 24), ('nsa_g', 512),
    ('gdn_q', 512), ('gdn_k', 512), ('gdn_v', 512), ('gdn_beta', 8), ('gdn_a', 8), ('gdn_g', 512),
)
_OFF = {}
_o = 0
for _n, _s in _COLS:
    _OFF[_n] = (_o, _s)
    _o += _s
N_IN = _o

F32_GROUPS = ('rw_r', 'rw_k', 'rw_v', 'rw_g', 'gdn_q', 'gdn_k', 'gdn_v', 'gdn_g', 'swa_g', 'nsa_g')
FCOL = {n: i for i, n in enumerate(F32_GROUPS)}
SMALL_W = 256
SMALL_BLK = len(F32_GROUPS) * 512 // SMALL_W
SM_WD, SM_AD, SM_GATE, SM_BETA, SM_A = 0, 32, 64, 88, 96
N_F32 = len(F32_GROUPS) * 512 + SMALL_W
BF_Q = {'swa_q': 0, 'nsa_q': 1}
BF_KV = {n: 8 + i for i, n in enumerate(('swa_k', 'swa_v', 'nsa_ks', 'nsa_vs', 'nsa_kw', 'nsa_vw'))}
N_BF = 1024 + 6 * 128


def _dot(a, b, prec=None):
    return lax.dot_general(a, b, (((1,), (0,)), ((), ())), precision=prec, preferred_element_type=F32)


def _dot_nt(a, b, prec=None):
    return lax.dot_general(a, b, (((1,), (1,)), ((), ())), precision=prec, preferred_element_type=F32)


def _dot_tn(a, b, prec=None):
    return lax.dot_general(a, b, (((0,), (0,)), ((), ())), precision=prec, preferred_element_type=F32)


def _rdot(a, b):
    return _dot(a.astype(BF16), b.astype(BF16))


def _rdot_nt(a, b):
    return _dot_nt(a.astype(BF16), b.astype(BF16))


def _rdot_tn(a, b):
    return _dot_tn(a.astype(BF16), b.astype(BF16))


def _sigmoid(x):
    return 1.0 / (1.0 + jnp.exp(-x))


def _silu(x):
    return x * _sigmoid(x)


def _softplus(x):
    return jnp.maximum(x, 0.0) + jnp.log(1.0 + jnp.exp(-jnp.abs(x)))


def _iota2(shape, dim):
    return lax.broadcasted_iota(jnp.int32, shape, dim)


def _heads(t):
    return [t[:, h * HEAD_DIM:(h + 1) * HEAD_DIM] for h in range(N_HEADS)]


def _inv_unit_lower(xs):
    n = xs[0].shape[0]
    eye = (_iota2((n, n), 0) == _iota2((n, n), 1)).astype(F32)
    ps = [eye + x for x in xs]
    for _ in range(int(math.log2(n)) - 1):
        xs = [_rdot(x, x) for x in xs]
        ps = [p + _rdot(x, p) for x, p in zip(xs, ps)]
    return ps


def _params(sem):
    return pltpu.CompilerParams(dimension_semantics=sem, vmem_limit_bytes=VMEM_LIMIT)


def _mm_kernel(x_ref, w_ref, o_ref):
    o_ref[...] = jnp.dot(x_ref[...], w_ref[...], preferred_element_type=F32).astype(o_ref.dtype)


def _matmul(x, w, out_dtype, tm, tn):
    m, k = x.shape
    n = w.shape[1]
    return pl.pallas_call(
        _mm_kernel,
        grid=(m // tm, n // tn),
        in_specs=[pl.BlockSpec((tm, k), lambda i, j: (i, 0)),
                  pl.BlockSpec((k, tn), lambda i, j: (0, j))],
        out_specs=pl.BlockSpec((tm, tn), lambda i, j: (i, j)),
        out_shape=jax.ShapeDtypeStruct((m, n), out_dtype),
        compiler_params=_params(("parallel", "arbitrary")),
        name="proj_in",
    )(x, w)


def _rwkv_kernel(r_ref, k_ref, v_ref, g_ref, sm_ref, mu_ref, musm_ref, w0_ref, w2_ref, a0_ref, a2_ref,
                 kk_ref, ka_ref, rk_ref, gng_ref, gnb_ref, o_ref, buf_ref, bufsm_ref, s_ref):
    c = pl.program_id(1)
    C = CHUNK

    NB = r_ref.shape[0]

    @pl.when(c == 0)
    def _():
        buf_ref[:, 0:TAIL, :] = jnp.zeros((NB, TAIL, 3 * D_GROUP), F32)
        bufsm_ref[:, 0:TAIL, :] = jnp.zeros((NB, TAIL, SMALL_W), F32)
        s_ref[...] = jnp.zeros(s_ref.shape, F32)

    ri = _iota2((C, C), 0)
    ci = _iota2((C, C), 1)
    tril = ri >= ci
    strict = ri > ci
    rk = rk_ref[...]
    gng = gng_ref[...]
    gnb = gnb_ref[...]
    pre = []
    for nb in range(NB):
        buf_ref[nb, TAIL:TAIL + C, 0:512] = r_ref[nb]
        buf_ref[nb, TAIL:TAIL + C, 512:1024] = k_ref[nb]
        buf_ref[nb, TAIL:TAIL + C, 1024:1536] = v_ref[nb]
        bufsm_ref[nb, TAIL:TAIL + C, :] = sm_ref[nb]
        cur = buf_ref[nb, TAIL:TAIL + C, :]
        prev = buf_ref[nb, TAIL - 1:TAIL - 1 + C, :]
        mixed = cur + (prev - cur) * mu_ref[...]
        cur_sm = bufsm_ref[nb, TAIL:TAIL + C, :]
        prev_sm = bufsm_ref[nb, TAIL - 1:TAIL - 1 + C, :]
        smix = cur_sm + (prev_sm - cur_sm) * musm_ref[...]
        buf_ref[nb, 0:TAIL, :] = buf_ref[nb, C:C + TAIL, :]
        bufsm_ref[nb, 0:TAIL, :] = bufsm_ref[nb, C:C + TAIL, :]

        r = mixed[:, 0:512]
        k = mixed[:, 512:1024]
        v = mixed[:, 1024:1536]
        wd = smix[:, SM_WD:SM_WD + RW_LORA]
        ad = smix[:, SM_AD:SM_AD + RW_LORA]
        wl = w0_ref[...] + _dot(jnp.tanh(wd), w2_ref[...], HI)
        logw = -jnp.exp(-_softplus(-wl) - 0.5)
        alpha = _sigmoid(a0_ref[...] + _dot(ad, a2_ref[...], HI))
        kkraw = k * kk_ref[...]
        k2 = k * (1.0 + (alpha - 1.0) * ka_ref[...])
        gcum = _dot(tril.astype(F32), logw, HI)
        pre.append((r, k2, v, alpha, kkraw, logw, gcum, _silu(g_ref[nb])))

    H = range(NB * N_HEADS)
    R, K2, V, AL, KKR, LW, G = ([t for p in pre for t in _heads(p[f])] for f in range(7))
    KK = [t * lax.rsqrt(jnp.sum(t * t, -1, keepdims=True) + 1e-6) for t in KKR]
    GM = [t[C // 2 - 1:C // 2, :] for t in G]
    GE = [t[C - 1:C, :] for t in G]
    BV = [KK[h] * AL[h] for h in H]
    EINV = [jnp.exp(GM[h] - G[h]) for h in H]
    EEND = [jnp.exp(GE[h] - G[h]) for h in H]
    LEFT = [jnp.concatenate([-KK[h] * jnp.exp(G[h] - LW[h] - GM[h]), R[h] * jnp.exp(G[h] - GM[h])], axis=0)
            for h in H]
    AB = [_rdot_nt(LEFT[h], BV[h] * EINV[h]) for h in H]
    AK = [_rdot_nt(LEFT[h], K2[h] * EINV[h]) for h in H]
    A_RB = [jnp.where(tril, t[C:], 0.0) for t in AB]
    A_K = [jnp.concatenate([jnp.where(strict, t[:C], 0.0), jnp.where(tril, t[C:], 0.0)], axis=0) for t in AK]
    TINV = _inv_unit_lower([jnp.where(strict, t[:C], 0.0) for t in AB])
    S = [s_ref[h] for h in H]
    LS = [_rdot_nt(LEFT[h], S[h] * jnp.exp(GM[h])) for h in H]
    AV = [_rdot(A_K[h], V[h]) for h in H]
    U = [_rdot(TINV[h], LS[h][:C] + AV[h][:C]) for h in H]
    O = [LS[h][C:] + _rdot(A_RB[h], U[h]) + AV[h][C:] for h in H]
    for h in H:
        s_ref[h] = S[h] * jnp.exp(GE[h]) + _rdot_tn(U[h], BV[h] * EEND[h]) + _rdot_tn(V[h], K2[h] * EEND[h])
    for n in H:
        nb, h = divmod(n, N_HEADS)
        sl = slice(h * HEAD_DIM, (h + 1) * HEAD_DIM)
        o = O[n]
        mu = jnp.mean(o, -1, keepdims=True)
        var = jnp.mean(jnp.square(o - mu), -1, keepdims=True)
        y = (o - mu) * lax.rsqrt(var + RW_GN_EPS) * gng[:, sl] + gnb[:, sl]
        bonus = jnp.sum(R[n] * K2[n] * rk[:, sl], -1, keepdims=True) * V[n]
        o_ref[nb, :, sl] = (y + bonus) * pre[nb][7][:, sl]


def _seq_specs(names, nbs):
    specs = [pl.BlockSpec((nbs, CHUNK, 512), lambda b, c, j=FCOL[n]: (b, c, j)) for n in names]
    return specs + [pl.BlockSpec((nbs, CHUNK, SMALL_W), lambda b, c: (b, c, SMALL_BLK))]


def _rwkv(cf, B, T, mu3, musm, w0, w2, a0, a2, kk, ka, rk, gng, gnb):
    nbs = RW_SEQS if B % RW_SEQS == 0 else 1
    cf3 = cf.reshape(B, T, cf.shape[-1])
    full = lambda a: pl.BlockSpec(a.shape, lambda b, c: (0,) * a.ndim)
    ps = (mu3, musm, w0, w2, a0, a2, kk, ka, rk, gng, gnb)
    out = pl.pallas_call(
        _rwkv_kernel,
        grid=(B // nbs, T // CHUNK),
        in_specs=_seq_specs(('rw_r', 'rw_k', 'rw_v', 'rw_g'), nbs) + [full(a) for a in ps],
        out_specs=pl.BlockSpec((nbs, CHUNK, 512), lambda b, c: (b, c, 0)),
        out_shape=jax.ShapeDtypeStruct((B, T, D_GROUP), F32),
        scratch_shapes=[pltpu.VMEM((nbs, TAIL + CHUNK, 3 * D_GROUP), F32),
                        pltpu.VMEM((nbs, TAIL + CHUNK, SMALL_W), F32),
                        pltpu.VMEM((nbs * N_HEADS, HEAD_DIM, HEAD_DIM), F32)],
        compiler_params=_params(("parallel", "arbitrary")),
        name="rwkv7",
    )(cf3, cf3, cf3, cf3, cf3, *ps)
    return out.reshape(B * T, D_GROUP)


def _gdn_kernel(q_ref, k_ref, v_ref, z_ref, sm_ref, conv_ref, alog_ref, dtb_ref, ng_ref, o_ref,
                buf_ref, s_ref):
    c = pl.program_id(1)
    C = CHUNK

    NB = q_ref.shape[0]

    @pl.when(c == 0)
    def _():
        buf_ref[:, 0:TAIL, :] = jnp.zeros((NB, TAIL, 3 * D_GROUP), F32)
        s_ref[...] = jnp.zeros(s_ref.shape, F32)

    conv = conv_ref[...]
    lane = _iota2((C, SMALL_W), 1)
    ri = _iota2((C, C), 0)
    ci = _iota2((C, C), 1)
    tril = ri >= ci
    strict = ri > ci
    ng = ng_ref[...]
    pick = (_iota2((N_HEADS, SMALL_W), 1) == SM_A + _iota2((N_HEADS, SMALL_W), 0)).astype(F32)
    pre = []
    for nb in range(NB):
        buf_ref[nb, TAIL:TAIL + C, 0:512] = q_ref[nb]
        buf_ref[nb, TAIL:TAIL + C, 512:1024] = k_ref[nb]
        buf_ref[nb, TAIL:TAIL + C, 1024:1536] = v_ref[nb]
        acc = buf_ref[nb, TAIL:TAIL + C, :] * conv[GDN_CONV - 1:GDN_CONV, :]
        for i in range(GDN_CONV - 1):
            sh = GDN_CONV - 1 - i
            acc = acc + buf_ref[nb, TAIL - sh:TAIL - sh + C, :] * conv[i:i + 1, :]
        buf_ref[nb, 0:TAIL, :] = buf_ref[nb, C:C + TAIL, :]
        qkv = _silu(acc)
        sm = sm_ref[nb]
        g_all = jnp.where((lane >= SM_A) & (lane < SM_A + N_HEADS),
                          -jnp.exp(alog_ref[...]) * _softplus(sm + dtb_ref[...]), 0.0)
        gam_all = _dot(tril.astype(F32), g_all, HI)
        gam_rows = _dot_nt(pick, gam_all, HI)
        pre.append((qkv[:, 0:512], qkv[:, 512:1024], qkv[:, 1024:1536], _sigmoid(sm), gam_all, gam_rows,
                    _silu(z_ref[nb])))

    H = range(NB * N_HEADS)
    hd = lambda n: divmod(n, N_HEADS)
    Q = [t * lax.rsqrt(jnp.sum(t * t, -1, keepdims=True) + 1e-6) * SCALE for p in pre for t in _heads(p[0])]
    K = [t * lax.rsqrt(jnp.sum(t * t, -1, keepdims=True) + 1e-6) for p in pre for t in _heads(p[1])]
    V = [t for p in pre for t in _heads(p[2])]
    BETA = [pre[hd(n)[0]][3][:, SM_BETA + hd(n)[1]:SM_BETA + hd(n)[1] + 1] for n in H]
    GAM = [pre[hd(n)[0]][4][:, SM_A + hd(n)[1]:SM_A + hd(n)[1] + 1] for n in H]
    DECAY = [jnp.exp(jnp.where(tril, GAM[n] - pre[hd(n)[0]][5][hd(n)[1]:hd(n)[1] + 1, :], NEG_INF)) for n in H]
    KB = [K[h] * BETA[h] for h in H]
    KKQ = [_rdot_nt(jnp.concatenate([KB[h], Q[h]], axis=0), K[h]) for h in H]
    QK = [KKQ[h][C:] * DECAY[h] for h in H]
    TM = _inv_unit_lower([-jnp.where(strict, KKQ[h][:C] * DECAY[h], 0.0) for h in H])
    UU = [_rdot(TM[h], V[h] * BETA[h]) for h in H]
    W = [_rdot(TM[h], KB[h] * jnp.exp(GAM[h])) for h in H]
    S = [s_ref[h] for h in H]
    WS = [_rdot(jnp.concatenate([W[h], Q[h] * jnp.exp(GAM[h])], axis=0), S[h]) for h in H]
    VN = [UU[h] - WS[h][:C] for h in H]
    O = [WS[h][C:] + _rdot(QK[h], VN[h]) for h in H]
    for h in H:
        g_last = GAM[h][C - 1:C, :]
        s_ref[h] = S[h] * jnp.exp(g_last) + _rdot_tn(K[h] * jnp.exp(g_last - GAM[h]), VN[h])
    for n in H:
        nb, h = hd(n)
        sl = slice(h * HEAD_DIM, (h + 1) * HEAD_DIM)
        o = O[n] * lax.rsqrt(jnp.mean(O[n] * O[n], -1, keepdims=True) + GDN_EPS) * ng
        o_ref[nb, :, sl] = o * pre[nb][6][:, sl]


def _gdn(cf, B, T, conv, alog_sm, dtb_sm, ng):
    nbs = GDN_SEQS if B % GDN_SEQS == 0 else 1
    cf3 = cf.reshape(B, T, cf.shape[-1])
    full = lambda a: pl.BlockSpec(a.shape, lambda b, c: (0,) * a.ndim)
    ps = (conv, alog_sm, dtb_sm, ng)
    out = pl.pallas_call(
        _gdn_kernel,
        grid=(B // nbs, T // CHUNK),
        in_specs=_seq_specs(('gdn_q', 'gdn_k', 'gdn_v', 'gdn_g'), nbs) + [full(a) for a in ps],
        out_specs=pl.BlockSpec((nbs, CHUNK, 512), lambda b, c: (b, c, 0)),
        out_shape=jax.ShapeDtypeStruct((B, T, D_GROUP), F32),
        scratch_shapes=[pltpu.VMEM((nbs, TAIL + CHUNK, 3 * D_GROUP), F32),
                        pltpu.VMEM((nbs * N_HEADS, HEAD_DIM, HEAD_DIM), F32)],
        compiler_params=_params(("parallel", "arbitrary")),
        name="gdn",
    )(cf3, cf3, cf3, cf3, cf3, *ps)
    return out.reshape(B * T, D_GROUP)


def _store_heads(o_ref, g, ot, gate, bgate, gate_col):
    for j in range(HPG):
        h = g * HPG + j
        sl = slice(h * HEAD_DIM, (h + 1) * HEAD_DIM)
        oh = ot[:, j * TQ:(j + 1) * TQ].T
        if bgate is not None:
            oh = oh * bgate[:, gate_col + h:gate_col + h + 1]
        o_ref[:, sl] = oh * gate[:, sl]


def _attn_kernel(*refs, window, use_sel, use_sink, gate_col, tk):
    it = iter(refs)
    q_ref, k_ref, v_ref, g_ref = next(it), next(it), next(it), next(it)
    sm_ref = next(it) if gate_col is not None else None
    sel_ref = next(it) if use_sel else None
    sink_ref = next(it) if use_sink else None
    o_ref = next(it)
    vt_ref = next(it)
    selx_ref = next(it) if use_sel else None
    i = pl.program_id(1)
    W = HPG * TQ

    @pl.when(i == 0)
    def _():
        eye = (_iota2((KV_W, KV_W), 0) == _iota2((KV_W, KV_W), 1)).astype(BF16)
        for c in range(vt_ref.shape[0]):
            vt_ref[c] = _dot_nt(eye, v_ref[c * tk:(c + 1) * tk, :]).astype(BF16)

    span = tk if window is None else window + TQ
    qpos = i * TQ + (_iota2((span, W), 1) & (TQ - 1))
    krow = _iota2((span, W), 0)
    gate = _silu(g_ref[...])
    if gate_col is not None:
        bgate = _sigmoid(sm_ref[...])

    G = range(KV_HEADS)
    QG = [jnp.concatenate([q_ref[:, h * HEAD_DIM:(h + 1) * HEAD_DIM] for h in range(g * HPG, (g + 1) * HPG)],
                          axis=0) * SCALE for g in G]
    if use_sel:
        for g in G:
            selx_ref[g] = jnp.concatenate([sel_ref[0, g]] * HPG, axis=1)
        nb = tk // NSA_SEL_BLOCK

    if window is not None:
        nt = span // TQ
        t0 = jnp.maximum(i - window // TQ, 0)
        start = pl.multiple_of(t0 * TQ, TQ)
        rel = qpos - (start + krow)
        pbias = jnp.where((rel >= 0) & (rel < window), 0.0, NEG_INF)
        for g in G:
            ksl = slice(g * HEAD_DIM, (g + 1) * HEAD_DIM)
            s = _dot_nt(k_ref[pl.ds(start, span), ksl], QG[g]) + pbias
            m = jnp.max(s, 0, keepdims=True)
            if use_sink:
                sink = jnp.concatenate([jnp.zeros((1, TQ), F32) + sink_ref[0:1, h:h + 1]
                                        for h in range(g * HPG, (g + 1) * HPG)], axis=1)
                m = jnp.maximum(m, sink)
            p = jnp.exp(s - m)
            l = jnp.sum(p, 0, keepdims=True)
            if use_sink:
                l = l + jnp.exp(sink - m)
            vt = jnp.concatenate([vt_ref[t0 + j, ksl, :] for j in range(nt)], axis=1)
            _store_heads(o_ref, g, _dot(vt, p.astype(BF16)) / l, gate, bgate if gate_col is not None else None,
                         gate_col)
        return

    hi = (i * TQ + TQ + tk - 1) // tk

    def body(kt, carry, masked):
        off = pl.multiple_of(kt * tk, tk)
        if masked:
            pbias = jnp.where(qpos - (kt * tk + krow) >= 0, 0.0, NEG_INF)
        out = []
        for g in G:
            m, l, acc = carry[3 * g:3 * g + 3]
            ksl = slice(g * HEAD_DIM, (g + 1) * HEAD_DIM)
            s = _dot_nt(k_ref[pl.ds(off, tk), ksl], QG[g])
            if masked:
                s = s + pbias
            if use_sel:
                s = s + jnp.concatenate(
                    [jnp.broadcast_to(selx_ref[g, pl.ds(kt * nb + j, 1), :], (NSA_SEL_BLOCK, W))
                     for j in range(nb)], axis=0)
            m_new = jnp.maximum(m, jnp.max(s, 0, keepdims=True))
            a = jnp.exp(m - m_new)
            p = jnp.exp(s - m_new)
            out += [m_new, a * l + jnp.sum(p, 0, keepdims=True),
                    a * acc + _dot(vt_ref[kt, ksl, :], p.astype(BF16))]
        return tuple(out)

    init = []
    for g in G:
        init += [jnp.full((1, W), NEG_INF, F32), jnp.zeros((1, W), F32), jnp.zeros((HEAD_DIM, W), F32)]
    res = lax.fori_loop(0, hi - 1, functools.partial(body, masked=False), tuple(init))
    res = body(hi - 1, res, True)
    for g in G:
        _store_heads(o_ref, g, res[3 * g + 2] / res[3 * g + 1], gate, bgate if gate_col is not None else None,
                     gate_col)


def _attn(cb, cf, B, T, qname, kname, vname, gname, *, window=None, sel=None, sinks=None, gate_col=None, tk=TQ):
    nq = T // TQ
    in_specs = [pl.BlockSpec((TQ, 512), lambda b, i, j=BF_Q[qname]: (b * nq + i, j)),
                pl.BlockSpec((T, KV_W), lambda b, i, j=BF_KV[kname]: (b, j)),
                pl.BlockSpec((T, KV_W), lambda b, i, j=BF_KV[vname]: (b, j)),
                pl.BlockSpec((TQ, 512), lambda b, i, j=FCOL[gname]: (b * nq + i, j))]
    args = [cb, cb, cb, cf]
    if gate_col is not None:
        in_specs.append(pl.BlockSpec((TQ, SMALL_W), lambda b, i: (b * nq + i, SMALL_BLK)))
        args.append(cf)
    scratch = [pltpu.VMEM((T // tk, KV_W, tk), BF16)]
    if sel is not None:
        in_specs.append(pl.BlockSpec((1, KV_HEADS, sel.shape[2], TQ), lambda b, i: (b, 0, 0, i)))
        args.append(sel)
        scratch.append(pltpu.VMEM((KV_HEADS, sel.shape[2], HPG * TQ), F32))
    if sinks is not None:
        in_specs.append(pl.BlockSpec(sinks.shape, lambda b, i: (0, 0)))
        args.append(sinks)
    return pl.pallas_call(
        functools.partial(_attn_kernel, window=window, use_sel=sel is not None,
                          use_sink=sinks is not None, gate_col=gate_col, tk=tk),
        grid=(B, nq),
        in_specs=in_specs,
        out_specs=pl.BlockSpec((TQ, 512), lambda b, i: (b * nq + i, 0)),
        out_shape=jax.ShapeDtypeStruct((B * T, D_GROUP), F32),
        scratch_shapes=scratch,
        compiler_params=_params(("parallel", "arbitrary")),
        name="attn_" + qname + ("_sel" if sel is not None else "_w%d" % window),
    )(*args)


def _gelu_tanh(x):
    return 0.5 * x * (1.0 + jnp.tanh(math.sqrt(2.0 / math.pi) * (x + 0.044715 * x * x * x)))


def _cmp_kernel(r_ref, pe1_ref, pe2_ref, w1a_ref, w1b_ref, w2_ref, o_ref):
    rows = r_ref[0]
    p1 = _dot((rows + pe1_ref[...]).astype(BF16), w1a_ref[...])
    p2 = _dot((rows + pe2_ref[...]).astype(BF16), w1b_ref[...])
    nr = rows.shape[0]
    hid = _gelu_tanh(p1 + pltpu.roll(p2, nr - 1, 0))
    o_ref[0] = _dot(hid.astype(BF16), w2_ref[...])


def _nsa_compress(ccmp, B, T, pe1, pe2, w1a, w1b, w2e):
    nr = T // NSA_CMP_STRIDE
    wide = NSA_CMP_STRIDE * 2 * KV_W
    rows = ccmp.reshape(B, nr, wide)
    full = lambda a: pl.BlockSpec(a.shape, lambda b: (0,) * a.ndim)
    return pl.pallas_call(
        _cmp_kernel,
        grid=(B,),
        in_specs=[pl.BlockSpec((1, nr, wide), lambda b: (b, 0, 0))] + [full(a) for a in (pe1, pe2, w1a, w1b, w2e)],
        out_specs=pl.BlockSpec((1, nr, 2 * KV_W), lambda b: (b, 0, 0)),
        out_shape=jax.ShapeDtypeStruct((B, nr, 2 * KV_W), F32),
        compiler_params=_params(("arbitrary",)),
        name="nsa_compress",
    )(rows, pe1, pe2, w1a, w1b, w2e)


def _cmpattn_kernel(q_ref, cmp_ref, g_ref, sm_ref, ov_ref, o_ref, sel_ref):
    i = pl.program_id(1)
    kv = cmp_ref[0]
    nr = kv.shape[0]
    ns = ov_ref.shape[0]
    tpos = i * TQ + _iota2((TQ, nr), 0)
    cmask = _iota2((TQ, nr), 1) * NSA_CMP_STRIDE + (NSA_CMP_BLOCK - 1) <= tpos
    gate = _silu(g_ref[...])
    bgate = _sigmoid(sm_ref[...])
    tblk = (i * TQ + _iota2((ns, TQ), 1)) // NSA_SEL_BLOCK
    jj = _iota2((ns, TQ), 0)
    forced = (jj == 0) | (jj == tblk) | (jj == tblk - 1)
    causal = jj <= tblk
    cbias = jnp.where(cmask, 0.0, NEG_INF)
    for g in range(KV_HEADS):
        kc = kv[:, g * HEAD_DIM:(g + 1) * HEAD_DIM].astype(BF16)
        vc = kv[:, KV_W + g * HEAD_DIM:KV_W + (g + 1) * HEAD_DIM].astype(BF16)
        heads = range(g * HPG, (g + 1) * HPG)
        qg = jnp.concatenate([q_ref[:, h * HEAD_DIM:(h + 1) * HEAD_DIM] for h in heads], axis=0) * SCALE
        s = _dot_nt(qg, kc).reshape(HPG, TQ, nr) + cbias[None]
        e = jnp.exp(s - jnp.max(s, -1, keepdims=True))
        p = jnp.where(cmask[None], e / jnp.sum(e, -1, keepdims=True), 0.0)
        psum = jnp.sum(p, axis=0)
        o = _dot(p.reshape(HPG * TQ, nr).astype(BF16), vc)
        for j, h in enumerate(heads):
            sl = slice(h * HEAD_DIM, (h + 1) * HEAD_DIM)
            o_ref[:, sl] = o[j * TQ:(j + 1) * TQ] * bgate[:, SM_GATE + h:SM_GATE + h + 1] * gate[:, sl]
        imp = _dot_nt(ov_ref[...], psum, HI)
        imp = jnp.where(causal, jnp.where(forced, NSA_FORCE, imp), NEG_INF)
        rank = jnp.zeros((ns, TQ), jnp.int32)
        for j in range(ns):
            row = imp[j:j + 1, :]
            rank = rank + ((row > imp) | ((row == imp) & (j < jj))).astype(jnp.int32)
        sel_ref[0, g] = jnp.where((rank < NSA_TOPN) & causal, 0.0, NEG_INF)


def _nsa_cmpattn(cb, cmp, cf, B, T, ov):
    nq = T // TQ
    nr = cmp.shape[1]
    ns = T // NSA_SEL_BLOCK
    return pl.pallas_call(
        _cmpattn_kernel,
        grid=(B, nq),
        in_specs=[pl.BlockSpec((TQ, 512), lambda b, i: (b * nq + i, BF_Q['nsa_q'])),
                  pl.BlockSpec((1, nr, 2 * KV_W), lambda b, i: (b, 0, 0)),
                  pl.BlockSpec((TQ, 512), lambda b, i: (b * nq + i, FCOL['nsa_g'])),
                  pl.BlockSpec((TQ, SMALL_W), lambda b, i: (b * nq + i, SMALL_BLK)),
                  pl.BlockSpec(ov.shape, lambda b, i: (0, 0))],
        out_specs=[pl.BlockSpec((TQ, 512), lambda b, i: (b * nq + i, 0)),
                   pl.BlockSpec((1, KV_HEADS, ns, TQ), lambda b, i: (b, 0, 0, i))],
        out_shape=[jax.ShapeDtypeStruct((B * T, D_GROUP), F32),
                   jax.ShapeDtypeStruct((B, KV_HEADS, ns, T), F32)],
        compiler_params=_params(("parallel", "arbitrary")),
        name="nsa_cmpattn",
    )(cb, cmp, cf, cf, ov)


def _out_kernel(ya_ref, yb_ref, c1_ref, c2_ref, c3_ref, yd_ref, x_ref, w_ref, lg_ref, lb_ref, o_ref, ob_ref):
    yc = c1_ref[...] + c2_ref[...] + c3_ref[...]
    acc = _dot(ya_ref[...].astype(BF16), w_ref[0:512, :])
    acc = acc + _dot(yb_ref[...].astype(BF16), w_ref[512:1024, :])
    acc = acc + _dot(yc.astype(BF16), w_ref[1024:1536, :])
    acc = acc + _dot(yd_ref[...].astype(BF16), w_ref[1536:2048, :])
    z = DEEPNORM_ALPHA * x_ref[...] + acc
    mu = jnp.mean(z, -1, keepdims=True)
    var = jnp.mean(jnp.square(z - mu), -1, keepdims=True)
    out = (z - mu) * lax.rsqrt(var + LN_EPS) * lg_ref[...] + lb_ref[...]
    o_ref[...] = out
    ob_ref[...] = out.astype(BF16)


def _out_proj(ys, x, w, lg, lb, tm=256):
    m = x.shape[0]
    yspec = pl.BlockSpec((tm, 512), lambda i: (i, 0))
    xspec = pl.BlockSpec((tm, D_MODEL), lambda i: (i, 0))
    return pl.pallas_call(
        _out_kernel,
        grid=(m // tm,),
        in_specs=[yspec] * 6 + [xspec, pl.BlockSpec(w.shape, lambda i: (0, 0)),
                                pl.BlockSpec(lg.shape, lambda i: (0, 0)), pl.BlockSpec(lb.shape, lambda i: (0, 0))],
        out_specs=[xspec, xspec],
        out_shape=[jax.ShapeDtypeStruct((m, D_MODEL), F32), jax.ShapeDtypeStruct((m, D_MODEL), BF16)],
        compiler_params=_params(("parallel",)),
        name="out_proj_ln",
    )(*ys, x, w, lg, lb)


def _cols(w, name):
    o, s = _OFF[name]
    return w[..., o:o + s]


def _small_row(pieces):
    row = jnp.zeros((1, SMALL_W), F32)
    for off, vals in pieces:
        row = row.at[0, off:off + vals.shape[-1]].set(vals.astype(F32))
    return row


def _overlap_matrix(T):
    nc = T // NSA_CMP_STRIDE
    ns = T // NSA_SEL_BLOCK
    cst = np.arange(nc) * NSA_CMP_STRIDE
    jst = np.arange(ns) * NSA_SEL_BLOCK
    ov = np.clip(np.minimum(cst[:, None] + NSA_CMP_BLOCK, jst[None, :] + NSA_SEL_BLOCK)
                 - np.maximum(cst[:, None], jst[None, :]), 0, None).astype(np.float32) / NSA_CMP_BLOCK
    return jnp.asarray(ov.T)


def _cmp_weights(pe_k, pe_v, k_w1, k_w2, v_w1, v_w2):
    half = NSA_CMP_STRIDE
    hid = NSA_CMP_HIDDEN

    def w1_half(lo):
        blocks = jnp.zeros((half, 4, HEAD_DIM, 4, hid), F32)
        kw = k_w1.reshape(NSA_CMP_BLOCK, HEAD_DIM, hid)[lo:lo + half]
        vw = v_w1.reshape(NSA_CMP_BLOCK, HEAD_DIM, hid)[lo:lo + half]
        for s, wsrc in ((0, kw), (1, kw), (2, vw), (3, vw)):
            blocks = blocks.at[:, s, :, s, :].set(wsrc)
        return blocks.reshape(half * 4 * HEAD_DIM, 4 * hid).astype(BF16)

    def pe_half(lo):
        pk, pv = pe_k[lo:lo + half], pe_v[lo:lo + half]
        return jnp.stack([pk, pk, pv, pv], axis=1).reshape(1, half * 4 * HEAD_DIM)

    w2 = jnp.zeros((4, hid, 4, HEAD_DIM), F32)
    for s, wsrc in ((0, k_w2), (1, k_w2), (2, v_w2), (3, v_w2)):
        w2 = w2.at[s, :, s, :].set(wsrc)
    return pe_half(0), pe_half(half), w1_half(0), w1_half(half), w2.reshape(4 * hid, 4 * HEAD_DIM).astype(BF16)


def _layer(x, xb, B, T, w_in, w_out, ln_g, ln_b, rw_mu, rw_w0, rw_w2, rw_a0, rw_a2, rw_kk, rw_ka, rw_rk,
           rw_gn_g, rw_gn_b, swa_sinks, nsa_pe_k, nsa_pe_v, nsa_k_w1, nsa_k_w2, nsa_v_w1, nsa_v_w2,
           gdn_conv, gdn_A_log, gdn_dt_bias, gdn_norm_g):
    small = jnp.concatenate([_cols(w_in, n) for n in ('rw_wd', 'rw_ad', 'nsa_gate', 'gdn_beta', 'gdn_a')], axis=1)
    small = jnp.pad(small, ((0, 0), (0, SMALL_W - small.shape[1])))
    w_f32 = jnp.concatenate([_cols(w_in, n) for n in F32_GROUPS] + [small], axis=1).astype(BF16)
    w_bf = jnp.concatenate([_cols(w_in, n) for n in ('swa_q', 'nsa_q', 'swa_k', 'swa_v', 'nsa_ks', 'nsa_vs',
                                                     'nsa_kw', 'nsa_vw')], axis=1).astype(BF16)
    w_cmp = jnp.concatenate([_cols(w_in, 'nsa_kc'), _cols(w_in, 'nsa_vc')], axis=1).astype(BF16)
    tm = min(1024, B * T)
    cf = _matmul(xb, w_f32, F32, tm, N_F32 // 7)
    cb = _matmul(xb, w_bf, BF16, tm, N_BF // 2)
    ccmp = _matmul(xb, w_cmp, F32, tm, 256)

    row = lambda a: a.reshape(1, -1).astype(F32)
    mu3 = row(rw_mu[:3 * D_GROUP])
    musm = _small_row([(SM_WD, rw_mu[3 * D_GROUP:3 * D_GROUP + RW_LORA]),
                       (SM_AD, rw_mu[3 * D_GROUP + RW_LORA:])])
    y_a = _rwkv(cf, B, T, mu3, musm, row(rw_w0), rw_w2, row(rw_a0), rw_a2, row(rw_kk), row(rw_ka),
                row(rw_rk), row(rw_gn_g), row(rw_gn_b))
    y_d = _gdn(cf, B, T, gdn_conv, _small_row([(SM_A, gdn_A_log)]), _small_row([(SM_A, gdn_dt_bias)]),
               row(gdn_norm_g))
    y_b = _attn(cb, cf, B, T, 'swa_q', 'swa_k', 'swa_v', 'swa_g', window=SWA_WINDOW, sinks=row(swa_sinks))
    cmp = _nsa_compress(ccmp, B, T, *_cmp_weights(nsa_pe_k, nsa_pe_v, nsa_k_w1, nsa_k_w2, nsa_v_w1, nsa_v_w2))
    c1, sel = _nsa_cmpattn(cb, cmp, cf, B, T, _overlap_matrix(T))
    c2 = _attn(cb, cf, B, T, 'nsa_q', 'nsa_ks', 'nsa_vs', 'nsa_g', sel=sel, gate_col=SM_GATE + N_HEADS, tk=4 * TQ)
    c3 = _attn(cb, cf, B, T, 'nsa_q', 'nsa_kw', 'nsa_vw', 'nsa_g', window=NSA_WINDOW,
               gate_col=SM_GATE + 2 * N_HEADS)
    return _out_proj((y_a, y_b, c1, c2, c3, y_d), x, w_out.astype(BF16), row(ln_g), row(ln_b))


def kernel(x, w_in, w_out, ln_g, ln_b, rw_mu, rw_w0, rw_w2, rw_a0, rw_a2, rw_kk, rw_ka, rw_rk, rw_gn_g, rw_gn_b, swa_sinks, nsa_pe_k, nsa_pe_v, nsa_k_w1, nsa_k_w2, nsa_v_w1, nsa_v_w2, gdn_conv, gdn_A_log, gdn_dt_bias, gdn_norm_g):
    B, T, D = x.shape
    params = (w_in, w_out, ln_g, ln_b, rw_mu, rw_w0, rw_w2, rw_a0, rw_a2, rw_kk, rw_ka, rw_rk, rw_gn_g, rw_gn_b,
              swa_sinks, nsa_pe_k, nsa_pe_v, nsa_k_w1, nsa_k_w2, nsa_v_w1, nsa_v_w2, gdn_conv, gdn_A_log,
              gdn_dt_bias, gdn_norm_g)
    xf = x.reshape(B * T, D)
    xb = xf.astype(BF16)
    for i in range(w_in.shape[0]):
        xf, xb = _layer(xf, xb, B, T, *(p[i] for p in params))
    return xf.reshape(B, T, D)
```

```python
import functools
import math

import numpy as np
import jax
import jax.numpy as jnp
from jax import lax
from jax.experimental import pallas as pl
from jax.experimental.pallas import tpu as pltpu

F32 = jnp.float32
BF16 = jnp.bfloat16
HI = lax.Precision.HIGHEST

D_MODEL = 2048
DEPTH = 4
D_GROUP = 512
HEAD_DIM = 64
N_HEADS = 8
KV_HEADS = 2
HPG = N_HEADS // KV_HEADS
PAIR = 2 * HEAD_DIM
KV_W = KV_HEADS * HEAD_DIM
NEG_INF = -1e30
LN_EPS = 1e-5
DEEPNORM_ALPHA = (2 * DEPTH) ** 0.25
RW_LORA = 32
RW_GN_EPS = 64e-5
SWA_WINDOW = 128
NSA_CMP_BLOCK = 32
NSA_CMP_STRIDE = 16
NSA_CMP_HIDDEN = 128
NSA_SEL_BLOCK = 64
NSA_TOPN = 16
NSA_WINDOW = 512
NSA_FORCE = 1e6
GDN_CONV = 4
GDN_EPS = 1e-6
SCALE = HEAD_DIM ** -0.5

CHUNK = 64
RW_SEQS = 4
GDN_SEQS = 4
TQ = 128
TAIL = 8
VT_ROWS = HEAD_DIM + 16
VMEM_LIMIT = 56 * 1024 * 1024

_COLS = (
    ('rw_r', 512), ('rw_k', 512), ('rw_v', 512), ('rw_wd', 32), ('rw_ad', 32), ('rw_g', 512),
    ('swa_q', 512), ('swa_k', 128), ('swa_v', 128), ('swa_g', 512),
    ('nsa_q', 512), ('nsa_kc', 128), ('nsa_vc', 128), ('nsa_ks', 128), ('nsa_vs', 128),
    ('nsa_kw', 128), ('nsa_vw', 128), ('nsa_gate', 24), ('nsa_g', 512),
    ('gdn_q', 512), ('gdn_k', 512), ('gdn_v', 512), ('gdn_beta', 8), ('gdn_a', 8), ('gdn_g', 512),
)
_OFF = {}
_o = 0
for _n, _s in _COLS:
    _OFF[_n] = (_o, _s)
    _o += _s
N_IN = _o

F32_GROUPS = ('rw_r', 'rw_k', 'rw_v', 'rw_g', 'gdn_q', 'gdn_k', 'gdn_v', 'gdn_g', 'swa_g', 'nsa_g')
FCOL = {n: i for i, n in enumerate(F32_GROUPS)}
SMALL_W = 256
SMALL_BLK = len(F32_GROUPS) * 512 // SMALL_W
SM_WD, SM_AD, SM_GATE, SM_BETA, SM_A = 0, 32, 64, 88, 96
N_F32 = len(F32_GROUPS) * 512 + SMALL_W
BF_Q = {'swa_q': 0, 'nsa_q': 1}
BF_KV = {n: 8 + i for i, n in enumerate(('swa_k', 'swa_v', 'nsa_ks', 'nsa_vs', 'nsa_kw', 'nsa_vw'))}
N_BF = 1024 + 6 * 128


def _dot(a, b, prec=None):
    return lax.dot_general(a, b, (((1,), (0,)), ((), ())), precision=prec, preferred_element_type=F32)


def _dot_nt(a, b, prec=None):
    return lax.dot_general(a, b, (((1,), (1,)), ((), ())), precision=prec, preferred_element_type=F32)


def _dot_tn(a, b, prec=None):
    return lax.dot_general(a, b, (((0,), (0,)), ((), ())), precision=prec, preferred_element_type=F32)


def _sigmoid(x):
    return 1.0 / (1.0 + jnp.exp(-x))


def _silu(x):
    return x * _sigmoid(x)


def _softplus(x):
    return jnp.maximum(x, 0.0) + jnp.log(1.0 + jnp.exp(-jnp.abs(x)))


def _iota2(shape, dim):
    return lax.broadcasted_iota(jnp.int32, shape, dim)


def _pairs(t):
    return [t[:, p * PAIR:(p + 1) * PAIR] for p in range(N_HEADS // 2)]


def _lo_mask(rows):
    return _iota2((rows, PAIR), 1) < HEAD_DIM


def _bd(x):
    xb = x.astype(BF16)
    lo = _lo_mask(x.shape[0]).astype(BF16)
    return jnp.concatenate([xb * lo, xb * (1 - lo)], axis=0)


def _pdot(a, b):
    return _dot(a.astype(BF16), _bd(b))


def _pdot_nt(a, b):
    return _dot_nt(a.astype(BF16), _bd(b))


def _pdot_tn(a, b):
    full = _dot_tn(a.astype(BF16), b.astype(BF16))
    return jnp.where(_lo_mask(HEAD_DIM), full[:HEAD_DIM], full[HEAD_DIM:])


def _segsum(t):
    lo = _lo_mask(t.shape[0])
    s_lo = jnp.sum(jnp.where(lo, t, 0.0), -1, keepdims=True)
    s_hi = jnp.sum(jnp.where(lo, 0.0, t), -1, keepdims=True)
    return jnp.where(lo, s_lo, s_hi)


def _inv_unit_lower(xs):
    n = xs[0].shape[0]
    eye2 = ((_iota2((n, 2 * n), 1) & (n - 1)) == _iota2((n, 2 * n), 0)).astype(F32)
    ps = [eye2 + x for x in xs]
    for _ in range(int(math.log2(n)) - 1):
        xs = [_pdot(x, x) for x in xs]
        ps = [p + _pdot(x, p) for x, p in zip(xs, ps)]
    return ps


def _tri_masks2(n):
    ri = _iota2((n, 2 * n), 0)
    ci = _iota2((n, 2 * n), 1) & (n - 1)
    return ri >= ci, ri > ci


def _params(sem):
    return pltpu.CompilerParams(dimension_semantics=sem, vmem_limit_bytes=VMEM_LIMIT)


def _mm_kernel(x_ref, w_ref, o_ref):
    o_ref[...] = jnp.dot(x_ref[...], w_ref[...], preferred_element_type=F32).astype(o_ref.dtype)


def _matmul(x, w, out_dtype, tm, tn):
    m, k = x.shape
    n = w.shape[1]
    return pl.pallas_call(
        _mm_kernel,
        grid=(m // tm, n // tn),
        in_specs=[pl.BlockSpec((tm, k), lambda i, j: (i, 0)),
                  pl.BlockSpec((k, tn), lambda i, j: (0, j))],
        out_specs=pl.BlockSpec((tm, tn), lambda i, j: (i, j)),
        out_shape=jax.ShapeDtypeStruct((m, n), out_dtype),
        compiler_params=_params(("parallel", "arbitrary")),
        name="proj_in",
    )(x, w)


def _rwkv_kernel(r_ref, k_ref, v_ref, g_ref, sm_ref, mu_ref, musm_ref, w0_ref, w2_ref, a0_ref, a2_ref,
                 kk_ref, ka_ref, rk_ref, gng_ref, gnb_ref, o_ref, buf_ref, bufsm_ref, s_ref):
    c = pl.program_id(1)
    C = CHUNK

    NB = r_ref.shape[0]

    @pl.when(c == 0)
    def _():
        buf_ref[:, 0:TAIL, :] = jnp.zeros((NB, TAIL, 3 * D_GROUP), F32)
        bufsm_ref[:, 0:TAIL, :] = jnp.zeros((NB, TAIL, SMALL_W), F32)
        s_ref[...] = jnp.zeros(s_ref.shape, F32)

    ri = _iota2((C, C), 0)
    ci = _iota2((C, C), 1)
    tril = ri >= ci
    strict = ri > ci
    rk = rk_ref[...]
    gng = gng_ref[...]
    gnb = gnb_ref[...]
    pre = []
    for nb in range(NB):
        buf_ref[nb, TAIL:TAIL + C, 0:512] = r_ref[nb]
        buf_ref[nb, TAIL:TAIL + C, 512:1024] = k_ref[nb]
        buf_ref[nb, TAIL:TAIL + C, 1024:1536] = v_ref[nb]
        bufsm_ref[nb, TAIL:TAIL + C, :] = sm_ref[nb]
        cur = buf_ref[nb, TAIL:TAIL + C, :]
        prev = buf_ref[nb, TAIL - 1:TAIL - 1 + C, :]
        mixed = cur + (prev - cur) * mu_ref[...]
        cur_sm = bufsm_ref[nb, TAIL:TAIL + C, :]
        prev_sm = bufsm_ref[nb, TAIL - 1:TAIL - 1 + C, :]
        smix = cur_sm + (prev_sm - cur_sm) * musm_ref[...]
        buf_ref[nb, 0:TAIL, :] = buf_ref[nb, C:C + TAIL, :]
        bufsm_ref[nb, 0:TAIL, :] = bufsm_ref[nb, C:C + TAIL, :]

        r = mixed[:, 0:512]
        k = mixed[:, 512:1024]
        v = mixed[:, 1024:1536]
        wd = smix[:, SM_WD:SM_WD + RW_LORA]
        ad = smix[:, SM_AD:SM_AD + RW_LORA]
        wl = w0_ref[...] + _dot(jnp.tanh(wd), w2_ref[...], HI)
        logw = -jnp.exp(-_softplus(-wl) - 0.5)
        alpha = _sigmoid(a0_ref[...] + _dot(ad, a2_ref[...], HI))
        kkraw = k * kk_ref[...]
        k2 = k * (1.0 + (alpha - 1.0) * ka_ref[...])
        gcum = _dot(tril.astype(F32), logw, HI)
        pre.append((r, k2, v, alpha, kkraw, logw, gcum, _silu(g_ref[nb])))

    H = range(NB * N_HEADS // 2)
    tril2, strict2 = _tri_masks2(C)
    R, K2, V, AL, KKR, LW, G = ([t for p in pre for t in _pairs(p[f])] for f in range(7))
    KK = [t * lax.rsqrt(_segsum(t * t) + 1e-6) for t in KKR]
    GM = [t[C // 2 - 1:C // 2, :] for t in G]
    GE = [t[C - 1:C, :] for t in G]
    BV = [KK[h] * AL[h] for h in H]
    EINV = [jnp.exp(GM[h] - G[h]) for h in H]
    EEND = [jnp.exp(GE[h] - G[h]) for h in H]
    LEFT = [jnp.concatenate([-KK[h] * jnp.exp(G[h] - LW[h] - GM[h]), R[h] * jnp.exp(G[h] - GM[h])], axis=0)
            for h in H]
    AB = [_pdot_nt(LEFT[h], BV[h] * EINV[h]) for h in H]
    AK = [_pdot_nt(LEFT[h], K2[h] * EINV[h]) for h in H]
    A_RB = [jnp.where(tril2, t[C:], 0.0) for t in AB]
    A_K = [jnp.concatenate([jnp.where(strict2, t[:C], 0.0), jnp.where(tril2, t[C:], 0.0)], axis=0) for t in AK]
    TINV = _inv_unit_lower([jnp.where(strict2, t[:C], 0.0) for t in AB])
    S = [s_ref[h] for h in H]
    LS = [_pdot_nt(LEFT[h], S[h] * jnp.exp(GM[h])) for h in H]
    AV = [_pdot(A_K[h], V[h]) for h in H]
    U = [_pdot(TINV[h], LS[h][:C] + AV[h][:C]) for h in H]
    O = [LS[h][C:] + _pdot(A_RB[h], U[h]) + AV[h][C:] for h in H]
    for h in H:
        s_ref[h] = S[h] * jnp.exp(GE[h]) + _pdot_tn(
            jnp.concatenate([U[h], V[h]], axis=0),
            jnp.concatenate([BV[h] * EEND[h], K2[h] * EEND[h]], axis=0))
    for n in H:
        nb, p = divmod(n, N_HEADS // 2)
        sl = slice(p * PAIR, (p + 1) * PAIR)
        mu = _segsum(O[n]) * (1.0 / HEAD_DIM)
        d = O[n] - mu
        var = _segsum(d * d) * (1.0 / HEAD_DIM)
        y = d * lax.rsqrt(var + RW_GN_EPS) * gng[:, sl] + gnb[:, sl]
        bonus = _segsum(R[n] * K2[n] * rk[:, sl]) * V[n]
        o_ref[nb, :, sl] = (y + bonus) * pre[nb][7][:, sl]


def _seq_specs(names, nbs):
    specs = [pl.BlockSpec((nbs, CHUNK, 512), lambda b, c, j=FCOL[n]: (b, c, j)) for n in names]
    return specs + [pl.BlockSpec((nbs, CHUNK, SMALL_W), lambda b, c: (b, c, SMALL_BLK))]


def _rwkv(cf, B, T, mu3, musm, w0, w2, a0, a2, kk, ka, rk, gng, gnb):
    nbs = RW_SEQS if B % RW_SEQS == 0 else 1
    cf3 = cf.reshape(B, T, cf.shape[-1])
    full = lambda a: pl.BlockSpec(a.shape, lambda b, c: (0,) * a.ndim)
    ps = (mu3, musm, w0, w2, a0, a2, kk, ka, rk, gng, gnb)
    out = pl.pallas_call(
        _rwkv_kernel,
        grid=(B // nbs, T // CHUNK),
        in_specs=_seq_specs(('rw_r', 'rw_k', 'rw_v', 'rw_g'), nbs) + [full(a) for a in ps],
        out_specs=pl.BlockSpec((nbs, CHUNK, 512), lambda b, c: (b, c, 0)),
        out_shape=jax.ShapeDtypeStruct((B, T, D_GROUP), F32),
        scratch_shapes=[pltpu.VMEM((nbs, TAIL + CHUNK, 3 * D_GROUP), F32),
                        pltpu.VMEM((nbs, TAIL + CHUNK, SMALL_W), F32),
                        pltpu.VMEM((nbs * N_HEADS // 2, HEAD_DIM, PAIR), F32)],
        compiler_params=_params(("parallel", "arbitrary")),
        name="rwkv7",
    )(cf3, cf3, cf3, cf3, cf3, *ps)
    return out.reshape(B * T, D_GROUP)


def _gdn_kernel(q_ref, k_ref, v_ref, z_ref, sm_ref, conv_ref, alog_ref, dtb_ref, ng_ref, o_ref,
                buf_ref, s_ref):
    c = pl.program_id(1)
    C = CHUNK

    NB = q_ref.shape[0]

    @pl.when(c == 0)
    def _():
        buf_ref[:, 0:TAIL, :] = jnp.zeros((NB, TAIL, 3 * D_GROUP), F32)
        s_ref[...] = jnp.zeros(s_ref.shape, F32)

    conv = conv_ref[...]
    lane = _iota2((C, SMALL_W), 1)
    ri = _iota2((C, C), 0)
    ci = _iota2((C, C), 1)
    tril = ri >= ci
    strict = ri > ci
    ng = ng_ref[...]
    pre = []
    for nb in range(NB):
        buf_ref[nb, TAIL:TAIL + C, 0:512] = q_ref[nb]
        buf_ref[nb, TAIL:TAIL + C, 512:1024] = k_ref[nb]
        buf_ref[nb, TAIL:TAIL + C, 1024:1536] = v_ref[nb]
        acc = buf_ref[nb, TAIL:TAIL + C, :] * conv[GDN_CONV - 1:GDN_CONV, :]
        for i in range(GDN_CONV - 1):
            sh = GDN_CONV - 1 - i
            acc = acc + buf_ref[nb, TAIL - sh:TAIL - sh + C, :] * conv[i:i + 1, :]
        buf_ref[nb, 0:TAIL, :] = buf_ref[nb, C:C + TAIL, :]
        qkv = _silu(acc)
        sm = sm_ref[nb]
        g_all = jnp.where((lane >= SM_A) & (lane < SM_A + N_HEADS),
                          -jnp.exp(alog_ref[...]) * _softplus(sm + dtb_ref[...]), 0.0)
        gam_all = _dot(tril.astype(F32), g_all, HI)
        pre.append((qkv[:, 0:512], qkv[:, 512:1024], qkv[:, 1024:1536], _sigmoid(sm), gam_all, _silu(z_ref[nb])))

    H = range(NB * N_HEADS // 2)
    hd = lambda n: divmod(n, N_HEADS // 2)
    lo = _lo_mask(C)
    tril2, strict2 = _tri_masks2(C)
    diag2 = (_iota2((C, PAIR), 1) & (HEAD_DIM - 1)) == _iota2((C, PAIR), 0)

    def per_head(n, field, col):
        nb, p = hd(n)
        t = pre[nb][field]
        return jnp.where(lo, t[:, col + 2 * p:col + 2 * p + 1], t[:, col + 2 * p + 1:col + 2 * p + 2])

    Q = [t * lax.rsqrt(_segsum(t * t) + 1e-6) * SCALE for p in pre for t in _pairs(p[0])]
    K = [t * lax.rsqrt(_segsum(t * t) + 1e-6) for p in pre for t in _pairs(p[1])]
    V = [t for p in pre for t in _pairs(p[2])]
    BETA = [per_head(n, 3, SM_BETA) for n in H]
    GAM = [per_head(n, 4, SM_A) for n in H]
    GROW = [jnp.sum(jnp.where(diag2, t, 0.0), 0, keepdims=True) for t in GAM]
    DECAY = [jnp.exp(jnp.where(tril2, GAM[n] - GROW[n], NEG_INF)) for n in H]
    EG = [jnp.exp(t) for t in GAM]
    KB = [K[h] * BETA[h] for h in H]
    KKQ = [_pdot_nt(jnp.concatenate([KB[h], Q[h]], axis=0), K[h]) for h in H]
    QK = [KKQ[h][C:] * DECAY[h] for h in H]
    TM = _inv_unit_lower([-jnp.where(strict2, KKQ[h][:C] * DECAY[h], 0.0) for h in H])
    UU = [_pdot(TM[h], V[h] * BETA[h]) for h in H]
    W = [_pdot(TM[h], KB[h] * EG[h]) for h in H]
    S = [s_ref[h] for h in H]
    WS = [_pdot(jnp.concatenate([W[h], Q[h] * EG[h]], axis=0), S[h]) for h in H]
    VN = [UU[h] - WS[h][:C] for h in H]
    O = [WS[h][C:] + _pdot(QK[h], VN[h]) for h in H]
    for h in H:
        g_last = GAM[h][C - 1:C, :]
        s_ref[h] = S[h] * jnp.exp(g_last) + _pdot_tn(K[h] * jnp.exp(g_last - GAM[h]), VN[h])
    for n in H:
        nb, p = hd(n)
        sl = slice(p * PAIR, (p + 1) * PAIR)
        o = O[n] * lax.rsqrt(_segsum(O[n] * O[n]) * (1.0 / HEAD_DIM) + GDN_EPS) * ng
        o_ref[nb, :, sl] = o * pre[nb][5][:, sl]


def _gdn(cf, B, T, conv, alog_sm, dtb_sm, ng):
    nbs = GDN_SEQS if B % GDN_SEQS == 0 else 1
    cf3 = cf.reshape(B, T, cf.shape[-1])
    full = lambda a: pl.BlockSpec(a.shape, lambda b, c: (0,) * a.ndim)
    ps = (conv, alog_sm, dtb_sm, ng)
    out = pl.pallas_call(
        _gdn_kernel,
        grid=(B // nbs, T // CHUNK),
        in_specs=_seq_specs(('gdn_q', 'gdn_k', 'gdn_v', 'gdn_g'), nbs) + [full(a) for a in ps],
        out_specs=pl.BlockSpec((nbs, CHUNK, 512), lambda b, c: (b, c, 0)),
        out_shape=jax.ShapeDtypeStruct((B, T, D_GROUP), F32),
        scratch_shapes=[pltpu.VMEM((nbs, TAIL + CHUNK, 3 * D_GROUP), F32),
                        pltpu.VMEM((nbs * N_HEADS // 2, HEAD_DIM, PAIR), F32)],
        compiler_params=_params(("parallel", "arbitrary")),
        name="gdn",
    )(cf3, cf3, cf3, cf3, cf3, *ps)
    return out.reshape(B * T, D_GROUP)


def _store_heads(o_ref, g, ot, gate, bgate, gate_col):
    for j in range(HPG):
        h = g * HPG + j
        sl = slice(h * HEAD_DIM, (h + 1) * HEAD_DIM)
        oh = ot[:, j * TQ:(j + 1) * TQ].T
        if bgate is not None:
            oh = oh * bgate[:, gate_col + h:gate_col + h + 1]
        o_ref[:, sl] = oh * gate[:, sl]


def _attn_kernel(*refs, window, use_sel, use_sink, gate_col, tk):
    it = iter(refs)
    q_ref, k_ref, v_ref, g_ref = next(it), next(it), next(it), next(it)
    sm_ref = next(it) if gate_col is not None else None
    sel_ref = next(it) if use_sel else None
    sink_ref = next(it) if use_sink else None
    o_ref = next(it)
    vt_ref = next(it)
    selx_ref = next(it) if use_sel else None
    i = pl.program_id(1)
    W = HPG * TQ

    @pl.when(i == 0)
    def _():
        eye = (_iota2((KV_W, KV_W), 0) == _iota2((KV_W, KV_W), 1)).astype(BF16)
        ones = jnp.ones((VT_ROWS - HEAD_DIM, tk), BF16)
        for c in range(vt_ref.shape[0]):
            vt = _dot_nt(eye, v_ref[c * tk:(c + 1) * tk, :]).astype(BF16)
            for g in range(KV_HEADS):
                vt_ref[c, g] = jnp.concatenate([vt[g * HEAD_DIM:(g + 1) * HEAD_DIM], ones], axis=0)

    span = tk if window is None else window + TQ
    qpos = i * TQ + (_iota2((span, W), 1) & (TQ - 1))
    krow = _iota2((span, W), 0)
    gate = _silu(g_ref[...])
    if gate_col is not None:
        bgate = _sigmoid(sm_ref[...])

    G = range(KV_HEADS)
    QG = [jnp.concatenate([q_ref[:, h * HEAD_DIM:(h + 1) * HEAD_DIM] for h in range(g * HPG, (g + 1) * HPG)],
                          axis=0) * SCALE for g in G]
    if use_sel:
        for g in G:
            selx_ref[g] = jnp.concatenate([sel_ref[0, g]] * HPG, axis=1)
        nb = tk // NSA_SEL_BLOCK

    if window is not None:
        nt = span // TQ
        t0 = jnp.maximum(i - window // TQ, 0)
        start = pl.multiple_of(t0 * TQ, TQ)
        rel = qpos - (start + krow)
        pbias = jnp.where((rel >= 0) & (rel < window), 0.0, NEG_INF)
        for g in G:
            ksl = slice(g * HEAD_DIM, (g + 1) * HEAD_DIM)
            s = _dot_nt(k_ref[pl.ds(start, span), ksl], QG[g]) + pbias
            m = jnp.max(s, 0, keepdims=True)
            if use_sink:
                sink = jnp.concatenate([jnp.zeros((1, TQ), F32) + sink_ref[0:1, h:h + 1]
                                        for h in range(g * HPG, (g + 1) * HPG)], axis=1)
                m = jnp.maximum(m, sink)
            p = jnp.exp((s - m).astype(BF16))
            vt = jnp.concatenate([vt_ref[t0 + j, g] for j in range(nt)], axis=1)
            pv = _dot(vt, p)
            l = pv[HEAD_DIM:HEAD_DIM + 1]
            if use_sink:
                l = l + jnp.exp(sink - m)
            _store_heads(o_ref, g, pv[:HEAD_DIM] / l, gate, bgate if gate_col is not None else None, gate_col)
        return

    hi = (i * TQ + TQ + tk - 1) // tk

    def body(kt, carry, masked):
        off = pl.multiple_of(kt * tk, tk)
        if masked:
            pbias = jnp.where(qpos - (kt * tk + krow) >= 0, 0.0, NEG_INF)
        out = []
        for g in G:
            m, acc = carry[2 * g:2 * g + 2]
            ksl = slice(g * HEAD_DIM, (g + 1) * HEAD_DIM)
            s = _dot_nt(k_ref[pl.ds(off, tk), ksl], QG[g])
            if masked:
                s = s + pbias
            if use_sel:
                s = s + jnp.concatenate(
                    [jnp.broadcast_to(selx_ref[g, pl.ds(kt * nb + j, 1), :], (NSA_SEL_BLOCK, W))
                     for j in range(nb)], axis=0)
            m_new = jnp.maximum(m, jnp.max(s, 0, keepdims=True))
            a = jnp.exp(m - m_new)
            p = jnp.exp((s - m_new).astype(BF16))
            out += [m_new, a * acc + _dot(vt_ref[kt, g], p)]
        return tuple(out)

    init = []
    for g in G:
        init += [jnp.full((1, W), NEG_INF, F32), jnp.zeros((VT_ROWS, W), F32)]
    res = lax.fori_loop(0, hi - 1, functools.partial(body, masked=False), tuple(init))
    res = body(hi - 1, res, True)
    for g in G:
        acc = res[2 * g + 1]
        _store_heads(o_ref, g, acc[:HEAD_DIM] / acc[HEAD_DIM:HEAD_DIM + 1], gate,
                     bgate if gate_col is not None else None, gate_col)


def _attn(cb, cf, B, T, qname, kname, vname, gname, *, window=None, sel=None, sinks=None, gate_col=None, tk=TQ):
    nq = T // TQ
    in_specs = [pl.BlockSpec((TQ, 512), lambda b, i, j=BF_Q[qname]: (b * nq + i, j)),
                pl.BlockSpec((T, KV_W), lambda b, i, j=BF_KV[kname]: (b, j)),
                pl.BlockSpec((T, KV_W), lambda b, i, j=BF_KV[vname]: (b, j)),
                pl.BlockSpec((TQ, 512), lambda b, i, j=FCOL[gname]: (b * nq + i, j))]
    args = [cb, cb, cb, cf]
    if gate_col is not None:
        in_specs.append(pl.BlockSpec((TQ, SMALL_W), lambda b, i: (b * nq + i, SMALL_BLK)))
        args.append(cf)
    scratch = [pltpu.VMEM((T // tk, KV_HEADS, VT_ROWS, tk), BF16)]
    if sel is not None:
        in_specs.append(pl.BlockSpec((1, KV_HEADS, sel.shape[2], TQ), lambda b, i: (b, 0, 0, i)))
        args.append(sel)
        scratch.append(pltpu.VMEM((KV_HEADS, sel.shape[2], HPG * TQ), F32))
    if sinks is not None:
        in_specs.append(pl.BlockSpec(sinks.shape, lambda b, i: (0, 0)))
        args.append(sinks)
    return pl.pallas_call(
        functools.partial(_attn_kernel, window=window, use_sel=sel is not None,
                          use_sink=sinks is not None, gate_col=gate_col, tk=tk),
        grid=(B, nq),
        in_specs=in_specs,
        out_specs=pl.BlockSpec((TQ, 512), lambda b, i: (b * nq + i, 0)),
        out_shape=jax.ShapeDtypeStruct((B * T, D_GROUP), F32),
        scratch_shapes=scratch,
        compiler_params=_params(("parallel", "arbitrary")),
        name="attn_" + qname + ("_sel" if sel is not None else "_w%d" % window),
    )(*args)


def _gelu_tanh(x):
    return 0.5 * x * (1.0 + jnp.tanh(math.sqrt(2.0 / math.pi) * (x + 0.044715 * x * x * x)))


def _cmp_kernel(r_ref, pe1_ref, pe2_ref, w1a_ref, w1b_ref, w2_ref, o_ref):
    rows = r_ref[0]
    p1 = _dot((rows + pe1_ref[...]).astype(BF16), w1a_ref[...])
    p2 = _dot((rows + pe2_ref[...]).astype(BF16), w1b_ref[...])
    nr = rows.shape[0]
    hid = _gelu_tanh(p1 + pltpu.roll(p2, nr - 1, 0))
    o_ref[0] = _dot(hid.astype(BF16), w2_ref[...])


def _nsa_compress(ccmp, B, T, pe1, pe2, w1a, w1b, w2e):
    nr = T // NSA_CMP_STRIDE
    wide = NSA_CMP_STRIDE * 2 * KV_W
    rows = ccmp.reshape(B, nr, wide)
    full = lambda a: pl.BlockSpec(a.shape, lambda b: (0,) * a.ndim)
    return pl.pallas_call(
        _cmp_kernel,
        grid=(B,),
        in_specs=[pl.BlockSpec((1, nr, wide), lambda b: (b, 0, 0))] + [full(a) for a in (pe1, pe2, w1a, w1b, w2e)],
        out_specs=pl.BlockSpec((1, nr, 2 * KV_W), lambda b: (b, 0, 0)),
        out_shape=jax.ShapeDtypeStruct((B, nr, 2 * KV_W), F32),
        compiler_params=_params(("arbitrary",)),
        name="nsa_compress",
    )(rows, pe1, pe2, w1a, w1b, w2e)


def _cmpattn_kernel(q_ref, cmp_ref, g_ref, sm_ref, ov_ref, o_ref, sel_ref):
    i = pl.program_id(1)
    kv = cmp_ref[0]
    nr = kv.shape[0]
    ns = ov_ref.shape[0]
    tpos = i * TQ + _iota2((TQ, nr), 0)
    cmask = _iota2((TQ, nr), 1) * NSA_CMP_STRIDE + (NSA_CMP_BLOCK - 1) <= tpos
    gate = _silu(g_ref[...])
    bgate = _sigmoid(sm_ref[...])
    tblk = (i * TQ + _iota2((ns, TQ), 1)) // NSA_SEL_BLOCK
    jj = _iota2((ns, TQ), 0)
    forced = (jj == 0) | (jj == tblk) | (jj == tblk - 1)
    causal = jj <= tblk
    cbias = jnp.where(cmask, 0.0, NEG_INF)
    for g in range(KV_HEADS):
        kc = kv[:, g * HEAD_DIM:(g + 1) * HEAD_DIM].astype(BF16)
        vc = kv[:, KV_W + g * HEAD_DIM:KV_W + (g + 1) * HEAD_DIM].astype(BF16)
        heads = range(g * HPG, (g + 1) * HPG)
        qg = jnp.concatenate([q_ref[:, h * HEAD_DIM:(h + 1) * HEAD_DIM] for h in heads], axis=0) * SCALE
        s = _dot_nt(qg, kc).reshape(HPG, TQ, nr) + cbias[None]
        e = jnp.exp(s - jnp.max(s, -1, keepdims=True))
        p = jnp.where(cmask[None], e / jnp.sum(e, -1, keepdims=True), 0.0)
        psum = jnp.sum(p, axis=0)
        o = _dot(p.reshape(HPG * TQ, nr).astype(BF16), vc)
        for j, h in enumerate(heads):
            sl = slice(h * HEAD_DIM, (h + 1) * HEAD_DIM)
            o_ref[:, sl] = o[j * TQ:(j + 1) * TQ] * bgate[:, SM_GATE + h:SM_GATE + h + 1] * gate[:, sl]
        imp = _dot_nt(ov_ref[...], psum, HI)
        imp = jnp.where(causal, jnp.where(forced, NSA_FORCE, imp), NEG_INF)
        rank = jnp.zeros((ns, TQ), jnp.int32)
        for j in range(ns):
            row = imp[j:j + 1, :]
            rank = rank + ((row > imp) | ((row == imp) & (j < jj))).astype(jnp.int32)
        sel_ref[0, g] = jnp.where((rank < NSA_TOPN) & causal, 0.0, NEG_INF)


def _nsa_cmpattn(cb, cmp, cf, B, T, ov):
    nq = T // TQ
    nr = cmp.shape[1]
    ns = T // NSA_SEL_BLOCK
    return pl.pallas_call(
        _cmpattn_kernel,
        grid=(B, nq),
        in_specs=[pl.BlockSpec((TQ, 512), lambda b, i: (b * nq + i, BF_Q['nsa_q'])),
                  pl.BlockSpec((1, nr, 2 * KV_W), lambda b, i: (b, 0, 0)),
                  pl.BlockSpec((TQ, 512), lambda b, i: (b * nq + i, FCOL['nsa_g'])),
                  pl.BlockSpec((TQ, SMALL_W), lambda b, i: (b * nq + i, SMALL_BLK)),
                  pl.BlockSpec(ov.shape, lambda b, i: (0, 0))],
        out_specs=[pl.BlockSpec((TQ, 512), lambda b, i: (b * nq + i, 0)),
                   pl.BlockSpec((1, KV_HEADS, ns, TQ), lambda b, i: (b, 0, 0, i))],
        out_shape=[jax.ShapeDtypeStruct((B * T, D_GROUP), F32),
                   jax.ShapeDtypeStruct((B, KV_HEADS, ns, T), F32)],
        compiler_params=_params(("parallel", "arbitrary")),
        name="nsa_cmpattn",
    )(cb, cmp, cf, cf, ov)


def _out_kernel(ya_ref, yb_ref, c1_ref, c2_ref, c3_ref, yd_ref, x_ref, w_ref, lg_ref, lb_ref, o_ref, ob_ref):
    yc = c1_ref[...] + c2_ref[...] + c3_ref[...]
    acc = _dot(ya_ref[...].astype(BF16), w_ref[0:512, :])
    acc = acc + _dot(yb_ref[...].astype(BF16), w_ref[512:1024, :])
    acc = acc + _dot(yc.astype(BF16), w_ref[1024:1536, :])
    acc = acc + _dot(yd_ref[...].astype(BF16), w_ref[1536:2048, :])
    z = DEEPNORM_ALPHA * x_ref[...] + acc
    mu = jnp.mean(z, -1, keepdims=True)
    var = jnp.mean(jnp.square(z - mu), -1, keepdims=True)
    out = (z - mu) * lax.rsqrt(var + LN_EPS) * lg_ref[...] + lb_ref[...]
    o_ref[...] = out
    ob_ref[...] = out.astype(BF16)


def _out_proj(ys, x, w, lg, lb, tm=256):
    m = x.shape[0]
    yspec = pl.BlockSpec((tm, 512), lambda i: (i, 0))
    xspec = pl.BlockSpec((tm, D_MODEL), lambda i: (i, 0))
    return pl.pallas_call(
        _out_kernel,
        grid=(m // tm,),
        in_specs=[yspec] * 6 + [xspec, pl.BlockSpec(w.shape, lambda i: (0, 0)),
                                pl.BlockSpec(lg.shape, lambda i: (0, 0)), pl.BlockSpec(lb.shape, lambda i: (0, 0))],
        out_specs=[xspec, xspec],
        out_shape=[jax.ShapeDtypeStruct((m, D_MODEL), F32), jax.ShapeDtypeStruct((m, D_MODEL), BF16)],
        compiler_params=_params(("parallel",)),
        name="out_proj_ln",
    )(*ys, x, w, lg, lb)


def _cols(w, name):
    o, s = _OFF[name]
    return w[..., o:o + s]


def _small_row(pieces):
    row = jnp.zeros((1, SMALL_W), F32)
    for off, vals in pieces:
        row = row.at[0, off:off + vals.shape[-1]].set(vals.astype(F32))
    return row


def _overlap_matrix(T):
    nc = T // NSA_CMP_STRIDE
    ns = T // NSA_SEL_BLOCK
    cst = np.arange(nc) * NSA_CMP_STRIDE
    jst = np.arange(ns) * NSA_SEL_BLOCK
    ov = np.clip(np.minimum(cst[:, None] + NSA_CMP_BLOCK, jst[None, :] + NSA_SEL_BLOCK)
                 - np.maximum(cst[:, None], jst[None, :]), 0, None).astype(np.float32) / NSA_CMP_BLOCK
    return jnp.asarray(ov.T)


def _cmp_weights(pe_k, pe_v, k_w1, k_w2, v_w1, v_w2):
    half = NSA_CMP_STRIDE
    hid = NSA_CMP_HIDDEN

    def w1_half(lo):
        blocks = jnp.zeros((half, 4, HEAD_DIM, 4, hid), F32)
        kw = k_w1.reshape(NSA_CMP_BLOCK, HEAD_DIM, hid)[lo:lo + half]
        vw = v_w1.reshape(NSA_CMP_BLOCK, HEAD_DIM, hid)[lo:lo + half]
        for s, wsrc in ((0, kw), (1, kw), (2, vw), (3, vw)):
            blocks = blocks.at[:, s, :, s, :].set(wsrc)
        return blocks.reshape(half * 4 * HEAD_DIM, 4 * hid).astype(BF16)

    def pe_half(lo):
        pk, pv = pe_k[lo:lo + half], pe_v[lo:lo + half]
        return jnp.stack([pk, pk, pv, pv], axis=1).reshape(1, half * 4 * HEAD_DIM)

    w2 = jnp.zeros((4, hid, 4, HEAD_DIM), F32)
    for s, wsrc in ((0, k_w2), (1, k_w2), (2, v_w2), (3, v_w2)):
        w2 = w2.at[s, :, s, :].set(wsrc)
    return pe_half(0), pe_half(half), w1_half(0), w1_half(half), w2.reshape(4 * hid, 4 * HEAD_DIM).astype(BF16)


def _layer(x, xb, B, T, w_in, w_out, ln_g, ln_b, rw_mu, rw_w0, rw_w2, rw_a0, rw_a2, rw_kk, rw_ka, rw_rk,
           rw_gn_g, rw_gn_b, swa_sinks, nsa_pe_k, nsa_pe_v, nsa_k_w1, nsa_k_w2, nsa_v_w1, nsa_v_w2,
           gdn_conv, gdn_A_log, gdn_dt_bias, gdn_norm_g):
    small = jnp.concatenate([_cols(w_in, n) for n in ('rw_wd', 'rw_ad', 'nsa_gate', 'gdn_beta', 'gdn_a')], axis=1)
    small = jnp.pad(small, ((0, 0), (0, SMALL_W - small.shape[1])))
    w_f32 = jnp.concatenate([_cols(w_in, n) for n in F32_GROUPS] + [small], axis=1).astype(BF16)
    w_bf = jnp.concatenate([_cols(w_in, n) for n in ('swa_q', 'nsa_q', 'swa_k', 'swa_v', 'nsa_ks', 'nsa_vs',
                                                     'nsa_kw', 'nsa_vw')], axis=1).astype(BF16)
    w_cmp = jnp.concatenate([_cols(w_in, 'nsa_kc'), _cols(w_in, 'nsa_vc')], axis=1).astype(BF16)
    tm = min(1024, B * T)
    cf = _matmul(xb, w_f32, F32, tm, N_F32 // 7)
    cb = _matmul(xb, w_bf, BF16, tm, N_BF // 2)
    ccmp = _matmul(xb, w_cmp, F32, tm, 256)

    row = lambda a: a.reshape(1, -1).astype(F32)
    mu3 = row(rw_mu[:3 * D_GROUP])
    musm = _small_row([(SM_WD, rw_mu[3 * D_GROUP:3 * D_GROUP + RW_LORA]),
                       (SM_AD, rw_mu[3 * D_GROUP + RW_LORA:])])
    y_a = _rwkv(cf, B, T, mu3, musm, row(rw_w0), rw_w2, row(rw_a0), rw_a2, row(rw_kk), row(rw_ka),
                row(rw_rk), row(rw_gn_g), row(rw_gn_b))
    y_d = _gdn(cf, B, T, gdn_conv, _small_row([(SM_A, gdn_A_log)]), _small_row([(SM_A, gdn_dt_bias)]),
               jnp.tile(row(gdn_norm_g), (1, 2)))
    y_b = _attn(cb, cf, B, T, 'swa_q', 'swa_k', 'swa_v', 'swa_g', window=SWA_WINDOW, sinks=row(swa_sinks))
    cmp = _nsa_compress(ccmp, B, T, *_cmp_weights(nsa_pe_k, nsa_pe_v, nsa_k_w1, nsa_k_w2, nsa_v_w1, nsa_v_w2))
    c1, sel = _nsa_cmpattn(cb, cmp, cf, B, T, _overlap_matrix(T))
    c2 = _attn(cb, cf, B, T, 'nsa_q', 'nsa_ks', 'nsa_vs', 'nsa_g', sel=sel, gate_col=SM_GATE + N_HEADS, tk=4 * TQ)
    c3 = _attn(cb, cf, B, T, 'nsa_q', 'nsa_kw', 'nsa_vw', 'nsa_g', window=NSA_WINDOW,
               gate_col=SM_GATE + 2 * N_HEADS)
    return _out_proj((y_a, y_b, c1, c2, c3, y_d), x, w_out.astype(BF16), row(ln_g), row(ln_b))


def kernel(x, w_in, w_out, ln_g, ln_b, rw_mu, rw_w0, rw_w2, rw_a0, rw_a2, rw_kk, rw_ka, rw_rk, rw_gn_g, rw_gn_b, swa_sinks, nsa_pe_k, nsa_pe_v, nsa_k_w1, nsa_k_w2, nsa_v_w1, nsa_v_w2, gdn_conv, gdn_A_log, gdn_dt_bias, gdn_norm_g):
    B, T, D = x.shape
    params = (w_in, w_out, ln_g, ln_b, rw_mu, rw_w0, rw_w2, rw_a0, rw_a2, rw_kk, rw_ka, rw_rk, rw_gn_g, rw_gn_b,
              swa_sinks, nsa_pe_k, nsa_pe_v, nsa_k_w1, nsa_k_w2, nsa_v_w1, nsa_v_w2, gdn_conv, gdn_A_log,
              gdn_dt_bias, gdn_norm_g)
    xf = x.reshape(B * T, D)
    xb = xf.astype(BF16)
    for i in range(w_in.shape[0]):
        xf, xb = _layer(xf, xb, B, T, *(p[i] for p in params))
    return xf.reshape(B, T, D)
```

```python
import functools
import math

import numpy as np
import jax
import jax.numpy as jnp
from jax import lax
from jax.experimental import pallas as pl
from jax.experimental.pallas import tpu as pltpu

F32 = jnp.float32
BF16 = jnp.bfloat16
HI = lax.Precision.HIGHEST

D_MODEL = 2048
DEPTH = 4
D_GROUP = 512
HEAD_DIM = 64
N_HEADS = 8
KV_HEADS = 2
HPG = N_HEADS // KV_HEADS
PAIR = 2 * HEAD_DIM
KV_W = KV_HEADS * HEAD_DIM
NEG_INF = -1e30
LN_EPS = 1e-5
DEEPNORM_ALPHA = (2 * DEPTH) ** 0.25
RW_LORA = 32
RW_GN_EPS = 64e-5
SWA_WINDOW = 128
NSA_CMP_BLOCK = 32
NSA_CMP_STRIDE = 16
NSA_CMP_HIDDEN = 128
NSA_SEL_BLOCK = 64
NSA_TOPN = 16
NSA_WINDOW = 512
NSA_FORCE = 1e6
GDN_CONV = 4
GDN_EPS = 1e-6
SCALE = HEAD_DIM ** -0.5

CHUNK = 64
RW_SEQS = 4
GDN_SEQS = 4
TQ = 128
AQ_BAND = 256
AQ_SEL = 512
TAIL = 8
VT_ROWS = HEAD_DIM + 16
VMEM_LIMIT = 56 * 1024 * 1024

_COLS = (
    ('rw_r', 512), ('rw_k', 512), ('rw_v', 512), ('rw_wd', 32), ('rw_ad', 32), ('rw_g', 512),
    ('swa_q', 512), ('swa_k', 128), ('swa_v', 128), ('swa_g', 512),
    ('nsa_q', 512), ('nsa_kc', 128), ('nsa_vc', 128), ('nsa_ks', 128), ('nsa_vs', 128),
    ('nsa_kw', 128), ('nsa_vw', 128), ('nsa_gate', 24), ('nsa_g', 512),
    ('gdn_q', 512), ('gdn_k', 512), ('gdn_v', 512), ('gdn_beta', 8), ('gdn_a', 8), ('gdn_g', 512),
)
_OFF = {}
_o = 0
for _n, _s in _COLS:
    _OFF[_n] = (_o, _s)
    _o += _s
N_IN = _o

F32_GROUPS = ('rw_r', 'rw_k', 'rw_v', 'rw_g', 'gdn_q', 'gdn_k', 'gdn_v', 'gdn_g', 'swa_g', 'nsa_g')
FCOL = {n: i for i, n in enumerate(F32_GROUPS)}
SMALL_W = 256
SMALL_BLK = len(F32_GROUPS) * 512 // SMALL_W
SM_WD, SM_AD, SM_GATE, SM_BETA, SM_A = 0, 32, 64, 88, 96
N_F32 = len(F32_GROUPS) * 512 + SMALL_W
BF_Q = {'swa_q': 0, 'nsa_q': 1}
BF_KV = {n: 8 + i for i, n in enumerate(('swa_k', 'swa_v', 'nsa_ks', 'nsa_vs', 'nsa_kw', 'nsa_vw'))}
N_BF = 1024 + 6 * 128


def _dot(a, b, prec=None):
    return lax.dot_general(a, b, (((1,), (0,)), ((), ())), precision=prec, preferred_element_type=F32)


def _dot_nt(a, b, prec=None):
    return lax.dot_general(a, b, (((1,), (1,)), ((), ())), precision=prec, preferred_element_type=F32)


def _dot_tn(a, b, prec=None):
    return lax.dot_general(a, b, (((0,), (0,)), ((), ())), precision=prec, preferred_element_type=F32)


def _sigmoid(x):
    return 1.0 / (1.0 + jnp.exp(-x))


def _silu(x):
    return x * _sigmoid(x)


def _softplus(x):
    return jnp.maximum(x, 0.0) + jnp.log(1.0 + jnp.exp(-jnp.abs(x)))


def _iota2(shape, dim):
    return lax.broadcasted_iota(jnp.int32, shape, dim)


def _pairs(t):
    return [t[:, p * PAIR:(p + 1) * PAIR] for p in range(N_HEADS // 2)]


def _lo_mask(rows):
    return _iota2((rows, PAIR), 1) < HEAD_DIM


def _bd(x):
    xb = x.astype(BF16)
    lo = _lo_mask(x.shape[0]).astype(BF16)
    return jnp.concatenate([xb * lo, xb * (1 - lo)], axis=0)


def _pdot(a, b):
    return _dot(a.astype(BF16), _bd(b))


def _pdot_nt(a, b):
    return _dot_nt(a.astype(BF16), _bd(b))


def _pdot_tn(a, b):
    full = _dot_tn(a.astype(BF16), b.astype(BF16))
    return jnp.where(_lo_mask(HEAD_DIM), full[:HEAD_DIM], full[HEAD_DIM:])


def _segsum(t):
    lo = _lo_mask(t.shape[0])
    s_lo = jnp.sum(jnp.where(lo, t, 0.0), -1, keepdims=True)
    s_hi = jnp.sum(jnp.where(lo, 0.0, t), -1, keepdims=True)
    return jnp.where(lo, s_lo, s_hi)


def _inv_unit_lower(xs):
    n = xs[0].shape[0]
    eye2 = ((_iota2((n, 2 * n), 1) & (n - 1)) == _iota2((n, 2 * n), 0)).astype(F32)
    ps = [eye2 + x for x in xs]
    for _ in range(int(math.log2(n)) - 1):
        xs = [_pdot(x, x) for x in xs]
        ps = [p + _pdot(x, p) for x, p in zip(xs, ps)]
    return ps


def _tri_masks2(n):
    ri = _iota2((n, 2 * n), 0)
    ci = _iota2((n, 2 * n), 1) & (n - 1)
    return ri >= ci, ri > ci


def _params(sem):
    return pltpu.CompilerParams(dimension_semantics=sem, vmem_limit_bytes=VMEM_LIMIT)


def _mm_kernel(x_ref, w_ref, o_ref):
    o_ref[...] = jnp.dot(x_ref[...], w_ref[...], preferred_element_type=F32).astype(o_ref.dtype)


def _matmul(x, w, out_dtype, tm, tn):
    m, k = x.shape
    n = w.shape[1]
    return pl.pallas_call(
        _mm_kernel,
        grid=(m // tm, n // tn),
        in_specs=[pl.BlockSpec((tm, k), lambda i, j: (i, 0)),
                  pl.BlockSpec((k, tn), lambda i, j: (0, j))],
        out_specs=pl.BlockSpec((tm, tn), lambda i, j: (i, j)),
        out_shape=jax.ShapeDtypeStruct((m, n), out_dtype),
        compiler_params=_params(("parallel", "arbitrary")),
        name="proj_in",
    )(x, w)


def _rwkv_kernel(r_ref, k_ref, v_ref, g_ref, sm_ref, mu_ref, musm_ref, w0_ref, w2_ref, a0_ref, a2_ref,
                 kk_ref, ka_ref, rk_ref, gng_ref, gnb_ref, o_ref, buf_ref, bufsm_ref, s_ref):
    c = pl.program_id(1)
    C = CHUNK

    NB = r_ref.shape[0]

    @pl.when(c == 0)
    def _():
        buf_ref[:, 0:TAIL, :] = jnp.zeros((NB, TAIL, 3 * D_GROUP), F32)
        bufsm_ref[:, 0:TAIL, :] = jnp.zeros((NB, TAIL, SMALL_W), F32)
        s_ref[...] = jnp.zeros(s_ref.shape, F32)

    ri = _iota2((C, C), 0)
    ci = _iota2((C, C), 1)
    tril = ri >= ci
    strict = ri > ci
    rk = rk_ref[...]
    gng = gng_ref[...]
    gnb = gnb_ref[...]
    pre = []
    for nb in range(NB):
        buf_ref[nb, TAIL:TAIL + C, 0:512] = r_ref[nb]
        buf_ref[nb, TAIL:TAIL + C, 512:1024] = k_ref[nb]
        buf_ref[nb, TAIL:TAIL + C, 1024:1536] = v_ref[nb]
        bufsm_ref[nb, TAIL:TAIL + C, :] = sm_ref[nb]
        cur = buf_ref[nb, TAIL:TAIL + C, :]
        prev = buf_ref[nb, TAIL - 1:TAIL - 1 + C, :]
        mixed = cur + (prev - cur) * mu_ref[...]
        cur_sm = bufsm_ref[nb, TAIL:TAIL + C, :]
        prev_sm = bufsm_ref[nb, TAIL - 1:TAIL - 1 + C, :]
        smix = cur_sm + (prev_sm - cur_sm) * musm_ref[...]
        buf_ref[nb, 0:TAIL, :] = buf_ref[nb, C:C + TAIL, :]
        bufsm_ref[nb, 0:TAIL, :] = bufsm_ref[nb, C:C + TAIL, :]

        r = mixed[:, 0:512]
        k = mixed[:, 512:1024]
        v = mixed[:, 1024:1536]
        wd = smix[:, SM_WD:SM_WD + RW_LORA]
        ad = smix[:, SM_AD:SM_AD + RW_LORA]
        wl = w0_ref[...] + _dot(jnp.tanh(wd), w2_ref[...], HI)
        logw = -jnp.exp(-_softplus(-wl) - 0.5)
        alpha = _sigmoid(a0_ref[...] + _dot(ad, a2_ref[...], HI))
        kkraw = k * kk_ref[...]
        k2 = k * (1.0 + (alpha - 1.0) * ka_ref[...])
        gcum = _dot(tril.astype(F32), logw, HI)
        pre.append((r, k2, v, alpha, kkraw, logw, gcum, _silu(g_ref[nb])))

    H = range(NB * N_HEADS // 2)
    tril2, strict2 = _tri_masks2(C)
    R, K2, V, AL, KKR, LW, G = ([t for p in pre for t in _pairs(p[f])] for f in range(7))
    KK = [t * lax.rsqrt(_segsum(t * t) + 1e-6) for t in KKR]
    GM = [t[C // 2 - 1:C // 2, :] for t in G]
    GE = [t[C - 1:C, :] for t in G]
    BV = [KK[h] * AL[h] for h in H]
    EINV = [jnp.exp(GM[h] - G[h]) for h in H]
    EEND = [jnp.exp(GE[h] - G[h]) for h in H]
    LEFT = [jnp.concatenate([-KK[h] * jnp.exp(G[h] - LW[h] - GM[h]), R[h] * jnp.exp(G[h] - GM[h])], axis=0)
            for h in H]
    AB = [_pdot_nt(LEFT[h], BV[h] * EINV[h]) for h in H]
    AK = [_pdot_nt(LEFT[h], K2[h] * EINV[h]) for h in H]
    A_RB = [jnp.where(tril2, t[C:], 0.0) for t in AB]
    A_K = [jnp.concatenate([jnp.where(strict2, t[:C], 0.0), jnp.where(tril2, t[C:], 0.0)], axis=0) for t in AK]
    TINV = _inv_unit_lower([jnp.where(strict2, t[:C], 0.0) for t in AB])
    S = [s_ref[h] for h in H]
    LS = [_pdot_nt(LEFT[h], S[h] * jnp.exp(GM[h])) for h in H]
    AV = [_pdot(A_K[h], V[h]) for h in H]
    U = [_pdot(TINV[h], LS[h][:C] + AV[h][:C]) for h in H]
    O = [LS[h][C:] + _pdot(A_RB[h], U[h]) + AV[h][C:] for h in H]
    for h in H:
        s_ref[h] = S[h] * jnp.exp(GE[h]) + _pdot_tn(
            jnp.concatenate([U[h], V[h]], axis=0),
            jnp.concatenate([BV[h] * EEND[h], K2[h] * EEND[h]], axis=0))
    for n in H:
        nb, p = divmod(n, N_HEADS // 2)
        sl = slice(p * PAIR, (p + 1) * PAIR)
        mu = _segsum(O[n]) * (1.0 / HEAD_DIM)
        d = O[n] - mu
        var = _segsum(d * d) * (1.0 / HEAD_DIM)
        y = d * lax.rsqrt(var + RW_GN_EPS) * gng[:, sl] + gnb[:, sl]
        bonus = _segsum(R[n] * K2[n] * rk[:, sl]) * V[n]
        o_ref[nb, :, sl] = (y + bonus) * pre[nb][7][:, sl]


def _seq_specs(names, nbs):
    specs = [pl.BlockSpec((nbs, CHUNK, 512), lambda b, c, j=FCOL[n]: (b, c, j)) for n in names]
    return specs + [pl.BlockSpec((nbs, CHUNK, SMALL_W), lambda b, c: (b, c, SMALL_BLK))]


def _rwkv(cf, B, T, mu3, musm, w0, w2, a0, a2, kk, ka, rk, gng, gnb):
    nbs = RW_SEQS if B % RW_SEQS == 0 else 1
    cf3 = cf.reshape(B, T, cf.shape[-1])
    full = lambda a: pl.BlockSpec(a.shape, lambda b, c: (0,) * a.ndim)
    ps = (mu3, musm, w0, w2, a0, a2, kk, ka, rk, gng, gnb)
    out = pl.pallas_call(
        _rwkv_kernel,
        grid=(B // nbs, T // CHUNK),
        in_specs=_seq_specs(('rw_r', 'rw_k', 'rw_v', 'rw_g'), nbs) + [full(a) for a in ps],
        out_specs=pl.BlockSpec((nbs, CHUNK, 512), lambda b, c: (b, c, 0)),
        out_shape=jax.ShapeDtypeStruct((B, T, D_GROUP), F32),
        scratch_shapes=[pltpu.VMEM((nbs, TAIL + CHUNK, 3 * D_GROUP), F32),
                        pltpu.VMEM((nbs, TAIL + CHUNK, SMALL_W), F32),
                        pltpu.VMEM((nbs * N_HEADS // 2, HEAD_DIM, PAIR), F32)],
        compiler_params=_params(("parallel", "arbitrary")),
        name="rwkv7",
    )(cf3, cf3, cf3, cf3, cf3, *ps)
    return out.reshape(B * T, D_GROUP)


def _gdn_kernel(q_ref, k_ref, v_ref, z_ref, sm_ref, conv_ref, alog_ref, dtb_ref, ng_ref, o_ref,
                buf_ref, s_ref):
    c = pl.program_id(1)
    C = CHUNK

    NB = q_ref.shape[0]

    @pl.when(c == 0)
    def _():
        buf_ref[:, 0:TAIL, :] = jnp.zeros((NB, TAIL, 3 * D_GROUP), F32)
        s_ref[...] = jnp.zeros(s_ref.shape, F32)

    conv = conv_ref[...]
    lane = _iota2((C, SMALL_W), 1)
    ri = _iota2((C, C), 0)
    ci = _iota2((C, C), 1)
    tril = ri >= ci
    strict = ri > ci
    ng = ng_ref[...]
    pre = []
    for nb in range(NB):
        buf_ref[nb, TAIL:TAIL + C, 0:512] = q_ref[nb]
        buf_ref[nb, TAIL:TAIL + C, 512:1024] = k_ref[nb]
        buf_ref[nb, TAIL:TAIL + C, 1024:1536] = v_ref[nb]
        acc = buf_ref[nb, TAIL:TAIL + C, :] * conv[GDN_CONV - 1:GDN_CONV, :]
        for i in range(GDN_CONV - 1):
            sh = GDN_CONV - 1 - i
            acc = acc + buf_ref[nb, TAIL - sh:TAIL - sh + C, :] * conv[i:i + 1, :]
        buf_ref[nb, 0:TAIL, :] = buf_ref[nb, C:C + TAIL, :]
        qkv = _silu(acc)
        sm = sm_ref[nb]
        g_all = jnp.where((lane >= SM_A) & (lane < SM_A + N_HEADS),
                          -jnp.exp(alog_ref[...]) * _softplus(sm + dtb_ref[...]), 0.0)
        gam_all = _dot(tril.astype(F32), g_all, HI)
        pre.append((qkv[:, 0:512], qkv[:, 512:1024], qkv[:, 1024:1536], _sigmoid(sm), gam_all, _silu(z_ref[nb])))

    H = range(NB * N_HEADS // 2)
    hd = lambda n: divmod(n, N_HEADS // 2)
    lo = _lo_mask(C)
    tril2, strict2 = _tri_masks2(C)
    diag2 = (_iota2((C, PAIR), 1) & (HEAD_DIM - 1)) == _iota2((C, PAIR), 0)

    def per_head(n, field, col):
        nb, p = hd(n)
        t = pre[nb][field]
        return jnp.where(lo, t[:, col + 2 * p:col + 2 * p + 1], t[:, col + 2 * p + 1:col + 2 * p + 2])

    Q = [t * lax.rsqrt(_segsum(t * t) + 1e-6) * SCALE for p in pre for t in _pairs(p[0])]
    K = [t * lax.rsqrt(_segsum(t * t) + 1e-6) for p in pre for t in _pairs(p[1])]
    V = [t for p in pre for t in _pairs(p[2])]
    BETA = [per_head(n, 3, SM_BETA) for n in H]
    GAM = [per_head(n, 4, SM_A) for n in H]
    GROW = [jnp.sum(jnp.where(diag2, t, 0.0), 0, keepdims=True) for t in GAM]
    DECAY = [jnp.exp(jnp.where(tril2, GAM[n] - GROW[n], NEG_INF)) for n in H]
    EG = [jnp.exp(t) for t in GAM]
    KB = [K[h] * BETA[h] for h in H]
    KKQ = [_pdot_nt(jnp.concatenate([KB[h], Q[h]], axis=0), K[h]) for h in H]
    QK = [KKQ[h][C:] * DECAY[h] for h in H]
    TM = _inv_unit_lower([-jnp.where(strict2, KKQ[h][:C] * DECAY[h], 0.0) for h in H])
    UU = [_pdot(TM[h], V[h] * BETA[h]) for h in H]
    W = [_pdot(TM[h], KB[h] * EG[h]) for h in H]
    S = [s_ref[h] for h in H]
    WS = [_pdot(jnp.concatenate([W[h], Q[h] * EG[h]], axis=0), S[h]) for h in H]
    VN = [UU[h] - WS[h][:C] for h in H]
    O = [WS[h][C:] + _pdot(QK[h], VN[h]) for h in H]
    for h in H:
        g_last = GAM[h][C - 1:C, :]
        s_ref[h] = S[h] * jnp.exp(g_last) + _pdot_tn(K[h] * jnp.exp(g_last - GAM[h]), VN[h])
    for n in H:
        nb, p = hd(n)
        sl = slice(p * PAIR, (p + 1) * PAIR)
        o = O[n] * lax.rsqrt(_segsum(O[n] * O[n]) * (1.0 / HEAD_DIM) + GDN_EPS) * ng
        o_ref[nb, :, sl] = o * pre[nb][5][:, sl]


def _gdn(cf, B, T, conv, alog_sm, dtb_sm, ng):
    nbs = GDN_SEQS if B % GDN_SEQS == 0 else 1
    cf3 = cf.reshape(B, T, cf.shape[-1])
    full = lambda a: pl.BlockSpec(a.shape, lambda b, c: (0,) * a.ndim)
    ps = (conv, alog_sm, dtb_sm, ng)
    out = pl.pallas_call(
        _gdn_kernel,
        grid=(B // nbs, T // CHUNK),
        in_specs=_seq_specs(('gdn_q', 'gdn_k', 'gdn_v', 'gdn_g'), nbs) + [full(a) for a in ps],
        out_specs=pl.BlockSpec((nbs, CHUNK, 512), lambda b, c: (b, c, 0)),
        out_shape=jax.ShapeDtypeStruct((B, T, D_GROUP), F32),
        scratch_shapes=[pltpu.VMEM((nbs, TAIL + CHUNK, 3 * D_GROUP), F32),
                        pltpu.VMEM((nbs * N_HEADS // 2, HEAD_DIM, PAIR), F32)],
        compiler_params=_params(("parallel", "arbitrary")),
        name="gdn",
    )(cf3, cf3, cf3, cf3, cf3, *ps)
    return out.reshape(B * T, D_GROUP)


def _store_heads(o_ref, g, ot, gate, bgate, gate_col):
    aq = o_ref.shape[0]
    for j in range(HPG):
        h = g * HPG + j
        sl = slice(h * HEAD_DIM, (h + 1) * HEAD_DIM)
        oh = ot[:, j * aq:(j + 1) * aq].T
        if bgate is not None:
            oh = oh * bgate[:, gate_col + h:gate_col + h + 1]
        o_ref[:, sl] = oh * gate[:, sl]


def _attn_kernel(*refs, window, use_sel, use_sink, gate_col, tk):
    it = iter(refs)
    q_ref, k_ref, v_ref, g_ref = next(it), next(it), next(it), next(it)
    sm_ref = next(it) if gate_col is not None else None
    sel_ref = next(it) if use_sel else None
    sink_ref = next(it) if use_sink else None
    o_ref = next(it)
    vt_ref = next(it)
    selx_ref = next(it) if use_sel else None
    i = pl.program_id(1)
    aq = q_ref.shape[0]
    W = HPG * aq

    @pl.when(i == 0)
    def _():
        eye = (_iota2((KV_W, KV_W), 0) == _iota2((KV_W, KV_W), 1)).astype(BF16)
        ones = jnp.ones((VT_ROWS - HEAD_DIM, tk), BF16)
        for c in range(vt_ref.shape[0]):
            vt = _dot_nt(eye, v_ref[c * tk:(c + 1) * tk, :]).astype(BF16)
            for g in range(KV_HEADS):
                vt_ref[c, g] = jnp.concatenate([vt[g * HEAD_DIM:(g + 1) * HEAD_DIM], ones], axis=0)

    span = tk if window is None else window + aq
    qpos = i * aq + (_iota2((span, W), 1) & (aq - 1))
    krow = _iota2((span, W), 0)
    gate = _silu(g_ref[...])
    if gate_col is not None:
        bgate = _sigmoid(sm_ref[...])

    G = range(KV_HEADS)
    QG = [jnp.concatenate([q_ref[:, h * HEAD_DIM:(h + 1) * HEAD_DIM] for h in range(g * HPG, (g + 1) * HPG)],
                          axis=0) * SCALE for g in G]
    if use_sel:
        for g in G:
            selx_ref[g] = jnp.concatenate([sel_ref[0, g]] * HPG, axis=1)
        nb = tk // NSA_SEL_BLOCK

    if window is not None:
        nt = span // tk
        t0 = jnp.maximum(i * aq - window, 0) // tk
        start = pl.multiple_of(t0 * tk, tk)
        rel = qpos - (start + krow)
        pbias = jnp.where((rel >= 0) & (rel < window), 0.0, NEG_INF)
        for g in G:
            ksl = slice(g * HEAD_DIM, (g + 1) * HEAD_DIM)
            s = _dot_nt(k_ref[pl.ds(start, span), ksl], QG[g]) + pbias
            m = jnp.max(s, 0, keepdims=True)
            if use_sink:
                sink = jnp.concatenate([jnp.zeros((1, aq), F32) + sink_ref[0:1, h:h + 1]
                                        for h in range(g * HPG, (g + 1) * HPG)], axis=1)
                m = jnp.maximum(m, sink)
            p = jnp.exp((s - m).astype(BF16))
            vt = jnp.concatenate([vt_ref[t0 + j, g] for j in range(nt)], axis=1)
            pv = _dot(vt, p)
            l = pv[HEAD_DIM:HEAD_DIM + 1]
            if use_sink:
                l = l + jnp.exp(sink - m)
            _store_heads(o_ref, g, pv[:HEAD_DIM] / l, gate, bgate if gate_col is not None else None, gate_col)
        return

    hi = (i * aq + aq + tk - 1) // tk

    def body(kt, carry, masked):
        off = pl.multiple_of(kt * tk, tk)
        if masked:
            pbias = jnp.where(qpos - (kt * tk + krow) >= 0, 0.0, NEG_INF)
        out = []
        for g in G:
            m, acc = carry[2 * g:2 * g + 2]
            ksl = slice(g * HEAD_DIM, (g + 1) * HEAD_DIM)
            s = _dot_nt(k_ref[pl.ds(off, tk), ksl], QG[g])
            if masked:
                s = s + pbias
            if use_sel:
                s = s + jnp.concatenate(
                    [jnp.broadcast_to(selx_ref[g, pl.ds(kt * nb + j, 1), :], (NSA_SEL_BLOCK, W))
                     for j in range(nb)], axis=0)
            m_new = jnp.maximum(m, jnp.max(s, 0, keepdims=True))
            a = jnp.exp(m - m_new)
            p = jnp.exp((s - m_new).astype(BF16))
            out += [m_new, a * acc + _dot(vt_ref[kt, g], p)]
        return tuple(out)

    init = []
    for g in G:
        init += [jnp.full((1, W), NEG_INF, F32), jnp.zeros((VT_ROWS, W), F32)]
    res = lax.fori_loop(0, hi - 1, functools.partial(body, masked=False), tuple(init))
    res = body(hi - 1, res, True)
    for g in G:
        acc = res[2 * g + 1]
        _store_heads(o_ref, g, acc[:HEAD_DIM] / acc[HEAD_DIM:HEAD_DIM + 1], gate,
                     bgate if gate_col is not None else None, gate_col)


def _attn(cb, cf, B, T, qname, kname, vname, gname, *, window=None, sel=None, sinks=None, gate_col=None, tk=TQ,
          aq=AQ_BAND):
    nq = T // aq
    in_specs = [pl.BlockSpec((aq, 512), lambda b, i, j=BF_Q[qname]: (b * nq + i, j)),
                pl.BlockSpec((T, KV_W), lambda b, i, j=BF_KV[kname]: (b, j)),
                pl.BlockSpec((T, KV_W), lambda b, i, j=BF_KV[vname]: (b, j)),
                pl.BlockSpec((aq, 512), lambda b, i, j=FCOL[gname]: (b * nq + i, j))]
    args = [cb, cb, cb, cf]
    if gate_col is not None:
        in_specs.append(pl.BlockSpec((aq, SMALL_W), lambda b, i: (b * nq + i, SMALL_BLK)))
        args.append(cf)
    scratch = [pltpu.VMEM((T // tk, KV_HEADS, VT_ROWS, tk), BF16)]
    if sel is not None:
        in_specs.append(pl.BlockSpec((1, KV_HEADS, sel.shape[2], aq), lambda b, i: (b, 0, 0, i)))
        args.append(sel)
        scratch.append(pltpu.VMEM((KV_HEADS, sel.shape[2], HPG * aq), F32))
    if sinks is not None:
        in_specs.append(pl.BlockSpec(sinks.shape, lambda b, i: (0, 0)))
        args.append(sinks)
    return pl.pallas_call(
        functools.partial(_attn_kernel, window=window, use_sel=sel is not None,
                          use_sink=sinks is not None, gate_col=gate_col, tk=tk),
        grid=(B, nq),
        in_specs=in_specs,
        out_specs=pl.BlockSpec((aq, 512), lambda b, i: (b * nq + i, 0)),
        out_shape=jax.ShapeDtypeStruct((B * T, D_GROUP), F32),
        scratch_shapes=scratch,
        compiler_params=_params(("parallel", "arbitrary")),
        name="attn_" + qname + ("_sel" if sel is not None else "_w%d" % window),
    )(*args)


def _gelu_tanh(x):
    return 0.5 * x * (1.0 + jnp.tanh(math.sqrt(2.0 / math.pi) * (x + 0.044715 * x * x * x)))


def _cmp_kernel(r_ref, pe1_ref, pe2_ref, w1a_ref, w1b_ref, w2_ref, o_ref):
    rows = r_ref[0]
    p1 = _dot((rows + pe1_ref[...]).astype(BF16), w1a_ref[...])
    p2 = _dot((rows + pe2_ref[...]).astype(BF16), w1b_ref[...])
    nr = rows.shape[0]
    hid = _gelu_tanh(p1 + pltpu.roll(p2, nr - 1, 0))
    o_ref[0] = _dot(hid.astype(BF16), w2_ref[...])


def _nsa_compress(ccmp, B, T, pe1, pe2, w1a, w1b, w2e):
    nr = T // NSA_CMP_STRIDE
    wide = NSA_CMP_STRIDE * 2 * KV_W
    rows = ccmp.reshape(B, nr, wide)
    full = lambda a: pl.BlockSpec(a.shape, lambda b: (0,) * a.ndim)
    return pl.pallas_call(
        _cmp_kernel,
        grid=(B,),
        in_specs=[pl.BlockSpec((1, nr, wide), lambda b: (b, 0, 0))] + [full(a) for a in (pe1, pe2, w1a, w1b, w2e)],
        out_specs=pl.BlockSpec((1, nr, 2 * KV_W), lambda b: (b, 0, 0)),
        out_shape=jax.ShapeDtypeStruct((B, nr, 2 * KV_W), F32),
        compiler_params=_params(("arbitrary",)),
        name="nsa_compress",
    )(rows, pe1, pe2, w1a, w1b, w2e)


def _cmpattn_kernel(q_ref, cmp_ref, g_ref, sm_ref, ov_ref, o_ref, sel_ref):
    i = pl.program_id(1)
    kv = cmp_ref[0]
    nr = kv.shape[0]
    ns = ov_ref.shape[0]
    tq = q_ref.shape[0]
    tpos = i * tq + _iota2((tq, nr), 0)
    cmask = _iota2((tq, nr), 1) * NSA_CMP_STRIDE + (NSA_CMP_BLOCK - 1) <= tpos
    gate = _silu(g_ref[...])
    bgate = _sigmoid(sm_ref[...])
    tblk = (i * tq + _iota2((ns, tq), 1)) // NSA_SEL_BLOCK
    jj = _iota2((ns, tq), 0)
    forced = (jj == 0) | (jj == tblk) | (jj == tblk - 1)
    causal = jj <= tblk
    cbias = jnp.where(cmask, 0.0, NEG_INF)
    for g in range(KV_HEADS):
        kc = kv[:, g * HEAD_DIM:(g + 1) * HEAD_DIM].astype(BF16)
        vc = kv[:, KV_W + g * HEAD_DIM:KV_W + (g + 1) * HEAD_DIM].astype(BF16)
        heads = range(g * HPG, (g + 1) * HPG)
        qg = jnp.concatenate([q_ref[:, h * HEAD_DIM:(h + 1) * HEAD_DIM] for h in heads], axis=0) * SCALE
        s = _dot_nt(qg, kc).reshape(HPG, tq, nr) + cbias[None]
        e = jnp.exp(s - jnp.max(s, -1, keepdims=True))
        p = jnp.where(cmask[None], e / jnp.sum(e, -1, keepdims=True), 0.0)
        psum = jnp.sum(p, axis=0)
        o = _dot(p.reshape(HPG * tq, nr).astype(BF16), vc)
        for j, h in enumerate(heads):
            sl = slice(h * HEAD_DIM, (h + 1) * HEAD_DIM)
            o_ref[:, sl] = o[j * tq:(j + 1) * tq] * bgate[:, SM_GATE + h:SM_GATE + h + 1] * gate[:, sl]
        imp = _dot_nt(ov_ref[...], psum, HI)
        imp = jnp.where(causal, jnp.where(forced, NSA_FORCE, imp), NEG_INF)
        rank = jnp.zeros((ns, tq), jnp.int32)
        for j in range(ns):
            row = imp[j:j + 1, :]
            rank = rank + ((row > imp) | ((row == imp) & (j < jj))).astype(jnp.int32)
        sel_ref[0, g] = jnp.where((rank < NSA_TOPN) & causal, 0.0, NEG_INF)


def _nsa_cmpattn(cb, cmp, cf, B, T, ov):
    tq = AQ_BAND
    nq = T // tq
    nr = cmp.shape[1]
    ns = T // NSA_SEL_BLOCK
    return pl.pallas_call(
        _cmpattn_kernel,
        grid=(B, nq),
        in_specs=[pl.BlockSpec((tq, 512), lambda b, i: (b * nq + i, BF_Q['nsa_q'])),
                  pl.BlockSpec((1, nr, 2 * KV_W), lambda b, i: (b, 0, 0)),
                  pl.BlockSpec((tq, 512), lambda b, i: (b * nq + i, FCOL['nsa_g'])),
                  pl.BlockSpec((tq, SMALL_W), lambda b, i: (b * nq + i, SMALL_BLK)),
                  pl.BlockSpec(ov.shape, lambda b, i: (0, 0))],
        out_specs=[pl.BlockSpec((tq, 512), lambda b, i: (b * nq + i, 0)),
                   pl.BlockSpec((1, KV_HEADS, ns, tq), lambda b, i: (b, 0, 0, i))],
        out_shape=[jax.ShapeDtypeStruct((B * T, D_GROUP), F32),
                   jax.ShapeDtypeStruct((B, KV_HEADS, ns, T), F32)],
        compiler_params=_params(("parallel", "arbitrary")),
        name="nsa_cmpattn",
    )(cb, cmp, cf, cf, ov)


def _out_kernel(ya_ref, yb_ref, c1_ref, c2_ref, c3_ref, yd_ref, x_ref, w_ref, lg_ref, lb_ref, o_ref, ob_ref):
    yc = c1_ref[...] + c2_ref[...] + c3_ref[...]
    acc = _dot(ya_ref[...].astype(BF16), w_ref[0:512, :])
    acc = acc + _dot(yb_ref[...].astype(BF16), w_ref[512:1024, :])
    acc = acc + _dot(yc.astype(BF16), w_ref[1024:1536, :])
    acc = acc + _dot(yd_ref[...].astype(BF16), w_ref[1536:2048, :])
    z = DEEPNORM_ALPHA * x_ref[...] + acc
    mu = jnp.mean(z, -1, keepdims=True)
    var = jnp.mean(jnp.square(z - mu), -1, keepdims=True)
    out = (z - mu) * lax.rsqrt(var + LN_EPS) * lg_ref[...] + lb_ref[...]
    o_ref[...] = out
    ob_ref[...] = out.astype(BF16)


def _out_proj(ys, x, w, lg, lb, tm=256):
    m = x.shape[0]
    yspec = pl.BlockSpec((tm, 512), lambda i: (i, 0))
    xspec = pl.BlockSpec((tm, D_MODEL), lambda i: (i, 0))
    return pl.pallas_call(
        _out_kernel,
        grid=(m // tm,),
        in_specs=[yspec] * 6 + [xspec, pl.BlockSpec(w.shape, lambda i: (0, 0)),
                                pl.BlockSpec(lg.shape, lambda i: (0, 0)), pl.BlockSpec(lb.shape, lambda i: (0, 0))],
        out_specs=[xspec, xspec],
        out_shape=[jax.ShapeDtypeStruct((m, D_MODEL), F32), jax.ShapeDtypeStruct((m, D_MODEL), BF16)],
        compiler_params=_params(("parallel",)),
        name="out_proj_ln",
    )(*ys, x, w, lg, lb)


def _cols(w, name):
    o, s = _OFF[name]
    return w[..., o:o + s]


def _small_row(pieces):
    row = jnp.zeros((1, SMALL_W), F32)
    for off, vals in pieces:
        row = row.at[0, off:off + vals.shape[-1]].set(vals.astype(F32))
    return row


def _overlap_matrix(T):
    nc = T // NSA_CMP_STRIDE
    ns = T // NSA_SEL_BLOCK
    cst = np.arange(nc) * NSA_CMP_STRIDE
    jst = np.arange(ns) * NSA_SEL_BLOCK
    ov = np.clip(np.minimum(cst[:, None] + NSA_CMP_BLOCK, jst[None, :] + NSA_SEL_BLOCK)
                 - np.maximum(cst[:, None], jst[None, :]), 0, None).astype(np.float32) / NSA_CMP_BLOCK
    return jnp.asarray(ov.T)


def _cmp_weights(pe_k, pe_v, k_w1, k_w2, v_w1, v_w2):
    half = NSA_CMP_STRIDE
    hid = NSA_CMP_HIDDEN

    def w1_half(lo):
        blocks = jnp.zeros((half, 4, HEAD_DIM, 4, hid), F32)
        kw = k_w1.reshape(NSA_CMP_BLOCK, HEAD_DIM, hid)[lo:lo + half]
        vw = v_w1.reshape(NSA_CMP_BLOCK, HEAD_DIM, hid)[lo:lo + half]
        for s, wsrc in ((0, kw), (1, kw), (2, vw), (3, vw)):
            blocks = blocks.at[:, s, :, s, :].set(wsrc)
        return blocks.reshape(half * 4 * HEAD_DIM, 4 * hid).astype(BF16)

    def pe_half(lo):
        pk, pv = pe_k[lo:lo + half], pe_v[lo:lo + half]
        return jnp.stack([pk, pk, pv, pv], axis=1).reshape(1, half * 4 * HEAD_DIM)

    w2 = jnp.zeros((4, hid, 4, HEAD_DIM), F32)
    for s, wsrc in ((0, k_w2), (1, k_w2), (2, v_w2), (3, v_w2)):
        w2 = w2.at[s, :, s, :].set(wsrc)
    return pe_half(0), pe_half(half), w1_half(0), w1_half(half), w2.reshape(4 * hid, 4 * HEAD_DIM).astype(BF16)


def _layer(x, xb, B, T, w_in, w_out, ln_g, ln_b, rw_mu, rw_w0, rw_w2, rw_a0, rw_a2, rw_kk, rw_ka, rw_rk,
           rw_gn_g, rw_gn_b, swa_sinks, nsa_pe_k, nsa_pe_v, nsa_k_w1, nsa_k_w2, nsa_v_w1, nsa_v_w2,
           gdn_conv, gdn_A_log, gdn_dt_bias, gdn_norm_g):
    small = jnp.concatenate([_cols(w_in, n) for n in ('rw_wd', 'rw_ad', 'nsa_gate', 'gdn_beta', 'gdn_a')], axis=1)
    small = jnp.pad(small, ((0, 0), (0, SMALL_W - small.shape[1])))
    w_f32 = jnp.concatenate([_cols(w_in, n) for n in F32_GROUPS] + [small], axis=1).astype(BF16)
    w_bf = jnp.concatenate([_cols(w_in, n) for n in ('swa_q', 'nsa_q', 'swa_k', 'swa_v', 'nsa_ks', 'nsa_vs',
                                                     'nsa_kw', 'nsa_vw')], axis=1).astype(BF16)
    w_cmp = jnp.concatenate([_cols(w_in, 'nsa_kc'), _cols(w_in, 'nsa_vc')], axis=1).astype(BF16)
    tm = min(1024, B * T)
    cf = _matmul(xb, w_f32, F32, tm, N_F32 // 7)
    cb = _matmul(xb, w_bf, BF16, tm, N_BF // 2)
    ccmp = _matmul(xb, w_cmp, F32, tm, 256)

    row = lambda a: a.reshape(1, -1).astype(F32)
    mu3 = row(rw_mu[:3 * D_GROUP])
    musm = _small_row([(SM_WD, rw_mu[3 * D_GROUP:3 * D_GROUP + RW_LORA]),
                       (SM_AD, rw_mu[3 * D_GROUP + RW_LORA:])])
    y_a = _rwkv(cf, B, T, mu3, musm, row(rw_w0), rw_w2, row(rw_a0), rw_a2, row(rw_kk), row(rw_ka),
                row(rw_rk), row(rw_gn_g), row(rw_gn_b))
    y_d = _gdn(cf, B, T, gdn_conv, _small_row([(SM_A, gdn_A_log)]), _small_row([(SM_A, gdn_dt_bias)]),
               jnp.tile(row(gdn_norm_g), (1, 2)))
    y_b = _attn(cb, cf, B, T, 'swa_q', 'swa_k', 'swa_v', 'swa_g', window=SWA_WINDOW, sinks=row(swa_sinks))
    cmp = _nsa_compress(ccmp, B, T, *_cmp_weights(nsa_pe_k, nsa_pe_v, nsa_k_w1, nsa_k_w2, nsa_v_w1, nsa_v_w2))
    c1, sel = _nsa_cmpattn(cb, cmp, cf, B, T, _overlap_matrix(T))
    c2 = _attn(cb, cf, B, T, 'nsa_q', 'nsa_ks', 'nsa_vs', 'nsa_g', sel=sel, gate_col=SM_GATE + N_HEADS, tk=4 * TQ,
               aq=AQ_SEL)
    c3 = _attn(cb, cf, B, T, 'nsa_q', 'nsa_kw', 'nsa_vw', 'nsa_g', window=NSA_WINDOW,
               gate_col=SM_GATE + 2 * N_HEADS)
    return _out_proj((y_a, y_b, c1, c2, c3, y_d), x, w_out.astype(BF16), row(ln_g), row(ln_b))


def kernel(x, w_in, w_out, ln_g, ln_b, rw_mu, rw_w0, rw_w2, rw_a0, rw_a2, rw_kk, rw_ka, rw_rk, rw_gn_g, rw_gn_b, swa_sinks, nsa_pe_k, nsa_pe_v, nsa_k_w1, nsa_k_w2, nsa_v_w1, nsa_v_w2, gdn_conv, gdn_A_log, gdn_dt_bias, gdn_norm_g):
    B, T, D = x.shape
    params = (w_in, w_out, ln_g, ln_b, rw_mu, rw_w0, rw_w2, rw_a0, rw_a2, rw_kk, rw_ka, rw_rk, rw_gn_g, rw_gn_b,
              swa_sinks, nsa_pe_k, nsa_pe_v, nsa_k_w1, nsa_k_w2, nsa_v_w1, nsa_v_w2, gdn_conv, gdn_A_log,
              gdn_dt_bias, gdn_norm_g)
    xf = x.reshape(B * T, D)
    xb = xf.astype(BF16)
    for i in range(w_in.shape[0]):
        xf, xb = _layer(xf, xb, B, T, *(p[i] for p in params))
    return xf.reshape(B, T, D)
```

```python
import functools
import math

import numpy as np
import jax
import jax.numpy as jnp
from jax import lax
from jax.experimental import pallas as pl
from jax.experimental.pallas import tpu as pltpu

F32 = jnp.float32
BF16 = jnp.bfloat16
HI = lax.Precision.HIGHEST

D_MODEL = 2048
DEPTH = 4
D_GROUP = 512
HEAD_DIM = 64
N_HEADS = 8
KV_HEADS = 2
HPG = N_HEADS // KV_HEADS
PAIR = 2 * HEAD_DIM
KV_W = KV_HEADS * HEAD_DIM
NEG_INF = -1e30
LN_EPS = 1e-5
DEEPNORM_ALPHA = (2 * DEPTH) ** 0.25
RW_LORA = 32
RW_GN_EPS = 64e-5
SWA_WINDOW = 128
NSA_CMP_BLOCK = 32
NSA_CMP_STRIDE = 16
NSA_CMP_HIDDEN = 128
NSA_SEL_BLOCK = 64
NSA_TOPN = 16
NSA_WINDOW = 512
NSA_FORCE = 1e6
GDN_CONV = 4
GDN_EPS = 1e-6
SCALE = HEAD_DIM ** -0.5

CHUNK = 64
RW_SEQS = 4
GDN_SEQS = 4
TQ = 128
AQ_BAND = 256
AQ_SEL = 512
TAIL = 8
VT_ROWS = HEAD_DIM + 16
VMEM_LIMIT = 56 * 1024 * 1024

_COLS = (
    ('rw_r', 512), ('rw_k', 512), ('rw_v', 512), ('rw_wd', 32), ('rw_ad', 32), ('rw_g', 512),
    ('swa_q', 512), ('swa_k', 128), ('swa_v', 128), ('swa_g', 512),
    ('nsa_q', 512), ('nsa_kc', 128), ('nsa_vc', 128), ('nsa_ks', 128), ('nsa_vs', 128),
    ('nsa_kw', 128), ('nsa_vw', 128), ('nsa_gate', 24), ('nsa_g', 512),
    ('gdn_q', 512), ('gdn_k', 512), ('gdn_v', 512), ('gdn_beta', 8), ('gdn_a', 8), ('gdn_g', 512),
)
_OFF = {}
_o = 0
for _n, _s in _COLS:
    _OFF[_n] = (_o, _s)
    _o += _s
N_IN = _o

F32_GROUPS = ('rw_r', 'rw_k', 'rw_v', 'rw_g', 'gdn_q', 'gdn_k', 'gdn_v', 'gdn_g', 'swa_g', 'nsa_g')
FCOL = {n: i for i, n in enumerate(F32_GROUPS)}
SMALL_W = 256
SMALL_BLK = len(F32_GROUPS) * 512 // SMALL_W
SM_WD, SM_AD, SM_GATE, SM_BETA, SM_A = 0, 32, 64, 88, 96
N_F32 = len(F32_GROUPS) * 512 + SMALL_W
BF_Q = {'swa_q': 0, 'nsa_q': 1}
BF_KV = {n: 8 + i for i, n in enumerate(('swa_k', 'swa_v', 'nsa_ks', 'nsa_vs', 'nsa_kw', 'nsa_vw'))}
N_BF = 1024 + 6 * 128


def _dot(a, b, prec=None):
    return lax.dot_general(a, b, (((1,), (0,)), ((), ())), precision=prec, preferred_element_type=F32)


def _dot_nt(a, b, prec=None):
    return lax.dot_general(a, b, (((1,), (1,)), ((), ())), precision=prec, preferred_element_type=F32)


def _dot_tn(a, b, prec=None):
    return lax.dot_general(a, b, (((0,), (0,)), ((), ())), precision=prec, preferred_element_type=F32)


def _sigmoid(x):
    return 1.0 / (1.0 + jnp.exp(-x))


def _silu(x):
    return x * _sigmoid(x)


def _softplus(x):
    return jnp.maximum(x, 0.0) + jnp.log(1.0 + jnp.exp(-jnp.abs(x)))


def _iota2(shape, dim):
    return lax.broadcasted_iota(jnp.int32, shape, dim)


def _pairs(t):
    return [t[:, p * PAIR:(p + 1) * PAIR] for p in range(N_HEADS // 2)]


def _lo_mask(rows):
    return _iota2((rows, PAIR), 1) < HEAD_DIM


def _bd(x):
    xb = x.astype(BF16)
    lo = _lo_mask(x.shape[0]).astype(BF16)
    return jnp.concatenate([xb * lo, xb * (1 - lo)], axis=0)


def _pdot(a, b):
    return _dot(a.astype(BF16), _bd(b))


def _pdot_nt(a, b):
    return _dot_nt(a.astype(BF16), _bd(b))


def _pdot_tn(a, b):
    full = _dot_tn(a.astype(BF16), b.astype(BF16))
    return jnp.where(_lo_mask(HEAD_DIM), full[:HEAD_DIM], full[HEAD_DIM:])


def _segsum(t):
    lo = _lo_mask(t.shape[0])
    s_lo = jnp.sum(jnp.where(lo, t, 0.0), -1, keepdims=True)
    s_hi = jnp.sum(jnp.where(lo, 0.0, t), -1, keepdims=True)
    return jnp.where(lo, s_lo, s_hi)


def _inv_unit_lower(xs):
    n = xs[0].shape[0]
    eye2 = ((_iota2((n, 2 * n), 1) & (n - 1)) == _iota2((n, 2 * n), 0)).astype(F32)
    ps = [eye2 + x for x in xs]
    for _ in range(int(math.log2(n)) - 1):
        xs = [_pdot(x, x) for x in xs]
        ps = [p + _pdot(x, p) for x, p in zip(xs, ps)]
    return ps


def _tri_masks2(n):
    ri = _iota2((n, 2 * n), 0)
    ci = _iota2((n, 2 * n), 1) & (n - 1)
    return ri >= ci, ri > ci


def _params(sem):
    return pltpu.CompilerParams(dimension_semantics=sem, vmem_limit_bytes=VMEM_LIMIT)


def _mm_kernel(x_ref, w_ref, o_ref):
    o_ref[...] = jnp.dot(x_ref[...], w_ref[...], preferred_element_type=F32).astype(o_ref.dtype)


def _matmul(x, w, out_dtype, tm, tn):
    m, k = x.shape
    n = w.shape[1]
    return pl.pallas_call(
        _mm_kernel,
        grid=(m // tm, n // tn),
        in_specs=[pl.BlockSpec((tm, k), lambda i, j: (i, 0)),
                  pl.BlockSpec((k, tn), lambda i, j: (0, j))],
        out_specs=pl.BlockSpec((tm, tn), lambda i, j: (i, j)),
        out_shape=jax.ShapeDtypeStruct((m, n), out_dtype),
        compiler_params=_params(("parallel", "arbitrary")),
        name="proj_in",
    )(x, w)


def _rwkv_kernel(r_ref, k_ref, v_ref, g_ref, sm_ref, mu_ref, musm_ref, w0_ref, w2_ref, a0_ref, a2_ref,
                 kk_ref, ka_ref, rk_ref, gng_ref, gnb_ref, o_ref, buf_ref, bufsm_ref, s_ref):
    c = pl.program_id(1)
    C = CHUNK

    NB = r_ref.shape[0]

    @pl.when(c == 0)
    def _():
        buf_ref[:, 0:TAIL, :] = jnp.zeros((NB, TAIL, 3 * D_GROUP), F32)
        bufsm_ref[:, 0:TAIL, :] = jnp.zeros((NB, TAIL, SMALL_W), F32)
        s_ref[...] = jnp.zeros(s_ref.shape, F32)

    ri = _iota2((C, C), 0)
    ci = _iota2((C, C), 1)
    tril = ri >= ci
    strict = ri > ci
    rk = rk_ref[...]
    gng = gng_ref[...]
    gnb = gnb_ref[...]
    pre = []
    for nb in range(NB):
        buf_ref[nb, TAIL:TAIL + C, 0:512] = r_ref[nb]
        buf_ref[nb, TAIL:TAIL + C, 512:1024] = k_ref[nb]
        buf_ref[nb, TAIL:TAIL + C, 1024:1536] = v_ref[nb]
        bufsm_ref[nb, TAIL:TAIL + C, :] = sm_ref[nb]
        cur = buf_ref[nb, TAIL:TAIL + C, :]
        prev = buf_ref[nb, TAIL - 1:TAIL - 1 + C, :]
        mixed = cur + (prev - cur) * mu_ref[...]
        cur_sm = bufsm_ref[nb, TAIL:TAIL + C, :]
        prev_sm = bufsm_ref[nb, TAIL - 1:TAIL - 1 + C, :]
        smix = cur_sm + (prev_sm - cur_sm) * musm_ref[...]
        buf_ref[nb, 0:TAIL, :] = buf_ref[nb, C:C + TAIL, :]
        bufsm_ref[nb, 0:TAIL, :] = bufsm_ref[nb, C:C + TAIL, :]

        r = mixed[:, 0:512]
        k = mixed[:, 512:1024]
        v = mixed[:, 1024:1536]
        wd = smix[:, SM_WD:SM_WD + RW_LORA]
        ad = smix[:, SM_AD:SM_AD + RW_LORA]
        wl = w0_ref[...] + _dot(jnp.tanh(wd), w2_ref[...], HI)
        logw = -jnp.exp(-_softplus(-wl) - 0.5)
        alpha = _sigmoid(a0_ref[...] + _dot(ad, a2_ref[...], HI))
        kkraw = k * kk_ref[...]
        k2 = k * (1.0 + (alpha - 1.0) * ka_ref[...])
        gcum = _dot(tril.astype(F32), logw, HI)
        pre.append((r, k2, v, alpha, kkraw, logw, gcum, _silu(g_ref[nb])))

    H = range(NB * N_HEADS // 2)
    tril2, strict2 = _tri_masks2(C)
    R, K2, V, AL, KKR, LW, G = ([t for p in pre for t in _pairs(p[f])] for f in range(7))
    KK = [t * lax.rsqrt(_segsum(t * t) + 1e-6) for t in KKR]
    GM = [t[C // 2 - 1:C // 2, :] for t in G]
    GE = [t[C - 1:C, :] for t in G]
    BV = [KK[h] * AL[h] for h in H]
    EINV = [jnp.exp(GM[h] - G[h]) for h in H]
    EEND = [jnp.exp(GE[h] - G[h]) for h in H]
    LEFT = [jnp.concatenate([-KK[h] * jnp.exp(G[h] - LW[h] - GM[h]), R[h] * jnp.exp(G[h] - GM[h])], axis=0)
            for h in H]
    AB = [_pdot_nt(LEFT[h], BV[h] * EINV[h]) for h in H]
    AK = [_pdot_nt(LEFT[h], K2[h] * EINV[h]) for h in H]
    A_RB = [jnp.where(tril2, t[C:], 0.0) for t in AB]
    A_K = [jnp.concatenate([jnp.where(strict2, t[:C], 0.0), jnp.where(tril2, t[C:], 0.0)], axis=0) for t in AK]
    TINV = _inv_unit_lower([jnp.where(strict2, t[:C], 0.0) for t in AB])
    S = [s_ref[h] for h in H]
    LS = [_pdot_nt(LEFT[h], S[h] * jnp.exp(GM[h])) for h in H]
    AV = [_pdot(A_K[h], V[h]) for h in H]
    U = [_pdot(TINV[h], LS[h][:C] + AV[h][:C]) for h in H]
    O = [LS[h][C:] + _pdot(A_RB[h], U[h]) + AV[h][C:] for h in H]
    for h in H:
        s_ref[h] = S[h] * jnp.exp(GE[h]) + _pdot_tn(
            jnp.concatenate([U[h], V[h]], axis=0),
            jnp.concatenate([BV[h] * EEND[h], K2[h] * EEND[h]], axis=0))
    for n in H:
        nb, p = divmod(n, N_HEADS // 2)
        sl = slice(p * PAIR, (p + 1) * PAIR)
        mu = _segsum(O[n]) * (1.0 / HEAD_DIM)
        d = O[n] - mu
        var = _segsum(d * d) * (1.0 / HEAD_DIM)
        y = d * lax.rsqrt(var + RW_GN_EPS) * gng[:, sl] + gnb[:, sl]
        bonus = _segsum(R[n] * K2[n] * rk[:, sl]) * V[n]
        o_ref[nb, :, sl] = ((y + bonus) * pre[nb][7][:, sl]).astype(o_ref.dtype)


def _seq_specs(names, nbs):
    specs = [pl.BlockSpec((nbs, CHUNK, 512), lambda b, c, j=FCOL[n]: (b, c, j)) for n in names]
    return specs + [pl.BlockSpec((nbs, CHUNK, SMALL_W), lambda b, c: (b, c, SMALL_BLK))]


def _rwkv(cf, B, T, mu3, musm, w0, w2, a0, a2, kk, ka, rk, gng, gnb):
    nbs = RW_SEQS if B % RW_SEQS == 0 else 1
    cf3 = cf.reshape(B, T, cf.shape[-1])
    full = lambda a: pl.BlockSpec(a.shape, lambda b, c: (0,) * a.ndim)
    ps = (mu3, musm, w0, w2, a0, a2, kk, ka, rk, gng, gnb)
    out = pl.pallas_call(
        _rwkv_kernel,
        grid=(B // nbs, T // CHUNK),
        in_specs=_seq_specs(('rw_r', 'rw_k', 'rw_v', 'rw_g'), nbs) + [full(a) for a in ps],
        out_specs=pl.BlockSpec((nbs, CHUNK, 512), lambda b, c: (b, c, 0)),
        out_shape=jax.ShapeDtypeStruct((B, T, D_GROUP), BF16),
        scratch_shapes=[pltpu.VMEM((nbs, TAIL + CHUNK, 3 * D_GROUP), F32),
                        pltpu.VMEM((nbs, TAIL + CHUNK, SMALL_W), F32),
                        pltpu.VMEM((nbs * N_HEADS // 2, HEAD_DIM, PAIR), F32)],
        compiler_params=_params(("parallel", "arbitrary")),
        name="rwkv7",
    )(cf3, cf3, cf3, cf3, cf3, *ps)
    return out.reshape(B * T, D_GROUP)


def _gdn_kernel(q_ref, k_ref, v_ref, z_ref, sm_ref, conv_ref, alog_ref, dtb_ref, ng_ref, o_ref,
                buf_ref, s_ref):
    c = pl.program_id(1)
    C = CHUNK

    NB = q_ref.shape[0]

    @pl.when(c == 0)
    def _():
        buf_ref[:, 0:TAIL, :] = jnp.zeros((NB, TAIL, 3 * D_GROUP), F32)
        s_ref[...] = jnp.zeros(s_ref.shape, F32)

    conv = conv_ref[...]
    lane = _iota2((C, SMALL_W), 1)
    ri = _iota2((C, C), 0)
    ci = _iota2((C, C), 1)
    tril = ri >= ci
    strict = ri > ci
    ng = ng_ref[...]
    pre = []
    for nb in range(NB):
        buf_ref[nb, TAIL:TAIL + C, 0:512] = q_ref[nb]
        buf_ref[nb, TAIL:TAIL + C, 512:1024] = k_ref[nb]
        buf_ref[nb, TAIL:TAIL + C, 1024:1536] = v_ref[nb]
        acc = buf_ref[nb, TAIL:TAIL + C, :] * conv[GDN_CONV - 1:GDN_CONV, :]
        for i in range(GDN_CONV - 1):
            sh = GDN_CONV - 1 - i
            acc = acc + buf_ref[nb, TAIL - sh:TAIL - sh + C, :] * conv[i:i + 1, :]
        buf_ref[nb, 0:TAIL, :] = buf_ref[nb, C:C + TAIL, :]
        qkv = _silu(acc)
        sm = sm_ref[nb]
        g_all = jnp.where((lane >= SM_A) & (lane < SM_A + N_HEADS),
                          -jnp.exp(alog_ref[...]) * _softplus(sm + dtb_ref[...]), 0.0)
        gam_all = _dot(tril.astype(F32), g_all, HI)
        pre.append((qkv[:, 0:512], qkv[:, 512:1024], qkv[:, 1024:1536], _sigmoid(sm), gam_all, _silu(z_ref[nb])))

    H = range(NB * N_HEADS // 2)
    hd = lambda n: divmod(n, N_HEADS // 2)
    lo = _lo_mask(C)
    tril2, strict2 = _tri_masks2(C)
    diag2 = (_iota2((C, PAIR), 1) & (HEAD_DIM - 1)) == _iota2((C, PAIR), 0)

    def per_head(n, field, col):
        nb, p = hd(n)
        t = pre[nb][field]
        return jnp.where(lo, t[:, col + 2 * p:col + 2 * p + 1], t[:, col + 2 * p + 1:col + 2 * p + 2])

    Q = [t * lax.rsqrt(_segsum(t * t) + 1e-6) * SCALE for p in pre for t in _pairs(p[0])]
    K = [t * lax.rsqrt(_segsum(t * t) + 1e-6) for p in pre for t in _pairs(p[1])]
    V = [t for p in pre for t in _pairs(p[2])]
    BETA = [per_head(n, 3, SM_BETA) for n in H]
    GAM = [per_head(n, 4, SM_A) for n in H]
    GROW = [jnp.sum(jnp.where(diag2, t, 0.0), 0, keepdims=True) for t in GAM]
    DECAY = [jnp.exp(jnp.where(tril2, GAM[n] - GROW[n], NEG_INF)) for n in H]
    EG = [jnp.exp(t) for t in GAM]
    KB = [K[h] * BETA[h] for h in H]
    KKQ = [_pdot_nt(jnp.concatenate([KB[h], Q[h]], axis=0), K[h]) for h in H]
    QK = [KKQ[h][C:] * DECAY[h] for h in H]
    TM = _inv_unit_lower([-jnp.where(strict2, KKQ[h][:C] * DECAY[h], 0.0) for h in H])
    UU = [_pdot(TM[h], V[h] * BETA[h]) for h in H]
    W = [_pdot(TM[h], KB[h] * EG[h]) for h in H]
    S = [s_ref[h] for h in H]
    WS = [_pdot(jnp.concatenate([W[h], Q[h] * EG[h]], axis=0), S[h]) for h in H]
    VN = [UU[h] - WS[h][:C] for h in H]
    O = [WS[h][C:] + _pdot(QK[h], VN[h]) for h in H]
    for h in H:
        g_last = GAM[h][C - 1:C, :]
        s_ref[h] = S[h] * jnp.exp(g_last) + _pdot_tn(K[h] * jnp.exp(g_last - GAM[h]), VN[h])
    for n in H:
        nb, p = hd(n)
        sl = slice(p * PAIR, (p + 1) * PAIR)
        o = O[n] * lax.rsqrt(_segsum(O[n] * O[n]) * (1.0 / HEAD_DIM) + GDN_EPS) * ng
        o_ref[nb, :, sl] = (o * pre[nb][5][:, sl]).astype(o_ref.dtype)


def _gdn(cf, B, T, conv, alog_sm, dtb_sm, ng):
    nbs = GDN_SEQS if B % GDN_SEQS == 0 else 1
    cf3 = cf.reshape(B, T, cf.shape[-1])
    full = lambda a: pl.BlockSpec(a.shape, lambda b, c: (0,) * a.ndim)
    ps = (conv, alog_sm, dtb_sm, ng)
    out = pl.pallas_call(
        _gdn_kernel,
        grid=(B // nbs, T // CHUNK),
        in_specs=_seq_specs(('gdn_q', 'gdn_k', 'gdn_v', 'gdn_g'), nbs) + [full(a) for a in ps],
        out_specs=pl.BlockSpec((nbs, CHUNK, 512), lambda b, c: (b, c, 0)),
        out_shape=jax.ShapeDtypeStruct((B, T, D_GROUP), BF16),
        scratch_shapes=[pltpu.VMEM((nbs, TAIL + CHUNK, 3 * D_GROUP), F32),
                        pltpu.VMEM((nbs * N_HEADS // 2, HEAD_DIM, PAIR), F32)],
        compiler_params=_params(("parallel", "arbitrary")),
        name="gdn",
    )(cf3, cf3, cf3, cf3, cf3, *ps)
    return out.reshape(B * T, D_GROUP)


def _store_heads(o_ref, g, ot, gate, bgate, gate_col):
    aq = o_ref.shape[0]
    for j in range(HPG):
        h = g * HPG + j
        sl = slice(h * HEAD_DIM, (h + 1) * HEAD_DIM)
        oh = ot[:, j * aq:(j + 1) * aq].T
        if bgate is not None:
            oh = oh * bgate[:, gate_col + h:gate_col + h + 1]
        o_ref[:, sl] = oh * gate[:, sl]


def _attn_kernel(*refs, window, use_sel, use_sink, gate_col, tk):
    it = iter(refs)
    q_ref, k_ref, v_ref, g_ref = next(it), next(it), next(it), next(it)
    sm_ref = next(it) if gate_col is not None else None
    sel_ref = next(it) if use_sel else None
    sink_ref = next(it) if use_sink else None
    o_ref = next(it)
    vt_ref = next(it)
    selx_ref = next(it) if use_sel else None
    i = pl.program_id(1)
    aq = q_ref.shape[0]
    W = HPG * aq

    @pl.when(i == 0)
    def _():
        eye = (_iota2((KV_W, KV_W), 0) == _iota2((KV_W, KV_W), 1)).astype(BF16)
        ones = jnp.ones((VT_ROWS - HEAD_DIM, tk), BF16)
        for c in range(vt_ref.shape[0]):
            vt = _dot_nt(eye, v_ref[c * tk:(c + 1) * tk, :]).astype(BF16)
            for g in range(KV_HEADS):
                vt_ref[c, g] = jnp.concatenate([vt[g * HEAD_DIM:(g + 1) * HEAD_DIM], ones], axis=0)

    span = tk if window is None else window + aq
    qpos = i * aq + (_iota2((span, W), 1) & (aq - 1))
    krow = _iota2((span, W), 0)
    gate = _silu(g_ref[...])
    if gate_col is not None:
        bgate = _sigmoid(sm_ref[...])

    G = range(KV_HEADS)
    QG = [jnp.concatenate([q_ref[:, h * HEAD_DIM:(h + 1) * HEAD_DIM] for h in range(g * HPG, (g + 1) * HPG)],
                          axis=0) * SCALE for g in G]
    if use_sel:
        for g in G:
            selx_ref[g] = jnp.concatenate([sel_ref[0, g]] * HPG, axis=1)
        nb = tk // NSA_SEL_BLOCK

    if window is not None:
        nt = span // tk
        t0 = jnp.maximum(i * aq - window, 0) // tk
        start = pl.multiple_of(t0 * tk, tk)
        rel = qpos - (start + krow)
        pbias = jnp.where((rel >= 0) & (rel < window), 0.0, NEG_INF)
        for g in G:
            ksl = slice(g * HEAD_DIM, (g + 1) * HEAD_DIM)
            s = _dot_nt(k_ref[pl.ds(start, span), ksl], QG[g]) + pbias
            m = jnp.max(s, 0, keepdims=True)
            if use_sink:
                sink = jnp.concatenate([jnp.zeros((1, aq), F32) + sink_ref[0:1, h:h + 1]
                                        for h in range(g * HPG, (g + 1) * HPG)], axis=1)
                m = jnp.maximum(m, sink)
            p = jnp.exp((s - m).astype(BF16))
            vt = jnp.concatenate([vt_ref[t0 + j, g] for j in range(nt)], axis=1)
            pv = _dot(vt, p)
            l = pv[HEAD_DIM:HEAD_DIM + 1]
            if use_sink:
                l = l + jnp.exp(sink - m)
            _store_heads(o_ref, g, pv[:HEAD_DIM] / l, gate, bgate if gate_col is not None else None, gate_col)
        return

    hi = (i * aq + aq + tk - 1) // tk

    def body(kt, carry, masked):
        off = pl.multiple_of(kt * tk, tk)
        if masked:
            pbias = jnp.where(qpos - (kt * tk + krow) >= 0, 0.0, NEG_INF)
        out = []
        for g in G:
            m, acc = carry[2 * g:2 * g + 2]
            ksl = slice(g * HEAD_DIM, (g + 1) * HEAD_DIM)
            s = _dot_nt(k_ref[pl.ds(off, tk), ksl], QG[g])
            if masked:
                s = s + pbias
            if use_sel:
                s = s + jnp.concatenate(
                    [jnp.broadcast_to(selx_ref[g, pl.ds(kt * nb + j, 1), :], (NSA_SEL_BLOCK, W))
                     for j in range(nb)], axis=0)
            m_new = jnp.maximum(m, jnp.max(s, 0, keepdims=True))
            a = jnp.exp(m - m_new)
            p = jnp.exp((s - m_new).astype(BF16))
            out += [m_new, a * acc + _dot(vt_ref[kt, g], p)]
        return tuple(out)

    init = []
    for g in G:
        init += [jnp.full((1, W), NEG_INF, F32), jnp.zeros((VT_ROWS, W), F32)]
    res = lax.fori_loop(0, hi - 1, functools.partial(body, masked=False), tuple(init))
    res = body(hi - 1, res, True)
    for g in G:
        acc = res[2 * g + 1]
        _store_heads(o_ref, g, acc[:HEAD_DIM] / acc[HEAD_DIM:HEAD_DIM + 1], gate,
                     bgate if gate_col is not None else None, gate_col)


def _attn(cb, cf, B, T, qname, kname, vname, gname, *, window=None, sel=None, sinks=None, gate_col=None, tk=TQ,
          aq=AQ_BAND):
    nq = T // aq
    in_specs = [pl.BlockSpec((aq, 512), lambda b, i, j=BF_Q[qname]: (b * nq + i, j)),
                pl.BlockSpec((T, KV_W), lambda b, i, j=BF_KV[kname]: (b, j)),
                pl.BlockSpec((T, KV_W), lambda b, i, j=BF_KV[vname]: (b, j)),
                pl.BlockSpec((aq, 512), lambda b, i, j=FCOL[gname]: (b * nq + i, j))]
    args = [cb, cb, cb, cf]
    if gate_col is not None:
        in_specs.append(pl.BlockSpec((aq, SMALL_W), lambda b, i: (b * nq + i, SMALL_BLK)))
        args.append(cf)
    scratch = [pltpu.VMEM((T // tk, KV_HEADS, VT_ROWS, tk), BF16)]
    if sel is not None:
        in_specs.append(pl.BlockSpec((1, KV_HEADS, sel.shape[2], aq), lambda b, i: (b, 0, 0, i)))
        args.append(sel)
        scratch.append(pltpu.VMEM((KV_HEADS, sel.shape[2], HPG * aq), F32))
    if sinks is not None:
        in_specs.append(pl.BlockSpec(sinks.shape, lambda b, i: (0, 0)))
        args.append(sinks)
    return pl.pallas_call(
        functools.partial(_attn_kernel, window=window, use_sel=sel is not None,
                          use_sink=sinks is not None, gate_col=gate_col, tk=tk),
        grid=(B, nq),
        in_specs=in_specs,
        out_specs=pl.BlockSpec((aq, 512), lambda b, i: (b * nq + i, 0)),
        out_shape=jax.ShapeDtypeStruct((B * T, D_GROUP), F32),
        scratch_shapes=scratch,
        compiler_params=_params(("parallel", "arbitrary")),
        name="attn_" + qname + ("_sel" if sel is not None else "_w%d" % window),
    )(*args)


def _gelu_tanh(x):
    return 0.5 * x * (1.0 + jnp.tanh(math.sqrt(2.0 / math.pi) * (x + 0.044715 * x * x * x)))


def _cmp_kernel(kc_ref, vc_ref, pe_ref, w1_ref, w2_ref, o_ref):
    nr = o_ref.shape[1]
    outs = []
    for s, ref in enumerate((kc_ref, vc_ref)):
        p1 = jnp.zeros((nr, 2 * NSA_CMP_HIDDEN), F32)
        p2 = jnp.zeros((nr, 2 * NSA_CMP_HIDDEN), F32)
        for l in range(NSA_CMP_STRIDE):
            x = ref[pl.ds(l, nr, stride=NSA_CMP_STRIDE), :]
            p1 = p1 + _dot((x + pe_ref[s, l]).astype(BF16), w1_ref[s, l])
            p2 = p2 + _dot((x + pe_ref[s, NSA_CMP_STRIDE + l]).astype(BF16), w1_ref[s, NSA_CMP_STRIDE + l])
        hid = _gelu_tanh(p1 + pltpu.roll(p2, nr - 1, 0))
        outs.append(_dot(hid.astype(BF16), w2_ref[s]))
    o_ref[0] = jnp.concatenate(outs, axis=1)


def _nsa_compress(ccmp, B, T, pe, w1, w2):
    nr = T // NSA_CMP_STRIDE
    full = lambda a: pl.BlockSpec(a.shape, lambda b: (0,) * a.ndim)
    return pl.pallas_call(
        _cmp_kernel,
        grid=(B,),
        in_specs=[pl.BlockSpec((T, KV_W), lambda b: (b, 0)), pl.BlockSpec((T, KV_W), lambda b: (b, 1))]
                 + [full(a) for a in (pe, w1, w2)],
        out_specs=pl.BlockSpec((1, nr, 2 * KV_W), lambda b: (b, 0, 0)),
        out_shape=jax.ShapeDtypeStruct((B, nr, 2 * KV_W), F32),
        compiler_params=_params(("arbitrary",)),
        name="nsa_compress",
    )(ccmp, ccmp, pe, w1, w2)


def _cmpattn_kernel(q_ref, cmp_ref, g_ref, sm_ref, ov_ref, o_ref, sel_ref):
    i = pl.program_id(1)
    kv = cmp_ref[0]
    nr = kv.shape[0]
    ns = ov_ref.shape[0]
    tq = q_ref.shape[0]
    tpos = i * tq + _iota2((tq, nr), 0)
    cmask = _iota2((tq, nr), 1) * NSA_CMP_STRIDE + (NSA_CMP_BLOCK - 1) <= tpos
    gate = _silu(g_ref[...])
    bgate = _sigmoid(sm_ref[...])
    tblk = (i * tq + _iota2((ns, tq), 1)) // NSA_SEL_BLOCK
    jj = _iota2((ns, tq), 0)
    forced = (jj == 0) | (jj == tblk) | (jj == tblk - 1)
    causal = jj <= tblk
    cbias = jnp.where(cmask, 0.0, NEG_INF)
    for g in range(KV_HEADS):
        kc = kv[:, g * HEAD_DIM:(g + 1) * HEAD_DIM].astype(BF16)
        vc = kv[:, KV_W + g * HEAD_DIM:KV_W + (g + 1) * HEAD_DIM].astype(BF16)
        heads = range(g * HPG, (g + 1) * HPG)
        qg = jnp.concatenate([q_ref[:, h * HEAD_DIM:(h + 1) * HEAD_DIM] for h in heads], axis=0) * SCALE
        s = _dot_nt(qg, kc).reshape(HPG, tq, nr) + cbias[None]
        e = jnp.exp(s - jnp.max(s, -1, keepdims=True))
        p = jnp.where(cmask[None], e / jnp.sum(e, -1, keepdims=True), 0.0)
        psum = jnp.sum(p, axis=0)
        o = _dot(p.reshape(HPG * tq, nr).astype(BF16), vc)
        for j, h in enumerate(heads):
            sl = slice(h * HEAD_DIM, (h + 1) * HEAD_DIM)
            o_ref[:, sl] = o[j * tq:(j + 1) * tq] * bgate[:, SM_GATE + h:SM_GATE + h + 1] * gate[:, sl]
        imp = _dot_nt(ov_ref[...], psum, HI)
        imp = jnp.where(causal, jnp.where(forced, NSA_FORCE, imp), NEG_INF)
        rank = jnp.zeros((ns, tq), jnp.int32)
        for j in range(ns):
            row = imp[j:j + 1, :]
            rank = rank + ((row > imp) | ((row == imp) & (j < jj))).astype(jnp.int32)
        sel_ref[0, g] = jnp.where((rank < NSA_TOPN) & causal, 0.0, NEG_INF)


def _nsa_cmpattn(cb, cmp, cf, B, T, ov):
    tq = AQ_BAND
    nq = T // tq
    nr = cmp.shape[1]
    ns = T // NSA_SEL_BLOCK
    return pl.pallas_call(
        _cmpattn_kernel,
        grid=(B, nq),
        in_specs=[pl.BlockSpec((tq, 512), lambda b, i: (b * nq + i, BF_Q['nsa_q'])),
                  pl.BlockSpec((1, nr, 2 * KV_W), lambda b, i: (b, 0, 0)),
                  pl.BlockSpec((tq, 512), lambda b, i: (b * nq + i, FCOL['nsa_g'])),
                  pl.BlockSpec((tq, SMALL_W), lambda b, i: (b * nq + i, SMALL_BLK)),
                  pl.BlockSpec(ov.shape, lambda b, i: (0, 0))],
        out_specs=[pl.BlockSpec((tq, 512), lambda b, i: (b * nq + i, 0)),
                   pl.BlockSpec((1, KV_HEADS, ns, tq), lambda b, i: (b, 0, 0, i))],
        out_shape=[jax.ShapeDtypeStruct((B * T, D_GROUP), F32),
                   jax.ShapeDtypeStruct((B, KV_HEADS, ns, T), F32)],
        compiler_params=_params(("parallel", "arbitrary")),
        name="nsa_cmpattn",
    )(cb, cmp, cf, cf, ov)


def _out_kernel(ya_ref, yb_ref, c1_ref, c2_ref, c3_ref, yd_ref, x_ref, w_ref, lg_ref, lb_ref, o_ref, ob_ref):
    yc = c1_ref[...] + c2_ref[...] + c3_ref[...]
    acc = _dot(ya_ref[...], w_ref[0:512, :])
    acc = acc + _dot(yb_ref[...].astype(BF16), w_ref[512:1024, :])
    acc = acc + _dot(yc.astype(BF16), w_ref[1024:1536, :])
    acc = acc + _dot(yd_ref[...], w_ref[1536:2048, :])
    z = DEEPNORM_ALPHA * x_ref[...] + acc
    mu = jnp.mean(z, -1, keepdims=True)
    var = jnp.mean(jnp.square(z - mu), -1, keepdims=True)
    out = (z - mu) * lax.rsqrt(var + LN_EPS) * lg_ref[...] + lb_ref[...]
    o_ref[...] = out
    ob_ref[...] = out.astype(BF16)


def _out_proj(ys, x, w, lg, lb, tm=256):
    m = x.shape[0]
    yspec = pl.BlockSpec((tm, 512), lambda i: (i, 0))
    xspec = pl.BlockSpec((tm, D_MODEL), lambda i: (i, 0))
    return pl.pallas_call(
        _out_kernel,
        grid=(m // tm,),
        in_specs=[yspec] * 6 + [xspec, pl.BlockSpec(w.shape, lambda i: (0, 0)),
                                pl.BlockSpec(lg.shape, lambda i: (0, 0)), pl.BlockSpec(lb.shape, lambda i: (0, 0))],
        out_specs=[xspec, xspec],
        out_shape=[jax.ShapeDtypeStruct((m, D_MODEL), F32), jax.ShapeDtypeStruct((m, D_MODEL), BF16)],
        compiler_params=_params(("parallel",)),
        name="out_proj_ln",
    )(*ys, x, w, lg, lb)


def _cols(w, name):
    o, s = _OFF[name]
    return w[..., o:o + s]


def _small_row(pieces):
    parts, pos = [], 0
    for off, vals in pieces:
        parts += [jnp.zeros((off - pos,), F32), vals.astype(F32)]
        pos = off + vals.shape[-1]
    return jnp.concatenate(parts + [jnp.zeros((SMALL_W - pos,), F32)]).reshape(1, SMALL_W)


def _overlap_matrix(T):
    nc = T // NSA_CMP_STRIDE
    ns = T // NSA_SEL_BLOCK
    cst = np.arange(nc) * NSA_CMP_STRIDE
    jst = np.arange(ns) * NSA_SEL_BLOCK
    ov = np.clip(np.minimum(cst[:, None] + NSA_CMP_BLOCK, jst[None, :] + NSA_SEL_BLOCK)
                 - np.maximum(cst[:, None], jst[None, :]), 0, None).astype(np.float32) / NSA_CMP_BLOCK
    return jnp.asarray(ov.T)


def _cmp_weights(pe_k, pe_v, k_w1, k_w2, v_w1, v_w2):
    eye = jnp.eye(KV_HEADS, dtype=F32)
    w1 = jnp.stack([k_w1, v_w1]).reshape(2, NSA_CMP_BLOCK, HEAD_DIM, NSA_CMP_HIDDEN)
    w1 = jnp.einsum('sldj,gh->slgdhj', w1, eye).reshape(2, NSA_CMP_BLOCK, KV_W, KV_HEADS * NSA_CMP_HIDDEN)
    w2 = jnp.einsum('sjd,gh->sgjhd', jnp.stack([k_w2, v_w2]), eye).reshape(2, KV_HEADS * NSA_CMP_HIDDEN, KV_W)
    pe = jnp.tile(jnp.stack([pe_k, pe_v])[:, :, None, :], (1, 1, 1, KV_HEADS))
    return pe, w1.astype(BF16), w2.astype(BF16)


def _layer(x, xb, B, T, w_in, w_out, ln_g, ln_b, rw_mu, rw_w0, rw_w2, rw_a0, rw_a2, rw_kk, rw_ka, rw_rk,
           rw_gn_g, rw_gn_b, swa_sinks, nsa_pe_k, nsa_pe_v, nsa_k_w1, nsa_k_w2, nsa_v_w1, nsa_v_w2,
           gdn_conv, gdn_A_log, gdn_dt_bias, gdn_norm_g):
    small = jnp.concatenate([_cols(w_in, n) for n in ('rw_wd', 'rw_ad', 'nsa_gate', 'gdn_beta', 'gdn_a')], axis=1)
    small = jnp.pad(small, ((0, 0), (0, SMALL_W - small.shape[1])))
    w_f32 = jnp.concatenate([_cols(w_in, n) for n in F32_GROUPS] + [small], axis=1).astype(BF16)
    w_bf = jnp.concatenate([_cols(w_in, n) for n in ('swa_q', 'nsa_q', 'swa_k', 'swa_v', 'nsa_ks', 'nsa_vs',
                                                     'nsa_kw', 'nsa_vw')], axis=1).astype(BF16)
    w_cmp = jnp.concatenate([_cols(w_in, 'nsa_kc'), _cols(w_in, 'nsa_vc')], axis=1).astype(BF16)
    tm = min(1024, B * T)
    cf = _matmul(xb, w_f32, F32, tm, N_F32 // 7)
    cb = _matmul(xb, w_bf, BF16, tm, N_BF // 2)
    ccmp = _matmul(xb, w_cmp, F32, tm, 256)

    row = lambda a: a.reshape(1, -1).astype(F32)
    mu3 = row(rw_mu[:3 * D_GROUP])
    musm = _small_row([(SM_WD, rw_mu[3 * D_GROUP:3 * D_GROUP + RW_LORA]),
                       (SM_AD, rw_mu[3 * D_GROUP + RW_LORA:])])
    y_a = _rwkv(cf, B, T, mu3, musm, row(rw_w0), rw_w2, row(rw_a0), rw_a2, row(rw_kk), row(rw_ka),
                row(rw_rk), row(rw_gn_g), row(rw_gn_b))
    y_d = _gdn(cf, B, T, gdn_conv, _small_row([(SM_A, gdn_A_log)]), _small_row([(SM_A, gdn_dt_bias)]),
               jnp.tile(row(gdn_norm_g), (1, 2)))
    y_b = _attn(cb, cf, B, T, 'swa_q', 'swa_k', 'swa_v', 'swa_g', window=SWA_WINDOW, sinks=row(swa_sinks))
    cmp = _nsa_compress(ccmp, B, T, *_cmp_weights(nsa_pe_k, nsa_pe_v, nsa_k_w1, nsa_k_w2, nsa_v_w1, nsa_v_w2))
    c1, sel = _nsa_cmpattn(cb, cmp, cf, B, T, _overlap_matrix(T))
    c2 = _attn(cb, cf, B, T, 'nsa_q', 'nsa_ks', 'nsa_vs', 'nsa_g', sel=sel, gate_col=SM_GATE + N_HEADS, tk=4 * TQ,
               aq=AQ_SEL)
    c3 = _attn(cb, cf, B, T, 'nsa_q', 'nsa_kw', 'nsa_vw', 'nsa_g', window=NSA_WINDOW,
               gate_col=SM_GATE + 2 * N_HEADS)
    return _out_proj((y_a, y_b, c1, c2, c3, y_d), x, w_out.astype(BF16), row(ln_g), row(ln_b))


def kernel(x, w_in, w_out, ln_g, ln_b, rw_mu, rw_w0, rw_w2, rw_a0, rw_a2, rw_kk, rw_ka, rw_rk, rw_gn_g, rw_gn_b, swa_sinks, nsa_pe_k, nsa_pe_v, nsa_k_w1, nsa_k_w2, nsa_v_w1, nsa_v_w2, gdn_conv, gdn_A_log, gdn_dt_bias, gdn_norm_g):
    B, T, D = x.shape
    params = (w_in, w_out, ln_g, ln_b, rw_mu, rw_w0, rw_w2, rw_a0, rw_a2, rw_kk, rw_ka, rw_rk, rw_gn_g, rw_gn_b,
              swa_sinks, nsa_pe_k, nsa_pe_v, nsa_k_w1, nsa_k_w2, nsa_v_w1, nsa_v_w2, gdn_conv, gdn_A_log,
              gdn_dt_bias, gdn_norm_g)
    xf = x.reshape(B * T, D)
    xb = xf.astype(BF16)
    for i in range(w_in.shape[0]):
        xf, xb = _layer(xf, xb, B, T, *(p[i] for p in params))
    return xf.reshape(B, T, D)
```

```python
import functools
import math

import numpy as np
import jax
import jax.numpy as jnp
from jax import lax
from jax.experimental import pallas as pl
from jax.experimental.pallas import tpu as pltpu

F32 = jnp.float32
BF16 = jnp.bfloat16
HI = lax.Precision.HIGHEST

D_MODEL = 2048
DEPTH = 4
D_GROUP = 512
HEAD_DIM = 64
N_HEADS = 8
KV_HEADS = 2
HPG = N_HEADS // KV_HEADS
PAIR = 2 * HEAD_DIM
KV_W = KV_HEADS * HEAD_DIM
NEG_INF = -1e30
LN_EPS = 1e-5
DEEPNORM_ALPHA = (2 * DEPTH) ** 0.25
RW_LORA = 32
RW_GN_EPS = 64e-5
SWA_WINDOW = 128
NSA_CMP_BLOCK = 32
NSA_CMP_STRIDE = 16
NSA_CMP_HIDDEN = 128
NSA_SEL_BLOCK = 64
NSA_TOPN = 16
NSA_WINDOW = 512
NSA_FORCE = 1e6
GDN_CONV = 4
GDN_EPS = 1e-6
SCALE = HEAD_DIM ** -0.5

CHUNK = 64
RW_SEQS = 4
GDN_SEQS = 4
TQ = 128
AQ_BAND = 256
AQ_SEL = 512
TAIL = 8
VT_ROWS = HEAD_DIM + 16
VMEM_LIMIT = 56 * 1024 * 1024

_COLS = (
    ('rw_r', 512), ('rw_k', 512), ('rw_v', 512), ('rw_wd', 32), ('rw_ad', 32), ('rw_g', 512),
    ('swa_q', 512), ('swa_k', 128), ('swa_v', 128), ('swa_g', 512),
    ('nsa_q', 512), ('nsa_kc', 128), ('nsa_vc', 128), ('nsa_ks', 128), ('nsa_vs', 128),
    ('nsa_kw', 128), ('nsa_vw', 128), ('nsa_gate', 24), ('nsa_g', 512),
    ('gdn_q', 512), ('gdn_k', 512), ('gdn_v', 512), ('gdn_beta', 8), ('gdn_a', 8), ('gdn_g', 512),
)
_OFF = {}
_o = 0
for _n, _s in _COLS:
    _OFF[_n] = (_o, _s)
    _o += _s
N_IN = _o

F32_GROUPS = ('rw_r', 'rw_k', 'rw_v', 'rw_g', 'gdn_q', 'gdn_k', 'gdn_v', 'gdn_g', 'swa_g', 'nsa_g')
FCOL = {n: i for i, n in enumerate(F32_GROUPS)}
SMALL_W = 256
SMALL_BLK = len(F32_GROUPS) * 512 // SMALL_W
SM_WD, SM_AD, SM_GATE, SM_BETA, SM_A = 0, 32, 64, 88, 96
CMP_BLK = (len(F32_GROUPS) * 512 + SMALL_W) // KV_W
N_F32 = len(F32_GROUPS) * 512 + SMALL_W + 2 * KV_W
BF_Q = {'swa_q': 0, 'nsa_q': 1}
BF_KV = {n: 8 + i for i, n in enumerate(('swa_k', 'swa_v', 'nsa_ks', 'nsa_vs', 'nsa_kw', 'nsa_vw'))}
N_BF = 1024 + 6 * 128


def _dot(a, b, prec=None):
    return lax.dot_general(a, b, (((1,), (0,)), ((), ())), precision=prec, preferred_element_type=F32)


def _dot_nt(a, b, prec=None):
    return lax.dot_general(a, b, (((1,), (1,)), ((), ())), precision=prec, preferred_element_type=F32)


def _dot_tn(a, b, prec=None):
    return lax.dot_general(a, b, (((0,), (0,)), ((), ())), precision=prec, preferred_element_type=F32)


def _sigmoid(x):
    return 1.0 / (1.0 + jnp.exp(-x))


def _silu(x):
    return x * _sigmoid(x)


def _softplus(x):
    return jnp.maximum(x, 0.0) + jnp.log(1.0 + jnp.exp(-jnp.abs(x)))


def _iota2(shape, dim):
    return lax.broadcasted_iota(jnp.int32, shape, dim)


def _pairs(t):
    return [t[:, p * PAIR:(p + 1) * PAIR] for p in range(N_HEADS // 2)]


def _lo_mask(rows):
    return _iota2((rows, PAIR), 1) < HEAD_DIM


def _bd(x):
    xb = x.astype(BF16)
    lo = _lo_mask(x.shape[0]).astype(BF16)
    return jnp.concatenate([xb * lo, xb * (1 - lo)], axis=0)


def _pdot(a, b):
    return _dot(a.astype(BF16), _bd(b))


def _pdot_nt(a, b):
    return _dot_nt(a.astype(BF16), _bd(b))


def _pdot_tn(a, b):
    full = _dot_tn(a.astype(BF16), b.astype(BF16))
    return jnp.where(_lo_mask(HEAD_DIM), full[:HEAD_DIM], full[HEAD_DIM:])


def _segsum(t):
    lo = _lo_mask(t.shape[0])
    s_lo = jnp.sum(jnp.where(lo, t, 0.0), -1, keepdims=True)
    s_hi = jnp.sum(jnp.where(lo, 0.0, t), -1, keepdims=True)
    return jnp.where(lo, s_lo, s_hi)


def _inv_unit_lower(xs):
    n = xs[0].shape[0]
    eye2 = ((_iota2((n, 2 * n), 1) & (n - 1)) == _iota2((n, 2 * n), 0)).astype(F32)
    ps = [eye2 + x for x in xs]
    for _ in range(int(math.log2(n)) - 1):
        xs = [_pdot(x, x) for x in xs]
        ps = [p + _pdot(x, p) for x, p in zip(xs, ps)]
    return ps


def _tri_masks2(n):
    ri = _iota2((n, 2 * n), 0)
    ci = _iota2((n, 2 * n), 1) & (n - 1)
    return ri >= ci, ri > ci


def _params(sem):
    return pltpu.CompilerParams(dimension_semantics=sem, vmem_limit_bytes=VMEM_LIMIT)


def _mm_kernel(x_ref, w_ref, o_ref):
    o_ref[...] = jnp.dot(x_ref[...], w_ref[...], preferred_element_type=F32).astype(o_ref.dtype)


def _matmul(x, w, layer, out_dtype, tm, tn):
    m, k = x.shape
    n = w.shape[2]
    return pl.pallas_call(
        _mm_kernel,
        grid=(m // tm, n // tn),
        in_specs=[pl.BlockSpec((tm, k), lambda i, j: (i, 0)),
                  pl.BlockSpec((None, k, tn), lambda i, j: (layer, 0, j))],
        out_specs=pl.BlockSpec((tm, tn), lambda i, j: (i, j)),
        out_shape=jax.ShapeDtypeStruct((m, n), out_dtype),
        compiler_params=_params(("parallel", "arbitrary")),
        name="proj_in",
    )(x, w)


def _rwkv_kernel(r_ref, k_ref, v_ref, g_ref, sm_ref, mu_ref, musm_ref, w0_ref, w2_ref, a0_ref, a2_ref,
                 kk_ref, ka_ref, rk_ref, gng_ref, gnb_ref, o_ref, buf_ref, bufsm_ref, s_ref):
    c = pl.program_id(1)
    C = CHUNK

    NB = r_ref.shape[0]

    @pl.when(c == 0)
    def _():
        buf_ref[:, 0:TAIL, :] = jnp.zeros((NB, TAIL, 3 * D_GROUP), F32)
        bufsm_ref[:, 0:TAIL, :] = jnp.zeros((NB, TAIL, SMALL_W), F32)
        s_ref[...] = jnp.zeros(s_ref.shape, F32)

    ri = _iota2((C, C), 0)
    ci = _iota2((C, C), 1)
    tril = ri >= ci
    strict = ri > ci
    rk = rk_ref[...]
    gng = gng_ref[...]
    gnb = gnb_ref[...]
    pre = []
    for nb in range(NB):
        buf_ref[nb, TAIL:TAIL + C, 0:512] = r_ref[nb]
        buf_ref[nb, TAIL:TAIL + C, 512:1024] = k_ref[nb]
        buf_ref[nb, TAIL:TAIL + C, 1024:1536] = v_ref[nb]
        bufsm_ref[nb, TAIL:TAIL + C, :] = sm_ref[nb]
        cur = buf_ref[nb, TAIL:TAIL + C, :]
        prev = buf_ref[nb, TAIL - 1:TAIL - 1 + C, :]
        mixed = cur + (prev - cur) * mu_ref[...]
        cur_sm = bufsm_ref[nb, TAIL:TAIL + C, :]
        prev_sm = bufsm_ref[nb, TAIL - 1:TAIL - 1 + C, :]
        smix = cur_sm + (prev_sm - cur_sm) * musm_ref[...]
        buf_ref[nb, 0:TAIL, :] = buf_ref[nb, C:C + TAIL, :]
        bufsm_ref[nb, 0:TAIL, :] = bufsm_ref[nb, C:C + TAIL, :]

        r = mixed[:, 0:512]
        k = mixed[:, 512:1024]
        v = mixed[:, 1024:1536]
        wd = smix[:, SM_WD:SM_WD + RW_LORA]
        ad = smix[:, SM_AD:SM_AD + RW_LORA]
        wl = w0_ref[...] + _dot(jnp.tanh(wd), w2_ref[...], HI)
        logw = -jnp.exp(-_softplus(-wl) - 0.5)
        alpha = _sigmoid(a0_ref[...] + _dot(ad, a2_ref[...], HI))
        kkraw = k * kk_ref[...]
        k2 = k * (1.0 + (alpha - 1.0) * ka_ref[...])
        gcum = _dot(tril.astype(F32), logw, HI)
        pre.append((r, k2, v, alpha, kkraw, logw, gcum, _silu(g_ref[nb])))

    H = range(NB * N_HEADS // 2)
    tril2, strict2 = _tri_masks2(C)
    R, K2, V, AL, KKR, LW, G = ([t for p in pre for t in _pairs(p[f])] for f in range(7))
    KK = [t * lax.rsqrt(_segsum(t * t) + 1e-6) for t in KKR]
    GM = [t[C // 2 - 1:C // 2, :] for t in G]
    GE = [t[C - 1:C, :] for t in G]
    BV = [KK[h] * AL[h] for h in H]
    EINV = [jnp.exp(GM[h] - G[h]) for h in H]
    EEND = [jnp.exp(GE[h] - G[h]) for h in H]
    LEFT = [jnp.concatenate([-KK[h] * jnp.exp(G[h] - LW[h] - GM[h]), R[h] * jnp.exp(G[h] - GM[h])], axis=0)
            for h in H]
    AB = [_pdot_nt(LEFT[h], BV[h] * EINV[h]) for h in H]
    AK = [_pdot_nt(LEFT[h], K2[h] * EINV[h]) for h in H]
    A_RB = [jnp.where(tril2, t[C:], 0.0) for t in AB]
    A_K = [jnp.concatenate([jnp.where(strict2, t[:C], 0.0), jnp.where(tril2, t[C:], 0.0)], axis=0) for t in AK]
    TINV = _inv_unit_lower([jnp.where(strict2, t[:C], 0.0) for t in AB])
    S = [s_ref[h] for h in H]
    LS = [_pdot_nt(LEFT[h], S[h] * jnp.exp(GM[h])) for h in H]
    AV = [_pdot(A_K[h], V[h]) for h in H]
    U = [_pdot(TINV[h], LS[h][:C] + AV[h][:C]) for h in H]
    O = [LS[h][C:] + _pdot(A_RB[h], U[h]) + AV[h][C:] for h in H]
    for h in H:
        s_ref[h] = S[h] * jnp.exp(GE[h]) + _pdot_tn(
            jnp.concatenate([U[h], V[h]], axis=0),
            jnp.concatenate([BV[h] * EEND[h], K2[h] * EEND[h]], axis=0))
    for n in H:
        nb, p = divmod(n, N_HEADS // 2)
        sl = slice(p * PAIR, (p + 1) * PAIR)
        mu = _segsum(O[n]) * (1.0 / HEAD_DIM)
        d = O[n] - mu
        var = _segsum(d * d) * (1.0 / HEAD_DIM)
        y = d * lax.rsqrt(var + RW_GN_EPS) * gng[:, sl] + gnb[:, sl]
        bonus = _segsum(R[n] * K2[n] * rk[:, sl]) * V[n]
        o_ref[nb, :, sl] = ((y + bonus) * pre[nb][7][:, sl]).astype(o_ref.dtype)


def _seq_specs(names, nbs):
    specs = [pl.BlockSpec((nbs, CHUNK, 512), lambda b, c, j=FCOL[n]: (b, c, j)) for n in names]
    return specs + [pl.BlockSpec((nbs, CHUNK, SMALL_W), lambda b, c: (b, c, SMALL_BLK))]


def _rwkv(cf, B, T, mu3, musm, w0, w2, a0, a2, kk, ka, rk, gng, gnb):
    nbs = RW_SEQS if B % RW_SEQS == 0 else 1
    cf3 = cf.reshape(B, T, cf.shape[-1])
    full = lambda a: pl.BlockSpec(a.shape, lambda b, c: (0,) * a.ndim)
    ps = (mu3, musm, w0, w2, a0, a2, kk, ka, rk, gng, gnb)
    out = pl.pallas_call(
        _rwkv_kernel,
        grid=(B // nbs, T // CHUNK),
        in_specs=_seq_specs(('rw_r', 'rw_k', 'rw_v', 'rw_g'), nbs) + [full(a) for a in ps],
        out_specs=pl.BlockSpec((nbs, CHUNK, 512), lambda b, c: (b, c, 0)),
        out_shape=jax.ShapeDtypeStruct((B, T, D_GROUP), BF16),
        scratch_shapes=[pltpu.VMEM((nbs, TAIL + CHUNK, 3 * D_GROUP), F32),
                        pltpu.VMEM((nbs, TAIL + CHUNK, SMALL_W), F32),
                        pltpu.VMEM((nbs * N_HEADS // 2, HEAD_DIM, PAIR), F32)],
        compiler_params=_params(("parallel", "arbitrary")),
        name="rwkv7",
    )(cf3, cf3, cf3, cf3, cf3, *ps)
    return out.reshape(B * T, D_GROUP)


def _gdn_kernel(q_ref, k_ref, v_ref, z_ref, sm_ref, conv_ref, alog_ref, dtb_ref, ng_ref, o_ref,
                buf_ref, s_ref):
    c = pl.program_id(1)
    C = CHUNK

    NB = q_ref.shape[0]

    @pl.when(c == 0)
    def _():
        buf_ref[:, 0:TAIL, :] = jnp.zeros((NB, TAIL, 3 * D_GROUP), F32)
        s_ref[...] = jnp.zeros(s_ref.shape, F32)

    conv = conv_ref[...]
    lane = _iota2((C, SMALL_W), 1)
    ri = _iota2((C, C), 0)
    ci = _iota2((C, C), 1)
    tril = ri >= ci
    strict = ri > ci
    ng = ng_ref[...]
    pre = []
    for nb in range(NB):
        buf_ref[nb, TAIL:TAIL + C, 0:512] = q_ref[nb]
        buf_ref[nb, TAIL:TAIL + C, 512:1024] = k_ref[nb]
        buf_ref[nb, TAIL:TAIL + C, 1024:1536] = v_ref[nb]
        acc = buf_ref[nb, TAIL:TAIL + C, :] * conv[GDN_CONV - 1:GDN_CONV, :]
        for i in range(GDN_CONV - 1):
            sh = GDN_CONV - 1 - i
            acc = acc + buf_ref[nb, TAIL - sh:TAIL - sh + C, :] * conv[i:i + 1, :]
        buf_ref[nb, 0:TAIL, :] = buf_ref[nb, C:C + TAIL, :]
        qkv = _silu(acc)
        sm = sm_ref[nb]
        g_all = jnp.where((lane >= SM_A) & (lane < SM_A + N_HEADS),
                          -jnp.exp(alog_ref[...]) * _softplus(sm + dtb_ref[...]), 0.0)
        gam_all = _dot(tril.astype(F32), g_all, HI)
        pre.append((qkv[:, 0:512], qkv[:, 512:1024], qkv[:, 1024:1536], _sigmoid(sm), gam_all, _silu(z_ref[nb])))

    H = range(NB * N_HEADS // 2)
    hd = lambda n: divmod(n, N_HEADS // 2)
    lo = _lo_mask(C)
    tril2, strict2 = _tri_masks2(C)
    diag2 = (_iota2((C, PAIR), 1) & (HEAD_DIM - 1)) == _iota2((C, PAIR), 0)

    def per_head(n, field, col):
        nb, p = hd(n)
        t = pre[nb][field]
        return jnp.where(lo, t[:, col + 2 * p:col + 2 * p + 1], t[:, col + 2 * p + 1:col + 2 * p + 2])

    Q = [t * lax.rsqrt(_segsum(t * t) + 1e-6) * SCALE for p in pre for t in _pairs(p[0])]
    K = [t * lax.rsqrt(_segsum(t * t) + 1e-6) for p in pre for t in _pairs(p[1])]
    V = [t for p in pre for t in _pairs(p[2])]
    BETA = [per_head(n, 3, SM_BETA) for n in H]
    GAM = [per_head(n, 4, SM_A) for n in H]
    GROW = [jnp.sum(jnp.where(diag2, t, 0.0), 0, keepdims=True) for t in GAM]
    DECAY = [jnp.exp(jnp.where(tril2, GAM[n] - GROW[n], NEG_INF)) for n in H]
    EG = [jnp.exp(t) for t in GAM]
    KB = [K[h] * BETA[h] for h in H]
    KKQ = [_pdot_nt(jnp.concatenate([KB[h], Q[h]], axis=0), K[h]) for h in H]
    QK = [KKQ[h][C:] * DECAY[h] for h in H]
    TM = _inv_unit_lower([-jnp.where(strict2, KKQ[h][:C] * DECAY[h], 0.0) for h in H])
    UU = [_pdot(TM[h], V[h] * BETA[h]) for h in H]
    W = [_pdot(TM[h], KB[h] * EG[h]) for h in H]
    S = [s_ref[h] for h in H]
    WS = [_pdot(jnp.concatenate([W[h], Q[h] * EG[h]], axis=0), S[h]) for h in H]
    VN = [UU[h] - WS[h][:C] for h in H]
    O = [WS[h][C:] + _pdot(QK[h], VN[h]) for h in H]
    for h in H:
        g_last = GAM[h][C - 1:C, :]
        s_ref[h] = S[h] * jnp.exp(g_last) + _pdot_tn(K[h] * jnp.exp(g_last - GAM[h]), VN[h])
    for n in H:
        nb, p = hd(n)
        sl = slice(p * PAIR, (p + 1) * PAIR)
        o = O[n] * lax.rsqrt(_segsum(O[n] * O[n]) * (1.0 / HEAD_DIM) + GDN_EPS) * ng
        o_ref[nb, :, sl] = (o * pre[nb][5][:, sl]).astype(o_ref.dtype)


def _gdn(cf, B, T, conv, alog_sm, dtb_sm, ng):
    nbs = GDN_SEQS if B % GDN_SEQS == 0 else 1
    cf3 = cf.reshape(B, T, cf.shape[-1])
    full = lambda a: pl.BlockSpec(a.shape, lambda b, c: (0,) * a.ndim)
    ps = (conv, alog_sm, dtb_sm, ng)
    out = pl.pallas_call(
        _gdn_kernel,
        grid=(B // nbs, T // CHUNK),
        in_specs=_seq_specs(('gdn_q', 'gdn_k', 'gdn_v', 'gdn_g'), nbs) + [full(a) for a in ps],
        out_specs=pl.BlockSpec((nbs, CHUNK, 512), lambda b, c: (b, c, 0)),
        out_shape=jax.ShapeDtypeStruct((B, T, D_GROUP), BF16),
        scratch_shapes=[pltpu.VMEM((nbs, TAIL + CHUNK, 3 * D_GROUP), F32),
                        pltpu.VMEM((nbs * N_HEADS // 2, HEAD_DIM, PAIR), F32)],
        compiler_params=_params(("parallel", "arbitrary")),
        name="gdn",
    )(cf3, cf3, cf3, cf3, cf3, *ps)
    return out.reshape(B * T, D_GROUP)


def _store_heads(o_ref, g, ot, gate, bgate, gate_col):
    aq = o_ref.shape[0]
    for j in range(HPG):
        h = g * HPG + j
        sl = slice(h * HEAD_DIM, (h + 1) * HEAD_DIM)
        oh = ot[:, j * aq:(j + 1) * aq].T
        if bgate is not None:
            oh = oh * bgate[:, gate_col + h:gate_col + h + 1]
        o_ref[:, sl] = oh * gate[:, sl]


def _attn_kernel(*refs, window, use_sel, use_sink, gate_col, tk):
    it = iter(refs)
    q_ref, k_ref, v_ref, g_ref = next(it), next(it), next(it), next(it)
    sm_ref = next(it) if gate_col is not None else None
    sel_ref = next(it) if use_sel else None
    sink_ref = next(it) if use_sink else None
    o_ref = next(it)
    vt_ref = next(it)
    selx_ref = next(it) if use_sel else None
    i = pl.program_id(1)
    aq = q_ref.shape[0]
    W = HPG * aq

    @pl.when(i == 0)
    def _():
        eye = (_iota2((KV_W, KV_W), 0) == _iota2((KV_W, KV_W), 1)).astype(BF16)
        ones = jnp.ones((VT_ROWS - HEAD_DIM, tk), BF16)
        for c in range(vt_ref.shape[0]):
            vt = _dot_nt(eye, v_ref[c * tk:(c + 1) * tk, :]).astype(BF16)
            for g in range(KV_HEADS):
                vt_ref[c, g] = jnp.concatenate([vt[g * HEAD_DIM:(g + 1) * HEAD_DIM], ones], axis=0)

    span = tk if window is None else window + aq
    qpos = i * aq + (_iota2((span, W), 1) & (aq - 1))
    krow = _iota2((span, W), 0)
    gate = _silu(g_ref[...])
    if gate_col is not None:
        bgate = _sigmoid(sm_ref[...])

    G = range(KV_HEADS)
    QG = [jnp.concatenate([q_ref[:, h * HEAD_DIM:(h + 1) * HEAD_DIM] for h in range(g * HPG, (g + 1) * HPG)],
                          axis=0) * SCALE for g in G]
    if use_sel:
        for g in G:
            selx_ref[g] = jnp.concatenate([sel_ref[0, g]] * HPG, axis=1)
        nb = tk // NSA_SEL_BLOCK

    if window is not None:
        nt = span // tk
        t0 = jnp.maximum(i * aq - window, 0) // tk
        start = pl.multiple_of(t0 * tk, tk)
        rel = qpos - (start + krow)
        pbias = jnp.where((rel >= 0) & (rel < window), 0.0, NEG_INF)
        for g in G:
            ksl = slice(g * HEAD_DIM, (g + 1) * HEAD_DIM)
            s = _dot_nt(k_ref[pl.ds(start, span), ksl], QG[g]) + pbias
            m = jnp.max(s, 0, keepdims=True)
            if use_sink:
                sink = jnp.concatenate([jnp.zeros((1, aq), F32) + sink_ref[0:1, h:h + 1]
                                        for h in range(g * HPG, (g + 1) * HPG)], axis=1)
                m = jnp.maximum(m, sink)
            p = jnp.exp((s - m).astype(BF16))
            vt = jnp.concatenate([vt_ref[t0 + j, g] for j in range(nt)], axis=1)
            pv = _dot(vt, p)
            l = pv[HEAD_DIM:HEAD_DIM + 1]
            if use_sink:
                l = l + jnp.exp(sink - m)
            _store_heads(o_ref, g, pv[:HEAD_DIM] / l, gate, bgate if gate_col is not None else None, gate_col)
        return

    hi = (i * aq + aq + tk - 1) // tk

    def body(kt, carry, masked):
        off = pl.multiple_of(kt * tk, tk)
        if masked:
            pbias = jnp.where(qpos - (kt * tk + krow) >= 0, 0.0, NEG_INF)
        out = []
        for g in G:
            m, acc = carry[2 * g:2 * g + 2]
            ksl = slice(g * HEAD_DIM, (g + 1) * HEAD_DIM)
            s = _dot_nt(k_ref[pl.ds(off, tk), ksl], QG[g])
            if masked:
                s = s + pbias
            if use_sel:
                s = s + jnp.concatenate(
                    [jnp.broadcast_to(selx_ref[g, pl.ds(kt * nb + j, 1), :], (NSA_SEL_BLOCK, W))
                     for j in range(nb)], axis=0)
            m_new = jnp.maximum(m, jnp.max(s, 0, keepdims=True))
            a = jnp.exp(m - m_new)
            p = jnp.exp((s - m_new).astype(BF16))
            out += [m_new, a * acc + _dot(vt_ref[kt, g], p)]
        return tuple(out)

    init = []
    for g in G:
        init += [jnp.full((1, W), NEG_INF, F32), jnp.zeros((VT_ROWS, W), F32)]
    res = lax.fori_loop(0, hi - 1, functools.partial(body, masked=False), tuple(init))
    res = body(hi - 1, res, True)
    for g in G:
        acc = res[2 * g + 1]
        _store_heads(o_ref, g, acc[:HEAD_DIM] / acc[HEAD_DIM:HEAD_DIM + 1], gate,
                     bgate if gate_col is not None else None, gate_col)


def _attn(cb, cf, B, T, qname, kname, vname, gname, *, window=None, sel=None, sinks=None, gate_col=None, tk=TQ,
          aq=AQ_BAND):
    nq = T // aq
    in_specs = [pl.BlockSpec((aq, 512), lambda b, i, j=BF_Q[qname]: (b * nq + i, j)),
                pl.BlockSpec((T, KV_W), lambda b, i, j=BF_KV[kname]: (b, j)),
                pl.BlockSpec((T, KV_W), lambda b, i, j=BF_KV[vname]: (b, j)),
                pl.BlockSpec((aq, 512), lambda b, i, j=FCOL[gname]: (b * nq + i, j))]
    args = [cb, cb, cb, cf]
    if gate_col is not None:
        in_specs.append(pl.BlockSpec((aq, SMALL_W), lambda b, i: (b * nq + i, SMALL_BLK)))
        args.append(cf)
    scratch = [pltpu.VMEM((T // tk, KV_HEADS, VT_ROWS, tk), BF16)]
    if sel is not None:
        in_specs.append(pl.BlockSpec((1, KV_HEADS, sel.shape[2], aq), lambda b, i: (b, 0, 0, i)))
        args.append(sel)
        scratch.append(pltpu.VMEM((KV_HEADS, sel.shape[2], HPG * aq), F32))
    if sinks is not None:
        in_specs.append(pl.BlockSpec(sinks.shape, lambda b, i: (0, 0)))
        args.append(sinks)
    return pl.pallas_call(
        functools.partial(_attn_kernel, window=window, use_sel=sel is not None,
                          use_sink=sinks is not None, gate_col=gate_col, tk=tk),
        grid=(B, nq),
        in_specs=in_specs,
        out_specs=pl.BlockSpec((aq, 512), lambda b, i: (b * nq + i, 0)),
        out_shape=jax.ShapeDtypeStruct((B * T, D_GROUP), F32),
        scratch_shapes=scratch,
        compiler_params=_params(("parallel", "arbitrary")),
        name="attn_" + qname + ("_sel" if sel is not None else "_w%d" % window),
    )(*args)


def _gelu_tanh(x):
    return 0.5 * x * (1.0 + jnp.tanh(math.sqrt(2.0 / math.pi) * (x + 0.044715 * x * x * x)))


def _cmp_kernel(kc_ref, vc_ref, pe_ref, w1_ref, w2_ref, o_ref):
    nr = o_ref.shape[1]
    outs = []
    for s, ref in enumerate((kc_ref, vc_ref)):
        p1 = jnp.zeros((nr, 2 * NSA_CMP_HIDDEN), F32)
        p2 = jnp.zeros((nr, 2 * NSA_CMP_HIDDEN), F32)
        for l in range(NSA_CMP_STRIDE):
            x = ref[pl.ds(l, nr, stride=NSA_CMP_STRIDE), :]
            p1 = p1 + _dot((x + pe_ref[s, l]).astype(BF16), w1_ref[s, l])
            p2 = p2 + _dot((x + pe_ref[s, NSA_CMP_STRIDE + l]).astype(BF16), w1_ref[s, NSA_CMP_STRIDE + l])
        hid = _gelu_tanh(p1 + pltpu.roll(p2, nr - 1, 0))
        outs.append(_dot(hid.astype(BF16), w2_ref[s]))
    o_ref[0] = jnp.concatenate(outs, axis=1)


def _nsa_compress(cf, B, T, pe, w1, w2):
    nr = T // NSA_CMP_STRIDE
    full = lambda a: pl.BlockSpec(a.shape, lambda b: (0,) * a.ndim)
    return pl.pallas_call(
        _cmp_kernel,
        grid=(B,),
        in_specs=[pl.BlockSpec((T, KV_W), lambda b: (b, CMP_BLK)), pl.BlockSpec((T, KV_W), lambda b: (b, CMP_BLK + 1))]
                 + [full(a) for a in (pe, w1, w2)],
        out_specs=pl.BlockSpec((1, nr, 2 * KV_W), lambda b: (b, 0, 0)),
        out_shape=jax.ShapeDtypeStruct((B, nr, 2 * KV_W), F32),
        compiler_params=_params(("arbitrary",)),
        name="nsa_compress",
    )(cf, cf, pe, w1, w2)


def _cmpattn_kernel(q_ref, cmp_ref, g_ref, sm_ref, ov_ref, o_ref, sel_ref):
    i = pl.program_id(1)
    kv = cmp_ref[0]
    nr = kv.shape[0]
    ns = ov_ref.shape[0]
    tq = q_ref.shape[0]
    tpos = i * tq + _iota2((tq, nr), 0)
    cmask = _iota2((tq, nr), 1) * NSA_CMP_STRIDE + (NSA_CMP_BLOCK - 1) <= tpos
    gate = _silu(g_ref[...])
    bgate = _sigmoid(sm_ref[...])
    tblk = (i * tq + _iota2((ns, tq), 1)) // NSA_SEL_BLOCK
    jj = _iota2((ns, tq), 0)
    forced = (jj == 0) | (jj == tblk) | (jj == tblk - 1)
    causal = jj <= tblk
    cbias = jnp.where(cmask, 0.0, NEG_INF)
    for g in range(KV_HEADS):
        kc = kv[:, g * HEAD_DIM:(g + 1) * HEAD_DIM].astype(BF16)
        vc = kv[:, KV_W + g * HEAD_DIM:KV_W + (g + 1) * HEAD_DIM].astype(BF16)
        heads = range(g * HPG, (g + 1) * HPG)
        qg = jnp.concatenate([q_ref[:, h * HEAD_DIM:(h + 1) * HEAD_DIM] for h in heads], axis=0) * SCALE
        s = _dot_nt(qg, kc).reshape(HPG, tq, nr) + cbias[None]
        e = jnp.exp(s - jnp.max(s, -1, keepdims=True))
        p = jnp.where(cmask[None], e / jnp.sum(e, -1, keepdims=True), 0.0)
        psum = jnp.sum(p, axis=0)
        o = _dot(p.reshape(HPG * tq, nr).astype(BF16), vc)
        for j, h in enumerate(heads):
            sl = slice(h * HEAD_DIM, (h + 1) * HEAD_DIM)
            o_ref[:, sl] = o[j * tq:(j + 1) * tq] * bgate[:, SM_GATE + h:SM_GATE + h + 1] * gate[:, sl]
        imp = _dot_nt(ov_ref[...], psum, HI)
        imp = jnp.where(causal, jnp.where(forced, NSA_FORCE, imp), NEG_INF)
        rank = jnp.zeros((ns, tq), jnp.int32)
        for j in range(ns):
            row = imp[j:j + 1, :]
            rank = rank + ((row > imp) | ((row == imp) & (j < jj))).astype(jnp.int32)
        sel_ref[0, g] = jnp.where((rank < NSA_TOPN) & causal, 0.0, NEG_INF)


def _nsa_cmpattn(cb, cmp, cf, B, T, ov):
    tq = AQ_BAND
    nq = T // tq
    nr = cmp.shape[1]
    ns = T // NSA_SEL_BLOCK
    return pl.pallas_call(
        _cmpattn_kernel,
        grid=(B, nq),
        in_specs=[pl.BlockSpec((tq, 512), lambda b, i: (b * nq + i, BF_Q['nsa_q'])),
                  pl.BlockSpec((1, nr, 2 * KV_W), lambda b, i: (b, 0, 0)),
                  pl.BlockSpec((tq, 512), lambda b, i: (b * nq + i, FCOL['nsa_g'])),
                  pl.BlockSpec((tq, SMALL_W), lambda b, i: (b * nq + i, SMALL_BLK)),
                  pl.BlockSpec(ov.shape, lambda b, i: (0, 0))],
        out_specs=[pl.BlockSpec((tq, 512), lambda b, i: (b * nq + i, 0)),
                   pl.BlockSpec((1, KV_HEADS, ns, tq), lambda b, i: (b, 0, 0, i))],
        out_shape=[jax.ShapeDtypeStruct((B * T, D_GROUP), F32),
                   jax.ShapeDtypeStruct((B, KV_HEADS, ns, T), F32)],
        compiler_params=_params(("parallel", "arbitrary")),
        name="nsa_cmpattn",
    )(cb, cmp, cf, cf, ov)


def _out_kernel(ya_ref, yb_ref, c1_ref, c2_ref, c3_ref, yd_ref, x_ref, w_ref, lg_ref, lb_ref, o_ref, ob_ref):
    yc = c1_ref[...] + c2_ref[...] + c3_ref[...]
    acc = _dot(ya_ref[...], w_ref[0:512, :])
    acc = acc + _dot(yb_ref[...].astype(BF16), w_ref[512:1024, :])
    acc = acc + _dot(yc.astype(BF16), w_ref[1024:1536, :])
    acc = acc + _dot(yd_ref[...], w_ref[1536:2048, :])
    z = DEEPNORM_ALPHA * x_ref[...] + acc
    mu = jnp.mean(z, -1, keepdims=True)
    var = jnp.mean(jnp.square(z - mu), -1, keepdims=True)
    out = (z - mu) * lax.rsqrt(var + LN_EPS) * lg_ref[...] + lb_ref[...]
    o_ref[...] = out
    ob_ref[...] = out.astype(BF16)


def _out_proj(ys, x, w, layer, lg, lb, tm=512):
    m = x.shape[0]
    yspec = pl.BlockSpec((tm, 512), lambda i: (i, 0))
    xspec = pl.BlockSpec((tm, D_MODEL), lambda i: (i, 0))
    return pl.pallas_call(
        _out_kernel,
        grid=(m // tm,),
        in_specs=[yspec] * 6 + [xspec, pl.BlockSpec((None,) + w.shape[1:], lambda i: (layer, 0, 0),
                                                    pipeline_mode=pl.Buffered(1)),
                                pl.BlockSpec(lg.shape, lambda i: (0, 0)), pl.BlockSpec(lb.shape, lambda i: (0, 0))],
        out_specs=[xspec, xspec],
        out_shape=[jax.ShapeDtypeStruct((m, D_MODEL), F32), jax.ShapeDtypeStruct((m, D_MODEL), BF16)],
        compiler_params=_params(("parallel",)),
        name="out_proj_ln",
    )(*ys, x, w, lg, lb)


def _cols(w, name):
    o, s = _OFF[name]
    return w[..., o:o + s]


def _small_row(pieces):
    parts, pos = [], 0
    for off, vals in pieces:
        parts += [jnp.zeros((off - pos,), F32), vals.astype(F32)]
        pos = off + vals.shape[-1]
    return jnp.concatenate(parts + [jnp.zeros((SMALL_W - pos,), F32)]).reshape(1, SMALL_W)


def _overlap_matrix(T):
    nc = T // NSA_CMP_STRIDE
    ns = T // NSA_SEL_BLOCK
    cst = np.arange(nc) * NSA_CMP_STRIDE
    jst = np.arange(ns) * NSA_SEL_BLOCK
    ov = np.clip(np.minimum(cst[:, None] + NSA_CMP_BLOCK, jst[None, :] + NSA_SEL_BLOCK)
                 - np.maximum(cst[:, None], jst[None, :]), 0, None).astype(np.float32) / NSA_CMP_BLOCK
    return jnp.asarray(ov.T)


def _cmp_weights(pe_k, pe_v, k_w1, k_w2, v_w1, v_w2):
    eye = jnp.eye(KV_HEADS, dtype=F32)
    w1 = jnp.stack([k_w1, v_w1]).reshape(2, NSA_CMP_BLOCK, HEAD_DIM, NSA_CMP_HIDDEN)
    w1 = jnp.einsum('sldj,gh->slgdhj', w1, eye).reshape(2, NSA_CMP_BLOCK, KV_W, KV_HEADS * NSA_CMP_HIDDEN)
    w2 = jnp.einsum('sjd,gh->sgjhd', jnp.stack([k_w2, v_w2]), eye).reshape(2, KV_HEADS * NSA_CMP_HIDDEN, KV_W)
    pe = jnp.tile(jnp.stack([pe_k, pe_v])[:, :, None, :], (1, 1, 1, KV_HEADS))
    return pe, w1.astype(BF16), w2.astype(BF16)


def _projection_weights(w_in):
    small = jnp.concatenate([_cols(w_in, n) for n in ('rw_wd', 'rw_ad', 'nsa_gate', 'gdn_beta', 'gdn_a')], axis=-1)
    small = jnp.pad(small, ((0, 0), (0, 0), (0, SMALL_W - small.shape[-1])))
    w_f32 = jnp.concatenate([_cols(w_in, n) for n in F32_GROUPS] + [small, _cols(w_in, 'nsa_kc'),
                                                                    _cols(w_in, 'nsa_vc')], axis=-1)
    w_bf = jnp.concatenate([_cols(w_in, n) for n in ('swa_q', 'nsa_q', 'swa_k', 'swa_v', 'nsa_ks', 'nsa_vs',
                                                     'nsa_kw', 'nsa_vw')], axis=-1)
    return w_f32.astype(BF16), w_bf.astype(BF16)


def _layer(x, xb, B, T, layer, w_f32, w_bf, w_out, ln_g, ln_b, rw_mu, rw_w0, rw_w2, rw_a0, rw_a2, rw_kk, rw_ka,
           rw_rk, rw_gn_g, rw_gn_b, swa_sinks, nsa_pe_k, nsa_pe_v, nsa_k_w1, nsa_k_w2, nsa_v_w1, nsa_v_w2,
           gdn_conv, gdn_A_log, gdn_dt_bias, gdn_norm_g):
    tm = min(1024, B * T)
    cf = _matmul(xb, w_f32, layer, F32, tm, 512)
    cb = _matmul(xb, w_bf, layer, BF16, tm, N_BF // 2)

    row = lambda a: a.reshape(1, -1).astype(F32)
    mu3 = row(rw_mu[:3 * D_GROUP])
    musm = _small_row([(SM_WD, rw_mu[3 * D_GROUP:3 * D_GROUP + RW_LORA]),
                       (SM_AD, rw_mu[3 * D_GROUP + RW_LORA:])])
    y_a = _rwkv(cf, B, T, mu3, musm, row(rw_w0), rw_w2, row(rw_a0), rw_a2, row(rw_kk), row(rw_ka),
                row(rw_rk), row(rw_gn_g), row(rw_gn_b))
    y_d = _gdn(cf, B, T, gdn_conv, _small_row([(SM_A, gdn_A_log)]), _small_row([(SM_A, gdn_dt_bias)]),
               jnp.tile(row(gdn_norm_g), (1, 2)))
    y_b = _attn(cb, cf, B, T, 'swa_q', 'swa_k', 'swa_v', 'swa_g', window=SWA_WINDOW, sinks=row(swa_sinks))
    cmp = _nsa_compress(cf, B, T, *_cmp_weights(nsa_pe_k, nsa_pe_v, nsa_k_w1, nsa_k_w2, nsa_v_w1, nsa_v_w2))
    c1, sel = _nsa_cmpattn(cb, cmp, cf, B, T, _overlap_matrix(T))
    c2 = _attn(cb, cf, B, T, 'nsa_q', 'nsa_ks', 'nsa_vs', 'nsa_g', sel=sel, gate_col=SM_GATE + N_HEADS, tk=4 * TQ,
               aq=AQ_SEL)
    c3 = _attn(cb, cf, B, T, 'nsa_q', 'nsa_kw', 'nsa_vw', 'nsa_g', window=NSA_WINDOW,
               gate_col=SM_GATE + 2 * N_HEADS)
    return _out_proj((y_a, y_b, c1, c2, c3, y_d), x, w_out, layer, row(ln_g), row(ln_b))


def kernel(x, w_in, w_out, ln_g, ln_b, rw_mu, rw_w0, rw_w2, rw_a0, rw_a2, rw_kk, rw_ka, rw_rk, rw_gn_g, rw_gn_b, swa_sinks, nsa_pe_k, nsa_pe_v, nsa_k_w1, nsa_k_w2, nsa_v_w1, nsa_v_w2, gdn_conv, gdn_A_log, gdn_dt_bias, gdn_norm_g):
    B, T, D = x.shape
    params = (ln_g, ln_b, rw_mu, rw_w0, rw_w2, rw_a0, rw_a2, rw_kk, rw_ka, rw_rk, rw_gn_g, rw_gn_b,
              swa_sinks, nsa_pe_k, nsa_pe_v, nsa_k_w1, nsa_k_w2, nsa_v_w1, nsa_v_w2, gdn_conv, gdn_A_log,
              gdn_dt_bias, gdn_norm_g)
    w_f32, w_bf = _projection_weights(w_in)
    w_out_b = w_out.astype(BF16)
    xf = x.reshape(B * T, D)
    xb = xf.astype(BF16)
    for i in range(w_in.shape[0]):
        xf, xb = _layer(xf, xb, B, T, i, w_f32, w_bf, w_out_b, *(p[i] for p in params))
    return xf.reshape(B, T, D)
```

```python
import functools
import math

import numpy as np
import jax
import jax.numpy as jnp
from jax import lax
from jax.experimental import pallas as pl
from jax.experimental.pallas import tpu as pltpu

F32 = jnp.float32
BF16 = jnp.bfloat16
HI = lax.Precision.HIGHEST

D_MODEL = 2048
DEPTH = 4
D_GROUP = 512
HEAD_DIM = 64
N_HEADS = 8
KV_HEADS = 2
HPG = N_HEADS // KV_HEADS
PAIR = 2 * HEAD_DIM
KV_W = KV_HEADS * HEAD_DIM
NEG_INF = -1e30
LN_EPS = 1e-5
DEEPNORM_ALPHA = (2 * DEPTH) ** 0.25
RW_LORA = 32
RW_GN_EPS = 64e-5
SWA_WINDOW = 128
NSA_CMP_BLOCK = 32
NSA_CMP_STRIDE = 16
NSA_CMP_HIDDEN = 128
NSA_SEL_BLOCK = 64
NSA_TOPN = 16
NSA_WINDOW = 512
NSA_FORCE = 1e6
GDN_CONV = 4
GDN_EPS = 1e-6
SCALE = HEAD_DIM ** -0.5
LOG2E = math.log2(math.e)

CHUNK = 64
RW_SEQS = 4
GDN_SEQS = 4
TQ = 128
AQ_BAND = 256
AQ_SEL = 512
TAIL = 8
VT_ROWS = HEAD_DIM + 16
VMEM_LIMIT = 56 * 1024 * 1024

_COLS = (
    ('rw_r', 512), ('rw_k', 512), ('rw_v', 512), ('rw_wd', 32), ('rw_ad', 32), ('rw_g', 512),
    ('swa_q', 512), ('swa_k', 128), ('swa_v', 128), ('swa_g', 512),
    ('nsa_q', 512), ('nsa_kc', 128), ('nsa_vc', 128), ('nsa_ks', 128), ('nsa_vs', 128),
    ('nsa_kw', 128), ('nsa_vw', 128), ('nsa_gate', 24), ('nsa_g', 512),
    ('gdn_q', 512), ('gdn_k', 512), ('gdn_v', 512), ('gdn_beta', 8), ('gdn_a', 8), ('gdn_g', 512),
)
_OFF = {}
_o = 0
for _n, _s in _COLS:
    _OFF[_n] = (_o, _s)
    _o += _s
N_IN = _o

F32_GROUPS = ('rw_r', 'rw_k', 'rw_v', 'rw_g', 'gdn_q', 'gdn_k', 'gdn_v', 'gdn_g', 'swa_g', 'nsa_g')
FCOL = {n: i for i, n in enumerate(F32_GROUPS)}
SMALL_W = 256
SMALL_BLK = len(F32_GROUPS) * 512 // SMALL_W
SM_WD, SM_AD, SM_GATE, SM_BETA, SM_A = 0, 32, 64, 88, 96
CMP_BLK = (len(F32_GROUPS) * 512 + SMALL_W) // KV_W
N_F32 = len(F32_GROUPS) * 512 + SMALL_W + 2 * KV_W
BF_Q = {'swa_q': 0, 'nsa_q': 1}
BF_KV = {n: 8 + i for i, n in enumerate(('swa_k', 'swa_v', 'nsa_ks', 'nsa_vs', 'nsa_kw', 'nsa_vw'))}
N_BF = 1024 + 6 * 128


def _dot(a, b, prec=None):
    return lax.dot_general(a, b, (((1,), (0,)), ((), ())), precision=prec, preferred_element_type=F32)


def _dot_nt(a, b, prec=None):
    return lax.dot_general(a, b, (((1,), (1,)), ((), ())), precision=prec, preferred_element_type=F32)


def _dot_tn(a, b, prec=None):
    return lax.dot_general(a, b, (((0,), (0,)), ((), ())), precision=prec, preferred_element_type=F32)


def _sigmoid(x):
    return 1.0 / (1.0 + jnp.exp(-x))


def _silu(x):
    return x * _sigmoid(x)


def _softplus(x):
    return jnp.maximum(x, 0.0) + jnp.log(1.0 + jnp.exp(-jnp.abs(x)))


def _iota2(shape, dim):
    return lax.broadcasted_iota(jnp.int32, shape, dim)


def _pairs(t):
    return [t[:, p * PAIR:(p + 1) * PAIR] for p in range(N_HEADS // 2)]


def _lo_mask(rows):
    return _iota2((rows, PAIR), 1) < HEAD_DIM


def _bd(x):
    xb = x.astype(BF16)
    lo = _lo_mask(x.shape[0]).astype(BF16)
    return jnp.concatenate([xb * lo, xb * (1 - lo)], axis=0)


def _pdot(a, b):
    return _dot(a.astype(BF16), _bd(b))


def _pdot_nt(a, b):
    return _dot_nt(a.astype(BF16), _bd(b))


def _pdot_tn(a, b):
    full = _dot_tn(a.astype(BF16), b.astype(BF16))
    return jnp.where(_lo_mask(HEAD_DIM), full[:HEAD_DIM], full[HEAD_DIM:])


def _segsum(t):
    lo = _lo_mask(t.shape[0])
    s_lo = jnp.sum(jnp.where(lo, t, 0.0), -1, keepdims=True)
    s_hi = jnp.sum(jnp.where(lo, 0.0, t), -1, keepdims=True)
    return jnp.where(lo, s_lo, s_hi)


def _inv_unit_lower(xs):
    n = xs[0].shape[0]
    eye2 = ((_iota2((n, 2 * n), 1) & (n - 1)) == _iota2((n, 2 * n), 0)).astype(F32)
    ps = [eye2 + x for x in xs]
    for _ in range(int(math.log2(n)) - 1):
        xs = [_pdot(x, x) for x in xs]
        ps = [p + _pdot(x, p) for x, p in zip(xs, ps)]
    return ps


def _tri_masks2(n):
    ri = _iota2((n, 2 * n), 0)
    ci = _iota2((n, 2 * n), 1) & (n - 1)
    return ri >= ci, ri > ci


def _params(sem):
    return pltpu.CompilerParams(dimension_semantics=sem, vmem_limit_bytes=VMEM_LIMIT)


def _mm_kernel(x_ref, w_ref, o_ref):
    o_ref[...] = jnp.dot(x_ref[...], w_ref[...], preferred_element_type=F32).astype(o_ref.dtype)


def _matmul(x, w, layer, out_dtype, tm, tn):
    m, k = x.shape
    n = w.shape[2]
    return pl.pallas_call(
        _mm_kernel,
        grid=(m // tm, n // tn),
        in_specs=[pl.BlockSpec((tm, k), lambda i, j: (i, 0)),
                  pl.BlockSpec((None, k, tn), lambda i, j: (layer, 0, j))],
        out_specs=pl.BlockSpec((tm, tn), lambda i, j: (i, j)),
        out_shape=jax.ShapeDtypeStruct((m, n), out_dtype),
        compiler_params=_params(("parallel", "arbitrary")),
        name="proj_in",
    )(x, w)


def _rwkv_kernel(r_ref, k_ref, v_ref, g_ref, sm_ref, mu_ref, musm_ref, w0_ref, w2_ref, a0_ref, a2_ref,
                 kk_ref, ka_ref, rk_ref, gng_ref, gnb_ref, o_ref, buf_ref, bufsm_ref, s_ref):
    c = pl.program_id(1)
    C = CHUNK

    NB = r_ref.shape[0]

    @pl.when(c == 0)
    def _():
        buf_ref[:, 0:TAIL, :] = jnp.zeros((NB, TAIL, 3 * D_GROUP), F32)
        bufsm_ref[:, 0:TAIL, :] = jnp.zeros((NB, TAIL, SMALL_W), F32)
        s_ref[...] = jnp.zeros(s_ref.shape, F32)

    ri = _iota2((C, C), 0)
    ci = _iota2((C, C), 1)
    tril = ri >= ci
    strict = ri > ci
    rk = rk_ref[...]
    gng = gng_ref[...]
    gnb = gnb_ref[...]
    pre = []
    for nb in range(NB):
        buf_ref[nb, TAIL:TAIL + C, 0:512] = r_ref[nb]
        buf_ref[nb, TAIL:TAIL + C, 512:1024] = k_ref[nb]
        buf_ref[nb, TAIL:TAIL + C, 1024:1536] = v_ref[nb]
        bufsm_ref[nb, TAIL:TAIL + C, :] = sm_ref[nb]
        cur = buf_ref[nb, TAIL:TAIL + C, :]
        prev = buf_ref[nb, TAIL - 1:TAIL - 1 + C, :]
        mixed = cur + (prev - cur) * mu_ref[...]
        cur_sm = bufsm_ref[nb, TAIL:TAIL + C, :]
        prev_sm = bufsm_ref[nb, TAIL - 1:TAIL - 1 + C, :]
        smix = cur_sm + (prev_sm - cur_sm) * musm_ref[...]
        buf_ref[nb, 0:TAIL, :] = buf_ref[nb, C:C + TAIL, :]
        bufsm_ref[nb, 0:TAIL, :] = bufsm_ref[nb, C:C + TAIL, :]

        r = mixed[:, 0:512]
        k = mixed[:, 512:1024]
        v = mixed[:, 1024:1536]
        wd = smix[:, SM_WD:SM_WD + RW_LORA]
        ad = smix[:, SM_AD:SM_AD + RW_LORA]
        wl = w0_ref[...] + _dot(jnp.tanh(wd), w2_ref[...], HI)
        logw = -jnp.exp(-_softplus(-wl) - 0.5)
        alpha = _sigmoid(a0_ref[...] + _dot(ad, a2_ref[...], HI))
        kkraw = k * kk_ref[...]
        k2 = k * (1.0 + (alpha - 1.0) * ka_ref[...])
        gcum = _dot(tril.astype(F32), logw, HI)
        pre.append((r, k2, v, alpha, kkraw, logw, gcum, _silu(g_ref[nb])))

    H = range(NB * N_HEADS // 2)
    tril2, strict2 = _tri_masks2(C)
    R, K2, V, AL, KKR, LW, G = ([t for p in pre for t in _pairs(p[f])] for f in range(7))
    KK = [t * lax.rsqrt(_segsum(t * t) + 1e-6) for t in KKR]
    GM = [t[C // 2 - 1:C // 2, :] for t in G]
    GE = [t[C - 1:C, :] for t in G]
    BV = [KK[h] * AL[h] for h in H]
    EINV = [jnp.exp(GM[h] - G[h]) for h in H]
    EEND = [jnp.exp(GE[h] - G[h]) for h in H]
    LEFT = [jnp.concatenate([-KK[h] * jnp.exp(G[h] - LW[h] - GM[h]), R[h] * jnp.exp(G[h] - GM[h])], axis=0)
            for h in H]
    AB = [_pdot_nt(LEFT[h], BV[h] * EINV[h]) for h in H]
    AK = [_pdot_nt(LEFT[h], K2[h] * EINV[h]) for h in H]
    A_RB = [jnp.where(tril2, t[C:], 0.0) for t in AB]
    A_K = [jnp.concatenate([jnp.where(strict2, t[:C], 0.0), jnp.where(tril2, t[C:], 0.0)], axis=0) for t in AK]
    TINV = _inv_unit_lower([jnp.where(strict2, t[:C], 0.0) for t in AB])
    S = [s_ref[h] for h in H]
    LS = [_pdot_nt(LEFT[h], S[h] * jnp.exp(GM[h])) for h in H]
    AV = [_pdot(A_K[h], V[h]) for h in H]
    U = [_pdot(TINV[h], LS[h][:C] + AV[h][:C]) for h in H]
    O = [LS[h][C:] + _pdot(A_RB[h], U[h]) + AV[h][C:] for h in H]
    for h in H:
        s_ref[h] = S[h] * jnp.exp(GE[h]) + _pdot_tn(
            jnp.concatenate([U[h], V[h]], axis=0),
            jnp.concatenate([BV[h] * EEND[h], K2[h] * EEND[h]], axis=0))
    for n in H:
        nb, p = divmod(n, N_HEADS // 2)
        sl = slice(p * PAIR, (p + 1) * PAIR)
        mu = _segsum(O[n]) * (1.0 / HEAD_DIM)
        d = O[n] - mu
        var = _segsum(d * d) * (1.0 / HEAD_DIM)
        y = d * lax.rsqrt(var + RW_GN_EPS) * gng[:, sl] + gnb[:, sl]
        bonus = _segsum(R[n] * K2[n] * rk[:, sl]) * V[n]
        o_ref[nb, :, sl] = ((y + bonus) * pre[nb][7][:, sl]).astype(o_ref.dtype)


def _seq_specs(names, nbs):
    specs = [pl.BlockSpec((nbs, CHUNK, 512), lambda b, c, j=FCOL[n]: (b, c, j)) for n in names]
    return specs + [pl.BlockSpec((nbs, CHUNK, SMALL_W), lambda b, c: (b, c, SMALL_BLK))]


def _rwkv(cf, B, T, mu3, musm, w0, w2, a0, a2, kk, ka, rk, gng, gnb):
    nbs = RW_SEQS if B % RW_SEQS == 0 else 1
    cf3 = cf.reshape(B, T, cf.shape[-1])
    full = lambda a: pl.BlockSpec(a.shape, lambda b, c: (0,) * a.ndim)
    ps = (mu3, musm, w0, w2, a0, a2, kk, ka, rk, gng, gnb)
    out = pl.pallas_call(
        _rwkv_kernel,
        grid=(B // nbs, T // CHUNK),
        in_specs=_seq_specs(('rw_r', 'rw_k', 'rw_v', 'rw_g'), nbs) + [full(a) for a in ps],
        out_specs=pl.BlockSpec((nbs, CHUNK, 512), lambda b, c: (b, c, 0)),
        out_shape=jax.ShapeDtypeStruct((B, T, D_GROUP), BF16),
        scratch_shapes=[pltpu.VMEM((nbs, TAIL + CHUNK, 3 * D_GROUP), F32),
                        pltpu.VMEM((nbs, TAIL + CHUNK, SMALL_W), F32),
                        pltpu.VMEM((nbs * N_HEADS // 2, HEAD_DIM, PAIR), F32)],
        compiler_params=_params(("parallel", "arbitrary")),
        name="rwkv7",
    )(cf3, cf3, cf3, cf3, cf3, *ps)
    return out.reshape(B * T, D_GROUP)


def _gdn_kernel(q_ref, k_ref, v_ref, z_ref, sm_ref, conv_ref, alog_ref, dtb_ref, ng_ref, o_ref,
                buf_ref, s_ref):
    c = pl.program_id(1)
    C = CHUNK

    NB = q_ref.shape[0]

    @pl.when(c == 0)
    def _():
        buf_ref[:, 0:TAIL, :] = jnp.zeros((NB, TAIL, 3 * D_GROUP), F32)
        s_ref[...] = jnp.zeros(s_ref.shape, F32)

    conv = conv_ref[...]
    lane = _iota2((C, SMALL_W), 1)
    ri = _iota2((C, C), 0)
    ci = _iota2((C, C), 1)
    tril = ri >= ci
    strict = ri > ci
    ng = ng_ref[...]
    pre = []
    for nb in range(NB):
        buf_ref[nb, TAIL:TAIL + C, 0:512] = q_ref[nb]
        buf_ref[nb, TAIL:TAIL + C, 512:1024] = k_ref[nb]
        buf_ref[nb, TAIL:TAIL + C, 1024:1536] = v_ref[nb]
        acc = buf_ref[nb, TAIL:TAIL + C, :] * conv[GDN_CONV - 1:GDN_CONV, :]
        for i in range(GDN_CONV - 1):
            sh = GDN_CONV - 1 - i
            acc = acc + buf_ref[nb, TAIL - sh:TAIL - sh + C, :] * conv[i:i + 1, :]
        buf_ref[nb, 0:TAIL, :] = buf_ref[nb, C:C + TAIL, :]
        qkv = _silu(acc)
        sm = sm_ref[nb]
        g_all = jnp.where((lane >= SM_A) & (lane < SM_A + N_HEADS),
                          -jnp.exp(alog_ref[...]) * _softplus(sm + dtb_ref[...]), 0.0)
        gam_all = _dot(tril.astype(F32), g_all, HI)
        pre.append((qkv[:, 0:512], qkv[:, 512:1024], qkv[:, 1024:1536], _sigmoid(sm), gam_all, _silu(z_ref[nb])))

    H = range(NB * N_HEADS // 2)
    hd = lambda n: divmod(n, N_HEADS // 2)
    lo = _lo_mask(C)
    tril2, strict2 = _tri_masks2(C)
    diag2 = (_iota2((C, PAIR), 1) & (HEAD_DIM - 1)) == _iota2((C, PAIR), 0)

    def per_head(n, field, col):
        nb, p = hd(n)
        t = pre[nb][field]
        return jnp.where(lo, t[:, col + 2 * p:col + 2 * p + 1], t[:, col + 2 * p + 1:col + 2 * p + 2])

    Q = [t * lax.rsqrt(_segsum(t * t) + 1e-6) * SCALE for p in pre for t in _pairs(p[0])]
    K = [t * lax.rsqrt(_segsum(t * t) + 1e-6) for p in pre for t in _pairs(p[1])]
    V = [t for p in pre for t in _pairs(p[2])]
    BETA = [per_head(n, 3, SM_BETA) for n in H]
    GAM = [per_head(n, 4, SM_A) for n in H]
    GROW = [jnp.sum(jnp.where(diag2, t, 0.0), 0, keepdims=True) for t in GAM]
    DECAY = [jnp.exp(jnp.where(tril2, GAM[n] - GROW[n], NEG_INF)) for n in H]
    EG = [jnp.exp(t) for t in GAM]
    KB = [K[h] * BETA[h] for h in H]
    KKQ = [_pdot_nt(jnp.concatenate([KB[h], Q[h]], axis=0), K[h]) for h in H]
    QK = [KKQ[h][C:] * DECAY[h] for h in H]
    TM = _inv_unit_lower([-jnp.where(strict2, KKQ[h][:C] * DECAY[h], 0.0) for h in H])
    UU = [_pdot(TM[h], V[h] * BETA[h]) for h in H]
    W = [_pdot(TM[h], KB[h] * EG[h]) for h in H]
    S = [s_ref[h] for h in H]
    WS = [_pdot(jnp.concatenate([W[h], Q[h] * EG[h]], axis=0), S[h]) for h in H]
    VN = [UU[h] - WS[h][:C] for h in H]
    O = [WS[h][C:] + _pdot(QK[h], VN[h]) for h in H]
    for h in H:
        g_last = GAM[h][C - 1:C, :]
        s_ref[h] = S[h] * jnp.exp(g_last) + _pdot_tn(K[h] * jnp.exp(g_last - GAM[h]), VN[h])
    for n in H:
        nb, p = hd(n)
        sl = slice(p * PAIR, (p + 1) * PAIR)
        o = O[n] * lax.rsqrt(_segsum(O[n] * O[n]) * (1.0 / HEAD_DIM) + GDN_EPS) * ng
        o_ref[nb, :, sl] = (o * pre[nb][5][:, sl]).astype(o_ref.dtype)


def _gdn(cf, B, T, conv, alog_sm, dtb_sm, ng):
    nbs = GDN_SEQS if B % GDN_SEQS == 0 else 1
    cf3 = cf.reshape(B, T, cf.shape[-1])
    full = lambda a: pl.BlockSpec(a.shape, lambda b, c: (0,) * a.ndim)
    ps = (conv, alog_sm, dtb_sm, ng)
    out = pl.pallas_call(
        _gdn_kernel,
        grid=(B // nbs, T // CHUNK),
        in_specs=_seq_specs(('gdn_q', 'gdn_k', 'gdn_v', 'gdn_g'), nbs) + [full(a) for a in ps],
        out_specs=pl.BlockSpec((nbs, CHUNK, 512), lambda b, c: (b, c, 0)),
        out_shape=jax.ShapeDtypeStruct((B, T, D_GROUP), BF16),
        scratch_shapes=[pltpu.VMEM((nbs, TAIL + CHUNK, 3 * D_GROUP), F32),
                        pltpu.VMEM((nbs * N_HEADS // 2, HEAD_DIM, PAIR), F32)],
        compiler_params=_params(("parallel", "arbitrary")),
        name="gdn",
    )(cf3, cf3, cf3, cf3, cf3, *ps)
    return out.reshape(B * T, D_GROUP)


def _store_heads(o_ref, g, ot, gate, bgate, gate_col):
    aq = o_ref.shape[0]
    for j in range(HPG):
        h = g * HPG + j
        sl = slice(h * HEAD_DIM, (h + 1) * HEAD_DIM)
        oh = ot[:, j * aq:(j + 1) * aq].T
        if bgate is not None:
            oh = oh * bgate[:, gate_col + h:gate_col + h + 1]
        o_ref[:, sl] = oh * gate[:, sl]


def _attn_kernel(*refs, window, use_sel, use_sink, gate_col, tk):
    it = iter(refs)
    q_ref, k_ref, v_ref, g_ref = next(it), next(it), next(it), next(it)
    sm_ref = next(it) if gate_col is not None else None
    sel_ref = next(it) if use_sel else None
    sink_ref = next(it) if use_sink else None
    o_ref = next(it)
    vt_ref = next(it)
    selx_ref = next(it) if use_sel else None
    i = pl.program_id(1)
    aq = q_ref.shape[0]
    W = HPG * aq

    @pl.when(i == 0)
    def _():
        eye = (_iota2((KV_W, KV_W), 0) == _iota2((KV_W, KV_W), 1)).astype(BF16)
        ones = jnp.ones((VT_ROWS - HEAD_DIM, tk), BF16)
        for c in range(vt_ref.shape[0]):
            vt = _dot_nt(eye, v_ref[c * tk:(c + 1) * tk, :]).astype(BF16)
            for g in range(KV_HEADS):
                vt_ref[c, g] = jnp.concatenate([vt[g * HEAD_DIM:(g + 1) * HEAD_DIM], ones], axis=0)

    span = tk if window is None else window + aq
    qpos = i * aq + (_iota2((span, W), 1) & (aq - 1))
    krow = _iota2((span, W), 0)
    gate = _silu(g_ref[...])
    if gate_col is not None:
        bgate = _sigmoid(sm_ref[...])

    G = range(KV_HEADS)
    QG = [jnp.concatenate([q_ref[:, h * HEAD_DIM:(h + 1) * HEAD_DIM] for h in range(g * HPG, (g + 1) * HPG)],
                          axis=0) * (SCALE * LOG2E) for g in G]
    if use_sel:
        for g in G:
            selx_ref[g] = jnp.concatenate([sel_ref[0, g]] * HPG, axis=1)
        nb = tk // NSA_SEL_BLOCK

    if window is not None:
        nt = span // tk
        t0 = jnp.maximum(i * aq - window, 0) // tk
        start = pl.multiple_of(t0 * tk, tk)
        rel = qpos - (start + krow)
        pbias = jnp.where((rel >= 0) & (rel < window), 0.0, NEG_INF)
        for g in G:
            ksl = slice(g * HEAD_DIM, (g + 1) * HEAD_DIM)
            s = _dot_nt(k_ref[pl.ds(start, span), ksl], QG[g]) + pbias
            m = jnp.max(s, 0, keepdims=True)
            if use_sink:
                sink = jnp.concatenate([jnp.zeros((1, aq), F32) + sink_ref[0:1, h:h + 1] * LOG2E
                                        for h in range(g * HPG, (g + 1) * HPG)], axis=1)
                m = jnp.maximum(m, sink)
            p = jnp.exp2((s - m).astype(BF16))
            vt = jnp.concatenate([vt_ref[t0 + j, g] for j in range(nt)], axis=1)
            pv = _dot(vt, p)
            l = pv[HEAD_DIM:HEAD_DIM + 1]
            if use_sink:
                l = l + jnp.exp2(sink - m)
            _store_heads(o_ref, g, pv[:HEAD_DIM] / l, gate, bgate if gate_col is not None else None, gate_col)
        return

    hi = (i * aq + aq + tk - 1) // tk

    def body(kt, carry, masked):
        off = pl.multiple_of(kt * tk, tk)
        if masked:
            pbias = jnp.where(qpos - (kt * tk + krow) >= 0, 0.0, NEG_INF)
        out = []
        for g in G:
            m, acc = carry[2 * g:2 * g + 2]
            ksl = slice(g * HEAD_DIM, (g + 1) * HEAD_DIM)
            s = _dot_nt(k_ref[pl.ds(off, tk), ksl], QG[g])
            if masked:
                s = s + pbias
            if use_sel:
                s = s + jnp.concatenate(
                    [jnp.broadcast_to(selx_ref[g, pl.ds(kt * nb + j, 1), :], (NSA_SEL_BLOCK, W))
                     for j in range(nb)], axis=0)
            m_new = jnp.maximum(m, jnp.max(s, 0, keepdims=True))
            a = jnp.exp2(m - m_new)
            p = jnp.exp2((s - m_new).astype(BF16))
            out += [m_new, a * acc + _dot(vt_ref[kt, g], p)]
        return tuple(out)

    init = []
    for g in G:
        init += [jnp.full((1, W), NEG_INF, F32), jnp.zeros((VT_ROWS, W), F32)]
    res = lax.fori_loop(0, hi - 1, functools.partial(body, masked=False), tuple(init))
    res = body(hi - 1, res, True)
    for g in G:
        acc = res[2 * g + 1]
        _store_heads(o_ref, g, acc[:HEAD_DIM] / acc[HEAD_DIM:HEAD_DIM + 1], gate,
                     bgate if gate_col is not None else None, gate_col)


def _attn(cb, cf, B, T, qname, kname, vname, gname, *, window=None, sel=None, sinks=None, gate_col=None, tk=TQ,
          aq=AQ_BAND):
    nq = T // aq
    in_specs = [pl.BlockSpec((aq, 512), lambda b, i, j=BF_Q[qname]: (b * nq + i, j)),
                pl.BlockSpec((T, KV_W), lambda b, i, j=BF_KV[kname]: (b, j)),
                pl.BlockSpec((T, KV_W), lambda b, i, j=BF_KV[vname]: (b, j)),
                pl.BlockSpec((aq, 512), lambda b, i, j=FCOL[gname]: (b * nq + i, j))]
    args = [cb, cb, cb, cf]
    if gate_col is not None:
        in_specs.append(pl.BlockSpec((aq, SMALL_W), lambda b, i: (b * nq + i, SMALL_BLK)))
        args.append(cf)
    scratch = [pltpu.VMEM((T // tk, KV_HEADS, VT_ROWS, tk), BF16)]
    if sel is not None:
        in_specs.append(pl.BlockSpec((1, KV_HEADS, sel.shape[2], aq), lambda b, i: (b, 0, 0, i)))
        args.append(sel)
        scratch.append(pltpu.VMEM((KV_HEADS, sel.shape[2], HPG * aq), F32))
    if sinks is not None:
        in_specs.append(pl.BlockSpec(sinks.shape, lambda b, i: (0, 0)))
        args.append(sinks)
    return pl.pallas_call(
        functools.partial(_attn_kernel, window=window, use_sel=sel is not None,
                          use_sink=sinks is not None, gate_col=gate_col, tk=tk),
        grid=(B, nq),
        in_specs=in_specs,
        out_specs=pl.BlockSpec((aq, 512), lambda b, i: (b * nq + i, 0)),
        out_shape=jax.ShapeDtypeStruct((B * T, D_GROUP), F32),
        scratch_shapes=scratch,
        compiler_params=_params(("parallel", "arbitrary")),
        name="attn_" + qname + ("_sel" if sel is not None else "_w%d" % window),
    )(*args)


def _gelu_tanh(x):
    return 0.5 * x * (1.0 + jnp.tanh(math.sqrt(2.0 / math.pi) * (x + 0.044715 * x * x * x)))


def _cmp_kernel(kc_ref, vc_ref, pe_ref, w1_ref, w2_ref, o_ref):
    nr = o_ref.shape[1]
    outs = []
    for s, ref in enumerate((kc_ref, vc_ref)):
        p1 = jnp.zeros((nr, 2 * NSA_CMP_HIDDEN), F32)
        p2 = jnp.zeros((nr, 2 * NSA_CMP_HIDDEN), F32)
        for l in range(NSA_CMP_STRIDE):
            x = ref[pl.ds(l, nr, stride=NSA_CMP_STRIDE), :]
            p1 = p1 + _dot((x + pe_ref[s, l]).astype(BF16), w1_ref[s, l])
            p2 = p2 + _dot((x + pe_ref[s, NSA_CMP_STRIDE + l]).astype(BF16), w1_ref[s, NSA_CMP_STRIDE + l])
        hid = _gelu_tanh(p1 + pltpu.roll(p2, nr - 1, 0))
        outs.append(_dot(hid.astype(BF16), w2_ref[s]))
    o_ref[0] = jnp.concatenate(outs, axis=1)


def _nsa_compress(cf, B, T, pe, w1, w2):
    nr = T // NSA_CMP_STRIDE
    full = lambda a: pl.BlockSpec(a.shape, lambda b: (0,) * a.ndim)
    return pl.pallas_call(
        _cmp_kernel,
        grid=(B,),
        in_specs=[pl.BlockSpec((T, KV_W), lambda b: (b, CMP_BLK)), pl.BlockSpec((T, KV_W), lambda b: (b, CMP_BLK + 1))]
                 + [full(a) for a in (pe, w1, w2)],
        out_specs=pl.BlockSpec((1, nr, 2 * KV_W), lambda b: (b, 0, 0)),
        out_shape=jax.ShapeDtypeStruct((B, nr, 2 * KV_W), F32),
        compiler_params=_params(("arbitrary",)),
        name="nsa_compress",
    )(cf, cf, pe, w1, w2)


def _cmpattn_kernel(q_ref, cmp_ref, g_ref, sm_ref, ov_ref, o_ref, sel_ref):
    i = pl.program_id(1)
    kv = cmp_ref[0]
    nr = kv.shape[0]
    ns = ov_ref.shape[0]
    tq = q_ref.shape[0]
    tpos = i * tq + _iota2((tq, nr), 0)
    cmask = _iota2((tq, nr), 1) * NSA_CMP_STRIDE + (NSA_CMP_BLOCK - 1) <= tpos
    gate = _silu(g_ref[...])
    bgate = _sigmoid(sm_ref[...])
    tblk = (i * tq + _iota2((ns, tq), 1)) // NSA_SEL_BLOCK
    jj = _iota2((ns, tq), 0)
    forced = (jj == 0) | (jj == tblk) | (jj == tblk - 1)
    causal = jj <= tblk
    cbias = jnp.where(cmask, 0.0, NEG_INF)
    for g in range(KV_HEADS):
        kc = kv[:, g * HEAD_DIM:(g + 1) * HEAD_DIM].astype(BF16)
        vc = kv[:, KV_W + g * HEAD_DIM:KV_W + (g + 1) * HEAD_DIM].astype(BF16)
        heads = range(g * HPG, (g + 1) * HPG)
        qg = jnp.concatenate([q_ref[:, h * HEAD_DIM:(h + 1) * HEAD_DIM] for h in heads], axis=0) * SCALE
        s = _dot_nt(qg, kc).reshape(HPG, tq, nr) + cbias[None]
        e = jnp.exp(s - jnp.max(s, -1, keepdims=True))
        p = jnp.where(cmask[None], e / jnp.sum(e, -1, keepdims=True), 0.0)
        psum = jnp.sum(p, axis=0)
        o = _dot(p.reshape(HPG * tq, nr).astype(BF16), vc)
        for j, h in enumerate(heads):
            sl = slice(h * HEAD_DIM, (h + 1) * HEAD_DIM)
            o_ref[:, sl] = o[j * tq:(j + 1) * tq] * bgate[:, SM_GATE + h:SM_GATE + h + 1] * gate[:, sl]
        imp = _dot_nt(ov_ref[...], psum, HI)
        imp = jnp.where(causal, jnp.where(forced, NSA_FORCE, imp), NEG_INF)
        rank = jnp.zeros((ns, tq), jnp.int32)
        for j in range(ns):
            row = imp[j:j + 1, :]
            rank = rank + ((row > imp) | ((row == imp) & (j < jj))).astype(jnp.int32)
        sel_ref[0, g] = jnp.where((rank < NSA_TOPN) & causal, 0.0, NEG_INF)


def _nsa_cmpattn(cb, cmp, cf, B, T, ov):
    tq = AQ_BAND
    nq = T // tq
    nr = cmp.shape[1]
    ns = T // NSA_SEL_BLOCK
    return pl.pallas_call(
        _cmpattn_kernel,
        grid=(B, nq),
        in_specs=[pl.BlockSpec((tq, 512), lambda b, i: (b * nq + i, BF_Q['nsa_q'])),
                  pl.BlockSpec((1, nr, 2 * KV_W), lambda b, i: (b, 0, 0)),
                  pl.BlockSpec((tq, 512), lambda b, i: (b * nq + i, FCOL['nsa_g'])),
                  pl.BlockSpec((tq, SMALL_W), lambda b, i: (b * nq + i, SMALL_BLK)),
                  pl.BlockSpec(ov.shape, lambda b, i: (0, 0))],
        out_specs=[pl.BlockSpec((tq, 512), lambda b, i: (b * nq + i, 0)),
                   pl.BlockSpec((1, KV_HEADS, ns, tq), lambda b, i: (b, 0, 0, i))],
        out_shape=[jax.ShapeDtypeStruct((B * T, D_GROUP), F32),
                   jax.ShapeDtypeStruct((B, KV_HEADS, ns, T), F32)],
        compiler_params=_params(("parallel", "arbitrary")),
        name="nsa_cmpattn",
    )(cb, cmp, cf, cf, ov)


def _out_kernel(ya_ref, yb_ref, c1_ref, c2_ref, c3_ref, yd_ref, x_ref, w_ref, lg_ref, lb_ref, o_ref, ob_ref):
    yc = c1_ref[...] + c2_ref[...] + c3_ref[...]
    acc = _dot(ya_ref[...], w_ref[0:512, :])
    acc = acc + _dot(yb_ref[...].astype(BF16), w_ref[512:1024, :])
    acc = acc + _dot(yc.astype(BF16), w_ref[1024:1536, :])
    acc = acc + _dot(yd_ref[...], w_ref[1536:2048, :])
    z = DEEPNORM_ALPHA * x_ref[...] + acc
    mu = jnp.mean(z, -1, keepdims=True)
    var = jnp.mean(jnp.square(z - mu), -1, keepdims=True)
    out = (z - mu) * lax.rsqrt(var + LN_EPS) * lg_ref[...] + lb_ref[...]
    o_ref[...] = out
    ob_ref[...] = out.astype(BF16)


def _out_proj(ys, x, w, layer, lg, lb, tm=512):
    m = x.shape[0]
    yspec = pl.BlockSpec((tm, 512), lambda i: (i, 0))
    xspec = pl.BlockSpec((tm, D_MODEL), lambda i: (i, 0))
    return pl.pallas_call(
        _out_kernel,
        grid=(m // tm,),
        in_specs=[yspec] * 6 + [xspec, pl.BlockSpec((None,) + w.shape[1:], lambda i: (layer, 0, 0),
                                                    pipeline_mode=pl.Buffered(1)),
                                pl.BlockSpec(lg.shape, lambda i: (0, 0)), pl.BlockSpec(lb.shape, lambda i: (0, 0))],
        out_specs=[xspec, xspec],
        out_shape=[jax.ShapeDtypeStruct((m, D_MODEL), F32), jax.ShapeDtypeStruct((m, D_MODEL), BF16)],
        compiler_params=_params(("parallel",)),
        name="out_proj_ln",
    )(*ys, x, w, lg, lb)


def _cols(w, name):
    o, s = _OFF[name]
    return w[..., o:o + s]


def _small_row(pieces):
    parts, pos = [], 0
    for off, vals in pieces:
        parts += [jnp.zeros((off - pos,), F32), vals.astype(F32)]
        pos = off + vals.shape[-1]
    return jnp.concatenate(parts + [jnp.zeros((SMALL_W - pos,), F32)]).reshape(1, SMALL_W)


def _overlap_matrix(T):
    nc = T // NSA_CMP_STRIDE
    ns = T // NSA_SEL_BLOCK
    cst = np.arange(nc) * NSA_CMP_STRIDE
    jst = np.arange(ns) * NSA_SEL_BLOCK
    ov = np.clip(np.minimum(cst[:, None] + NSA_CMP_BLOCK, jst[None, :] + NSA_SEL_BLOCK)
                 - np.maximum(cst[:, None], jst[None, :]), 0, None).astype(np.float32) / NSA_CMP_BLOCK
    return jnp.asarray(ov.T)


def _cmp_weights(pe_k, pe_v, k_w1, k_w2, v_w1, v_w2):
    eye = jnp.eye(KV_HEADS, dtype=F32)
    w1 = jnp.stack([k_w1, v_w1]).reshape(2, NSA_CMP_BLOCK, HEAD_DIM, NSA_CMP_HIDDEN)
    w1 = jnp.einsum('sldj,gh->slgdhj', w1, eye).reshape(2, NSA_CMP_BLOCK, KV_W, KV_HEADS * NSA_CMP_HIDDEN)
    w2 = jnp.einsum('sjd,gh->sgjhd', jnp.stack([k_w2, v_w2]), eye).reshape(2, KV_HEADS * NSA_CMP_HIDDEN, KV_W)
    pe = jnp.tile(jnp.stack([pe_k, pe_v])[:, :, None, :], (1, 1, 1, KV_HEADS))
    return pe, w1.astype(BF16), w2.astype(BF16)


def _projection_weights(w_in):
    small = jnp.concatenate([_cols(w_in, n) for n in ('rw_wd', 'rw_ad', 'nsa_gate', 'gdn_beta', 'gdn_a')], axis=-1)
    small = jnp.pad(small, ((0, 0), (0, 0), (0, SMALL_W - small.shape[-1])))
    w_f32 = jnp.concatenate([_cols(w_in, n) for n in F32_GROUPS] + [small, _cols(w_in, 'nsa_kc'),
                                                                    _cols(w_in, 'nsa_vc')], axis=-1)
    w_bf = jnp.concatenate([_cols(w_in, n) for n in ('swa_q', 'nsa_q', 'swa_k', 'swa_v', 'nsa_ks', 'nsa_vs',
                                                     'nsa_kw', 'nsa_vw')], axis=-1)
    return w_f32.astype(BF16), w_bf.astype(BF16)


def _layer(x, xb, B, T, layer, w_f32, w_bf, w_out, ln_g, ln_b, rw_mu, rw_w0, rw_w2, rw_a0, rw_a2, rw_kk, rw_ka,
           rw_rk, rw_gn_g, rw_gn_b, swa_sinks, nsa_pe_k, nsa_pe_v, nsa_k_w1, nsa_k_w2, nsa_v_w1, nsa_v_w2,
           gdn_conv, gdn_A_log, gdn_dt_bias, gdn_norm_g):
    tm = min(2048, B * T)
    cf = _matmul(xb, w_f32, layer, F32, tm, 512)
    cb = _matmul(xb, w_bf, layer, BF16, tm, N_BF // 2)

    row = lambda a: a.reshape(1, -1).astype(F32)
    mu3 = row(rw_mu[:3 * D_GROUP])
    musm = _small_row([(SM_WD, rw_mu[3 * D_GROUP:3 * D_GROUP + RW_LORA]),
                       (SM_AD, rw_mu[3 * D_GROUP + RW_LORA:])])
    y_a = _rwkv(cf, B, T, mu3, musm, row(rw_w0), rw_w2, row(rw_a0), rw_a2, row(rw_kk), row(rw_ka),
                row(rw_rk), row(rw_gn_g), row(rw_gn_b))
    y_d = _gdn(cf, B, T, gdn_conv, _small_row([(SM_A, gdn_A_log)]), _small_row([(SM_A, gdn_dt_bias)]),
               jnp.tile(row(gdn_norm_g), (1, 2)))
    y_b = _attn(cb, cf, B, T, 'swa_q', 'swa_k', 'swa_v', 'swa_g', window=SWA_WINDOW, sinks=row(swa_sinks))
    cmp = _nsa_compress(cf, B, T, *_cmp_weights(nsa_pe_k, nsa_pe_v, nsa_k_w1, nsa_k_w2, nsa_v_w1, nsa_v_w2))
    c1, sel = _nsa_cmpattn(cb, cmp, cf, B, T, _overlap_matrix(T))
    c2 = _attn(cb, cf, B, T, 'nsa_q', 'nsa_ks', 'nsa_vs', 'nsa_g', sel=sel, gate_col=SM_GATE + N_HEADS, tk=4 * TQ,
               aq=AQ_SEL)
    c3 = _attn(cb, cf, B, T, 'nsa_q', 'nsa_kw', 'nsa_vw', 'nsa_g', window=NSA_WINDOW,
               gate_col=SM_GATE + 2 * N_HEADS)
    return _out_proj((y_a, y_b, c1, c2, c3, y_d), x, w_out, layer, row(ln_g), row(ln_b))


def kernel(x, w_in, w_out, ln_g, ln_b, rw_mu, rw_w0, rw_w2, rw_a0, rw_a2, rw_kk, rw_ka, rw_rk, rw_gn_g, rw_gn_b, swa_sinks, nsa_pe_k, nsa_pe_v, nsa_k_w1, nsa_k_w2, nsa_v_w1, nsa_v_w2, gdn_conv, gdn_A_log, gdn_dt_bias, gdn_norm_g):
    B, T, D = x.shape
    params = (ln_g, ln_b, rw_mu, rw_w0, rw_w2, rw_a0, rw_a2, rw_kk, rw_ka, rw_rk, rw_gn_g, rw_gn_b,
              swa_sinks, nsa_pe_k, nsa_pe_v, nsa_k_w1, nsa_k_w2, nsa_v_w1, nsa_v_w2, gdn_conv, gdn_A_log,
              gdn_dt_bias, gdn_norm_g)
    w_f32, w_bf = _projection_weights(w_in)
    w_out_b = w_out.astype(BF16)
    xf = x.reshape(B * T, D)
    xb = xf.astype(BF16)
    for i in range(w_in.shape[0]):
        xf, xb = _layer(xf, xb, B, T, i, w_f32, w_bf, w_out_b, *(p[i] for p in params))
    return xf.reshape(B, T, D)
```

```python
import functools
import math

import numpy as np
import jax
import jax.numpy as jnp
from jax import lax
from jax.experimental import pallas as pl
from jax.experimental.pallas import tpu as pltpu

F32 = jnp.float32
BF16 = jnp.bfloat16
HI = lax.Precision.HIGHEST

D_MODEL = 2048
DEPTH = 4
D_GROUP = 512
HEAD_DIM = 64
N_HEADS = 8
KV_HEADS = 2
HPG = N_HEADS // KV_HEADS
PAIR = 2 * HEAD_DIM
KV_W = KV_HEADS * HEAD_DIM
NEG_INF = -1e30
LN_EPS = 1e-5
DEEPNORM_ALPHA = (2 * DEPTH) ** 0.25
RW_LORA = 32
RW_GN_EPS = 64e-5
SWA_WINDOW = 128
NSA_CMP_BLOCK = 32
NSA_CMP_STRIDE = 16
NSA_CMP_HIDDEN = 128
NSA_SEL_BLOCK = 64
NSA_TOPN = 16
NSA_WINDOW = 512
NSA_FORCE = 1e6
GDN_CONV = 4
GDN_EPS = 1e-6
SCALE = HEAD_DIM ** -0.5
LOG2E = math.log2(math.e)

CHUNK = 64
RW_SEQS = 4
GDN_SEQS = 4
TQ = 128
AQ_BAND = 256
AQ_SEL = 512
TAIL = 8
VT_ROWS = HEAD_DIM + 16
VMEM_LIMIT = 56 * 1024 * 1024

_COLS = (
    ('rw_r', 512), ('rw_k', 512), ('rw_v', 512), ('rw_wd', 32), ('rw_ad', 32), ('rw_g', 512),
    ('swa_q', 512), ('swa_k', 128), ('swa_v', 128), ('swa_g', 512),
    ('nsa_q', 512), ('nsa_kc', 128), ('nsa_vc', 128), ('nsa_ks', 128), ('nsa_vs', 128),
    ('nsa_kw', 128), ('nsa_vw', 128), ('nsa_gate', 24), ('nsa_g', 512),
    ('gdn_q', 512), ('gdn_k', 512), ('gdn_v', 512), ('gdn_beta', 8), ('gdn_a', 8), ('gdn_g', 512),
)
_OFF = {}
_o = 0
for _n, _s in _COLS:
    _OFF[_n] = (_o, _s)
    _o += _s
N_IN = _o

F32_GROUPS = ('rw_r', 'rw_k', 'rw_v', 'rw_g', 'gdn_q', 'gdn_k', 'gdn_v', 'gdn_g', 'swa_g', 'nsa_g')
FCOL = {n: i for i, n in enumerate(F32_GROUPS)}
SMALL_W = 256
SMALL_BLK = len(F32_GROUPS) * 512 // SMALL_W
SM_WD, SM_AD, SM_GATE, SM_BETA, SM_A = 0, 32, 64, 88, 96
CMP_BLK = (len(F32_GROUPS) * 512 + SMALL_W) // KV_W
N_F32 = len(F32_GROUPS) * 512 + SMALL_W + 2 * KV_W
BF_Q = {'swa_q': 0, 'nsa_q': 1}
BF_KV = {n: 8 + i for i, n in enumerate(('swa_k', 'swa_v', 'nsa_ks', 'nsa_vs', 'nsa_kw', 'nsa_vw'))}
N_BF = 1024 + 6 * 128


def _dot(a, b, prec=None):
    return lax.dot_general(a, b, (((1,), (0,)), ((), ())), precision=prec, preferred_element_type=F32)


def _dot_nt(a, b, prec=None):
    return lax.dot_general(a, b, (((1,), (1,)), ((), ())), precision=prec, preferred_element_type=F32)


def _dot_tn(a, b, prec=None):
    return lax.dot_general(a, b, (((0,), (0,)), ((), ())), precision=prec, preferred_element_type=F32)


def _sigmoid(x):
    return 1.0 / (1.0 + jnp.exp(-x))


def _silu(x):
    return x * _sigmoid(x)


def _softplus(x):
    return jnp.maximum(x, 0.0) + jnp.log(1.0 + jnp.exp(-jnp.abs(x)))


def _iota2(shape, dim):
    return lax.broadcasted_iota(jnp.int32, shape, dim)


def _pairs(t):
    return [t[:, p * PAIR:(p + 1) * PAIR] for p in range(N_HEADS // 2)]


def _lo_mask(rows):
    return _iota2((rows, PAIR), 1) < HEAD_DIM


def _bd(x):
    xb = x.astype(BF16)
    lo = _lo_mask(x.shape[0]).astype(BF16)
    return jnp.concatenate([xb * lo, xb * (1 - lo)], axis=0)


def _pdot(a, b):
    return _dot(a.astype(BF16), _bd(b))


def _pdot_nt(a, b):
    return _dot_nt(a.astype(BF16), _bd(b))


def _pdot_tn(a, b):
    full = _dot_tn(a.astype(BF16), b.astype(BF16))
    return jnp.where(_lo_mask(HEAD_DIM), full[:HEAD_DIM], full[HEAD_DIM:])


def _segsum(t):
    lo = _lo_mask(t.shape[0])
    s_lo = jnp.sum(jnp.where(lo, t, 0.0), -1, keepdims=True)
    s_hi = jnp.sum(jnp.where(lo, 0.0, t), -1, keepdims=True)
    return jnp.where(lo, s_lo, s_hi)


def _inv_unit_lower(xs):
    n = xs[0].shape[0]
    eye2 = ((_iota2((n, 2 * n), 1) & (n - 1)) == _iota2((n, 2 * n), 0)).astype(F32)
    ps = [eye2 + x for x in xs]
    for _ in range(int(math.log2(n)) - 1):
        xs = [_pdot(x, x) for x in xs]
        ps = [p + _pdot(x, p) for x, p in zip(xs, ps)]
    return ps


def _tri_masks2(n):
    ri = _iota2((n, 2 * n), 0)
    ci = _iota2((n, 2 * n), 1) & (n - 1)
    return ri >= ci, ri > ci


def _params(sem):
    return pltpu.CompilerParams(dimension_semantics=sem, vmem_limit_bytes=VMEM_LIMIT)


def _mm_kernel(x_ref, w_ref, o_ref):
    o_ref[...] = jnp.dot(x_ref[...], w_ref[...], preferred_element_type=F32).astype(o_ref.dtype)


def _matmul(x, w, layer, out_dtype, tm, tn):
    m, k = x.shape
    n = w.shape[2]
    return pl.pallas_call(
        _mm_kernel,
        grid=(m // tm, n // tn),
        in_specs=[pl.BlockSpec((tm, k), lambda i, j: (i, 0)),
                  pl.BlockSpec((None, k, tn), lambda i, j: (layer, 0, j))],
        out_specs=pl.BlockSpec((tm, tn), lambda i, j: (i, j)),
        out_shape=jax.ShapeDtypeStruct((m, n), out_dtype),
        compiler_params=_params(("parallel", "arbitrary")),
        name="proj_in",
    )(x, w)


def _rwkv_kernel(r_ref, k_ref, v_ref, g_ref, sm_ref, mu_ref, musm_ref, w0_ref, w2_ref, a0_ref, a2_ref,
                 kk_ref, ka_ref, rk_ref, gng_ref, gnb_ref, o_ref, buf_ref, bufsm_ref, s_ref):
    c = pl.program_id(1)
    C = CHUNK

    NB = r_ref.shape[0]

    @pl.when(c == 0)
    def _():
        buf_ref[:, 0:TAIL, :] = jnp.zeros((NB, TAIL, 3 * D_GROUP), F32)
        bufsm_ref[:, 0:TAIL, :] = jnp.zeros((NB, TAIL, SMALL_W), F32)
        s_ref[...] = jnp.zeros(s_ref.shape, F32)

    ri = _iota2((C, C), 0)
    ci = _iota2((C, C), 1)
    tril = ri >= ci
    strict = ri > ci
    rk = rk_ref[...]
    gng = gng_ref[...]
    gnb = gnb_ref[...]
    pre = []
    for nb in range(NB):
        buf_ref[nb, TAIL:TAIL + C, 0:512] = r_ref[nb]
        buf_ref[nb, TAIL:TAIL + C, 512:1024] = k_ref[nb]
        buf_ref[nb, TAIL:TAIL + C, 1024:1536] = v_ref[nb]
        bufsm_ref[nb, TAIL:TAIL + C, :] = sm_ref[nb]
        cur = buf_ref[nb, TAIL:TAIL + C, :]
        prev = buf_ref[nb, TAIL - 1:TAIL - 1 + C, :]
        mixed = cur + (prev - cur) * mu_ref[...]
        cur_sm = bufsm_ref[nb, TAIL:TAIL + C, :]
        prev_sm = bufsm_ref[nb, TAIL - 1:TAIL - 1 + C, :]
        smix = cur_sm + (prev_sm - cur_sm) * musm_ref[...]
        buf_ref[nb, 0:TAIL, :] = buf_ref[nb, C:C + TAIL, :]
        bufsm_ref[nb, 0:TAIL, :] = bufsm_ref[nb, C:C + TAIL, :]

        r = mixed[:, 0:512]
        k = mixed[:, 512:1024]
        v = mixed[:, 1024:1536]
        wd = smix[:, SM_WD:SM_WD + RW_LORA]
        ad = smix[:, SM_AD:SM_AD + RW_LORA]
        wl = w0_ref[...] + _dot(jnp.tanh(wd), w2_ref[...], HI)
        logw = -jnp.exp(-_softplus(-wl) - 0.5)
        alpha = _sigmoid(a0_ref[...] + _dot(ad, a2_ref[...], HI))
        kkraw = k * kk_ref[...]
        k2 = k * (1.0 + (alpha - 1.0) * ka_ref[...])
        gcum = _dot(tril.astype(F32), logw, HI)
        pre.append((r, k2, v, alpha, kkraw, logw, gcum, _silu(g_ref[nb])))

    H = range(NB * N_HEADS // 2)
    tril2, strict2 = _tri_masks2(C)
    R, K2, V, AL, KKR, LW, G = ([t for p in pre for t in _pairs(p[f])] for f in range(7))
    KK = [t * lax.rsqrt(_segsum(t * t) + 1e-6) for t in KKR]
    GM = [t[C // 2 - 1:C // 2, :] for t in G]
    GE = [t[C - 1:C, :] for t in G]
    BV = [KK[h] * AL[h] for h in H]
    EINV = [jnp.exp(GM[h] - G[h]) for h in H]
    EEND = [jnp.exp(GE[h] - G[h]) for h in H]
    LEFT = [jnp.concatenate([-KK[h] * jnp.exp(G[h] - LW[h] - GM[h]), R[h] * jnp.exp(G[h] - GM[h])], axis=0)
            for h in H]
    AB = [_pdot_nt(LEFT[h], BV[h] * EINV[h]) for h in H]
    AK = [_pdot_nt(LEFT[h], K2[h] * EINV[h]) for h in H]
    A_RB = [jnp.where(tril2, t[C:], 0.0) for t in AB]
    A_K = [jnp.concatenate([jnp.where(strict2, t[:C], 0.0), jnp.where(tril2, t[C:], 0.0)], axis=0) for t in AK]
    TINV = _inv_unit_lower([jnp.where(strict2, t[:C], 0.0) for t in AB])
    S = [s_ref[h] for h in H]
    LS = [_pdot_nt(LEFT[h], S[h] * jnp.exp(GM[h])) for h in H]
    AV = [_pdot(A_K[h], V[h]) for h in H]
    U = [_pdot(TINV[h], LS[h][:C] + AV[h][:C]) for h in H]
    O = [LS[h][C:] + _pdot(A_RB[h], U[h]) + AV[h][C:] for h in H]
    for h in H:
        s_ref[h] = S[h] * jnp.exp(GE[h]) + _pdot_tn(
            jnp.concatenate([U[h], V[h]], axis=0),
            jnp.concatenate([BV[h] * EEND[h], K2[h] * EEND[h]], axis=0))
    for n in H:
        nb, p = divmod(n, N_HEADS // 2)
        sl = slice(p * PAIR, (p + 1) * PAIR)
        mu = _segsum(O[n]) * (1.0 / HEAD_DIM)
        d = O[n] - mu
        var = _segsum(d * d) * (1.0 / HEAD_DIM)
        y = d * lax.rsqrt(var + RW_GN_EPS) * gng[:, sl] + gnb[:, sl]
        bonus = _segsum(R[n] * K2[n] * rk[:, sl]) * V[n]
        o_ref[nb, :, sl] = ((y + bonus) * pre[nb][7][:, sl]).astype(o_ref.dtype)


def _seq_specs(names, nbs):
    specs = [pl.BlockSpec((nbs, CHUNK, 512), lambda b, c, j=FCOL[n]: (b, c, j)) for n in names]
    return specs + [pl.BlockSpec((nbs, CHUNK, SMALL_W), lambda b, c: (b, c, SMALL_BLK))]


def _rwkv(cf, B, T, mu3, musm, w0, w2, a0, a2, kk, ka, rk, gng, gnb):
    nbs = RW_SEQS if B % RW_SEQS == 0 else 1
    cf3 = cf.reshape(B, T, cf.shape[-1])
    full = lambda a: pl.BlockSpec(a.shape, lambda b, c: (0,) * a.ndim)
    ps = (mu3, musm, w0, w2, a0, a2, kk, ka, rk, gng, gnb)
    out = pl.pallas_call(
        _rwkv_kernel,
        grid=(B // nbs, T // CHUNK),
        in_specs=_seq_specs(('rw_r', 'rw_k', 'rw_v', 'rw_g'), nbs) + [full(a) for a in ps],
        out_specs=pl.BlockSpec((nbs, CHUNK, 512), lambda b, c: (b, c, 0)),
        out_shape=jax.ShapeDtypeStruct((B, T, D_GROUP), BF16),
        scratch_shapes=[pltpu.VMEM((nbs, TAIL + CHUNK, 3 * D_GROUP), F32),
                        pltpu.VMEM((nbs, TAIL + CHUNK, SMALL_W), F32),
                        pltpu.VMEM((nbs * N_HEADS // 2, HEAD_DIM, PAIR), F32)],
        compiler_params=_params(("parallel", "arbitrary")),
        name="rwkv7",
    )(cf3, cf3, cf3, cf3, cf3, *ps)
    return out.reshape(B * T, D_GROUP)


def _gdn_kernel(q_ref, k_ref, v_ref, z_ref, sm_ref, conv_ref, alog_ref, dtb_ref, ng_ref, o_ref,
                buf_ref, s_ref):
    c = pl.program_id(1)
    C = CHUNK

    NB = q_ref.shape[0]

    @pl.when(c == 0)
    def _():
        buf_ref[:, 0:TAIL, :] = jnp.zeros((NB, TAIL, 3 * D_GROUP), F32)
        s_ref[...] = jnp.zeros(s_ref.shape, F32)

    conv = conv_ref[...]
    lane = _iota2((C, SMALL_W), 1)
    ri = _iota2((C, C), 0)
    ci = _iota2((C, C), 1)
    tril = ri >= ci
    strict = ri > ci
    ng = ng_ref[...]
    pre = []
    for nb in range(NB):
        buf_ref[nb, TAIL:TAIL + C, 0:512] = q_ref[nb]
        buf_ref[nb, TAIL:TAIL + C, 512:1024] = k_ref[nb]
        buf_ref[nb, TAIL:TAIL + C, 1024:1536] = v_ref[nb]
        acc = buf_ref[nb, TAIL:TAIL + C, :] * conv[GDN_CONV - 1:GDN_CONV, :]
        for i in range(GDN_CONV - 1):
            sh = GDN_CONV - 1 - i
            acc = acc + buf_ref[nb, TAIL - sh:TAIL - sh + C, :] * conv[i:i + 1, :]
        buf_ref[nb, 0:TAIL, :] = buf_ref[nb, C:C + TAIL, :]
        qkv = _silu(acc)
        sm = sm_ref[nb]
        g_all = jnp.where((lane >= SM_A) & (lane < SM_A + N_HEADS),
                          -jnp.exp(alog_ref[...]) * _softplus(sm + dtb_ref[...]), 0.0)
        gam_all = _dot(tril.astype(F32), g_all, HI)
        pre.append((qkv[:, 0:512], qkv[:, 512:1024], qkv[:, 1024:1536], _sigmoid(sm), gam_all, _silu(z_ref[nb])))

    H = range(NB * N_HEADS // 2)
    hd = lambda n: divmod(n, N_HEADS // 2)
    lo = _lo_mask(C)
    tril2, strict2 = _tri_masks2(C)
    diag2 = (_iota2((C, PAIR), 1) & (HEAD_DIM - 1)) == _iota2((C, PAIR), 0)

    def per_head(n, field, col):
        nb, p = hd(n)
        t = pre[nb][field]
        return jnp.where(lo, t[:, col + 2 * p:col + 2 * p + 1], t[:, col + 2 * p + 1:col + 2 * p + 2])

    Q = [t * lax.rsqrt(_segsum(t * t) + 1e-6) * SCALE for p in pre for t in _pairs(p[0])]
    K = [t * lax.rsqrt(_segsum(t * t) + 1e-6) for p in pre for t in _pairs(p[1])]
    V = [t for p in pre for t in _pairs(p[2])]
    BETA = [per_head(n, 3, SM_BETA) for n in H]
    GAM = [per_head(n, 4, SM_A) for n in H]
    GROW = [jnp.sum(jnp.where(diag2, t, 0.0), 0, keepdims=True) for t in GAM]
    DECAY = [jnp.exp(jnp.where(tril2, GAM[n] - GROW[n], NEG_INF)) for n in H]
    EG = [jnp.exp(t) for t in GAM]
    KB = [K[h] * BETA[h] for h in H]
    KKQ = [_pdot_nt(jnp.concatenate([KB[h], Q[h]], axis=0), K[h]) for h in H]
    QK = [KKQ[h][C:] * DECAY[h] for h in H]
    TM = _inv_unit_lower([-jnp.where(strict2, KKQ[h][:C] * DECAY[h], 0.0) for h in H])
    UU = [_pdot(TM[h], V[h] * BETA[h]) for h in H]
    W = [_pdot(TM[h], KB[h] * EG[h]) for h in H]
    S = [s_ref[h] for h in H]
    WS = [_pdot(jnp.concatenate([W[h], Q[h] * EG[h]], axis=0), S[h]) for h in H]
    VN = [UU[h] - WS[h][:C] for h in H]
    O = [WS[h][C:] + _pdot(QK[h], VN[h]) for h in H]
    for h in H:
        g_last = GAM[h][C - 1:C, :]
        s_ref[h] = S[h] * jnp.exp(g_last) + _pdot_tn(K[h] * jnp.exp(g_last - GAM[h]), VN[h])
    for n in H:
        nb, p = hd(n)
        sl = slice(p * PAIR, (p + 1) * PAIR)
        o = O[n] * lax.rsqrt(_segsum(O[n] * O[n]) * (1.0 / HEAD_DIM) + GDN_EPS) * ng
        o_ref[nb, :, sl] = (o * pre[nb][5][:, sl]).astype(o_ref.dtype)


def _gdn(cf, B, T, conv, alog_sm, dtb_sm, ng):
    nbs = GDN_SEQS if B % GDN_SEQS == 0 else 1
    cf3 = cf.reshape(B, T, cf.shape[-1])
    full = lambda a: pl.BlockSpec(a.shape, lambda b, c: (0,) * a.ndim)
    ps = (conv, alog_sm, dtb_sm, ng)
    out = pl.pallas_call(
        _gdn_kernel,
        grid=(B // nbs, T // CHUNK),
        in_specs=_seq_specs(('gdn_q', 'gdn_k', 'gdn_v', 'gdn_g'), nbs) + [full(a) for a in ps],
        out_specs=pl.BlockSpec((nbs, CHUNK, 512), lambda b, c: (b, c, 0)),
        out_shape=jax.ShapeDtypeStruct((B, T, D_GROUP), BF16),
        scratch_shapes=[pltpu.VMEM((nbs, TAIL + CHUNK, 3 * D_GROUP), F32),
                        pltpu.VMEM((nbs * N_HEADS // 2, HEAD_DIM, PAIR), F32)],
        compiler_params=_params(("parallel", "arbitrary")),
        name="gdn",
    )(cf3, cf3, cf3, cf3, cf3, *ps)
    return out.reshape(B * T, D_GROUP)


def _store_heads(o_ref, g, ot, gate, bgate, gate_col):
    aq = o_ref.shape[0]
    for j in range(HPG):
        h = g * HPG + j
        sl = slice(h * HEAD_DIM, (h + 1) * HEAD_DIM)
        oh = ot[:, j * aq:(j + 1) * aq].T
        if bgate is not None:
            oh = oh * bgate[:, gate_col + h:gate_col + h + 1]
        o_ref[:, sl] = oh * gate[:, sl]


def _attn_kernel(*refs, window, use_sel, use_sink, gate_col, tk):
    it = iter(refs)
    q_ref, k_ref, v_ref, g_ref = next(it), next(it), next(it), next(it)
    sm_ref = next(it) if gate_col is not None else None
    sel_ref = next(it) if use_sel else None
    sink_ref = next(it) if use_sink else None
    o_ref = next(it)
    vt_ref = next(it)
    selx_ref = next(it) if use_sel else None
    i = pl.program_id(1)
    aq = q_ref.shape[0]
    W = HPG * aq

    @pl.when(i == 0)
    def _():
        eye = (_iota2((KV_W, KV_W), 0) == _iota2((KV_W, KV_W), 1)).astype(BF16)
        ones = jnp.ones((VT_ROWS - HEAD_DIM, tk), BF16)
        for c in range(vt_ref.shape[0]):
            vt = _dot_nt(eye, v_ref[c * tk:(c + 1) * tk, :]).astype(BF16)
            for g in range(KV_HEADS):
                vt_ref[c, g] = jnp.concatenate([vt[g * HEAD_DIM:(g + 1) * HEAD_DIM], ones], axis=0)

    span = tk if window is None else window + aq
    qpos = i * aq + (_iota2((span, W), 1) & (aq - 1))
    krow = _iota2((span, W), 0)
    gate = _silu(g_ref[...])
    if gate_col is not None:
        bgate = _sigmoid(sm_ref[...])

    G = range(KV_HEADS)
    QG = [jnp.concatenate([q_ref[:, h * HEAD_DIM:(h + 1) * HEAD_DIM] for h in range(g * HPG, (g + 1) * HPG)],
                          axis=0) * (SCALE * LOG2E) for g in G]
    if use_sel:
        for g in G:
            selx_ref[g] = jnp.concatenate([sel_ref[0, g]] * HPG, axis=1)
        nb = tk // NSA_SEL_BLOCK

    if window is not None:
        nt = span // tk
        t0 = jnp.maximum(i * aq - window, 0) // tk
        start = pl.multiple_of(t0 * tk, tk)
        rel = qpos - (start + krow)
        pbias = jnp.where((rel >= 0) & (rel < window), 0.0, NEG_INF)
        for g in G:
            ksl = slice(g * HEAD_DIM, (g + 1) * HEAD_DIM)
            s = _dot_nt(k_ref[pl.ds(start, span), ksl], QG[g]) + pbias
            m = jnp.max(s, 0, keepdims=True)
            if use_sink:
                sink = jnp.concatenate([jnp.zeros((1, aq), F32) + sink_ref[0:1, h:h + 1] * LOG2E
                                        for h in range(g * HPG, (g + 1) * HPG)], axis=1)
                m = jnp.maximum(m, sink)
            p = jnp.exp2((s - m).astype(BF16))
            vt = jnp.concatenate([vt_ref[t0 + j, g] for j in range(nt)], axis=1)
            pv = _dot(vt, p)
            l = pv[HEAD_DIM:HEAD_DIM + 1]
            if use_sink:
                l = l + jnp.exp2(sink - m)
            _store_heads(o_ref, g, pv[:HEAD_DIM] / l, gate, bgate if gate_col is not None else None, gate_col)
        return

    hi = (i * aq + aq + tk - 1) // tk

    def body(kt, carry, masked):
        off = pl.multiple_of(kt * tk, tk)
        if masked:
            pbias = jnp.where(qpos - (kt * tk + krow) >= 0, 0.0, NEG_INF)
        out = []
        for g in G:
            m, acc = carry[2 * g:2 * g + 2]
            ksl = slice(g * HEAD_DIM, (g + 1) * HEAD_DIM)
            s = _dot_nt(k_ref[pl.ds(off, tk), ksl], QG[g])
            if masked:
                s = s + pbias
            if use_sel:
                s = s + jnp.concatenate(
                    [jnp.broadcast_to(selx_ref[g, pl.ds(kt * nb + j, 1), :], (NSA_SEL_BLOCK, W))
                     for j in range(nb)], axis=0)
            m_new = jnp.maximum(m, jnp.max(s, 0, keepdims=True))
            a = jnp.exp2(m - m_new)
            p = jnp.exp2((s - m_new).astype(BF16))
            out += [m_new, a * acc + _dot(vt_ref[kt, g], p)]
        return tuple(out)

    init = []
    for g in G:
        init += [jnp.full((1, W), NEG_INF, F32), jnp.zeros((VT_ROWS, W), F32)]
    res = lax.fori_loop(0, hi - 1, functools.partial(body, masked=False), tuple(init))
    res = body(hi - 1, res, True)
    for g in G:
        acc = res[2 * g + 1]
        _store_heads(o_ref, g, acc[:HEAD_DIM] / acc[HEAD_DIM:HEAD_DIM + 1], gate,
                     bgate if gate_col is not None else None, gate_col)


def _attn(cb, cf, B, T, qname, kname, vname, gname, *, window=None, sel=None, sinks=None, gate_col=None, tk=TQ,
          aq=AQ_BAND):
    nq = T // aq
    in_specs = [pl.BlockSpec((aq, 512), lambda b, i, j=BF_Q[qname]: (b * nq + i, j)),
                pl.BlockSpec((T, KV_W), lambda b, i, j=BF_KV[kname]: (b, j)),
                pl.BlockSpec((T, KV_W), lambda b, i, j=BF_KV[vname]: (b, j)),
                pl.BlockSpec((aq, 512), lambda b, i, j=FCOL[gname]: (b * nq + i, j))]
    args = [cb, cb, cb, cf]
    if gate_col is not None:
        in_specs.append(pl.BlockSpec((aq, SMALL_W), lambda b, i: (b * nq + i, SMALL_BLK)))
        args.append(cf)
    scratch = [pltpu.VMEM((T // tk, KV_HEADS, VT_ROWS, tk), BF16)]
    if sel is not None:
        in_specs.append(pl.BlockSpec((1, KV_HEADS, sel.shape[2], aq), lambda b, i: (b, 0, 0, i)))
        args.append(sel)
        scratch.append(pltpu.VMEM((KV_HEADS, sel.shape[2], HPG * aq), F32))
    if sinks is not None:
        in_specs.append(pl.BlockSpec(sinks.shape, lambda b, i: (0, 0)))
        args.append(sinks)
    return pl.pallas_call(
        functools.partial(_attn_kernel, window=window, use_sel=sel is not None,
                          use_sink=sinks is not None, gate_col=gate_col, tk=tk),
        grid=(B, nq),
        in_specs=in_specs,
        out_specs=pl.BlockSpec((aq, 512), lambda b, i: (b * nq + i, 0)),
        out_shape=jax.ShapeDtypeStruct((B * T, D_GROUP), F32),
        scratch_shapes=scratch,
        compiler_params=_params(("parallel", "arbitrary")),
        name="attn_" + qname + ("_sel" if sel is not None else "_w%d" % window),
    )(*args)


def _gelu_tanh(x):
    return 0.5 * x * (1.0 + jnp.tanh(math.sqrt(2.0 / math.pi) * (x + 0.044715 * x * x * x)))


def _cmp_kernel(kc_ref, vc_ref, pe_ref, w1_ref, w2_ref, o_ref):
    nr = o_ref.shape[1]
    outs = []
    for s, ref in enumerate((kc_ref, vc_ref)):
        p1 = jnp.zeros((nr, 2 * NSA_CMP_HIDDEN), F32)
        p2 = jnp.zeros((nr, 2 * NSA_CMP_HIDDEN), F32)
        for l in range(NSA_CMP_STRIDE):
            x = ref[pl.ds(l, nr, stride=NSA_CMP_STRIDE), :]
            p1 = p1 + _dot((x + pe_ref[s, l]).astype(BF16), w1_ref[s, l])
            p2 = p2 + _dot((x + pe_ref[s, NSA_CMP_STRIDE + l]).astype(BF16), w1_ref[s, NSA_CMP_STRIDE + l])
        hid = _gelu_tanh(p1 + pltpu.roll(p2, nr - 1, 0))
        outs.append(_dot(hid.astype(BF16), w2_ref[s]))
    o_ref[0] = jnp.concatenate(outs, axis=1)


def _nsa_compress(cf, B, T, pe, w1, w2):
    nr = T // NSA_CMP_STRIDE
    full = lambda a: pl.BlockSpec(a.shape, lambda b: (0,) * a.ndim)
    return pl.pallas_call(
        _cmp_kernel,
        grid=(B,),
        in_specs=[pl.BlockSpec((T, KV_W), lambda b: (b, CMP_BLK)), pl.BlockSpec((T, KV_W), lambda b: (b, CMP_BLK + 1))]
                 + [full(a) for a in (pe, w1, w2)],
        out_specs=pl.BlockSpec((1, nr, 2 * KV_W), lambda b: (b, 0, 0)),
        out_shape=jax.ShapeDtypeStruct((B, nr, 2 * KV_W), F32),
        compiler_params=_params(("arbitrary",)),
        name="nsa_compress",
    )(cf, cf, pe, w1, w2)


def _cmpattn_kernel(q_ref, cmp_ref, g_ref, sm_ref, ov_ref, o_ref, sel_ref):
    i = pl.program_id(1)
    kv = cmp_ref[0]
    nr = kv.shape[0]
    ns = ov_ref.shape[0]
    tq = q_ref.shape[0]
    tpos = i * tq + _iota2((tq, nr), 0)
    cmask = _iota2((tq, nr), 1) * NSA_CMP_STRIDE + (NSA_CMP_BLOCK - 1) <= tpos
    gate = _silu(g_ref[...])
    bgate = _sigmoid(sm_ref[...])
    tblk = (i * tq + _iota2((ns, tq), 1)) // NSA_SEL_BLOCK
    jj = _iota2((ns, tq), 0)
    forced = (jj == 0) | (jj == tblk) | (jj == tblk - 1)
    causal = jj <= tblk
    cbias = jnp.where(cmask, 0.0, NEG_INF)
    for g in range(KV_HEADS):
        kc = kv[:, g * HEAD_DIM:(g + 1) * HEAD_DIM].astype(BF16)
        vc = kv[:, KV_W + g * HEAD_DIM:KV_W + (g + 1) * HEAD_DIM].astype(BF16)
        heads = range(g * HPG, (g + 1) * HPG)
        qg = jnp.concatenate([q_ref[:, h * HEAD_DIM:(h + 1) * HEAD_DIM] for h in heads], axis=0) * SCALE
        s = _dot_nt(qg, kc).reshape(HPG, tq, nr) + cbias[None]
        e = jnp.exp(s - jnp.max(s, -1, keepdims=True))
        p = jnp.where(cmask[None], e / jnp.sum(e, -1, keepdims=True), 0.0)
        psum = jnp.sum(p, axis=0)
        o = _dot(p.reshape(HPG * tq, nr).astype(BF16), vc)
        for j, h in enumerate(heads):
            sl = slice(h * HEAD_DIM, (h + 1) * HEAD_DIM)
            o_ref[:, sl] = o[j * tq:(j + 1) * tq] * bgate[:, SM_GATE + h:SM_GATE + h + 1] * gate[:, sl]
        imp = _dot_nt(ov_ref[...], psum, HI)
        imp = jnp.where(causal, jnp.where(forced, NSA_FORCE, imp), NEG_INF)
        rank = jnp.zeros((ns, tq), jnp.int32)
        for j in range(ns):
            row = imp[j:j + 1, :]
            rank = rank + ((row > imp) | ((row == imp) & (j < jj))).astype(jnp.int32)
        sel_ref[0, g] = jnp.where((rank < NSA_TOPN) & causal, 0.0, NEG_INF)


def _nsa_cmpattn(cb, cmp, cf, B, T, ov):
    tq = AQ_BAND
    nq = T // tq
    nr = cmp.shape[1]
    ns = T // NSA_SEL_BLOCK
    return pl.pallas_call(
        _cmpattn_kernel,
        grid=(B, nq),
        in_specs=[pl.BlockSpec((tq, 512), lambda b, i: (b * nq + i, BF_Q['nsa_q'])),
                  pl.BlockSpec((1, nr, 2 * KV_W), lambda b, i: (b, 0, 0)),
                  pl.BlockSpec((tq, 512), lambda b, i: (b * nq + i, FCOL['nsa_g'])),
                  pl.BlockSpec((tq, SMALL_W), lambda b, i: (b * nq + i, SMALL_BLK)),
                  pl.BlockSpec(ov.shape, lambda b, i: (0, 0))],
        out_specs=[pl.BlockSpec((tq, 512), lambda b, i: (b * nq + i, 0)),
                   pl.BlockSpec((1, KV_HEADS, ns, tq), lambda b, i: (b, 0, 0, i))],
        out_shape=[jax.ShapeDtypeStruct((B * T, D_GROUP), F32),
                   jax.ShapeDtypeStruct((B, KV_HEADS, ns, T), F32)],
        compiler_params=_params(("parallel", "arbitrary")),
        name="nsa_cmpattn",
    )(cb, cmp, cf, cf, ov)


def _out_kernel(ya_ref, yb_ref, c1_ref, c2_ref, c3_ref, yd_ref, x_ref, w_ref, lg_ref, lb_ref, o_ref, ob_ref):
    yc = c1_ref[...] + c2_ref[...] + c3_ref[...]
    acc = _dot(ya_ref[...], w_ref[0:512, :])
    acc = acc + _dot(yb_ref[...].astype(BF16), w_ref[512:1024, :])
    acc = acc + _dot(yc.astype(BF16), w_ref[1024:1536, :])
    acc = acc + _dot(yd_ref[...], w_ref[1536:2048, :])
    z = DEEPNORM_ALPHA * x_ref[...] + acc
    mu = jnp.mean(z, -1, keepdims=True)
    var = jnp.mean(jnp.square(z - mu), -1, keepdims=True)
    out = (z - mu) * lax.rsqrt(var + LN_EPS) * lg_ref[...] + lb_ref[...]
    o_ref[...] = out
    ob_ref[...] = out.astype(BF16)


def _out_proj(ys, x, w, layer, lg, lb, tm=512):
    m = x.shape[0]
    yspec = pl.BlockSpec((tm, 512), lambda i: (i, 0))
    xspec = pl.BlockSpec((tm, D_MODEL), lambda i: (i, 0))
    return pl.pallas_call(
        _out_kernel,
        grid=(m // tm,),
        in_specs=[yspec] * 6 + [xspec, pl.BlockSpec((None,) + w.shape[1:], lambda i: (layer, 0, 0),
                                                    pipeline_mode=pl.Buffered(1)),
                                pl.BlockSpec(lg.shape, lambda i: (0, 0)), pl.BlockSpec(lb.shape, lambda i: (0, 0))],
        out_specs=[xspec, xspec],
        out_shape=[jax.ShapeDtypeStruct((m, D_MODEL), F32), jax.ShapeDtypeStruct((m, D_MODEL), BF16)],
        compiler_params=_params(("parallel",)),
        name="out_proj_ln",
    )(*ys, x, w, lg, lb)


def _cols(w, name):
    o, s = _OFF[name]
    return w[..., o:o + s]


def _small_row(pieces):
    parts, pos = [], 0
    for off, vals in pieces:
        parts += [jnp.zeros((off - pos,), F32), vals.astype(F32)]
        pos = off + vals.shape[-1]
    return jnp.concatenate(parts + [jnp.zeros((SMALL_W - pos,), F32)]).reshape(1, SMALL_W)


def _overlap_matrix(T):
    nc = T // NSA_CMP_STRIDE
    ns = T // NSA_SEL_BLOCK
    cst = np.arange(nc) * NSA_CMP_STRIDE
    jst = np.arange(ns) * NSA_SEL_BLOCK
    ov = np.clip(np.minimum(cst[:, None] + NSA_CMP_BLOCK, jst[None, :] + NSA_SEL_BLOCK)
                 - np.maximum(cst[:, None], jst[None, :]), 0, None).astype(np.float32) / NSA_CMP_BLOCK
    return jnp.asarray(ov.T)


def _cmp_weights(pe_k, pe_v, k_w1, k_w2, v_w1, v_w2):
    eye = jnp.eye(KV_HEADS, dtype=F32)
    w1 = jnp.stack([k_w1, v_w1]).reshape(2, NSA_CMP_BLOCK, HEAD_DIM, NSA_CMP_HIDDEN)
    w1 = jnp.einsum('sldj,gh->slgdhj', w1, eye).reshape(2, NSA_CMP_BLOCK, KV_W, KV_HEADS * NSA_CMP_HIDDEN)
    w2 = jnp.einsum('sjd,gh->sgjhd', jnp.stack([k_w2, v_w2]), eye).reshape(2, KV_HEADS * NSA_CMP_HIDDEN, KV_W)
    pe = jnp.tile(jnp.stack([pe_k, pe_v])[:, :, None, :], (1, 1, 1, KV_HEADS))
    return pe, w1.astype(BF16), w2.astype(BF16)


_SMALL_OFF = len(F32_GROUPS) * 512
_F32_MAP = ([(FCOL[n] * 512, n) for n in F32_GROUPS]
            + [(_SMALL_OFF + o, n) for o, n in ((SM_WD, 'rw_wd'), (SM_AD, 'rw_ad'), (SM_GATE, 'nsa_gate'),
                                                (SM_BETA, 'gdn_beta'), (SM_A, 'gdn_a'))]
            + [(CMP_BLK * KV_W, 'nsa_kc'), ((CMP_BLK + 1) * KV_W, 'nsa_vc')])
_BF_MAP = ([(BF_Q[n] * 512, n) for n in BF_Q] + [(BF_KV[n] * KV_W, n) for n in BF_KV])
_SMALL_USED = SM_A + N_HEADS


def _regroup_kernel(w_ref, of_ref, ob_ref):
    for dst, name in _F32_MAP:
        src, wd = _OFF[name]
        of_ref[:, dst:dst + wd] = w_ref[:, src:src + wd].astype(BF16)
    of_ref[:, _SMALL_OFF + _SMALL_USED:_SMALL_OFF + SMALL_W] = jnp.zeros(
        (of_ref.shape[0], SMALL_W - _SMALL_USED), BF16)
    for dst, name in _BF_MAP:
        src, wd = _OFF[name]
        ob_ref[:, dst:dst + wd] = w_ref[:, src:src + wd].astype(BF16)


def _projection_weights(w_in, tr=256):
    L, k, n = w_in.shape
    return pl.pallas_call(
        _regroup_kernel,
        grid=(L, k // tr),
        in_specs=[pl.BlockSpec((None, tr, n), lambda l, i: (l, i, 0))],
        out_specs=[pl.BlockSpec((None, tr, N_F32), lambda l, i: (l, i, 0)),
                   pl.BlockSpec((None, tr, N_BF), lambda l, i: (l, i, 0))],
        out_shape=[jax.ShapeDtypeStruct((L, k, N_F32), BF16), jax.ShapeDtypeStruct((L, k, N_BF), BF16)],
        compiler_params=_params(("parallel", "parallel")),
        name="regroup_w_in",
    )(w_in)


def _layer(x, xb, B, T, layer, w_f32, w_bf, w_out, ln_g, ln_b, rw_mu, rw_w0, rw_w2, rw_a0, rw_a2, rw_kk, rw_ka,
           rw_rk, rw_gn_g, rw_gn_b, swa_sinks, nsa_pe_k, nsa_pe_v, nsa_k_w1, nsa_k_w2, nsa_v_w1, nsa_v_w2,
           gdn_conv, gdn_A_log, gdn_dt_bias, gdn_norm_g):
    tm = min(2048, B * T)
    cf = _matmul(xb, w_f32, layer, F32, tm, 512)
    cb = _matmul(xb, w_bf, layer, BF16, tm, N_BF // 2)

    row = lambda a: a.reshape(1, -1).astype(F32)
    mu3 = row(rw_mu[:3 * D_GROUP])
    musm = _small_row([(SM_WD, rw_mu[3 * D_GROUP:3 * D_GROUP + RW_LORA]),
                       (SM_AD, rw_mu[3 * D_GROUP + RW_LORA:])])
    y_a = _rwkv(cf, B, T, mu3, musm, row(rw_w0), rw_w2, row(rw_a0), rw_a2, row(rw_kk), row(rw_ka),
                row(rw_rk), row(rw_gn_g), row(rw_gn_b))
    y_d = _gdn(cf, B, T, gdn_conv, _small_row([(SM_A, gdn_A_log)]), _small_row([(SM_A, gdn_dt_bias)]),
               jnp.tile(row(gdn_norm_g), (1, 2)))
    y_b = _attn(cb, cf, B, T, 'swa_q', 'swa_k', 'swa_v', 'swa_g', window=SWA_WINDOW, sinks=row(swa_sinks))
    cmp = _nsa_compress(cf, B, T, *_cmp_weights(nsa_pe_k, nsa_pe_v, nsa_k_w1, nsa_k_w2, nsa_v_w1, nsa_v_w2))
    c1, sel = _nsa_cmpattn(cb, cmp, cf, B, T, _overlap_matrix(T))
    c2 = _attn(cb, cf, B, T, 'nsa_q', 'nsa_ks', 'nsa_vs', 'nsa_g', sel=sel, gate_col=SM_GATE + N_HEADS, tk=4 * TQ,
               aq=AQ_SEL)
    c3 = _attn(cb, cf, B, T, 'nsa_q', 'nsa_kw', 'nsa_vw', 'nsa_g', window=NSA_WINDOW,
               gate_col=SM_GATE + 2 * N_HEADS)
    return _out_proj((y_a, y_b, c1, c2, c3, y_d), x, w_out, layer, row(ln_g), row(ln_b))


def kernel(x, w_in, w_out, ln_g, ln_b, rw_mu, rw_w0, rw_w2, rw_a0, rw_a2, rw_kk, rw_ka, rw_rk, rw_gn_g, rw_gn_b, swa_sinks, nsa_pe_k, nsa_pe_v, nsa_k_w1, nsa_k_w2, nsa_v_w1, nsa_v_w2, gdn_conv, gdn_A_log, gdn_dt_bias, gdn_norm_g):
    B, T, D = x.shape
    params = (ln_g, ln_b, rw_mu, rw_w0, rw_w2, rw_a0, rw_a2, rw_kk, rw_ka, rw_rk, rw_gn_g, rw_gn_b,
              swa_sinks, nsa_pe_k, nsa_pe_v, nsa_k_w1, nsa_k_w2, nsa_v_w1, nsa_v_w2, gdn_conv, gdn_A_log,
              gdn_dt_bias, gdn_norm_g)
    w_f32, w_bf = _projection_weights(w_in)
    w_out_b = w_out.astype(BF16)
    xf = x.reshape(B * T, D)
    xb = xf.astype(BF16)
    for i in range(w_in.shape[0]):
        xf, xb = _layer(xf, xb, B, T, i, w_f32, w_bf, w_out_b, *(p[i] for p in params))
    return xf.reshape(B, T, D)
```

```python
import functools
import math

import numpy as np
import jax
import jax.numpy as jnp
from jax import lax
from jax.experimental import pallas as pl
from jax.experimental.pallas import tpu as pltpu

F32 = jnp.float32
BF16 = jnp.bfloat16
HI = lax.Precision.HIGHEST

D_MODEL = 2048
DEPTH = 4
D_GROUP = 512
HEAD_DIM = 64
N_HEADS = 8
KV_HEADS = 2
HPG = N_HEADS // KV_HEADS
PAIR = 2 * HEAD_DIM
KV_W = KV_HEADS * HEAD_DIM
NEG_INF = -1e30
LN_EPS = 1e-5
DEEPNORM_ALPHA = (2 * DEPTH) ** 0.25
RW_LORA = 32
RW_GN_EPS = 64e-5
SWA_WINDOW = 128
NSA_CMP_BLOCK = 32
NSA_CMP_STRIDE = 16
NSA_CMP_HIDDEN = 128
NSA_SEL_BLOCK = 64
NSA_TOPN = 16
NSA_WINDOW = 512
NSA_FORCE = 1e6
GDN_CONV = 4
GDN_EPS = 1e-6
SCALE = HEAD_DIM ** -0.5
LOG2E = math.log2(math.e)

CHUNK = 64
RW_SEQS = 4
GDN_SEQS = 4
TQ = 128
AQ_BAND = 256
AQ_SEL = 512
TAIL = 8
VT_ROWS = HEAD_DIM + 16
VMEM_LIMIT = 56 * 1024 * 1024

_COLS = (
    ('rw_r', 512), ('rw_k', 512), ('rw_v', 512), ('rw_wd', 32), ('rw_ad', 32), ('rw_g', 512),
    ('swa_q', 512), ('swa_k', 128), ('swa_v', 128), ('swa_g', 512),
    ('nsa_q', 512), ('nsa_kc', 128), ('nsa_vc', 128), ('nsa_ks', 128), ('nsa_vs', 128),
    ('nsa_kw', 128), ('nsa_vw', 128), ('nsa_gate', 24), ('nsa_g', 512),
    ('gdn_q', 512), ('gdn_k', 512), ('gdn_v', 512), ('gdn_beta', 8), ('gdn_a', 8), ('gdn_g', 512),
)
_OFF = {}
_o = 0
for _n, _s in _COLS:
    _OFF[_n] = (_o, _s)
    _o += _s
N_IN = _o

F32_GROUPS = ('rw_r', 'rw_k', 'rw_v', 'rw_g', 'gdn_q', 'gdn_k', 'gdn_v', 'gdn_g', 'swa_g', 'nsa_g')
FCOL = {n: i for i, n in enumerate(F32_GROUPS)}
SMALL_W = 256
SMALL_BLK = len(F32_GROUPS) * 512 // SMALL_W
SM_WD, SM_AD, SM_GATE, SM_BETA, SM_A = 0, 32, 64, 96, 112
CMP_BLK = (len(F32_GROUPS) * 512 + SMALL_W) // KV_W
N_F32 = len(F32_GROUPS) * 512 + SMALL_W + 2 * KV_W
BF_Q = {'swa_q': 0, 'nsa_q': 1}
BF_KV = {n: 8 + i for i, n in enumerate(('swa_k', 'swa_v', 'nsa_ks', 'nsa_vs', 'nsa_kw', 'nsa_vw'))}
N_BF = 1024 + 6 * 128


def _dot(a, b, prec=None):
    return lax.dot_general(a, b, (((1,), (0,)), ((), ())), precision=prec, preferred_element_type=F32)


def _dot_nt(a, b, prec=None):
    return lax.dot_general(a, b, (((1,), (1,)), ((), ())), precision=prec, preferred_element_type=F32)


def _dot_tn(a, b, prec=None):
    return lax.dot_general(a, b, (((0,), (0,)), ((), ())), precision=prec, preferred_element_type=F32)


def _sigmoid(x):
    return 1.0 / (1.0 + jnp.exp(-x))


def _silu(x):
    return x * _sigmoid(x)


def _softplus(x):
    return jnp.maximum(x, 0.0) + jnp.log(1.0 + jnp.exp(-jnp.abs(x)))


def _iota2(shape, dim):
    return lax.broadcasted_iota(jnp.int32, shape, dim)


def _pairs(t):
    return [t[:, p * PAIR:(p + 1) * PAIR] for p in range(N_HEADS // 2)]


def _lo_mask(rows):
    return _iota2((rows, PAIR), 1) < HEAD_DIM


def _bd(x):
    xb = x.astype(BF16)
    lo = _lo_mask(x.shape[0]).astype(BF16)
    return jnp.concatenate([xb * lo, xb * (1 - lo)], axis=0)


def _pdot(a, b):
    return _dot(a.astype(BF16), _bd(b))


def _pdot_nt(a, b):
    return _dot_nt(a.astype(BF16), _bd(b))


def _pdot_tn(a, b):
    full = _dot_tn(a.astype(BF16), b.astype(BF16))
    return jnp.where(_lo_mask(HEAD_DIM), full[:HEAD_DIM], full[HEAD_DIM:])


def _segsum(t):
    lo = _lo_mask(t.shape[0])
    s_lo = jnp.sum(jnp.where(lo, t, 0.0), -1, keepdims=True)
    s_hi = jnp.sum(jnp.where(lo, 0.0, t), -1, keepdims=True)
    return jnp.where(lo, s_lo, s_hi)


def _inv_unit_lower(xs):
    n = xs[0].shape[0]
    eye2 = ((_iota2((n, 2 * n), 1) & (n - 1)) == _iota2((n, 2 * n), 0)).astype(F32)
    ps = [eye2 + x for x in xs]
    for _ in range(int(math.log2(n)) - 1):
        xs = [_pdot(x, x) for x in xs]
        ps = [p + _pdot(x, p) for x, p in zip(xs, ps)]
    return ps


def _tri_masks2(n):
    ri = _iota2((n, 2 * n), 0)
    ci = _iota2((n, 2 * n), 1) & (n - 1)
    return ri >= ci, ri > ci


def _params(sem):
    return pltpu.CompilerParams(dimension_semantics=sem, vmem_limit_bytes=VMEM_LIMIT)


def _mm_kernel(x_ref, w_ref, o_ref):
    o_ref[...] = _dot_nt(x_ref[...], w_ref[...]).astype(o_ref.dtype)


def _matmul(x, w, layer, out_dtype, tm, tn):
    m, k = x.shape
    n = w.shape[1]
    return pl.pallas_call(
        _mm_kernel,
        grid=(m // tm, n // tn),
        in_specs=[pl.BlockSpec((tm, k), lambda i, j: (i, 0)),
                  pl.BlockSpec((None, tn, k), lambda i, j: (layer, j, 0))],
        out_specs=pl.BlockSpec((tm, tn), lambda i, j: (i, j)),
        out_shape=jax.ShapeDtypeStruct((m, n), out_dtype),
        compiler_params=_params(("parallel", "arbitrary")),
        name="proj_in",
    )(x, w)


def _rwkv_kernel(r_ref, k_ref, v_ref, g_ref, sm_ref, mu_ref, musm_ref, w0_ref, w2_ref, a0_ref, a2_ref,
                 kk_ref, ka_ref, rk_ref, gng_ref, gnb_ref, o_ref, buf_ref, bufsm_ref, s_ref):
    c = pl.program_id(1)
    C = CHUNK

    NB = r_ref.shape[0]

    @pl.when(c == 0)
    def _():
        buf_ref[:, 0:TAIL, :] = jnp.zeros((NB, TAIL, 3 * D_GROUP), F32)
        bufsm_ref[:, 0:TAIL, :] = jnp.zeros((NB, TAIL, SMALL_W), F32)
        s_ref[...] = jnp.zeros(s_ref.shape, F32)

    ri = _iota2((C, C), 0)
    ci = _iota2((C, C), 1)
    tril = ri >= ci
    strict = ri > ci
    rk = rk_ref[...]
    gng = gng_ref[...]
    gnb = gnb_ref[...]
    pre = []
    for nb in range(NB):
        buf_ref[nb, TAIL:TAIL + C, 0:512] = r_ref[nb]
        buf_ref[nb, TAIL:TAIL + C, 512:1024] = k_ref[nb]
        buf_ref[nb, TAIL:TAIL + C, 1024:1536] = v_ref[nb]
        bufsm_ref[nb, TAIL:TAIL + C, :] = sm_ref[nb]
        cur = buf_ref[nb, TAIL:TAIL + C, :]
        prev = buf_ref[nb, TAIL - 1:TAIL - 1 + C, :]
        mixed = cur + (prev - cur) * mu_ref[...]
        cur_sm = bufsm_ref[nb, TAIL:TAIL + C, :]
        prev_sm = bufsm_ref[nb, TAIL - 1:TAIL - 1 + C, :]
        smix = cur_sm + (prev_sm - cur_sm) * musm_ref[...]
        buf_ref[nb, 0:TAIL, :] = buf_ref[nb, C:C + TAIL, :]
        bufsm_ref[nb, 0:TAIL, :] = bufsm_ref[nb, C:C + TAIL, :]

        r = mixed[:, 0:512]
        k = mixed[:, 512:1024]
        v = mixed[:, 1024:1536]
        wd = smix[:, SM_WD:SM_WD + RW_LORA]
        ad = smix[:, SM_AD:SM_AD + RW_LORA]
        wl = w0_ref[...] + _dot(jnp.tanh(wd), w2_ref[...], HI)
        logw = -jnp.exp(-_softplus(-wl) - 0.5)
        alpha = _sigmoid(a0_ref[...] + _dot(ad, a2_ref[...], HI))
        kkraw = k * kk_ref[...]
        k2 = k * (1.0 + (alpha - 1.0) * ka_ref[...])
        gcum = _dot(tril.astype(F32), logw, HI)
        pre.append((r, k2, v, alpha, kkraw, logw, gcum, _silu(g_ref[nb])))

    H = range(NB * N_HEADS // 2)
    tril2, strict2 = _tri_masks2(C)
    R, K2, V, AL, KKR, LW, G = ([t for p in pre for t in _pairs(p[f])] for f in range(7))
    KK = [t * lax.rsqrt(_segsum(t * t) + 1e-6) for t in KKR]
    GM = [t[C // 2 - 1:C // 2, :] for t in G]
    GE = [t[C - 1:C, :] for t in G]
    BV = [KK[h] * AL[h] for h in H]
    EINV = [jnp.exp(GM[h] - G[h]) for h in H]
    EEND = [jnp.exp(GE[h] - G[h]) for h in H]
    LEFT = [jnp.concatenate([-KK[h] * jnp.exp(G[h] - LW[h] - GM[h]), R[h] * jnp.exp(G[h] - GM[h])], axis=0)
            for h in H]
    AB = [_pdot_nt(LEFT[h], BV[h] * EINV[h]) for h in H]
    AK = [_pdot_nt(LEFT[h], K2[h] * EINV[h]) for h in H]
    A_RB = [jnp.where(tril2, t[C:], 0.0) for t in AB]
    A_K = [jnp.concatenate([jnp.where(strict2, t[:C], 0.0), jnp.where(tril2, t[C:], 0.0)], axis=0) for t in AK]
    TINV = _inv_unit_lower([jnp.where(strict2, t[:C], 0.0) for t in AB])
    S = [s_ref[h] for h in H]
    LS = [_pdot_nt(LEFT[h], S[h] * jnp.exp(GM[h])) for h in H]
    AV = [_pdot(A_K[h], V[h]) for h in H]
    U = [_pdot(TINV[h], LS[h][:C] + AV[h][:C]) for h in H]
    O = [LS[h][C:] + _pdot(A_RB[h], U[h]) + AV[h][C:] for h in H]
    for h in H:
        s_ref[h] = S[h] * jnp.exp(GE[h]) + _pdot_tn(
            jnp.concatenate([U[h], V[h]], axis=0),
            jnp.concatenate([BV[h] * EEND[h], K2[h] * EEND[h]], axis=0))
    for n in H:
        nb, p = divmod(n, N_HEADS // 2)
        sl = slice(p * PAIR, (p + 1) * PAIR)
        mu = _segsum(O[n]) * (1.0 / HEAD_DIM)
        d = O[n] - mu
        var = _segsum(d * d) * (1.0 / HEAD_DIM)
        y = d * lax.rsqrt(var + RW_GN_EPS) * gng[:, sl] + gnb[:, sl]
        bonus = _segsum(R[n] * K2[n] * rk[:, sl]) * V[n]
        o_ref[nb, :, sl] = ((y + bonus) * pre[nb][7][:, sl]).astype(o_ref.dtype)


def _seq_specs(names, nbs):
    specs = [pl.BlockSpec((nbs, CHUNK, 512), lambda b, c, j=FCOL[n]: (b, c, j)) for n in names]
    return specs + [pl.BlockSpec((nbs, CHUNK, SMALL_W), lambda b, c: (b, c, SMALL_BLK))]


def _rwkv(cf, B, T, mu3, musm, w0, w2, a0, a2, kk, ka, rk, gng, gnb):
    nbs = RW_SEQS if B % RW_SEQS == 0 else 1
    cf3 = cf.reshape(B, T, cf.shape[-1])
    full = lambda a: pl.BlockSpec(a.shape, lambda b, c: (0,) * a.ndim)
    ps = (mu3, musm, w0, w2, a0, a2, kk, ka, rk, gng, gnb)
    out = pl.pallas_call(
        _rwkv_kernel,
        grid=(B // nbs, T // CHUNK),
        in_specs=_seq_specs(('rw_r', 'rw_k', 'rw_v', 'rw_g'), nbs) + [full(a) for a in ps],
        out_specs=pl.BlockSpec((nbs, CHUNK, 512), lambda b, c: (b, c, 0)),
        out_shape=jax.ShapeDtypeStruct((B, T, D_GROUP), BF16),
        scratch_shapes=[pltpu.VMEM((nbs, TAIL + CHUNK, 3 * D_GROUP), F32),
                        pltpu.VMEM((nbs, TAIL + CHUNK, SMALL_W), F32),
                        pltpu.VMEM((nbs * N_HEADS // 2, HEAD_DIM, PAIR), F32)],
        compiler_params=_params(("parallel", "arbitrary")),
        name="rwkv7",
    )(cf3, cf3, cf3, cf3, cf3, *ps)
    return out.reshape(B * T, D_GROUP)


def _gdn_kernel(q_ref, k_ref, v_ref, z_ref, sm_ref, conv_ref, alog_ref, dtb_ref, ng_ref, o_ref,
                buf_ref, s_ref):
    c = pl.program_id(1)
    C = CHUNK

    NB = q_ref.shape[0]

    @pl.when(c == 0)
    def _():
        buf_ref[:, 0:TAIL, :] = jnp.zeros((NB, TAIL, 3 * D_GROUP), F32)
        s_ref[...] = jnp.zeros(s_ref.shape, F32)

    conv = conv_ref[...]
    lane = _iota2((C, SMALL_W), 1)
    ri = _iota2((C, C), 0)
    ci = _iota2((C, C), 1)
    tril = ri >= ci
    strict = ri > ci
    ng = ng_ref[...]
    pre = []
    for nb in range(NB):
        buf_ref[nb, TAIL:TAIL + C, 0:512] = q_ref[nb]
        buf_ref[nb, TAIL:TAIL + C, 512:1024] = k_ref[nb]
        buf_ref[nb, TAIL:TAIL + C, 1024:1536] = v_ref[nb]
        acc = buf_ref[nb, TAIL:TAIL + C, :] * conv[GDN_CONV - 1:GDN_CONV, :]
        for i in range(GDN_CONV - 1):
            sh = GDN_CONV - 1 - i
            acc = acc + buf_ref[nb, TAIL - sh:TAIL - sh + C, :] * conv[i:i + 1, :]
        buf_ref[nb, 0:TAIL, :] = buf_ref[nb, C:C + TAIL, :]
        qkv = _silu(acc)
        sm = sm_ref[nb]
        g_all = jnp.where((lane >= SM_A) & (lane < SM_A + N_HEADS),
                          -jnp.exp(alog_ref[...]) * _softplus(sm + dtb_ref[...]), 0.0)
        gam_all = _dot(tril.astype(F32), g_all, HI)
        pre.append((qkv[:, 0:512], qkv[:, 512:1024], qkv[:, 1024:1536], _sigmoid(sm), gam_all, _silu(z_ref[nb])))

    H = range(NB * N_HEADS // 2)
    hd = lambda n: divmod(n, N_HEADS // 2)
    lo = _lo_mask(C)
    tril2, strict2 = _tri_masks2(C)
    diag2 = (_iota2((C, PAIR), 1) & (HEAD_DIM - 1)) == _iota2((C, PAIR), 0)

    def per_head(n, field, col):
        nb, p = hd(n)
        t = pre[nb][field]
        return jnp.where(lo, t[:, col + 2 * p:col + 2 * p + 1], t[:, col + 2 * p + 1:col + 2 * p + 2])

    Q = [t * lax.rsqrt(_segsum(t * t) + 1e-6) * SCALE for p in pre for t in _pairs(p[0])]
    K = [t * lax.rsqrt(_segsum(t * t) + 1e-6) for p in pre for t in _pairs(p[1])]
    V = [t for p in pre for t in _pairs(p[2])]
    BETA = [per_head(n, 3, SM_BETA) for n in H]
    GAM = [per_head(n, 4, SM_A) for n in H]
    GROW = [jnp.sum(jnp.where(diag2, t, 0.0), 0, keepdims=True) for t in GAM]
    DECAY = [jnp.exp(jnp.where(tril2, GAM[n] - GROW[n], NEG_INF)) for n in H]
    EG = [jnp.exp(t) for t in GAM]
    KB = [K[h] * BETA[h] for h in H]
    KKQ = [_pdot_nt(jnp.concatenate([KB[h], Q[h]], axis=0), K[h]) for h in H]
    QK = [KKQ[h][C:] * DECAY[h] for h in H]
    TM = _inv_unit_lower([-jnp.where(strict2, KKQ[h][:C] * DECAY[h], 0.0) for h in H])
    UU = [_pdot(TM[h], V[h] * BETA[h]) for h in H]
    W = [_pdot(TM[h], KB[h] * EG[h]) for h in H]
    S = [s_ref[h] for h in H]
    WS = [_pdot(jnp.concatenate([W[h], Q[h] * EG[h]], axis=0), S[h]) for h in H]
    VN = [UU[h] - WS[h][:C] for h in H]
    O = [WS[h][C:] + _pdot(QK[h], VN[h]) for h in H]
    for h in H:
        g_last = GAM[h][C - 1:C, :]
        s_ref[h] = S[h] * jnp.exp(g_last) + _pdot_tn(K[h] * jnp.exp(g_last - GAM[h]), VN[h])
    for n in H:
        nb, p = hd(n)
        sl = slice(p * PAIR, (p + 1) * PAIR)
        o = O[n] * lax.rsqrt(_segsum(O[n] * O[n]) * (1.0 / HEAD_DIM) + GDN_EPS) * ng
        o_ref[nb, :, sl] = (o * pre[nb][5][:, sl]).astype(o_ref.dtype)


def _gdn(cf, B, T, conv, alog_sm, dtb_sm, ng):
    nbs = GDN_SEQS if B % GDN_SEQS == 0 else 1
    cf3 = cf.reshape(B, T, cf.shape[-1])
    full = lambda a: pl.BlockSpec(a.shape, lambda b, c: (0,) * a.ndim)
    ps = (conv, alog_sm, dtb_sm, ng)
    out = pl.pallas_call(
        _gdn_kernel,
        grid=(B // nbs, T // CHUNK),
        in_specs=_seq_specs(('gdn_q', 'gdn_k', 'gdn_v', 'gdn_g'), nbs) + [full(a) for a in ps],
        out_specs=pl.BlockSpec((nbs, CHUNK, 512), lambda b, c: (b, c, 0)),
        out_shape=jax.ShapeDtypeStruct((B, T, D_GROUP), BF16),
        scratch_shapes=[pltpu.VMEM((nbs, TAIL + CHUNK, 3 * D_GROUP), F32),
                        pltpu.VMEM((nbs * N_HEADS // 2, HEAD_DIM, PAIR), F32)],
        compiler_params=_params(("parallel", "arbitrary")),
        name="gdn",
    )(cf3, cf3, cf3, cf3, cf3, *ps)
    return out.reshape(B * T, D_GROUP)


def _store_heads(o_ref, g, ot, gate, bgate, gate_col):
    aq = o_ref.shape[0]
    for j in range(HPG):
        h = g * HPG + j
        sl = slice(h * HEAD_DIM, (h + 1) * HEAD_DIM)
        oh = ot[:, j * aq:(j + 1) * aq].T
        if bgate is not None:
            oh = oh * bgate[:, gate_col + h:gate_col + h + 1]
        o_ref[:, sl] = oh * gate[:, sl]


def _attn_kernel(*refs, window, use_sel, use_sink, gate_col, tk):
    it = iter(refs)
    q_ref, k_ref, v_ref, g_ref = next(it), next(it), next(it), next(it)
    sm_ref = next(it) if gate_col is not None else None
    sel_ref = next(it) if use_sel else None
    sink_ref = next(it) if use_sink else None
    o_ref = next(it)
    vt_ref = next(it)
    selx_ref = next(it) if use_sel else None
    i = pl.program_id(1)
    aq = q_ref.shape[0]
    W = HPG * aq

    @pl.when(i == 0)
    def _():
        eye = (_iota2((KV_W, KV_W), 0) == _iota2((KV_W, KV_W), 1)).astype(BF16)
        ones = jnp.ones((VT_ROWS - HEAD_DIM, tk), BF16)
        for c in range(vt_ref.shape[0]):
            vt = _dot_nt(eye, v_ref[c * tk:(c + 1) * tk, :]).astype(BF16)
            for g in range(KV_HEADS):
                vt_ref[c, g] = jnp.concatenate([vt[g * HEAD_DIM:(g + 1) * HEAD_DIM], ones], axis=0)

    span = tk if window is None else window + aq
    qpos = i * aq + (_iota2((span, W), 1) & (aq - 1))
    krow = _iota2((span, W), 0)
    gate = _silu(g_ref[...])
    if gate_col is not None:
        bgate = _sigmoid(sm_ref[...])

    G = range(KV_HEADS)
    QG = [jnp.concatenate([q_ref[:, h * HEAD_DIM:(h + 1) * HEAD_DIM] for h in range(g * HPG, (g + 1) * HPG)],
                          axis=0) * (SCALE * LOG2E) for g in G]
    if use_sel:
        for g in G:
            selx_ref[g] = jnp.concatenate([sel_ref[0, g]] * HPG, axis=1)
        nb = tk // NSA_SEL_BLOCK

    if window is not None:
        nt = span // tk
        t0 = jnp.maximum(i * aq - window, 0) // tk
        start = pl.multiple_of(t0 * tk, tk)
        rel = qpos - (start + krow)
        pbias = jnp.where((rel >= 0) & (rel < window), 0.0, NEG_INF)
        for g in G:
            ksl = slice(g * HEAD_DIM, (g + 1) * HEAD_DIM)
            s = _dot_nt(k_ref[pl.ds(start, span), ksl], QG[g]) + pbias
            m = jnp.max(s, 0, keepdims=True)
            if use_sink:
                sink = jnp.concatenate([jnp.zeros((1, aq), F32) + sink_ref[0:1, h:h + 1] * LOG2E
                                        for h in range(g * HPG, (g + 1) * HPG)], axis=1)
                m = jnp.maximum(m, sink)
            p = jnp.exp2((s - m).astype(BF16))
            vt = jnp.concatenate([vt_ref[t0 + j, g] for j in range(nt)], axis=1)
            pv = _dot(vt, p)
            l = pv[HEAD_DIM:HEAD_DIM + 1]
            if use_sink:
                l = l + jnp.exp2(sink - m)
            _store_heads(o_ref, g, pv[:HEAD_DIM] / l, gate, bgate if gate_col is not None else None, gate_col)
        return

    hi = (i * aq + aq + tk - 1) // tk

    def body(kt, carry, masked):
        off = pl.multiple_of(kt * tk, tk)
        if masked:
            pbias = jnp.where(qpos - (kt * tk + krow) >= 0, 0.0, NEG_INF)
        out = []
        for g in G:
            m, acc = carry[2 * g:2 * g + 2]
            ksl = slice(g * HEAD_DIM, (g + 1) * HEAD_DIM)
            s = _dot_nt(k_ref[pl.ds(off, tk), ksl], QG[g])
            if masked:
                s = s + pbias
            if use_sel:
                s = s + jnp.concatenate(
                    [jnp.broadcast_to(selx_ref[g, pl.ds(kt * nb + j, 1), :], (NSA_SEL_BLOCK, W))
                     for j in range(nb)], axis=0)
            m_new = jnp.maximum(m, jnp.max(s, 0, keepdims=True))
            a = jnp.exp2(m - m_new)
            p = jnp.exp2((s - m_new).astype(BF16))
            out += [m_new, a * acc + _dot(vt_ref[kt, g], p)]
        return tuple(out)

    init = []
    for g in G:
        init += [jnp.full((1, W), NEG_INF, F32), jnp.zeros((VT_ROWS, W), F32)]
    res = lax.fori_loop(0, hi - 1, functools.partial(body, masked=False), tuple(init))
    res = body(hi - 1, res, True)
    for g in G:
        acc = res[2 * g + 1]
        _store_heads(o_ref, g, acc[:HEAD_DIM] / acc[HEAD_DIM:HEAD_DIM + 1], gate,
                     bgate if gate_col is not None else None, gate_col)


def _attn(cb, cf, B, T, qname, kname, vname, gname, *, window=None, sel=None, sinks=None, gate_col=None, tk=TQ,
          aq=AQ_BAND):
    nq = T // aq
    in_specs = [pl.BlockSpec((aq, 512), lambda b, i, j=BF_Q[qname]: (b * nq + i, j)),
                pl.BlockSpec((T, KV_W), lambda b, i, j=BF_KV[kname]: (b, j)),
                pl.BlockSpec((T, KV_W), lambda b, i, j=BF_KV[vname]: (b, j)),
                pl.BlockSpec((aq, 512), lambda b, i, j=FCOL[gname]: (b * nq + i, j))]
    args = [cb, cb, cb, cf]
    if gate_col is not None:
        in_specs.append(pl.BlockSpec((aq, SMALL_W), lambda b, i: (b * nq + i, SMALL_BLK)))
        args.append(cf)
    scratch = [pltpu.VMEM((T // tk, KV_HEADS, VT_ROWS, tk), BF16)]
    if sel is not None:
        in_specs.append(pl.BlockSpec((1, KV_HEADS, sel.shape[2], aq), lambda b, i: (b, 0, 0, i)))
        args.append(sel)
        scratch.append(pltpu.VMEM((KV_HEADS, sel.shape[2], HPG * aq), F32))
    if sinks is not None:
        in_specs.append(pl.BlockSpec(sinks.shape, lambda b, i: (0, 0)))
        args.append(sinks)
    return pl.pallas_call(
        functools.partial(_attn_kernel, window=window, use_sel=sel is not None,
                          use_sink=sinks is not None, gate_col=gate_col, tk=tk),
        grid=(B, nq),
        in_specs=in_specs,
        out_specs=pl.BlockSpec((aq, 512), lambda b, i: (b * nq + i, 0)),
        out_shape=jax.ShapeDtypeStruct((B * T, D_GROUP), F32),
        scratch_shapes=scratch,
        compiler_params=_params(("parallel", "arbitrary")),
        name="attn_" + qname + ("_sel" if sel is not None else "_w%d" % window),
    )(*args)


def _gelu_tanh(x):
    return 0.5 * x * (1.0 + jnp.tanh(math.sqrt(2.0 / math.pi) * (x + 0.044715 * x * x * x)))


def _cmp_kernel(kc_ref, vc_ref, pe_ref, w1_ref, w2_ref, o_ref):
    nr = o_ref.shape[1]
    outs = []
    for s, ref in enumerate((kc_ref, vc_ref)):
        p1 = jnp.zeros((nr, 2 * NSA_CMP_HIDDEN), F32)
        p2 = jnp.zeros((nr, 2 * NSA_CMP_HIDDEN), F32)
        for l in range(NSA_CMP_STRIDE):
            x = ref[pl.ds(l, nr, stride=NSA_CMP_STRIDE), :]
            p1 = p1 + _dot((x + pe_ref[s, l]).astype(BF16), w1_ref[s, l])
            p2 = p2 + _dot((x + pe_ref[s, NSA_CMP_STRIDE + l]).astype(BF16), w1_ref[s, NSA_CMP_STRIDE + l])
        hid = _gelu_tanh(p1 + pltpu.roll(p2, nr - 1, 0))
        outs.append(_dot(hid.astype(BF16), w2_ref[s]))
    o_ref[0] = jnp.concatenate(outs, axis=1)


def _nsa_compress(cf, B, T, pe, w1, w2):
    nr = T // NSA_CMP_STRIDE
    full = lambda a: pl.BlockSpec(a.shape, lambda b: (0,) * a.ndim)
    return pl.pallas_call(
        _cmp_kernel,
        grid=(B,),
        in_specs=[pl.BlockSpec((T, KV_W), lambda b: (b, CMP_BLK)), pl.BlockSpec((T, KV_W), lambda b: (b, CMP_BLK + 1))]
                 + [full(a) for a in (pe, w1, w2)],
        out_specs=pl.BlockSpec((1, nr, 2 * KV_W), lambda b: (b, 0, 0)),
        out_shape=jax.ShapeDtypeStruct((B, nr, 2 * KV_W), F32),
        compiler_params=_params(("arbitrary",)),
        name="nsa_compress",
    )(cf, cf, pe, w1, w2)


def _cmpattn_kernel(q_ref, cmp_ref, g_ref, sm_ref, ov_ref, o_ref, sel_ref):
    i = pl.program_id(1)
    kv = cmp_ref[0]
    nr = kv.shape[0]
    ns = ov_ref.shape[0]
    tq = q_ref.shape[0]
    tpos = i * tq + _iota2((tq, nr), 0)
    cmask = _iota2((tq, nr), 1) * NSA_CMP_STRIDE + (NSA_CMP_BLOCK - 1) <= tpos
    gate = _silu(g_ref[...])
    bgate = _sigmoid(sm_ref[...])
    tblk = (i * tq + _iota2((ns, tq), 1)) // NSA_SEL_BLOCK
    jj = _iota2((ns, tq), 0)
    forced = (jj == 0) | (jj == tblk) | (jj == tblk - 1)
    causal = jj <= tblk
    cbias = jnp.where(cmask, 0.0, NEG_INF)
    for g in range(KV_HEADS):
        kc = kv[:, g * HEAD_DIM:(g + 1) * HEAD_DIM].astype(BF16)
        vc = kv[:, KV_W + g * HEAD_DIM:KV_W + (g + 1) * HEAD_DIM].astype(BF16)
        heads = range(g * HPG, (g + 1) * HPG)
        qg = jnp.concatenate([q_ref[:, h * HEAD_DIM:(h + 1) * HEAD_DIM] for h in heads], axis=0) * SCALE
        s = _dot_nt(qg, kc).reshape(HPG, tq, nr) + cbias[None]
        e = jnp.exp(s - jnp.max(s, -1, keepdims=True))
        p = jnp.where(cmask[None], e / jnp.sum(e, -1, keepdims=True), 0.0)
        psum = jnp.sum(p, axis=0)
        o = _dot(p.reshape(HPG * tq, nr).astype(BF16), vc)
        for j, h in enumerate(heads):
            sl = slice(h * HEAD_DIM, (h + 1) * HEAD_DIM)
            o_ref[:, sl] = o[j * tq:(j + 1) * tq] * bgate[:, SM_GATE + h:SM_GATE + h + 1] * gate[:, sl]
        imp = _dot_nt(ov_ref[...], psum, HI)
        imp = jnp.where(causal, jnp.where(forced, NSA_FORCE, imp), NEG_INF)
        rank = jnp.zeros((ns, tq), jnp.int32)
        for j in range(ns):
            row = imp[j:j + 1, :]
            rank = rank + ((row > imp) | ((row == imp) & (j < jj))).astype(jnp.int32)
        sel_ref[0, g] = jnp.where((rank < NSA_TOPN) & causal, 0.0, NEG_INF)


def _nsa_cmpattn(cb, cmp, cf, B, T, ov):
    tq = AQ_BAND
    nq = T // tq
    nr = cmp.shape[1]
    ns = T // NSA_SEL_BLOCK
    return pl.pallas_call(
        _cmpattn_kernel,
        grid=(B, nq),
        in_specs=[pl.BlockSpec((tq, 512), lambda b, i: (b * nq + i, BF_Q['nsa_q'])),
                  pl.BlockSpec((1, nr, 2 * KV_W), lambda b, i: (b, 0, 0)),
                  pl.BlockSpec((tq, 512), lambda b, i: (b * nq + i, FCOL['nsa_g'])),
                  pl.BlockSpec((tq, SMALL_W), lambda b, i: (b * nq + i, SMALL_BLK)),
                  pl.BlockSpec(ov.shape, lambda b, i: (0, 0))],
        out_specs=[pl.BlockSpec((tq, 512), lambda b, i: (b * nq + i, 0)),
                   pl.BlockSpec((1, KV_HEADS, ns, tq), lambda b, i: (b, 0, 0, i))],
        out_shape=[jax.ShapeDtypeStruct((B * T, D_GROUP), F32),
                   jax.ShapeDtypeStruct((B, KV_HEADS, ns, T), F32)],
        compiler_params=_params(("parallel", "arbitrary")),
        name="nsa_cmpattn",
    )(cb, cmp, cf, cf, ov)


def _out_kernel(ya_ref, yb_ref, c1_ref, c2_ref, c3_ref, yd_ref, x_ref, w_ref, lg_ref, lb_ref, o_ref, ob_ref):
    yc = c1_ref[...] + c2_ref[...] + c3_ref[...]
    acc = _dot(ya_ref[...], w_ref[0:512, :])
    acc = acc + _dot(yb_ref[...].astype(BF16), w_ref[512:1024, :])
    acc = acc + _dot(yc.astype(BF16), w_ref[1024:1536, :])
    acc = acc + _dot(yd_ref[...], w_ref[1536:2048, :])
    z = DEEPNORM_ALPHA * x_ref[...] + acc
    mu = jnp.mean(z, -1, keepdims=True)
    var = jnp.mean(jnp.square(z - mu), -1, keepdims=True)
    out = (z - mu) * lax.rsqrt(var + LN_EPS) * lg_ref[...] + lb_ref[...]
    o_ref[...] = out
    ob_ref[...] = out.astype(BF16)


def _out_proj(ys, x, w, layer, lg, lb, tm=512):
    m = x.shape[0]
    yspec = pl.BlockSpec((tm, 512), lambda i: (i, 0))
    xspec = pl.BlockSpec((tm, D_MODEL), lambda i: (i, 0))
    return pl.pallas_call(
        _out_kernel,
        grid=(m // tm,),
        in_specs=[yspec] * 6 + [xspec, pl.BlockSpec((None,) + w.shape[1:], lambda i: (layer, 0, 0),
                                                    pipeline_mode=pl.Buffered(1)),
                                pl.BlockSpec(lg.shape, lambda i: (0, 0)), pl.BlockSpec(lb.shape, lambda i: (0, 0))],
        out_specs=[xspec, xspec],
        out_shape=[jax.ShapeDtypeStruct((m, D_MODEL), F32), jax.ShapeDtypeStruct((m, D_MODEL), BF16)],
        compiler_params=_params(("parallel",)),
        name="out_proj_ln",
    )(*ys, x, w, lg, lb)


def _cols(w, name):
    o, s = _OFF[name]
    return w[..., o:o + s]


def _small_row(pieces):
    parts, pos = [], 0
    for off, vals in pieces:
        parts += [jnp.zeros((off - pos,), F32), vals.astype(F32)]
        pos = off + vals.shape[-1]
    return jnp.concatenate(parts + [jnp.zeros((SMALL_W - pos,), F32)]).reshape(1, SMALL_W)


def _overlap_matrix(T):
    nc = T // NSA_CMP_STRIDE
    ns = T // NSA_SEL_BLOCK
    cst = np.arange(nc) * NSA_CMP_STRIDE
    jst = np.arange(ns) * NSA_SEL_BLOCK
    ov = np.clip(np.minimum(cst[:, None] + NSA_CMP_BLOCK, jst[None, :] + NSA_SEL_BLOCK)
                 - np.maximum(cst[:, None], jst[None, :]), 0, None).astype(np.float32) / NSA_CMP_BLOCK
    return jnp.asarray(ov.T)


def _cmp_weights(pe_k, pe_v, k_w1, k_w2, v_w1, v_w2):
    eye = jnp.eye(KV_HEADS, dtype=F32)
    w1 = jnp.stack([k_w1, v_w1]).reshape(2, NSA_CMP_BLOCK, HEAD_DIM, NSA_CMP_HIDDEN)
    w1 = jnp.einsum('sldj,gh->slgdhj', w1, eye).reshape(2, NSA_CMP_BLOCK, KV_W, KV_HEADS * NSA_CMP_HIDDEN)
    w2 = jnp.einsum('sjd,gh->sgjhd', jnp.stack([k_w2, v_w2]), eye).reshape(2, KV_HEADS * NSA_CMP_HIDDEN, KV_W)
    pe = jnp.tile(jnp.stack([pe_k, pe_v])[:, :, None, :], (1, 1, 1, KV_HEADS))
    return pe, w1.astype(BF16), w2.astype(BF16)


_SMALL_OFF = len(F32_GROUPS) * 512
_F32_MAP = ([(FCOL[n] * 512, n) for n in F32_GROUPS]
            + [(_SMALL_OFF + o, n) for o, n in ((SM_WD, 'rw_wd'), (SM_AD, 'rw_ad'), (SM_GATE, 'nsa_gate'),
                                                (SM_BETA, 'gdn_beta'), (SM_A, 'gdn_a'))]
            + [(CMP_BLK * KV_W, 'nsa_kc'), ((CMP_BLK + 1) * KV_W, 'nsa_vc')])
_BF_MAP = ([(BF_Q[n] * 512, n) for n in BF_Q] + [(BF_KV[n] * KV_W, n) for n in BF_KV])


def _regroup_kernel(w_ref, of_ref, ob_ref):
    tc = w_ref.shape[1]
    rows = lambda name: w_ref[_OFF[name][0]:_OFF[name][0] + _OFF[name][1], :]
    for dst, name in _F32_MAP:
        if dst < _SMALL_OFF or dst >= _SMALL_OFF + SMALL_W:
            of_ref[dst:dst + _OFF[name][1], :] = rows(name).astype(BF16)
    parts, pos = [], 0
    for dst, name in _F32_MAP:
        if _SMALL_OFF <= dst < _SMALL_OFF + SMALL_W:
            parts += [jnp.zeros((dst - _SMALL_OFF - pos, tc), F32), rows(name)]
            pos = dst - _SMALL_OFF + _OFF[name][1]
    parts.append(jnp.zeros((SMALL_W - pos, tc), F32))
    of_ref[_SMALL_OFF:_SMALL_OFF + SMALL_W, :] = jnp.concatenate([p for p in parts if p.shape[0]], axis=0).astype(BF16)
    for dst, name in _BF_MAP:
        ob_ref[dst:dst + _OFF[name][1], :] = rows(name).astype(BF16)


def _projection_weights(w_in, tc=256):
    L, k, n = w_in.shape
    return pl.pallas_call(
        _regroup_kernel,
        grid=(L, k // tc),
        in_specs=[pl.BlockSpec((None, n, tc), lambda l, i: (l, 0, i))],
        out_specs=[pl.BlockSpec((None, N_F32, tc), lambda l, i: (l, 0, i)),
                   pl.BlockSpec((None, N_BF, tc), lambda l, i: (l, 0, i))],
        out_shape=[jax.ShapeDtypeStruct((L, N_F32, k), BF16), jax.ShapeDtypeStruct((L, N_BF, k), BF16)],
        compiler_params=_params(("parallel", "parallel")),
        name="regroup_w_in",
    )(jnp.transpose(w_in, (0, 2, 1)))


def _layer(x, xb, B, T, layer, w_f32, w_bf, w_out, ln_g, ln_b, rw_mu, rw_w0, rw_w2, rw_a0, rw_a2, rw_kk, rw_ka,
           rw_rk, rw_gn_g, rw_gn_b, swa_sinks, nsa_pe_k, nsa_pe_v, nsa_k_w1, nsa_k_w2, nsa_v_w1, nsa_v_w2,
           gdn_conv, gdn_A_log, gdn_dt_bias, gdn_norm_g):
    tm = min(2048, B * T)
    cf = _matmul(xb, w_f32, layer, F32, tm, 512)
    cb = _matmul(xb, w_bf, layer, BF16, tm, N_BF // 2)

    row = lambda a: a.reshape(1, -1).astype(F32)
    mu3 = row(rw_mu[:3 * D_GROUP])
    musm = _small_row([(SM_WD, rw_mu[3 * D_GROUP:3 * D_GROUP + RW_LORA]),
                       (SM_AD, rw_mu[3 * D_GROUP + RW_LORA:])])
    y_a = _rwkv(cf, B, T, mu3, musm, row(rw_w0), rw_w2, row(rw_a0), rw_a2, row(rw_kk), row(rw_ka),
                row(rw_rk), row(rw_gn_g), row(rw_gn_b))
    y_d = _gdn(cf, B, T, gdn_conv, _small_row([(SM_A, gdn_A_log)]), _small_row([(SM_A, gdn_dt_bias)]),
               jnp.tile(row(gdn_norm_g), (1, 2)))
    y_b = _attn(cb, cf, B, T, 'swa_q', 'swa_k', 'swa_v', 'swa_g', window=SWA_WINDOW, sinks=row(swa_sinks))
    cmp = _nsa_compress(cf, B, T, *_cmp_weights(nsa_pe_k, nsa_pe_v, nsa_k_w1, nsa_k_w2, nsa_v_w1, nsa_v_w2))
    c1, sel = _nsa_cmpattn(cb, cmp, cf, B, T, _overlap_matrix(T))
    c2 = _attn(cb, cf, B, T, 'nsa_q', 'nsa_ks', 'nsa_vs', 'nsa_g', sel=sel, gate_col=SM_GATE + N_HEADS, tk=4 * TQ,
               aq=AQ_SEL)
    c3 = _attn(cb, cf, B, T, 'nsa_q', 'nsa_kw', 'nsa_vw', 'nsa_g', window=NSA_WINDOW,
               gate_col=SM_GATE + 2 * N_HEADS)
    return _out_proj((y_a, y_b, c1, c2, c3, y_d), x, w_out, layer, row(ln_g), row(ln_b))


def kernel(x, w_in, w_out, ln_g, ln_b, rw_mu, rw_w0, rw_w2, rw_a0, rw_a2, rw_kk, rw_ka, rw_rk, rw_gn_g, rw_gn_b, swa_sinks, nsa_pe_k, nsa_pe_v, nsa_k_w1, nsa_k_w2, nsa_v_w1, nsa_v_w2, gdn_conv, gdn_A_log, gdn_dt_bias, gdn_norm_g):
    B, T, D = x.shape
    params = (ln_g, ln_b, rw_mu, rw_w0, rw_w2, rw_a0, rw_a2, rw_kk, rw_ka, rw_rk, rw_gn_g, rw_gn_b,
              swa_sinks, nsa_pe_k, nsa_pe_v, nsa_k_w1, nsa_k_w2, nsa_v_w1, nsa_v_w2, gdn_conv, gdn_A_log,
              gdn_dt_bias, gdn_norm_g)
    w_f32, w_bf = _projection_weights(w_in)
    w_out_b = w_out.astype(BF16)
    xf = x.reshape(B * T, D)
    xb = xf.astype(BF16)
    for i in range(w_in.shape[0]):
        xf, xb = _layer(xf, xb, B, T, i, w_f32, w_bf, w_out_b, *(p[i] for p in params))
    return xf.reshape(B, T, D)
```

```python
import functools
import math

import numpy as np
import jax
import jax.numpy as jnp
from jax import lax
from jax.experimental import pallas as pl
from jax.experimental.pallas import tpu as pltpu

F32 = jnp.float32
BF16 = jnp.bfloat16
HI = lax.Precision.HIGHEST

D_MODEL = 2048
DEPTH = 4
D_GROUP = 512
HEAD_DIM = 64
N_HEADS = 8
KV_HEADS = 2
HPG = N_HEADS // KV_HEADS
PAIR = 2 * HEAD_DIM
KV_W = KV_HEADS * HEAD_DIM
NEG_INF = -1e30
LN_EPS = 1e-5
DEEPNORM_ALPHA = (2 * DEPTH) ** 0.25
RW_LORA = 32
RW_GN_EPS = 64e-5
SWA_WINDOW = 128
NSA_CMP_BLOCK = 32
NSA_CMP_STRIDE = 16
NSA_CMP_HIDDEN = 128
NSA_SEL_BLOCK = 64
NSA_TOPN = 16
NSA_WINDOW = 512
NSA_FORCE = 1e6
GDN_CONV = 4
GDN_EPS = 1e-6
SCALE = HEAD_DIM ** -0.5
LOG2E = math.log2(math.e)

CHUNK = 64
STEP_ROWS = 2 * CHUNK
RW_SEQS = 4
GDN_SEQS = 4
TQ = 128
AQ_BAND = 256
AQ_SEL = 512
TAIL = 8
VT_ROWS = HEAD_DIM + 16
VMEM_LIMIT = 56 * 1024 * 1024

_COLS = (
    ('rw_r', 512), ('rw_k', 512), ('rw_v', 512), ('rw_wd', 32), ('rw_ad', 32), ('rw_g', 512),
    ('swa_q', 512), ('swa_k', 128), ('swa_v', 128), ('swa_g', 512),
    ('nsa_q', 512), ('nsa_kc', 128), ('nsa_vc', 128), ('nsa_ks', 128), ('nsa_vs', 128),
    ('nsa_kw', 128), ('nsa_vw', 128), ('nsa_gate', 24), ('nsa_g', 512),
    ('gdn_q', 512), ('gdn_k', 512), ('gdn_v', 512), ('gdn_beta', 8), ('gdn_a', 8), ('gdn_g', 512),
)
_OFF = {}
_o = 0
for _n, _s in _COLS:
    _OFF[_n] = (_o, _s)
    _o += _s

F32_GROUPS = ('rw_r', 'rw_k', 'rw_v', 'rw_g', 'gdn_q', 'gdn_k', 'gdn_v', 'gdn_g', 'swa_g', 'nsa_g')
FCOL = {n: i for i, n in enumerate(F32_GROUPS)}
SMALL_W = 256
SMALL_BLK = len(F32_GROUPS) * 512 // SMALL_W
SM_WD, SM_AD, SM_GATE, SM_BETA, SM_A = 0, 32, 64, 96, 112
CMP_BLK = (len(F32_GROUPS) * 512 + SMALL_W) // KV_W
N_F32 = len(F32_GROUPS) * 512 + SMALL_W + 2 * KV_W
BF_Q = {'swa_q': 0, 'nsa_q': 1}
BF_KV = {n: 8 + i for i, n in enumerate(('swa_k', 'swa_v', 'nsa_ks', 'nsa_vs', 'nsa_kw', 'nsa_vw'))}
N_BF = 1024 + 6 * 128


def _dot(a, b, prec=None):
    return lax.dot_general(a, b, (((1,), (0,)), ((), ())), precision=prec, preferred_element_type=F32)


def _dot_nt(a, b, prec=None):
    return lax.dot_general(a, b, (((1,), (1,)), ((), ())), precision=prec, preferred_element_type=F32)


def _dot_tn(a, b, prec=None):
    return lax.dot_general(a, b, (((0,), (0,)), ((), ())), precision=prec, preferred_element_type=F32)


def _sigmoid(x):
    return 1.0 / (1.0 + jnp.exp(-x))


def _silu(x):
    return x * _sigmoid(x)


def _softplus(x):
    return jnp.maximum(x, 0.0) + jnp.log(1.0 + jnp.exp(-jnp.abs(x)))


def _iota2(shape, dim):
    return lax.broadcasted_iota(jnp.int32, shape, dim)


def _pairs(t):
    return [t[:, p * PAIR:(p + 1) * PAIR] for p in range(N_HEADS // 2)]


def _lo_mask(rows):
    return _iota2((rows, PAIR), 1) < HEAD_DIM


def _bd(x):
    xb = x.astype(BF16)
    lo = _lo_mask(x.shape[0]).astype(BF16)
    return jnp.concatenate([xb * lo, xb * (1 - lo)], axis=0)


def _pdot(a, b):
    return _dot(a.astype(BF16), _bd(b))


def _pdot_nt(a, b):
    return _dot_nt(a.astype(BF16), _bd(b))


def _pdot_tn(a, b):
    full = _dot_tn(a.astype(BF16), b.astype(BF16))
    return jnp.where(_lo_mask(HEAD_DIM), full[:HEAD_DIM], full[HEAD_DIM:])


def _segsum(t):
    lo = _lo_mask(t.shape[0])
    s_lo = jnp.sum(jnp.where(lo, t, 0.0), -1, keepdims=True)
    s_hi = jnp.sum(jnp.where(lo, 0.0, t), -1, keepdims=True)
    return jnp.where(lo, s_lo, s_hi)


def _inv_unit_lower(xs):
    n = xs[0].shape[0]
    eye2 = ((_iota2((n, 2 * n), 1) & (n - 1)) == _iota2((n, 2 * n), 0)).astype(F32)
    ps = [eye2 + x for x in xs]
    for _ in range(int(math.log2(n)) - 1):
        xs = [_pdot(x, x) for x in xs]
        ps = [p + _pdot(x, p) for x, p in zip(xs, ps)]
    return ps


def _tri_masks2(n):
    ri = _iota2((n, 2 * n), 0)
    ci = _iota2((n, 2 * n), 1) & (n - 1)
    return ri >= ci, ri > ci


def _params(sem):
    return pltpu.CompilerParams(dimension_semantics=sem, vmem_limit_bytes=VMEM_LIMIT)


def _mm_kernel(x_ref, w_ref, o_ref):
    o_ref[...] = _dot_nt(x_ref[...], w_ref[...]).astype(o_ref.dtype)


def _matmul(x, w, layer, out_dtype, tm, tn):
    m, k = x.shape
    n = w.shape[1]
    return pl.pallas_call(
        _mm_kernel,
        grid=(m // tm, n // tn),
        in_specs=[pl.BlockSpec((tm, k), lambda i, j: (i, 0)),
                  pl.BlockSpec((None, tn, k), lambda i, j: (layer, j, 0))],
        out_specs=pl.BlockSpec((tm, tn), lambda i, j: (i, j)),
        out_shape=jax.ShapeDtypeStruct((m, n), out_dtype),
        compiler_params=_params(("parallel", "arbitrary")),
        name="proj_in",
    )(x, w)


def _chunk_views(refs, sub):
    return [r.at[:, pl.ds(sub * CHUNK, CHUNK), :] for r in refs]


def _rwkv_kernel(r_ref, k_ref, v_ref, g_ref, sm_ref, *rest):
    params, (o_ref, buf_ref, bufsm_ref, s_ref) = rest[:-4], rest[-4:]

    @pl.when(pl.program_id(1) == 0)
    def _():
        buf_ref[:, 0:TAIL, :] = jnp.zeros((buf_ref.shape[0], TAIL, 3 * D_GROUP), F32)
        bufsm_ref[:, 0:TAIL, :] = jnp.zeros((bufsm_ref.shape[0], TAIL, SMALL_W), F32)
        s_ref[...] = jnp.zeros(s_ref.shape, F32)

    for sub in range(r_ref.shape[1] // CHUNK):
        r, k, v, g, sm, o = _chunk_views((r_ref, k_ref, v_ref, g_ref, sm_ref, o_ref), sub)
        _rwkv_chunk(r, k, v, g, sm, *params, o, buf_ref, bufsm_ref, s_ref)


def _rwkv_chunk(r_ref, k_ref, v_ref, g_ref, sm_ref, mu_ref, musm_ref, w0_ref, w2_ref, a0_ref, a2_ref,
                kk_ref, ka_ref, rk_ref, gng_ref, gnb_ref, o_ref, buf_ref, bufsm_ref, s_ref):
    C = CHUNK
    NB = r_ref.shape[0]
    ri = _iota2((C, C), 0)
    ci = _iota2((C, C), 1)
    tril = ri >= ci
    strict = ri > ci
    rk = rk_ref[...]
    gng = gng_ref[...]
    gnb = gnb_ref[...]
    pre = []
    for nb in range(NB):
        buf_ref[nb, TAIL:TAIL + C, 0:512] = r_ref[nb]
        buf_ref[nb, TAIL:TAIL + C, 512:1024] = k_ref[nb]
        buf_ref[nb, TAIL:TAIL + C, 1024:1536] = v_ref[nb]
        bufsm_ref[nb, TAIL:TAIL + C, :] = sm_ref[nb]
        cur = buf_ref[nb, TAIL:TAIL + C, :]
        prev = buf_ref[nb, TAIL - 1:TAIL - 1 + C, :]
        mixed = cur + (prev - cur) * mu_ref[...]
        cur_sm = bufsm_ref[nb, TAIL:TAIL + C, :]
        prev_sm = bufsm_ref[nb, TAIL - 1:TAIL - 1 + C, :]
        smix = cur_sm + (prev_sm - cur_sm) * musm_ref[...]
        buf_ref[nb, 0:TAIL, :] = buf_ref[nb, C:C + TAIL, :]
        bufsm_ref[nb, 0:TAIL, :] = bufsm_ref[nb, C:C + TAIL, :]

        r = mixed[:, 0:512]
        k = mixed[:, 512:1024]
        v = mixed[:, 1024:1536]
        wd = smix[:, SM_WD:SM_WD + RW_LORA]
        ad = smix[:, SM_AD:SM_AD + RW_LORA]
        wl = w0_ref[...] + _dot(jnp.tanh(wd), w2_ref[...], HI)
        logw = -jnp.exp(-_softplus(-wl) - 0.5)
        alpha = _sigmoid(a0_ref[...] + _dot(ad, a2_ref[...], HI))
        kkraw = k * kk_ref[...]
        k2 = k * (1.0 + (alpha - 1.0) * ka_ref[...])
        gcum = _dot(tril.astype(F32), logw, HI)
        pre.append((r, k2, v, alpha, kkraw, logw, gcum, _silu(g_ref[nb])))

    H = range(NB * N_HEADS // 2)
    tril2, strict2 = _tri_masks2(C)
    R, K2, V, AL, KKR, LW, G = ([t for p in pre for t in _pairs(p[f])] for f in range(7))
    KK = [t * lax.rsqrt(_segsum(t * t) + 1e-6) for t in KKR]
    GM = [t[C // 2 - 1:C // 2, :] for t in G]
    GE = [t[C - 1:C, :] for t in G]
    BV = [KK[h] * AL[h] for h in H]
    EINV = [jnp.exp(GM[h] - G[h]) for h in H]
    EEND = [jnp.exp(GE[h] - G[h]) for h in H]
    LEFT = [jnp.concatenate([-KK[h] * jnp.exp(G[h] - LW[h] - GM[h]), R[h] * jnp.exp(G[h] - GM[h])], axis=0)
            for h in H]
    AB = [_pdot_nt(LEFT[h], BV[h] * EINV[h]) for h in H]
    AK = [_pdot_nt(LEFT[h], K2[h] * EINV[h]) for h in H]
    A_RB = [jnp.where(tril2, t[C:], 0.0) for t in AB]
    A_K = [jnp.concatenate([jnp.where(strict2, t[:C], 0.0), jnp.where(tril2, t[C:], 0.0)], axis=0) for t in AK]
    TINV = _inv_unit_lower([jnp.where(strict2, t[:C], 0.0) for t in AB])
    S = [s_ref[h] for h in H]
    LS = [_pdot_nt(LEFT[h], S[h] * jnp.exp(GM[h])) for h in H]
    AV = [_pdot(A_K[h], V[h]) for h in H]
    U = [_pdot(TINV[h], LS[h][:C] + AV[h][:C]) for h in H]
    O = [LS[h][C:] + _pdot(A_RB[h], U[h]) + AV[h][C:] for h in H]
    for h in H:
        s_ref[h] = S[h] * jnp.exp(GE[h]) + _pdot_tn(
            jnp.concatenate([U[h], V[h]], axis=0),
            jnp.concatenate([BV[h] * EEND[h], K2[h] * EEND[h]], axis=0))
    for n in H:
        nb, p = divmod(n, N_HEADS // 2)
        sl = slice(p * PAIR, (p + 1) * PAIR)
        mu = _segsum(O[n]) * (1.0 / HEAD_DIM)
        d = O[n] - mu
        var = _segsum(d * d) * (1.0 / HEAD_DIM)
        y = d * lax.rsqrt(var + RW_GN_EPS) * gng[:, sl] + gnb[:, sl]
        bonus = _segsum(R[n] * K2[n] * rk[:, sl]) * V[n]
        o_ref[nb, :, sl] = ((y + bonus) * pre[nb][7][:, sl]).astype(o_ref.dtype)


def _seq_specs(names, nbs):
    specs = [pl.BlockSpec((nbs, STEP_ROWS, 512), lambda b, c, j=FCOL[n]: (b, c, j)) for n in names]
    return specs + [pl.BlockSpec((nbs, STEP_ROWS, SMALL_W), lambda b, c: (b, c, SMALL_BLK))]


def _rwkv(cf, B, T, mu3, musm, w0, w2, a0, a2, kk, ka, rk, gng, gnb):
    nbs = RW_SEQS if B % RW_SEQS == 0 else 1
    cf3 = cf.reshape(B, T, cf.shape[-1])
    full = lambda a: pl.BlockSpec(a.shape, lambda b, c: (0,) * a.ndim)
    ps = (mu3, musm, w0, w2, a0, a2, kk, ka, rk, gng, gnb)
    out = pl.pallas_call(
        _rwkv_kernel,
        grid=(B // nbs, T // STEP_ROWS),
        in_specs=_seq_specs(('rw_r', 'rw_k', 'rw_v', 'rw_g'), nbs) + [full(a) for a in ps],
        out_specs=pl.BlockSpec((nbs, STEP_ROWS, 512), lambda b, c: (b, c, 0)),
        out_shape=jax.ShapeDtypeStruct((B, T, D_GROUP), BF16),
        scratch_shapes=[pltpu.VMEM((nbs, TAIL + CHUNK, 3 * D_GROUP), F32),
                        pltpu.VMEM((nbs, TAIL + CHUNK, SMALL_W), F32),
                        pltpu.VMEM((nbs * N_HEADS // 2, HEAD_DIM, PAIR), F32)],
        compiler_params=_params(("parallel", "arbitrary")),
        name="rwkv7",
    )(cf3, cf3, cf3, cf3, cf3, *ps)
    return out.reshape(B * T, D_GROUP)


def _gdn_kernel(q_ref, k_ref, v_ref, z_ref, sm_ref, *rest):
    params, (o_ref, buf_ref, s_ref) = rest[:-3], rest[-3:]

    @pl.when(pl.program_id(1) == 0)
    def _():
        buf_ref[:, 0:TAIL, :] = jnp.zeros((buf_ref.shape[0], TAIL, 3 * D_GROUP), F32)
        s_ref[...] = jnp.zeros(s_ref.shape, F32)

    for sub in range(q_ref.shape[1] // CHUNK):
        q, k, v, z, sm, o = _chunk_views((q_ref, k_ref, v_ref, z_ref, sm_ref, o_ref), sub)
        _gdn_chunk(q, k, v, z, sm, *params, o, buf_ref, s_ref)


def _gdn_chunk(q_ref, k_ref, v_ref, z_ref, sm_ref, conv_ref, alog_ref, dtb_ref, ng_ref, o_ref, buf_ref, s_ref):
    C = CHUNK
    NB = q_ref.shape[0]
    conv = conv_ref[...]
    lane = _iota2((C, SMALL_W), 1)
    ri = _iota2((C, C), 0)
    ci = _iota2((C, C), 1)
    tril = ri >= ci
    strict = ri > ci
    ng = ng_ref[...]
    pre = []
    for nb in range(NB):
        buf_ref[nb, TAIL:TAIL + C, 0:512] = q_ref[nb]
        buf_ref[nb, TAIL:TAIL + C, 512:1024] = k_ref[nb]
        buf_ref[nb, TAIL:TAIL + C, 1024:1536] = v_ref[nb]
        acc = buf_ref[nb, TAIL:TAIL + C, :] * conv[GDN_CONV - 1:GDN_CONV, :]
        for i in range(GDN_CONV - 1):
            sh = GDN_CONV - 1 - i
            acc = acc + buf_ref[nb, TAIL - sh:TAIL - sh + C, :] * conv[i:i + 1, :]
        buf_ref[nb, 0:TAIL, :] = buf_ref[nb, C:C + TAIL, :]
        qkv = _silu(acc)
        sm = sm_ref[nb]
        g_all = jnp.where((lane >= SM_A) & (lane < SM_A + N_HEADS),
                          -jnp.exp(alog_ref[...]) * _softplus(sm + dtb_ref[...]), 0.0)
        gam_all = _dot(tril.astype(F32), g_all, HI)
        pre.append((qkv[:, 0:512], qkv[:, 512:1024], qkv[:, 1024:1536], _sigmoid(sm), gam_all, _silu(z_ref[nb])))

    H = range(NB * N_HEADS // 2)
    hd = lambda n: divmod(n, N_HEADS // 2)
    lo = _lo_mask(C)
    tril2, strict2 = _tri_masks2(C)
    diag2 = (_iota2((C, PAIR), 1) & (HEAD_DIM - 1)) == _iota2((C, PAIR), 0)

    def per_head(n, field, col):
        nb, p = hd(n)
        t = pre[nb][field]
        return jnp.where(lo, t[:, col + 2 * p:col + 2 * p + 1], t[:, col + 2 * p + 1:col + 2 * p + 2])

    Q = [t * lax.rsqrt(_segsum(t * t) + 1e-6) * SCALE for p in pre for t in _pairs(p[0])]
    K = [t * lax.rsqrt(_segsum(t * t) + 1e-6) for p in pre for t in _pairs(p[1])]
    V = [t for p in pre for t in _pairs(p[2])]
    BETA = [per_head(n, 3, SM_BETA) for n in H]
    GAM = [per_head(n, 4, SM_A) for n in H]
    GROW = [jnp.sum(jnp.where(diag2, t, 0.0), 0, keepdims=True) for t in GAM]
    DECAY = [jnp.exp(jnp.where(tril2, GAM[n] - GROW[n], NEG_INF)) for n in H]
    EG = [jnp.exp(t) for t in GAM]
    KB = [K[h] * BETA[h] for h in H]
    KKQ = [_pdot_nt(jnp.concatenate([KB[h], Q[h]], axis=0), K[h]) for h in H]
    QK = [KKQ[h][C:] * DECAY[h] for h in H]
    TM = _inv_unit_lower([-jnp.where(strict2, KKQ[h][:C] * DECAY[h], 0.0) for h in H])
    UU = [_pdot(TM[h], V[h] * BETA[h]) for h in H]
    W = [_pdot(TM[h], KB[h] * EG[h]) for h in H]
    S = [s_ref[h] for h in H]
    WS = [_pdot(jnp.concatenate([W[h], Q[h] * EG[h]], axis=0), S[h]) for h in H]
    VN = [UU[h] - WS[h][:C] for h in H]
    O = [WS[h][C:] + _pdot(QK[h], VN[h]) for h in H]
    for h in H:
        g_last = GAM[h][C - 1:C, :]
        s_ref[h] = S[h] * jnp.exp(g_last) + _pdot_tn(K[h] * jnp.exp(g_last - GAM[h]), VN[h])
    for n in H:
        nb, p = hd(n)
        sl = slice(p * PAIR, (p + 1) * PAIR)
        o = O[n] * lax.rsqrt(_segsum(O[n] * O[n]) * (1.0 / HEAD_DIM) + GDN_EPS) * ng
        o_ref[nb, :, sl] = (o * pre[nb][5][:, sl]).astype(o_ref.dtype)


def _gdn(cf, B, T, conv, alog_sm, dtb_sm, ng):
    nbs = GDN_SEQS if B % GDN_SEQS == 0 else 1
    cf3 = cf.reshape(B, T, cf.shape[-1])
    full = lambda a: pl.BlockSpec(a.shape, lambda b, c: (0,) * a.ndim)
    ps = (conv, alog_sm, dtb_sm, ng)
    out = pl.pallas_call(
        _gdn_kernel,
        grid=(B // nbs, T // STEP_ROWS),
        in_specs=_seq_specs(('gdn_q', 'gdn_k', 'gdn_v', 'gdn_g'), nbs) + [full(a) for a in ps],
        out_specs=pl.BlockSpec((nbs, STEP_ROWS, 512), lambda b, c: (b, c, 0)),
        out_shape=jax.ShapeDtypeStruct((B, T, D_GROUP), BF16),
        scratch_shapes=[pltpu.VMEM((nbs, TAIL + CHUNK, 3 * D_GROUP), F32),
                        pltpu.VMEM((nbs * N_HEADS // 2, HEAD_DIM, PAIR), F32)],
        compiler_params=_params(("parallel", "arbitrary")),
        name="gdn",
    )(cf3, cf3, cf3, cf3, cf3, *ps)
    return out.reshape(B * T, D_GROUP)


def _store_heads(o_ref, g, ot, gate, bgate, gate_col):
    aq = o_ref.shape[0]
    for j in range(HPG):
        h = g * HPG + j
        sl = slice(h * HEAD_DIM, (h + 1) * HEAD_DIM)
        oh = ot[:, j * aq:(j + 1) * aq].T
        if bgate is not None:
            oh = oh * bgate[:, gate_col + h:gate_col + h + 1]
        o_ref[:, sl] = oh * gate[:, sl]


def _attn_kernel(*refs, window, use_sel, use_sink, gate_col, tk):
    it = iter(refs)
    q_ref, k_ref, v_ref, g_ref = next(it), next(it), next(it), next(it)
    sm_ref = next(it) if gate_col is not None else None
    sel_ref = next(it) if use_sel else None
    sink_ref = next(it) if use_sink else None
    o_ref = next(it)
    vt_ref = next(it)
    selx_ref = next(it) if use_sel else None
    i = pl.program_id(1)
    aq = q_ref.shape[0]
    W = HPG * aq

    @pl.when(i == 0)
    def _():
        eye = (_iota2((KV_W, KV_W), 0) == _iota2((KV_W, KV_W), 1)).astype(BF16)
        ones = jnp.ones((VT_ROWS - HEAD_DIM, tk), BF16)
        for c in range(vt_ref.shape[0]):
            vt = _dot_nt(eye, v_ref[c * tk:(c + 1) * tk, :]).astype(BF16)
            for g in range(KV_HEADS):
                vt_ref[c, g] = jnp.concatenate([vt[g * HEAD_DIM:(g + 1) * HEAD_DIM], ones], axis=0)

    span = tk if window is None else window + aq
    qpos = i * aq + (_iota2((span, W), 1) & (aq - 1))
    krow = _iota2((span, W), 0)
    gate = _silu(g_ref[...])
    if gate_col is not None:
        bgate = _sigmoid(sm_ref[...])

    G = range(KV_HEADS)
    QG = [jnp.concatenate([q_ref[:, h * HEAD_DIM:(h + 1) * HEAD_DIM] for h in range(g * HPG, (g + 1) * HPG)],
                          axis=0) * (SCALE * LOG2E) for g in G]
    if use_sel:
        for g in G:
            selx_ref[g] = jnp.concatenate([sel_ref[0, g]] * HPG, axis=1)
        nb = tk // NSA_SEL_BLOCK

    if window is not None:
        nt = span // tk
        t0 = jnp.maximum(i * aq - window, 0) // tk
        start = pl.multiple_of(t0 * tk, tk)
        rel = qpos - (start + krow)
        pbias = jnp.where((rel >= 0) & (rel < window), 0.0, NEG_INF)
        for g in G:
            ksl = slice(g * HEAD_DIM, (g + 1) * HEAD_DIM)
            s = _dot_nt(k_ref[pl.ds(start, span), ksl], QG[g]) + pbias
            m = jnp.max(s, 0, keepdims=True)
            if use_sink:
                sink = jnp.concatenate([jnp.zeros((1, aq), F32) + sink_ref[0:1, h:h + 1] * LOG2E
                                        for h in range(g * HPG, (g + 1) * HPG)], axis=1)
                m = jnp.maximum(m, sink)
            p = jnp.exp2((s - m).astype(BF16))
            vt = jnp.concatenate([vt_ref[t0 + j, g] for j in range(nt)], axis=1)
            pv = _dot(vt, p)
            l = pv[HEAD_DIM:HEAD_DIM + 1]
            if use_sink:
                l = l + jnp.exp2(sink - m)
            _store_heads(o_ref, g, pv[:HEAD_DIM] / l, gate, bgate if gate_col is not None else None, gate_col)
        return

    hi = (i * aq + aq + tk - 1) // tk

    def body(kt, carry, masked):
        off = pl.multiple_of(kt * tk, tk)
        if masked:
            pbias = jnp.where(qpos - (kt * tk + krow) >= 0, 0.0, NEG_INF)
        out = []
        for g in G:
            m, acc = carry[2 * g:2 * g + 2]
            ksl = slice(g * HEAD_DIM, (g + 1) * HEAD_DIM)
            s = _dot_nt(k_ref[pl.ds(off, tk), ksl], QG[g])
            if masked:
                s = s + pbias
            if use_sel:
                s = s + jnp.concatenate(
                    [jnp.broadcast_to(selx_ref[g, pl.ds(kt * nb + j, 1), :], (NSA_SEL_BLOCK, W))
                     for j in range(nb)], axis=0)
            m_new = jnp.maximum(m, jnp.max(s, 0, keepdims=True))
            a = jnp.exp2(m - m_new)
            p = jnp.exp2((s - m_new).astype(BF16))
            out += [m_new, a * acc + _dot(vt_ref[kt, g], p)]
        return tuple(out)

    init = []
    for g in G:
        init += [jnp.full((1, W), NEG_INF, F32), jnp.zeros((VT_ROWS, W), F32)]
    res = lax.fori_loop(0, hi - 1, functools.partial(body, masked=False), tuple(init))
    res = body(hi - 1, res, True)
    for g in G:
        acc = res[2 * g + 1]
        _store_heads(o_ref, g, acc[:HEAD_DIM] / acc[HEAD_DIM:HEAD_DIM + 1], gate,
                     bgate if gate_col is not None else None, gate_col)


def _attn(cb, cf, B, T, qname, kname, vname, gname, *, window=None, sel=None, sinks=None, gate_col=None, tk=TQ,
          aq=AQ_BAND):
    nq = T // aq
    in_specs = [pl.BlockSpec((aq, 512), lambda b, i, j=BF_Q[qname]: (b * nq + i, j)),
                pl.BlockSpec((T, KV_W), lambda b, i, j=BF_KV[kname]: (b, j)),
                pl.BlockSpec((T, KV_W), lambda b, i, j=BF_KV[vname]: (b, j)),
                pl.BlockSpec((aq, 512), lambda b, i, j=FCOL[gname]: (b * nq + i, j))]
    args = [cb, cb, cb, cf]
    if gate_col is not None:
        in_specs.append(pl.BlockSpec((aq, SMALL_W), lambda b, i: (b * nq + i, SMALL_BLK)))
        args.append(cf)
    scratch = [pltpu.VMEM((T // tk, KV_HEADS, VT_ROWS, tk), BF16)]
    if sel is not None:
        in_specs.append(pl.BlockSpec((1, KV_HEADS, sel.shape[2], aq), lambda b, i: (b, 0, 0, i)))
        args.append(sel)
        scratch.append(pltpu.VMEM((KV_HEADS, sel.shape[2], HPG * aq), F32))
    if sinks is not None:
        in_specs.append(pl.BlockSpec(sinks.shape, lambda b, i: (0, 0)))
        args.append(sinks)
    return pl.pallas_call(
        functools.partial(_attn_kernel, window=window, use_sel=sel is not None,
                          use_sink=sinks is not None, gate_col=gate_col, tk=tk),
        grid=(B, nq),
        in_specs=in_specs,
        out_specs=pl.BlockSpec((aq, 512), lambda b, i: (b * nq + i, 0)),
        out_shape=jax.ShapeDtypeStruct((B * T, D_GROUP), F32),
        scratch_shapes=scratch,
        compiler_params=_params(("parallel", "arbitrary")),
        name="attn_" + qname + ("_sel" if sel is not None else "_w%d" % window),
    )(*args)


def _gelu_tanh(x):
    return 0.5 * x * (1.0 + jnp.tanh(math.sqrt(2.0 / math.pi) * (x + 0.044715 * x * x * x)))


def _cmp_kernel(kc_ref, vc_ref, pe_ref, w1_ref, w2_ref, o_ref):
    nr = o_ref.shape[1]
    outs = []
    for s, ref in enumerate((kc_ref, vc_ref)):
        p1 = jnp.zeros((nr, 2 * NSA_CMP_HIDDEN), F32)
        p2 = jnp.zeros((nr, 2 * NSA_CMP_HIDDEN), F32)
        for l in range(NSA_CMP_STRIDE):
            x = ref[pl.ds(l, nr, stride=NSA_CMP_STRIDE), :]
            p1 = p1 + _dot((x + pe_ref[s, l]).astype(BF16), w1_ref[s, l])
            p2 = p2 + _dot((x + pe_ref[s, NSA_CMP_STRIDE + l]).astype(BF16), w1_ref[s, NSA_CMP_STRIDE + l])
        hid = _gelu_tanh(p1 + pltpu.roll(p2, nr - 1, 0))
        outs.append(_dot(hid.astype(BF16), w2_ref[s]))
    o_ref[0] = jnp.concatenate(outs, axis=1)


def _nsa_compress(cf, B, T, pe, w1, w2):
    nr = T // NSA_CMP_STRIDE
    full = lambda a: pl.BlockSpec(a.shape, lambda b: (0,) * a.ndim)
    return pl.pallas_call(
        _cmp_kernel,
        grid=(B,),
        in_specs=[pl.BlockSpec((T, KV_W), lambda b: (b, CMP_BLK)), pl.BlockSpec((T, KV_W), lambda b: (b, CMP_BLK + 1))]
                 + [full(a) for a in (pe, w1, w2)],
        out_specs=pl.BlockSpec((1, nr, 2 * KV_W), lambda b: (b, 0, 0)),
        out_shape=jax.ShapeDtypeStruct((B, nr, 2 * KV_W), F32),
        compiler_params=_params(("arbitrary",)),
        name="nsa_compress",
    )(cf, cf, pe, w1, w2)


def _cmpattn_kernel(q_ref, cmp_ref, g_ref, sm_ref, ov_ref, o_ref, sel_ref):
    i = pl.program_id(1)
    kv = cmp_ref[0]
    nr = kv.shape[0]
    ns = ov_ref.shape[0]
    tq = q_ref.shape[0]
    tpos = i * tq + _iota2((tq, nr), 0)
    cmask = _iota2((tq, nr), 1) * NSA_CMP_STRIDE + (NSA_CMP_BLOCK - 1) <= tpos
    gate = _silu(g_ref[...])
    bgate = _sigmoid(sm_ref[...])
    tblk = (i * tq + _iota2((ns, tq), 1)) // NSA_SEL_BLOCK
    jj = _iota2((ns, tq), 0)
    forced = (jj == 0) | (jj == tblk) | (jj == tblk - 1)
    causal = jj <= tblk
    cbias = jnp.where(cmask, 0.0, NEG_INF)
    for g in range(KV_HEADS):
        kc = kv[:, g * HEAD_DIM:(g + 1) * HEAD_DIM].astype(BF16)
        vc = kv[:, KV_W + g * HEAD_DIM:KV_W + (g + 1) * HEAD_DIM].astype(BF16)
        heads = range(g * HPG, (g + 1) * HPG)
        qg = jnp.concatenate([q_ref[:, h * HEAD_DIM:(h + 1) * HEAD_DIM] for h in heads], axis=0) * SCALE
        s = _dot_nt(qg, kc).reshape(HPG, tq, nr) + cbias[None]
        e = jnp.exp(s - jnp.max(s, -1, keepdims=True))
        p = jnp.where(cmask[None], e / jnp.sum(e, -1, keepdims=True), 0.0)
        psum = jnp.sum(p, axis=0)
        o = _dot(p.reshape(HPG * tq, nr).astype(BF16), vc)
        for j, h in enumerate(heads):
            sl = slice(h * HEAD_DIM, (h + 1) * HEAD_DIM)
            o_ref[:, sl] = o[j * tq:(j + 1) * tq] * bgate[:, SM_GATE + h:SM_GATE + h + 1] * gate[:, sl]
        imp = _dot_nt(ov_ref[...], psum, HI)
        imp = jnp.where(causal, jnp.where(forced, NSA_FORCE, imp), NEG_INF)
        rank = jnp.zeros((ns, tq), jnp.int32)
        for j in range(ns):
            row = imp[j:j + 1, :]
            rank = rank + ((row > imp) | ((row == imp) & (j < jj))).astype(jnp.int32)
        sel_ref[0, g] = jnp.where((rank < NSA_TOPN) & causal, 0.0, NEG_INF)


def _nsa_cmpattn(cb, cmp, cf, B, T, ov):
    tq = AQ_BAND
    nq = T // tq
    nr = cmp.shape[1]
    ns = T // NSA_SEL_BLOCK
    return pl.pallas_call(
        _cmpattn_kernel,
        grid=(B, nq),
        in_specs=[pl.BlockSpec((tq, 512), lambda b, i: (b * nq + i, BF_Q['nsa_q'])),
                  pl.BlockSpec((1, nr, 2 * KV_W), lambda b, i: (b, 0, 0)),
                  pl.BlockSpec((tq, 512), lambda b, i: (b * nq + i, FCOL['nsa_g'])),
                  pl.BlockSpec((tq, SMALL_W), lambda b, i: (b * nq + i, SMALL_BLK)),
                  pl.BlockSpec(ov.shape, lambda b, i: (0, 0))],
        out_specs=[pl.BlockSpec((tq, 512), lambda b, i: (b * nq + i, 0)),
                   pl.BlockSpec((1, KV_HEADS, ns, tq), lambda b, i: (b, 0, 0, i))],
        out_shape=[jax.ShapeDtypeStruct((B * T, D_GROUP), F32),
                   jax.ShapeDtypeStruct((B, KV_HEADS, ns, T), F32)],
        compiler_params=_params(("parallel", "arbitrary")),
        name="nsa_cmpattn",
    )(cb, cmp, cf, cf, ov)


def _out_kernel(ya_ref, yb_ref, c1_ref, c2_ref, c3_ref, yd_ref, x_ref, w_ref, lg_ref, lb_ref, o_ref, ob_ref):
    yc = c1_ref[...] + c2_ref[...] + c3_ref[...]
    acc = _dot(ya_ref[...], w_ref[0:512, :])
    acc = acc + _dot(yb_ref[...].astype(BF16), w_ref[512:1024, :])
    acc = acc + _dot(yc.astype(BF16), w_ref[1024:1536, :])
    acc = acc + _dot(yd_ref[...], w_ref[1536:2048, :])
    z = DEEPNORM_ALPHA * x_ref[...] + acc
    mu = jnp.mean(z, -1, keepdims=True)
    var = jnp.mean(jnp.square(z - mu), -1, keepdims=True)
    out = (z - mu) * lax.rsqrt(var + LN_EPS) * lg_ref[...] + lb_ref[...]
    o_ref[...] = out
    ob_ref[...] = out.astype(BF16)


def _out_proj(ys, x, w, layer, lg, lb, tm=512):
    m = x.shape[0]
    yspec = pl.BlockSpec((tm, 512), lambda i: (i, 0))
    xspec = pl.BlockSpec((tm, D_MODEL), lambda i: (i, 0))
    return pl.pallas_call(
        _out_kernel,
        grid=(m // tm,),
        in_specs=[yspec] * 6 + [xspec, pl.BlockSpec((None,) + w.shape[1:], lambda i: (layer, 0, 0),
                                                    pipeline_mode=pl.Buffered(1)),
                                pl.BlockSpec(lg.shape, lambda i: (0, 0)), pl.BlockSpec(lb.shape, lambda i: (0, 0))],
        out_specs=[xspec, xspec],
        out_shape=[jax.ShapeDtypeStruct((m, D_MODEL), F32), jax.ShapeDtypeStruct((m, D_MODEL), BF16)],
        compiler_params=_params(("parallel",)),
        name="out_proj_ln",
    )(*ys, x, w, lg, lb)


def _small_row(pieces):
    parts, pos = [], 0
    for off, vals in pieces:
        parts += [jnp.zeros((off - pos,), F32), vals.astype(F32)]
        pos = off + vals.shape[-1]
    return jnp.concatenate(parts + [jnp.zeros((SMALL_W - pos,), F32)]).reshape(1, SMALL_W)


def _overlap_matrix(T):
    nc = T // NSA_CMP_STRIDE
    ns = T // NSA_SEL_BLOCK
    cst = np.arange(nc) * NSA_CMP_STRIDE
    jst = np.arange(ns) * NSA_SEL_BLOCK
    ov = np.clip(np.minimum(cst[:, None] + NSA_CMP_BLOCK, jst[None, :] + NSA_SEL_BLOCK)
                 - np.maximum(cst[:, None], jst[None, :]), 0, None).astype(np.float32) / NSA_CMP_BLOCK
    return jnp.asarray(ov.T)


def _cmp_weights(pe_k, pe_v, k_w1, k_w2, v_w1, v_w2):
    eye = jnp.eye(KV_HEADS, dtype=F32)
    w1 = jnp.stack([k_w1, v_w1]).reshape(2, NSA_CMP_BLOCK, HEAD_DIM, NSA_CMP_HIDDEN)
    w1 = jnp.einsum('sldj,gh->slgdhj', w1, eye).reshape(2, NSA_CMP_BLOCK, KV_W, KV_HEADS * NSA_CMP_HIDDEN)
    w2 = jnp.einsum('sjd,gh->sgjhd', jnp.stack([k_w2, v_w2]), eye).reshape(2, KV_HEADS * NSA_CMP_HIDDEN, KV_W)
    pe = jnp.tile(jnp.stack([pe_k, pe_v])[:, :, None, :], (1, 1, 1, KV_HEADS))
    return pe, w1.astype(BF16), w2.astype(BF16)


_SMALL_OFF = len(F32_GROUPS) * 512
_F32_MAP = ([(FCOL[n] * 512, n) for n in F32_GROUPS]
            + [(_SMALL_OFF + o, n) for o, n in ((SM_WD, 'rw_wd'), (SM_AD, 'rw_ad'), (SM_GATE, 'nsa_gate'),
                                                (SM_BETA, 'gdn_beta'), (SM_A, 'gdn_a'))]
            + [(CMP_BLK * KV_W, 'nsa_kc'), ((CMP_BLK + 1) * KV_W, 'nsa_vc')])
_BF_MAP = ([(BF_Q[n] * 512, n) for n in BF_Q] + [(BF_KV[n] * KV_W, n) for n in BF_KV])


def _regroup_kernel(w_ref, of_ref, ob_ref):
    tc = w_ref.shape[1]
    rows = lambda name: w_ref[_OFF[name][0]:_OFF[name][0] + _OFF[name][1], :]
    for dst, name in _F32_MAP:
        if dst < _SMALL_OFF or dst >= _SMALL_OFF + SMALL_W:
            of_ref[dst:dst + _OFF[name][1], :] = rows(name).astype(BF16)
    parts, pos = [], 0
    for dst, name in _F32_MAP:
        if _SMALL_OFF <= dst < _SMALL_OFF + SMALL_W:
            parts += [jnp.zeros((dst - _SMALL_OFF - pos, tc), F32), rows(name)]
            pos = dst - _SMALL_OFF + _OFF[name][1]
    parts.append(jnp.zeros((SMALL_W - pos, tc), F32))
    of_ref[_SMALL_OFF:_SMALL_OFF + SMALL_W, :] = jnp.concatenate([p for p in parts if p.shape[0]], axis=0).astype(BF16)
    for dst, name in _BF_MAP:
        ob_ref[dst:dst + _OFF[name][1], :] = rows(name).astype(BF16)


def _projection_weights(w_in, tc=256):
    L, k, n = w_in.shape
    return pl.pallas_call(
        _regroup_kernel,
        grid=(L, k // tc),
        in_specs=[pl.BlockSpec((None, n, tc), lambda l, i: (l, 0, i))],
        out_specs=[pl.BlockSpec((None, N_F32, tc), lambda l, i: (l, 0, i)),
                   pl.BlockSpec((None, N_BF, tc), lambda l, i: (l, 0, i))],
        out_shape=[jax.ShapeDtypeStruct((L, N_F32, k), BF16), jax.ShapeDtypeStruct((L, N_BF, k), BF16)],
        compiler_params=_params(("parallel", "parallel")),
        name="regroup_w_in",
    )(jnp.transpose(w_in, (0, 2, 1)))


def _layer(x, xb, B, T, layer, w_f32, w_bf, w_out, ln_g, ln_b, rw_mu, rw_w0, rw_w2, rw_a0, rw_a2, rw_kk, rw_ka,
           rw_rk, rw_gn_g, rw_gn_b, swa_sinks, nsa_pe_k, nsa_pe_v, nsa_k_w1, nsa_k_w2, nsa_v_w1, nsa_v_w2,
           gdn_conv, gdn_A_log, gdn_dt_bias, gdn_norm_g):
    tm = min(2048, B * T)
    cf = _matmul(xb, w_f32, layer, F32, tm, 512)
    cb = _matmul(xb, w_bf, layer, BF16, min(1024, tm), N_BF)

    row = lambda a: a.reshape(1, -1).astype(F32)
    mu3 = row(rw_mu[:3 * D_GROUP])
    musm = _small_row([(SM_WD, rw_mu[3 * D_GROUP:3 * D_GROUP + RW_LORA]),
                       (SM_AD, rw_mu[3 * D_GROUP + RW_LORA:])])
    y_a = _rwkv(cf, B, T, mu3, musm, row(rw_w0), rw_w2, row(rw_a0), rw_a2, row(rw_kk), row(rw_ka),
                row(rw_rk), row(rw_gn_g), row(rw_gn_b))
    y_d = _gdn(cf, B, T, gdn_conv, _small_row([(SM_A, gdn_A_log)]), _small_row([(SM_A, gdn_dt_bias)]),
               jnp.tile(row(gdn_norm_g), (1, 2)))
    y_b = _attn(cb, cf, B, T, 'swa_q', 'swa_k', 'swa_v', 'swa_g', window=SWA_WINDOW, sinks=row(swa_sinks))
    cmp = _nsa_compress(cf, B, T, *_cmp_weights(nsa_pe_k, nsa_pe_v, nsa_k_w1, nsa_k_w2, nsa_v_w1, nsa_v_w2))
    c1, sel = _nsa_cmpattn(cb, cmp, cf, B, T, _overlap_matrix(T))
    c2 = _attn(cb, cf, B, T, 'nsa_q', 'nsa_ks', 'nsa_vs', 'nsa_g', sel=sel, gate_col=SM_GATE + N_HEADS, tk=4 * TQ,
               aq=AQ_SEL)
    c3 = _attn(cb, cf, B, T, 'nsa_q', 'nsa_kw', 'nsa_vw', 'nsa_g', window=NSA_WINDOW,
               gate_col=SM_GATE + 2 * N_HEADS)
    return _out_proj((y_a, y_b, c1, c2, c3, y_d), x, w_out, layer, row(ln_g), row(ln_b))


def kernel(x, w_in, w_out, ln_g, ln_b, rw_mu, rw_w0, rw_w2, rw_a0, rw_a2, rw_kk, rw_ka, rw_rk, rw_gn_g, rw_gn_b, swa_sinks, nsa_pe_k, nsa_pe_v, nsa_k_w1, nsa_k_w2, nsa_v_w1, nsa_v_w2, gdn_conv, gdn_A_log, gdn_dt_bias, gdn_norm_g):
    B, T, D = x.shape
    params = (ln_g, ln_b, rw_mu, rw_w0, rw_w2, rw_a0, rw_a2, rw_kk, rw_ka, rw_rk, rw_gn_g, rw_gn_b,
              swa_sinks, nsa_pe_k, nsa_pe_v, nsa_k_w1, nsa_k_w2, nsa_v_w1, nsa_v_w2, gdn_conv, gdn_A_log,
              gdn_dt_bias, gdn_norm_g)
    w_f32, w_bf = _projection_weights(w_in)
    w_out_b = w_out.astype(BF16)
    xf = x.reshape(B * T, D)
    xb = xf.astype(BF16)
    for i in range(w_in.shape[0]):
        xf, xb = _layer(xf, xb, B, T, i, w_f32, w_bf, w_out_b, *(p[i] for p in params))
    return xf.reshape(B, T, D)
```

```python
import functools
import math

import numpy as np
import jax
import jax.numpy as jnp
from jax import lax
from jax.experimental import pallas as pl
from jax.experimental.pallas import tpu as pltpu

F32 = jnp.float32
BF16 = jnp.bfloat16
HI = lax.Precision.HIGHEST

D_MODEL = 2048
DEPTH = 4
D_GROUP = 512
HEAD_DIM = 64
N_HEADS = 8
KV_HEADS = 2
HPG = N_HEADS // KV_HEADS
PAIR = 2 * HEAD_DIM
KV_W = KV_HEADS * HEAD_DIM
NEG_INF = -1e30
LN_EPS = 1e-5
DEEPNORM_ALPHA = (2 * DEPTH) ** 0.25
RW_LORA = 32
RW_GN_EPS = 64e-5
SWA_WINDOW = 128
NSA_CMP_BLOCK = 32
NSA_CMP_STRIDE = 16
NSA_CMP_HIDDEN = 128
NSA_SEL_BLOCK = 64
NSA_TOPN = 16
NSA_WINDOW = 512
NSA_FORCE = 1e6
GDN_CONV = 4
GDN_EPS = 1e-6
SCALE = HEAD_DIM ** -0.5
LOG2E = math.log2(math.e)

CHUNK = 64
RW_STEP_ROWS = 2 * CHUNK
GDN_STEP_ROWS = 4 * CHUNK
RW_SEQS = 4
GDN_SEQS = 4
TQ = 128
AQ_BAND = 256
AQ_SEL = 512
TAIL = 8
VT_ROWS = HEAD_DIM + 16
VMEM_LIMIT = 56 * 1024 * 1024

_COLS = (
    ('rw_r', 512), ('rw_k', 512), ('rw_v', 512), ('rw_wd', 32), ('rw_ad', 32), ('rw_g', 512),
    ('swa_q', 512), ('swa_k', 128), ('swa_v', 128), ('swa_g', 512),
    ('nsa_q', 512), ('nsa_kc', 128), ('nsa_vc', 128), ('nsa_ks', 128), ('nsa_vs', 128),
    ('nsa_kw', 128), ('nsa_vw', 128), ('nsa_gate', 24), ('nsa_g', 512),
    ('gdn_q', 512), ('gdn_k', 512), ('gdn_v', 512), ('gdn_beta', 8), ('gdn_a', 8), ('gdn_g', 512),
)
_OFF = {}
_o = 0
for _n, _s in _COLS:
    _OFF[_n] = (_o, _s)
    _o += _s

F32_GROUPS = ('rw_r', 'rw_k', 'rw_v', 'rw_g', 'gdn_q', 'gdn_k', 'gdn_v', 'gdn_g', 'swa_g', 'nsa_g')
FCOL = {n: i for i, n in enumerate(F32_GROUPS)}
SMALL_W = 256
SMALL_BLK = len(F32_GROUPS) * 512 // SMALL_W
SM_WD, SM_AD, SM_GATE, SM_BETA, SM_A = 0, 32, 64, 96, 112
CMP_BLK = (len(F32_GROUPS) * 512 + SMALL_W) // KV_W
N_F32 = len(F32_GROUPS) * 512 + SMALL_W + 2 * KV_W
BF_Q = {'swa_q': 0, 'nsa_q': 1}
BF_KV = {n: 8 + i for i, n in enumerate(('swa_k', 'swa_v', 'nsa_ks', 'nsa_vs', 'nsa_kw', 'nsa_vw'))}
N_BF = 1024 + 6 * 128


def _dot(a, b, prec=None):
    return lax.dot_general(a, b, (((1,), (0,)), ((), ())), precision=prec, preferred_element_type=F32)


def _dot_nt(a, b, prec=None):
    return lax.dot_general(a, b, (((1,), (1,)), ((), ())), precision=prec, preferred_element_type=F32)


def _dot_tn(a, b, prec=None):
    return lax.dot_general(a, b, (((0,), (0,)), ((), ())), precision=prec, preferred_element_type=F32)


def _sigmoid(x):
    return 1.0 / (1.0 + jnp.exp(-x))


def _silu(x):
    return x * _sigmoid(x)


def _softplus(x):
    return jnp.maximum(x, 0.0) + jnp.log(1.0 + jnp.exp(-jnp.abs(x)))


def _iota2(shape, dim):
    return lax.broadcasted_iota(jnp.int32, shape, dim)


def _pairs(t):
    return [t[:, p * PAIR:(p + 1) * PAIR] for p in range(N_HEADS // 2)]


def _lo_mask(rows):
    return _iota2((rows, PAIR), 1) < HEAD_DIM


def _bd(x):
    xb = x.astype(BF16)
    lo = _lo_mask(x.shape[0]).astype(BF16)
    return jnp.concatenate([xb * lo, xb * (1 - lo)], axis=0)


def _pdot(a, b):
    return _dot(a.astype(BF16), _bd(b))


def _pdot_nt(a, b):
    return _dot_nt(a.astype(BF16), _bd(b))


def _pdot_tn(a, b):
    full = _dot_tn(a.astype(BF16), b.astype(BF16))
    return jnp.where(_lo_mask(HEAD_DIM), full[:HEAD_DIM], full[HEAD_DIM:])


def _segsum(t):
    lo = _lo_mask(t.shape[0])
    s_lo = jnp.sum(jnp.where(lo, t, 0.0), -1, keepdims=True)
    s_hi = jnp.sum(jnp.where(lo, 0.0, t), -1, keepdims=True)
    return jnp.where(lo, s_lo, s_hi)


def _inv_unit_lower(xs):
    n = xs[0].shape[0]
    eye2 = ((_iota2((n, 2 * n), 1) & (n - 1)) == _iota2((n, 2 * n), 0)).astype(F32)
    ps = [eye2 + x for x in xs]
    for _ in range(int(math.log2(n)) - 1):
        xs = [_pdot(x, x) for x in xs]
        ps = [p + _pdot(x, p) for x, p in zip(xs, ps)]
    return ps


def _tri_masks2(n):
    ri = _iota2((n, 2 * n), 0)
    ci = _iota2((n, 2 * n), 1) & (n - 1)
    return ri >= ci, ri > ci


def _params(sem):
    return pltpu.CompilerParams(dimension_semantics=sem, vmem_limit_bytes=VMEM_LIMIT)


def _mm_kernel(x_ref, w_ref, o_ref):
    o_ref[...] = _dot_nt(x_ref[...], w_ref[...]).astype(o_ref.dtype)


def _matmul(x, w, layer, out_dtype, tm, tn):
    m, k = x.shape
    n = w.shape[1]
    return pl.pallas_call(
        _mm_kernel,
        grid=(m // tm, n // tn),
        in_specs=[pl.BlockSpec((tm, k), lambda i, j: (i, 0)),
                  pl.BlockSpec((None, tn, k), lambda i, j: (layer, j, 0))],
        out_specs=pl.BlockSpec((tm, tn), lambda i, j: (i, j)),
        out_shape=jax.ShapeDtypeStruct((m, n), out_dtype),
        compiler_params=_params(("parallel", "arbitrary")),
        name="proj_in",
    )(x, w)


def _chunk_views(refs, sub):
    return [r.at[:, pl.ds(sub * CHUNK, CHUNK), :] for r in refs]


def _rwkv_kernel(r_ref, k_ref, v_ref, g_ref, sm_ref, *rest):
    params, (o_ref, buf_ref, bufsm_ref, s_ref) = rest[:-4], rest[-4:]

    @pl.when(pl.program_id(1) == 0)
    def _():
        buf_ref[:, 0:TAIL, :] = jnp.zeros((buf_ref.shape[0], TAIL, 3 * D_GROUP), F32)
        bufsm_ref[:, 0:TAIL, :] = jnp.zeros((bufsm_ref.shape[0], TAIL, SMALL_W), F32)
        s_ref[...] = jnp.zeros(s_ref.shape, F32)

    for sub in range(r_ref.shape[1] // CHUNK):
        r, k, v, g, sm, o = _chunk_views((r_ref, k_ref, v_ref, g_ref, sm_ref, o_ref), sub)
        _rwkv_chunk(r, k, v, g, sm, *params, o, buf_ref, bufsm_ref, s_ref)


def _rwkv_chunk(r_ref, k_ref, v_ref, g_ref, sm_ref, mu_ref, musm_ref, w0_ref, w2_ref, a0_ref, a2_ref,
                kk_ref, ka_ref, rk_ref, gng_ref, gnb_ref, o_ref, buf_ref, bufsm_ref, s_ref):
    C = CHUNK
    NB = r_ref.shape[0]
    ri = _iota2((C, C), 0)
    ci = _iota2((C, C), 1)
    tril = ri >= ci
    strict = ri > ci
    rk = rk_ref[...]
    gng = gng_ref[...]
    gnb = gnb_ref[...]
    pre = []
    for nb in range(NB):
        buf_ref[nb, TAIL:TAIL + C, 0:512] = r_ref[nb]
        buf_ref[nb, TAIL:TAIL + C, 512:1024] = k_ref[nb]
        buf_ref[nb, TAIL:TAIL + C, 1024:1536] = v_ref[nb]
        bufsm_ref[nb, TAIL:TAIL + C, :] = sm_ref[nb]
        cur = buf_ref[nb, TAIL:TAIL + C, :]
        prev = buf_ref[nb, TAIL - 1:TAIL - 1 + C, :]
        mixed = cur + (prev - cur) * mu_ref[...]
        cur_sm = bufsm_ref[nb, TAIL:TAIL + C, :]
        prev_sm = bufsm_ref[nb, TAIL - 1:TAIL - 1 + C, :]
        smix = cur_sm + (prev_sm - cur_sm) * musm_ref[...]
        buf_ref[nb, 0:TAIL, :] = buf_ref[nb, C:C + TAIL, :]
        bufsm_ref[nb, 0:TAIL, :] = bufsm_ref[nb, C:C + TAIL, :]

        r = mixed[:, 0:512]
        k = mixed[:, 512:1024]
        v = mixed[:, 1024:1536]
        wd = smix[:, SM_WD:SM_WD + RW_LORA]
        ad = smix[:, SM_AD:SM_AD + RW_LORA]
        wl = w0_ref[...] + _dot(jnp.tanh(wd), w2_ref[...], HI)
        logw = -jnp.exp(-_softplus(-wl) - 0.5)
        alpha = _sigmoid(a0_ref[...] + _dot(ad, a2_ref[...], HI))
        kkraw = k * kk_ref[...]
        k2 = k * (1.0 + (alpha - 1.0) * ka_ref[...])
        gcum = _dot(tril.astype(F32), logw, HI)
        pre.append((r, k2, v, alpha, kkraw, logw, gcum, _silu(g_ref[nb])))

    H = range(NB * N_HEADS // 2)
    tril2, strict2 = _tri_masks2(C)
    R, K2, V, AL, KKR, LW, G = ([t for p in pre for t in _pairs(p[f])] for f in range(7))
    KK = [t * lax.rsqrt(_segsum(t * t) + 1e-6) for t in KKR]
    GM = [t[C // 2 - 1:C // 2, :] for t in G]
    GE = [t[C - 1:C, :] for t in G]
    BV = [KK[h] * AL[h] for h in H]
    EINV = [jnp.exp(GM[h] - G[h]) for h in H]
    EEND = [jnp.exp(GE[h] - G[h]) for h in H]
    LEFT = [jnp.concatenate([-KK[h] * jnp.exp(G[h] - LW[h] - GM[h]), R[h] * jnp.exp(G[h] - GM[h])], axis=0)
            for h in H]
    AB = [_pdot_nt(LEFT[h], BV[h] * EINV[h]) for h in H]
    AK = [_pdot_nt(LEFT[h], K2[h] * EINV[h]) for h in H]
    A_RB = [jnp.where(tril2, t[C:], 0.0) for t in AB]
    A_K = [jnp.concatenate([jnp.where(strict2, t[:C], 0.0), jnp.where(tril2, t[C:], 0.0)], axis=0) for t in AK]
    TINV = _inv_unit_lower([jnp.where(strict2, t[:C], 0.0) for t in AB])
    S = [s_ref[h] for h in H]
    LS = [_pdot_nt(LEFT[h], S[h] * jnp.exp(GM[h])) for h in H]
    AV = [_pdot(A_K[h], V[h]) for h in H]
    U = [_pdot(TINV[h], LS[h][:C] + AV[h][:C]) for h in H]
    O = [LS[h][C:] + _pdot(A_RB[h], U[h]) + AV[h][C:] for h in H]
    for h in H:
        s_ref[h] = S[h] * jnp.exp(GE[h]) + _pdot_tn(
            jnp.concatenate([U[h], V[h]], axis=0),
            jnp.concatenate([BV[h] * EEND[h], K2[h] * EEND[h]], axis=0))
    for n in H:
        nb, p = divmod(n, N_HEADS // 2)
        sl = slice(p * PAIR, (p + 1) * PAIR)
        mu = _segsum(O[n]) * (1.0 / HEAD_DIM)
        d = O[n] - mu
        var = _segsum(d * d) * (1.0 / HEAD_DIM)
        y = d * lax.rsqrt(var + RW_GN_EPS) * gng[:, sl] + gnb[:, sl]
        bonus = _segsum(R[n] * K2[n] * rk[:, sl]) * V[n]
        o_ref[nb, :, sl] = ((y + bonus) * pre[nb][7][:, sl]).astype(o_ref.dtype)


def _seq_specs(names, nbs, rows):
    specs = [pl.BlockSpec((nbs, rows, 512), lambda b, c, j=FCOL[n]: (b, c, j)) for n in names]
    return specs + [pl.BlockSpec((nbs, rows, SMALL_W), lambda b, c: (b, c, SMALL_BLK))]


def _rwkv(cf, B, T, mu3, musm, w0, w2, a0, a2, kk, ka, rk, gng, gnb):
    nbs = RW_SEQS if B % RW_SEQS == 0 else 1
    cf3 = cf.reshape(B, T, cf.shape[-1])
    full = lambda a: pl.BlockSpec(a.shape, lambda b, c: (0,) * a.ndim)
    ps = (mu3, musm, w0, w2, a0, a2, kk, ka, rk, gng, gnb)
    out = pl.pallas_call(
        _rwkv_kernel,
        grid=(B // nbs, T // RW_STEP_ROWS),
        in_specs=_seq_specs(('rw_r', 'rw_k', 'rw_v', 'rw_g'), nbs, RW_STEP_ROWS) + [full(a) for a in ps],
        out_specs=pl.BlockSpec((nbs, RW_STEP_ROWS, 512), lambda b, c: (b, c, 0)),
        out_shape=jax.ShapeDtypeStruct((B, T, D_GROUP), BF16),
        scratch_shapes=[pltpu.VMEM((nbs, TAIL + CHUNK, 3 * D_GROUP), F32),
                        pltpu.VMEM((nbs, TAIL + CHUNK, SMALL_W), F32),
                        pltpu.VMEM((nbs * N_HEADS // 2, HEAD_DIM, PAIR), F32)],
        compiler_params=_params(("parallel", "arbitrary")),
        name="rwkv7",
    )(cf3, cf3, cf3, cf3, cf3, *ps)
    return out.reshape(B * T, D_GROUP)


def _gdn_kernel(q_ref, k_ref, v_ref, z_ref, sm_ref, *rest):
    params, (o_ref, buf_ref, s_ref) = rest[:-3], rest[-3:]

    @pl.when(pl.program_id(1) == 0)
    def _():
        buf_ref[:, 0:TAIL, :] = jnp.zeros((buf_ref.shape[0], TAIL, 3 * D_GROUP), F32)
        s_ref[...] = jnp.zeros(s_ref.shape, F32)

    for sub in range(q_ref.shape[1] // CHUNK):
        q, k, v, z, sm, o = _chunk_views((q_ref, k_ref, v_ref, z_ref, sm_ref, o_ref), sub)
        _gdn_chunk(q, k, v, z, sm, *params, o, buf_ref, s_ref)


def _gdn_chunk(q_ref, k_ref, v_ref, z_ref, sm_ref, conv_ref, alog_ref, dtb_ref, ng_ref, o_ref, buf_ref, s_ref):
    C = CHUNK
    NB = q_ref.shape[0]
    conv = conv_ref[...]
    lane = _iota2((C, SMALL_W), 1)
    ri = _iota2((C, C), 0)
    ci = _iota2((C, C), 1)
    tril = ri >= ci
    strict = ri > ci
    ng = ng_ref[...]
    pre = []
    for nb in range(NB):
        buf_ref[nb, TAIL:TAIL + C, 0:512] = q_ref[nb]
        buf_ref[nb, TAIL:TAIL + C, 512:1024] = k_ref[nb]
        buf_ref[nb, TAIL:TAIL + C, 1024:1536] = v_ref[nb]
        acc = buf_ref[nb, TAIL:TAIL + C, :] * conv[GDN_CONV - 1:GDN_CONV, :]
        for i in range(GDN_CONV - 1):
            sh = GDN_CONV - 1 - i
            acc = acc + buf_ref[nb, TAIL - sh:TAIL - sh + C, :] * conv[i:i + 1, :]
        buf_ref[nb, 0:TAIL, :] = buf_ref[nb, C:C + TAIL, :]
        qkv = _silu(acc)
        sm = sm_ref[nb]
        g_all = jnp.where((lane >= SM_A) & (lane < SM_A + N_HEADS),
                          -jnp.exp(alog_ref[...]) * _softplus(sm + dtb_ref[...]), 0.0)
        gam_all = _dot(tril.astype(F32), g_all, HI)
        pre.append((qkv[:, 0:512], qkv[:, 512:1024], qkv[:, 1024:1536], _sigmoid(sm), gam_all, _silu(z_ref[nb])))

    H = range(NB * N_HEADS // 2)
    hd = lambda n: divmod(n, N_HEADS // 2)
    lo = _lo_mask(C)
    tril2, strict2 = _tri_masks2(C)
    diag2 = (_iota2((C, PAIR), 1) & (HEAD_DIM - 1)) == _iota2((C, PAIR), 0)

    def per_head(n, field, col):
        nb, p = hd(n)
        t = pre[nb][field]
        return jnp.where(lo, t[:, col + 2 * p:col + 2 * p + 1], t[:, col + 2 * p + 1:col + 2 * p + 2])

    Q = [t * lax.rsqrt(_segsum(t * t) + 1e-6) * SCALE for p in pre for t in _pairs(p[0])]
    K = [t * lax.rsqrt(_segsum(t * t) + 1e-6) for p in pre for t in _pairs(p[1])]
    V = [t for p in pre for t in _pairs(p[2])]
    BETA = [per_head(n, 3, SM_BETA) for n in H]
    GAM = [per_head(n, 4, SM_A) for n in H]
    GROW = [jnp.sum(jnp.where(diag2, t, 0.0), 0, keepdims=True) for t in GAM]
    DECAY = [jnp.exp(jnp.where(tril2, GAM[n] - GROW[n], NEG_INF)) for n in H]
    EG = [jnp.exp(t) for t in GAM]
    KB = [K[h] * BETA[h] for h in H]
    KKQ = [_pdot_nt(jnp.concatenate([KB[h], Q[h]], axis=0), K[h]) for h in H]
    QK = [KKQ[h][C:] * DECAY[h] for h in H]
    TM = _inv_unit_lower([-jnp.where(strict2, KKQ[h][:C] * DECAY[h], 0.0) for h in H])
    UU = [_pdot(TM[h], V[h] * BETA[h]) for h in H]
    W = [_pdot(TM[h], KB[h] * EG[h]) for h in H]
    S = [s_ref[h] for h in H]
    WS = [_pdot(jnp.concatenate([W[h], Q[h] * EG[h]], axis=0), S[h]) for h in H]
    VN = [UU[h] - WS[h][:C] for h in H]
    O = [WS[h][C:] + _pdot(QK[h], VN[h]) for h in H]
    for h in H:
        g_last = GAM[h][C - 1:C, :]
        s_ref[h] = S[h] * jnp.exp(g_last) + _pdot_tn(K[h] * jnp.exp(g_last - GAM[h]), VN[h])
    for n in H:
        nb, p = hd(n)
        sl = slice(p * PAIR, (p + 1) * PAIR)
        o = O[n] * lax.rsqrt(_segsum(O[n] * O[n]) * (1.0 / HEAD_DIM) + GDN_EPS) * ng
        o_ref[nb, :, sl] = (o * pre[nb][5][:, sl]).astype(o_ref.dtype)


def _gdn(cf, B, T, conv, alog_sm, dtb_sm, ng):
    nbs = GDN_SEQS if B % GDN_SEQS == 0 else 1
    cf3 = cf.reshape(B, T, cf.shape[-1])
    full = lambda a: pl.BlockSpec(a.shape, lambda b, c: (0,) * a.ndim)
    ps = (conv, alog_sm, dtb_sm, ng)
    out = pl.pallas_call(
        _gdn_kernel,
        grid=(B // nbs, T // GDN_STEP_ROWS),
        in_specs=_seq_specs(('gdn_q', 'gdn_k', 'gdn_v', 'gdn_g'), nbs, GDN_STEP_ROWS) + [full(a) for a in ps],
        out_specs=pl.BlockSpec((nbs, GDN_STEP_ROWS, 512), lambda b, c: (b, c, 0)),
        out_shape=jax.ShapeDtypeStruct((B, T, D_GROUP), BF16),
        scratch_shapes=[pltpu.VMEM((nbs, TAIL + CHUNK, 3 * D_GROUP), F32),
                        pltpu.VMEM((nbs * N_HEADS // 2, HEAD_DIM, PAIR), F32)],
        compiler_params=_params(("parallel", "arbitrary")),
        name="gdn",
    )(cf3, cf3, cf3, cf3, cf3, *ps)
    return out.reshape(B * T, D_GROUP)


def _store_heads(o_ref, g, ot, gate, bgate, gate_col):
    aq = o_ref.shape[0]
    for j in range(HPG):
        h = g * HPG + j
        sl = slice(h * HEAD_DIM, (h + 1) * HEAD_DIM)
        oh = ot[:, j * aq:(j + 1) * aq].T
        if bgate is not None:
            oh = oh * bgate[:, gate_col + h:gate_col + h + 1]
        o_ref[:, sl] = oh * gate[:, sl]


def _attn_kernel(*refs, window, use_sel, use_sink, gate_col, tk):
    it = iter(refs)
    q_ref, k_ref, v_ref, g_ref = next(it), next(it), next(it), next(it)
    sm_ref = next(it) if gate_col is not None else None
    sel_ref = next(it) if use_sel else None
    sink_ref = next(it) if use_sink else None
    o_ref = next(it)
    vt_ref = next(it)
    selx_ref = next(it) if use_sel else None
    i = pl.program_id(1)
    aq = q_ref.shape[0]
    W = HPG * aq

    @pl.when(i == 0)
    def _():
        eye = (_iota2((KV_W, KV_W), 0) == _iota2((KV_W, KV_W), 1)).astype(BF16)
        ones = jnp.ones((VT_ROWS - HEAD_DIM, tk), BF16)
        for c in range(vt_ref.shape[0]):
            vt = _dot_nt(eye, v_ref[c * tk:(c + 1) * tk, :]).astype(BF16)
            for g in range(KV_HEADS):
                vt_ref[c, g] = jnp.concatenate([vt[g * HEAD_DIM:(g + 1) * HEAD_DIM], ones], axis=0)

    span = tk if window is None else window + aq
    qpos = i * aq + (_iota2((span, W), 1) & (aq - 1))
    krow = _iota2((span, W), 0)
    gate = _silu(g_ref[...])
    if gate_col is not None:
        bgate = _sigmoid(sm_ref[...])

    G = range(KV_HEADS)
    QG = [jnp.concatenate([q_ref[:, h * HEAD_DIM:(h + 1) * HEAD_DIM] for h in range(g * HPG, (g + 1) * HPG)],
                          axis=0) * (SCALE * LOG2E) for g in G]
    if use_sel:
        for g in G:
            selx_ref[g] = jnp.concatenate([sel_ref[0, g]] * HPG, axis=1)
        nb = tk // NSA_SEL_BLOCK

    if window is not None:
        nt = span // tk
        t0 = jnp.maximum(i * aq - window, 0) // tk
        start = pl.multiple_of(t0 * tk, tk)
        rel = qpos - (start + krow)
        pbias = jnp.where((rel >= 0) & (rel < window), 0.0, NEG_INF)
        for g in G:
            ksl = slice(g * HEAD_DIM, (g + 1) * HEAD_DIM)
            s = _dot_nt(k_ref[pl.ds(start, span), ksl], QG[g]) + pbias
            m = jnp.max(s, 0, keepdims=True)
            if use_sink:
                sink = jnp.concatenate([jnp.zeros((1, aq), F32) + sink_ref[0:1, h:h + 1] * LOG2E
                                        for h in range(g * HPG, (g + 1) * HPG)], axis=1)
                m = jnp.maximum(m, sink)
            p = jnp.exp2((s - m).astype(BF16))
            vt = jnp.concatenate([vt_ref[t0 + j, g] for j in range(nt)], axis=1)
            pv = _dot(vt, p)
            l = pv[HEAD_DIM:HEAD_DIM + 1]
            if use_sink:
                l = l + jnp.exp2(sink - m)
            _store_heads(o_ref, g, pv[:HEAD_DIM] / l, gate, bgate if gate_col is not None else None, gate_col)
        return

    hi = (i * aq + aq + tk - 1) // tk

    def body(kt, carry, masked):
        off = pl.multiple_of(kt * tk, tk)
        if masked:
            pbias = jnp.where(qpos - (kt * tk + krow) >= 0, 0.0, NEG_INF)
        out = []
        for g in G:
            m, acc = carry[2 * g:2 * g + 2]
            ksl = slice(g * HEAD_DIM, (g + 1) * HEAD_DIM)
            s = _dot_nt(k_ref[pl.ds(off, tk), ksl], QG[g])
            if masked:
                s = s + pbias
            if use_sel:
                s = s + jnp.concatenate(
                    [jnp.broadcast_to(selx_ref[g, pl.ds(kt * nb + j, 1), :], (NSA_SEL_BLOCK, W))
                     for j in range(nb)], axis=0)
            m_new = jnp.maximum(m, jnp.max(s, 0, keepdims=True))
            a = jnp.exp2(m - m_new)
            p = jnp.exp2((s - m_new).astype(BF16))
            out += [m_new, a * acc + _dot(vt_ref[kt, g], p)]
        return tuple(out)

    init = []
    for g in G:
        init += [jnp.full((1, W), NEG_INF, F32), jnp.zeros((VT_ROWS, W), F32)]
    res = lax.fori_loop(0, hi - 1, functools.partial(body, masked=False), tuple(init))
    res = body(hi - 1, res, True)
    for g in G:
        acc = res[2 * g + 1]
        _store_heads(o_ref, g, acc[:HEAD_DIM] / acc[HEAD_DIM:HEAD_DIM + 1], gate,
                     bgate if gate_col is not None else None, gate_col)


def _attn(cb, cf, B, T, qname, kname, vname, gname, *, window=None, sel=None, sinks=None, gate_col=None, tk=TQ,
          aq=AQ_BAND):
    nq = T // aq
    in_specs = [pl.BlockSpec((aq, 512), lambda b, i, j=BF_Q[qname]: (b * nq + i, j)),
                pl.BlockSpec((T, KV_W), lambda b, i, j=BF_KV[kname]: (b, j)),
                pl.BlockSpec((T, KV_W), lambda b, i, j=BF_KV[vname]: (b, j)),
                pl.BlockSpec((aq, 512), lambda b, i, j=FCOL[gname]: (b * nq + i, j))]
    args = [cb, cb, cb, cf]
    if gate_col is not None:
        in_specs.append(pl.BlockSpec((aq, SMALL_W), lambda b, i: (b * nq + i, SMALL_BLK)))
        args.append(cf)
    scratch = [pltpu.VMEM((T // tk, KV_HEADS, VT_ROWS, tk), BF16)]
    if sel is not None:
        in_specs.append(pl.BlockSpec((1, KV_HEADS, sel.shape[2], aq), lambda b, i: (b, 0, 0, i)))
        args.append(sel)
        scratch.append(pltpu.VMEM((KV_HEADS, sel.shape[2], HPG * aq), F32))
    if sinks is not None:
        in_specs.append(pl.BlockSpec(sinks.shape, lambda b, i: (0, 0)))
        args.append(sinks)
    return pl.pallas_call(
        functools.partial(_attn_kernel, window=window, use_sel=sel is not None,
                          use_sink=sinks is not None, gate_col=gate_col, tk=tk),
        grid=(B, nq),
        in_specs=in_specs,
        out_specs=pl.BlockSpec((aq, 512), lambda b, i: (b * nq + i, 0)),
        out_shape=jax.ShapeDtypeStruct((B * T, D_GROUP), F32),
        scratch_shapes=scratch,
        compiler_params=_params(("parallel", "arbitrary")),
        name="attn_" + qname + ("_sel" if sel is not None else "_w%d" % window),
    )(*args)


def _gelu_tanh(x):
    return 0.5 * x * (1.0 + jnp.tanh(math.sqrt(2.0 / math.pi) * (x + 0.044715 * x * x * x)))


def _cmp_kernel(kc_ref, vc_ref, pe_ref, w1_ref, w2_ref, o_ref):
    nr = o_ref.shape[1]
    outs = []
    for s, ref in enumerate((kc_ref, vc_ref)):
        p1 = jnp.zeros((nr, 2 * NSA_CMP_HIDDEN), F32)
        p2 = jnp.zeros((nr, 2 * NSA_CMP_HIDDEN), F32)
        for l in range(NSA_CMP_STRIDE):
            x = ref[pl.ds(l, nr, stride=NSA_CMP_STRIDE), :]
            p1 = p1 + _dot((x + pe_ref[s, l]).astype(BF16), w1_ref[s, l])
            p2 = p2 + _dot((x + pe_ref[s, NSA_CMP_STRIDE + l]).astype(BF16), w1_ref[s, NSA_CMP_STRIDE + l])
        hid = _gelu_tanh(p1 + pltpu.roll(p2, nr - 1, 0))
        outs.append(_dot(hid.astype(BF16), w2_ref[s]))
    o_ref[0] = jnp.concatenate(outs, axis=1)


def _nsa_compress(cf, B, T, pe, w1, w2):
    nr = T // NSA_CMP_STRIDE
    full = lambda a: pl.BlockSpec(a.shape, lambda b: (0,) * a.ndim)
    return pl.pallas_call(
        _cmp_kernel,
        grid=(B,),
        in_specs=[pl.BlockSpec((T, KV_W), lambda b: (b, CMP_BLK)), pl.BlockSpec((T, KV_W), lambda b: (b, CMP_BLK + 1))]
                 + [full(a) for a in (pe, w1, w2)],
        out_specs=pl.BlockSpec((1, nr, 2 * KV_W), lambda b: (b, 0, 0)),
        out_shape=jax.ShapeDtypeStruct((B, nr, 2 * KV_W), F32),
        compiler_params=_params(("arbitrary",)),
        name="nsa_compress",
    )(cf, cf, pe, w1, w2)


def _cmpattn_kernel(q_ref, cmp_ref, g_ref, sm_ref, ov_ref, o_ref, sel_ref):
    i = pl.program_id(1)
    kv = cmp_ref[0]
    nr = kv.shape[0]
    ns = ov_ref.shape[0]
    tq = q_ref.shape[0]
    tpos = i * tq + _iota2((tq, nr), 0)
    cmask = _iota2((tq, nr), 1) * NSA_CMP_STRIDE + (NSA_CMP_BLOCK - 1) <= tpos
    gate = _silu(g_ref[...])
    bgate = _sigmoid(sm_ref[...])
    tblk = (i * tq + _iota2((ns, tq), 1)) // NSA_SEL_BLOCK
    jj = _iota2((ns, tq), 0)
    forced = (jj == 0) | (jj == tblk) | (jj == tblk - 1)
    causal = jj <= tblk
    cbias = jnp.where(cmask, 0.0, NEG_INF)
    for g in range(KV_HEADS):
        kc = kv[:, g * HEAD_DIM:(g + 1) * HEAD_DIM].astype(BF16)
        vc = kv[:, KV_W + g * HEAD_DIM:KV_W + (g + 1) * HEAD_DIM].astype(BF16)
        heads = range(g * HPG, (g + 1) * HPG)
        qg = jnp.concatenate([q_ref[:, h * HEAD_DIM:(h + 1) * HEAD_DIM] for h in heads], axis=0) * SCALE
        s = _dot_nt(qg, kc).reshape(HPG, tq, nr) + cbias[None]
        e = jnp.exp(s - jnp.max(s, -1, keepdims=True))
        p = jnp.where(cmask[None], e / jnp.sum(e, -1, keepdims=True), 0.0)
        psum = jnp.sum(p, axis=0)
        o = _dot(p.reshape(HPG * tq, nr).astype(BF16), vc)
        for j, h in enumerate(heads):
            sl = slice(h * HEAD_DIM, (h + 1) * HEAD_DIM)
            o_ref[:, sl] = o[j * tq:(j + 1) * tq] * bgate[:, SM_GATE + h:SM_GATE + h + 1] * gate[:, sl]
        imp = _dot_nt(ov_ref[...], psum, HI)
        imp = jnp.where(causal, jnp.where(forced, NSA_FORCE, imp), NEG_INF)
        rank = jnp.zeros((ns, tq), jnp.int32)
        for j in range(ns):
            row = imp[j:j + 1, :]
            rank = rank + ((row > imp) | ((row == imp) & (j < jj))).astype(jnp.int32)
        sel_ref[0, g] = jnp.where((rank < NSA_TOPN) & causal, 0.0, NEG_INF)


def _nsa_cmpattn(cb, cmp, cf, B, T, ov):
    tq = AQ_BAND
    nq = T // tq
    nr = cmp.shape[1]
    ns = T // NSA_SEL_BLOCK
    return pl.pallas_call(
        _cmpattn_kernel,
        grid=(B, nq),
        in_specs=[pl.BlockSpec((tq, 512), lambda b, i: (b * nq + i, BF_Q['nsa_q'])),
                  pl.BlockSpec((1, nr, 2 * KV_W), lambda b, i: (b, 0, 0)),
                  pl.BlockSpec((tq, 512), lambda b, i: (b * nq + i, FCOL['nsa_g'])),
                  pl.BlockSpec((tq, SMALL_W), lambda b, i: (b * nq + i, SMALL_BLK)),
                  pl.BlockSpec(ov.shape, lambda b, i: (0, 0))],
        out_specs=[pl.BlockSpec((tq, 512), lambda b, i: (b * nq + i, 0)),
                   pl.BlockSpec((1, KV_HEADS, ns, tq), lambda b, i: (b, 0, 0, i))],
        out_shape=[jax.ShapeDtypeStruct((B * T, D_GROUP), F32),
                   jax.ShapeDtypeStruct((B, KV_HEADS, ns, T), F32)],
        compiler_params=_params(("parallel", "arbitrary")),
        name="nsa_cmpattn",
    )(cb, cmp, cf, cf, ov)


def _out_kernel(ya_ref, yb_ref, c1_ref, c2_ref, c3_ref, yd_ref, x_ref, w_ref, lg_ref, lb_ref, o_ref, ob_ref):
    yc = c1_ref[...] + c2_ref[...] + c3_ref[...]
    acc = _dot(ya_ref[...], w_ref[0:512, :])
    acc = acc + _dot(yb_ref[...].astype(BF16), w_ref[512:1024, :])
    acc = acc + _dot(yc.astype(BF16), w_ref[1024:1536, :])
    acc = acc + _dot(yd_ref[...], w_ref[1536:2048, :])
    z = DEEPNORM_ALPHA * x_ref[...] + acc
    mu = jnp.mean(z, -1, keepdims=True)
    var = jnp.mean(jnp.square(z - mu), -1, keepdims=True)
    out = (z - mu) * lax.rsqrt(var + LN_EPS) * lg_ref[...] + lb_ref[...]
    o_ref[...] = out
    ob_ref[...] = out.astype(BF16)


def _out_proj(ys, x, w, layer, lg, lb, tm=512):
    m = x.shape[0]
    yspec = pl.BlockSpec((tm, 512), lambda i: (i, 0))
    xspec = pl.BlockSpec((tm, D_MODEL), lambda i: (i, 0))
    return pl.pallas_call(
        _out_kernel,
        grid=(m // tm,),
        in_specs=[yspec] * 6 + [xspec, pl.BlockSpec((None,) + w.shape[1:], lambda i: (layer, 0, 0),
                                                    pipeline_mode=pl.Buffered(1)),
                                pl.BlockSpec(lg.shape, lambda i: (0, 0)), pl.BlockSpec(lb.shape, lambda i: (0, 0))],
        out_specs=[xspec, xspec],
        out_shape=[jax.ShapeDtypeStruct((m, D_MODEL), F32), jax.ShapeDtypeStruct((m, D_MODEL), BF16)],
        compiler_params=_params(("parallel",)),
        name="out_proj_ln",
    )(*ys, x, w, lg, lb)


def _small_row(pieces):
    parts, pos = [], 0
    for off, vals in pieces:
        parts += [jnp.zeros((off - pos,), F32), vals.astype(F32)]
        pos = off + vals.shape[-1]
    return jnp.concatenate(parts + [jnp.zeros((SMALL_W - pos,), F32)]).reshape(1, SMALL_W)


def _overlap_matrix(T):
    nc = T // NSA_CMP_STRIDE
    ns = T // NSA_SEL_BLOCK
    cst = np.arange(nc) * NSA_CMP_STRIDE
    jst = np.arange(ns) * NSA_SEL_BLOCK
    ov = np.clip(np.minimum(cst[:, None] + NSA_CMP_BLOCK, jst[None, :] + NSA_SEL_BLOCK)
                 - np.maximum(cst[:, None], jst[None, :]), 0, None).astype(np.float32) / NSA_CMP_BLOCK
    return jnp.asarray(ov.T)


def _cmp_weights(pe_k, pe_v, k_w1, k_w2, v_w1, v_w2):
    eye = jnp.eye(KV_HEADS, dtype=F32)
    w1 = jnp.stack([k_w1, v_w1]).reshape(2, NSA_CMP_BLOCK, HEAD_DIM, NSA_CMP_HIDDEN)
    w1 = jnp.einsum('sldj,gh->slgdhj', w1, eye).reshape(2, NSA_CMP_BLOCK, KV_W, KV_HEADS * NSA_CMP_HIDDEN)
    w2 = jnp.einsum('sjd,gh->sgjhd', jnp.stack([k_w2, v_w2]), eye).reshape(2, KV_HEADS * NSA_CMP_HIDDEN, KV_W)
    pe = jnp.tile(jnp.stack([pe_k, pe_v])[:, :, None, :], (1, 1, 1, KV_HEADS))
    return pe, w1.astype(BF16), w2.astype(BF16)


_SMALL_OFF = len(F32_GROUPS) * 512
_F32_MAP = ([(FCOL[n] * 512, n) for n in F32_GROUPS]
            + [(_SMALL_OFF + o, n) for o, n in ((SM_WD, 'rw_wd'), (SM_AD, 'rw_ad'), (SM_GATE, 'nsa_gate'),
                                                (SM_BETA, 'gdn_beta'), (SM_A, 'gdn_a'))]
            + [(CMP_BLK * KV_W, 'nsa_kc'), ((CMP_BLK + 1) * KV_W, 'nsa_vc')])
_BF_MAP = ([(BF_Q[n] * 512, n) for n in BF_Q] + [(BF_KV[n] * KV_W, n) for n in BF_KV])


def _regroup_kernel(w_ref, of_ref, ob_ref):
    tc = w_ref.shape[1]
    rows = lambda name: w_ref[_OFF[name][0]:_OFF[name][0] + _OFF[name][1], :]
    for dst, name in _F32_MAP:
        if dst < _SMALL_OFF or dst >= _SMALL_OFF + SMALL_W:
            of_ref[dst:dst + _OFF[name][1], :] = rows(name).astype(BF16)
    parts, pos = [], 0
    for dst, name in _F32_MAP:
        if _SMALL_OFF <= dst < _SMALL_OFF + SMALL_W:
            parts += [jnp.zeros((dst - _SMALL_OFF - pos, tc), F32), rows(name)]
            pos = dst - _SMALL_OFF + _OFF[name][1]
    parts.append(jnp.zeros((SMALL_W - pos, tc), F32))
    of_ref[_SMALL_OFF:_SMALL_OFF + SMALL_W, :] = jnp.concatenate([p for p in parts if p.shape[0]], axis=0).astype(BF16)
    for dst, name in _BF_MAP:
        ob_ref[dst:dst + _OFF[name][1], :] = rows(name).astype(BF16)


def _projection_weights(w_in, tc=256):
    L, k, n = w_in.shape
    return pl.pallas_call(
        _regroup_kernel,
        grid=(L, k // tc),
        in_specs=[pl.BlockSpec((None, n, tc), lambda l, i: (l, 0, i))],
        out_specs=[pl.BlockSpec((None, N_F32, tc), lambda l, i: (l, 0, i)),
                   pl.BlockSpec((None, N_BF, tc), lambda l, i: (l, 0, i))],
        out_shape=[jax.ShapeDtypeStruct((L, N_F32, k), BF16), jax.ShapeDtypeStruct((L, N_BF, k), BF16)],
        compiler_params=_params(("parallel", "parallel")),
        name="regroup_w_in",
    )(jnp.transpose(w_in, (0, 2, 1)))


def _layer(x, xb, B, T, layer, w_f32, w_bf, w_out, ln_g, ln_b, rw_mu, rw_w0, rw_w2, rw_a0, rw_a2, rw_kk, rw_ka,
           rw_rk, rw_gn_g, rw_gn_b, swa_sinks, nsa_pe_k, nsa_pe_v, nsa_k_w1, nsa_k_w2, nsa_v_w1, nsa_v_w2,
           gdn_conv, gdn_A_log, gdn_dt_bias, gdn_norm_g):
    tm = min(2048, B * T)
    cf = _matmul(xb, w_f32, layer, F32, tm, 512)
    cb = _matmul(xb, w_bf, layer, BF16, min(1024, tm), N_BF)

    row = lambda a: a.reshape(1, -1).astype(F32)
    mu3 = row(rw_mu[:3 * D_GROUP])
    musm = _small_row([(SM_WD, rw_mu[3 * D_GROUP:3 * D_GROUP + RW_LORA]),
                       (SM_AD, rw_mu[3 * D_GROUP + RW_LORA:])])
    y_a = _rwkv(cf, B, T, mu3, musm, row(rw_w0), rw_w2, row(rw_a0), rw_a2, row(rw_kk), row(rw_ka),
                row(rw_rk), row(rw_gn_g), row(rw_gn_b))
    y_d = _gdn(cf, B, T, gdn_conv, _small_row([(SM_A, gdn_A_log)]), _small_row([(SM_A, gdn_dt_bias)]),
               jnp.tile(row(gdn_norm_g), (1, 2)))
    y_b = _attn(cb, cf, B, T, 'swa_q', 'swa_k', 'swa_v', 'swa_g', window=SWA_WINDOW, sinks=row(swa_sinks))
    cmp = _nsa_compress(cf, B, T, *_cmp_weights(nsa_pe_k, nsa_pe_v, nsa_k_w1, nsa_k_w2, nsa_v_w1, nsa_v_w2))
    c1, sel = _nsa_cmpattn(cb, cmp, cf, B, T, _overlap_matrix(T))
    c2 = _attn(cb, cf, B, T, 'nsa_q', 'nsa_ks', 'nsa_vs', 'nsa_g', sel=sel, gate_col=SM_GATE + N_HEADS, tk=4 * TQ,
               aq=AQ_SEL)
    c3 = _attn(cb, cf, B, T, 'nsa_q', 'nsa_kw', 'nsa_vw', 'nsa_g', window=NSA_WINDOW,
               gate_col=SM_GATE + 2 * N_HEADS)
    return _out_proj((y_a, y_b, c1, c2, c3, y_d), x, w_out, layer, row(ln_g), row(ln_b))


def kernel(x, w_in, w_out, ln_g, ln_b, rw_mu, rw_w0, rw_w2, rw_a0, rw_a2, rw_kk, rw_ka, rw_rk, rw_gn_g, rw_gn_b, swa_sinks, nsa_pe_k, nsa_pe_v, nsa_k_w1, nsa_k_w2, nsa_v_w1, nsa_v_w2, gdn_conv, gdn_A_log, gdn_dt_bias, gdn_norm_g):
    B, T, D = x.shape
    params = (ln_g, ln_b, rw_mu, rw_w0, rw_w2, rw_a0, rw_a2, rw_kk, rw_ka, rw_rk, rw_gn_g, rw_gn_b,
              swa_sinks, nsa_pe_k, nsa_pe_v, nsa_k_w1, nsa_k_w2, nsa_v_w1, nsa_v_w2, gdn_conv, gdn_A_log,
              gdn_dt_bias, gdn_norm_g)
    w_f32, w_bf = _projection_weights(w_in)
    w_out_b = w_out.astype(BF16)
    xf = x.reshape(B * T, D)
    xb = xf.astype(BF16)
    for i in range(w_in.shape[0]):
        xf, xb = _layer(xf, xb, B, T, i, w_f32, w_bf, w_out_b, *(p[i] for p in params))
    return xf.reshape(B, T, D)
```

```python
import functools
import math

import numpy as np
import jax
import jax.numpy as jnp
from jax import lax
from jax.experimental import pallas as pl
from jax.experimental.pallas import tpu as pltpu

F32 = jnp.float32
BF16 = jnp.bfloat16
HI = lax.Precision.HIGHEST

D_MODEL = 2048
DEPTH = 4
D_GROUP = 512
HEAD_DIM = 64
N_HEADS = 8
KV_HEADS = 2
HPG = N_HEADS // KV_HEADS
PAIR = 2 * HEAD_DIM
KV_W = KV_HEADS * HEAD_DIM
NEG_INF = -1e30
LN_EPS = 1e-5
DEEPNORM_ALPHA = (2 * DEPTH) ** 0.25
RW_LORA = 32
RW_GN_EPS = 64e-5
SWA_WINDOW = 128
NSA_CMP_BLOCK = 32
NSA_CMP_STRIDE = 16
NSA_CMP_HIDDEN = 128
NSA_SEL_BLOCK = 64
NSA_TOPN = 16
NSA_WINDOW = 512
NSA_FORCE = 1e6
GDN_CONV = 4
GDN_EPS = 1e-6
SCALE = HEAD_DIM ** -0.5
LOG2E = math.log2(math.e)

CHUNK = 64
RW_STEP_ROWS = 2 * CHUNK
GDN_STEP_ROWS = 4 * CHUNK
RW_SEQS = 4
GDN_SEQS = 4
TQ = 128
AQ_BAND = 256
AQ_SEL = 512
TAIL = 8
VT_ROWS = HEAD_DIM + 16
VMEM_LIMIT = 56 * 1024 * 1024

_COLS = (
    ('rw_r', 512), ('rw_k', 512), ('rw_v', 512), ('rw_wd', 32), ('rw_ad', 32), ('rw_g', 512),
    ('swa_q', 512), ('swa_k', 128), ('swa_v', 128), ('swa_g', 512),
    ('nsa_q', 512), ('nsa_kc', 128), ('nsa_vc', 128), ('nsa_ks', 128), ('nsa_vs', 128),
    ('nsa_kw', 128), ('nsa_vw', 128), ('nsa_gate', 24), ('nsa_g', 512),
    ('gdn_q', 512), ('gdn_k', 512), ('gdn_v', 512), ('gdn_beta', 8), ('gdn_a', 8), ('gdn_g', 512),
)
_OFF = {}
_o = 0
for _n, _s in _COLS:
    _OFF[_n] = (_o, _s)
    _o += _s

F32_GROUPS = ('rw_r', 'rw_k', 'rw_v', 'rw_g', 'gdn_q', 'gdn_k', 'gdn_v', 'gdn_g', 'swa_g', 'nsa_g')
FCOL = {n: i for i, n in enumerate(F32_GROUPS)}
SMALL_W = 256
SMALL_BLK = len(F32_GROUPS) * 512 // SMALL_W
SM_WD, SM_AD, SM_GATE, SM_BETA, SM_A = 0, 32, 64, 96, 112
CMP_BLK = (len(F32_GROUPS) * 512 + SMALL_W) // KV_W
N_F32 = len(F32_GROUPS) * 512 + SMALL_W + 2 * KV_W
BF_Q = {'swa_q': 0, 'nsa_q': 1}
BF_KV = {n: 8 + i for i, n in enumerate(('swa_k', 'swa_v', 'nsa_ks', 'nsa_vs', 'nsa_kw', 'nsa_vw'))}
N_BF = 1024 + 6 * 128


def _dot(a, b, prec=None):
    return lax.dot_general(a, b, (((1,), (0,)), ((), ())), precision=prec, preferred_element_type=F32)


def _dot_nt(a, b, prec=None):
    return lax.dot_general(a, b, (((1,), (1,)), ((), ())), precision=prec, preferred_element_type=F32)


def _dot_tn(a, b, prec=None):
    return lax.dot_general(a, b, (((0,), (0,)), ((), ())), precision=prec, preferred_element_type=F32)


def _sigmoid(x):
    return 1.0 / (1.0 + jnp.exp(-x))


def _silu(x):
    return x * _sigmoid(x)


def _softplus(x):
    return jnp.maximum(x, 0.0) + jnp.log(1.0 + jnp.exp(-jnp.abs(x)))


def _iota2(shape, dim):
    return lax.broadcasted_iota(jnp.int32, shape, dim)


def _prefix_sums(tril, x):
    t = tril.astype(BF16)
    hi = x.astype(BF16)
    r = x - hi.astype(F32)
    mid = r.astype(BF16)
    lo = (r - mid.astype(F32)).astype(BF16)
    return _dot(t, hi) + _dot(t, mid) + _dot(t, lo)


def _pairs(t):
    return [t[:, p * PAIR:(p + 1) * PAIR] for p in range(N_HEADS // 2)]


def _lo_mask(rows):
    return _iota2((rows, PAIR), 1) < HEAD_DIM


def _bd(x):
    xb = x.astype(BF16)
    lo = _lo_mask(x.shape[0]).astype(BF16)
    return jnp.concatenate([xb * lo, xb * (1 - lo)], axis=0)


def _pdot(a, b):
    return _dot(a.astype(BF16), _bd(b))


def _pdot_nt(a, b):
    return _dot_nt(a.astype(BF16), _bd(b))


def _pdot_tn(a, b):
    full = _dot_tn(a.astype(BF16), b.astype(BF16))
    return jnp.where(_lo_mask(HEAD_DIM), full[:HEAD_DIM], full[HEAD_DIM:])


def _segsum(t):
    lo = _lo_mask(t.shape[0])
    s_lo = jnp.sum(jnp.where(lo, t, 0.0), -1, keepdims=True)
    s_hi = jnp.sum(jnp.where(lo, 0.0, t), -1, keepdims=True)
    return jnp.where(lo, s_lo, s_hi)


def _inv_unit_lower(xs):
    n = xs[0].shape[0]
    eye2 = ((_iota2((n, 2 * n), 1) & (n - 1)) == _iota2((n, 2 * n), 0)).astype(F32)
    ps = [eye2 + x for x in xs]
    for _ in range(int(math.log2(n)) - 1):
        xs = [_pdot(x, x) for x in xs]
        ps = [p + _pdot(x, p) for x, p in zip(xs, ps)]
    return ps


def _tri_masks2(n):
    ri = _iota2((n, 2 * n), 0)
    ci = _iota2((n, 2 * n), 1) & (n - 1)
    return ri >= ci, ri > ci


def _params(sem):
    return pltpu.CompilerParams(dimension_semantics=sem, vmem_limit_bytes=VMEM_LIMIT)


def _mm_kernel(x_ref, w_ref, o_ref):
    o_ref[...] = _dot_nt(x_ref[...], w_ref[...]).astype(o_ref.dtype)


def _matmul(x, w, layer, out_dtype, tm, tn):
    m, k = x.shape
    n = w.shape[1]
    return pl.pallas_call(
        _mm_kernel,
        grid=(m // tm, n // tn),
        in_specs=[pl.BlockSpec((tm, k), lambda i, j: (i, 0)),
                  pl.BlockSpec((None, tn, k), lambda i, j: (layer, j, 0))],
        out_specs=pl.BlockSpec((tm, tn), lambda i, j: (i, j)),
        out_shape=jax.ShapeDtypeStruct((m, n), out_dtype),
        compiler_params=_params(("parallel", "arbitrary")),
        name="proj_in",
    )(x, w)


def _chunk_views(refs, sub):
    return [r.at[:, pl.ds(sub * CHUNK, CHUNK), :] for r in refs]


def _rwkv_kernel(r_ref, k_ref, v_ref, g_ref, sm_ref, *rest):
    params, (o_ref, buf_ref, bufsm_ref, s_ref) = rest[:-4], rest[-4:]

    @pl.when(pl.program_id(1) == 0)
    def _():
        buf_ref[:, 0:TAIL, :] = jnp.zeros((buf_ref.shape[0], TAIL, 3 * D_GROUP), F32)
        bufsm_ref[:, 0:TAIL, :] = jnp.zeros((bufsm_ref.shape[0], TAIL, SMALL_W), F32)
        s_ref[...] = jnp.zeros(s_ref.shape, F32)

    for sub in range(r_ref.shape[1] // CHUNK):
        r, k, v, g, sm, o = _chunk_views((r_ref, k_ref, v_ref, g_ref, sm_ref, o_ref), sub)
        _rwkv_chunk(r, k, v, g, sm, *params, o, buf_ref, bufsm_ref, s_ref)


def _rwkv_chunk(r_ref, k_ref, v_ref, g_ref, sm_ref, mu_ref, musm_ref, w0_ref, w2_ref, a0_ref, a2_ref,
                kk_ref, ka_ref, rk_ref, gng_ref, gnb_ref, o_ref, buf_ref, bufsm_ref, s_ref):
    C = CHUNK
    NB = r_ref.shape[0]
    ri = _iota2((C, C), 0)
    ci = _iota2((C, C), 1)
    tril = ri >= ci
    strict = ri > ci
    rk = rk_ref[...]
    gng = gng_ref[...]
    gnb = gnb_ref[...]
    pre = []
    for nb in range(NB):
        buf_ref[nb, TAIL:TAIL + C, 0:512] = r_ref[nb]
        buf_ref[nb, TAIL:TAIL + C, 512:1024] = k_ref[nb]
        buf_ref[nb, TAIL:TAIL + C, 1024:1536] = v_ref[nb]
        bufsm_ref[nb, TAIL:TAIL + C, :] = sm_ref[nb]
        cur = buf_ref[nb, TAIL:TAIL + C, :]
        prev = buf_ref[nb, TAIL - 1:TAIL - 1 + C, :]
        mixed = cur + (prev - cur) * mu_ref[...]
        cur_sm = bufsm_ref[nb, TAIL:TAIL + C, :]
        prev_sm = bufsm_ref[nb, TAIL - 1:TAIL - 1 + C, :]
        smix = cur_sm + (prev_sm - cur_sm) * musm_ref[...]
        buf_ref[nb, 0:TAIL, :] = buf_ref[nb, C:C + TAIL, :]
        bufsm_ref[nb, 0:TAIL, :] = bufsm_ref[nb, C:C + TAIL, :]

        r = mixed[:, 0:512]
        k = mixed[:, 512:1024]
        v = mixed[:, 1024:1536]
        wd = smix[:, SM_WD:SM_WD + RW_LORA]
        ad = smix[:, SM_AD:SM_AD + RW_LORA]
        wl = w0_ref[...] + _dot(jnp.tanh(wd), w2_ref[...], HI)
        logw = -jnp.exp(-_softplus(-wl) - 0.5)
        alpha = _sigmoid(a0_ref[...] + _dot(ad, a2_ref[...], HI))
        kkraw = k * kk_ref[...]
        k2 = k * (1.0 + (alpha - 1.0) * ka_ref[...])
        gcum = _prefix_sums(tril, logw)
        pre.append((r, k2, v, alpha, kkraw, logw, gcum, _silu(g_ref[nb])))

    H = range(NB * N_HEADS // 2)
    tril2, strict2 = _tri_masks2(C)
    R, K2, V, AL, KKR, LW, G = ([t for p in pre for t in _pairs(p[f])] for f in range(7))
    KK = [t * lax.rsqrt(_segsum(t * t) + 1e-6) for t in KKR]
    GM = [t[C // 2 - 1:C // 2, :] for t in G]
    GE = [t[C - 1:C, :] for t in G]
    BV = [KK[h] * AL[h] for h in H]
    EINV = [jnp.exp(GM[h] - G[h]) for h in H]
    EEND = [jnp.exp(GE[h] - G[h]) for h in H]
    LEFT = [jnp.concatenate([-KK[h] * jnp.exp(G[h] - LW[h] - GM[h]), R[h] * jnp.exp(G[h] - GM[h])], axis=0)
            for h in H]
    AB = [_pdot_nt(LEFT[h], BV[h] * EINV[h]) for h in H]
    AK = [_pdot_nt(LEFT[h], K2[h] * EINV[h]) for h in H]
    A_RB = [jnp.where(tril2, t[C:], 0.0) for t in AB]
    A_K = [jnp.concatenate([jnp.where(strict2, t[:C], 0.0), jnp.where(tril2, t[C:], 0.0)], axis=0) for t in AK]
    TINV = _inv_unit_lower([jnp.where(strict2, t[:C], 0.0) for t in AB])
    S = [s_ref[h] for h in H]
    LS = [_pdot_nt(LEFT[h], S[h] * jnp.exp(GM[h])) for h in H]
    AV = [_pdot(A_K[h], V[h]) for h in H]
    U = [_pdot(TINV[h], LS[h][:C] + AV[h][:C]) for h in H]
    O = [LS[h][C:] + _pdot(A_RB[h], U[h]) + AV[h][C:] for h in H]
    for h in H:
        s_ref[h] = S[h] * jnp.exp(GE[h]) + _pdot_tn(
            jnp.concatenate([U[h], V[h]], axis=0),
            jnp.concatenate([BV[h] * EEND[h], K2[h] * EEND[h]], axis=0))
    for n in H:
        nb, p = divmod(n, N_HEADS // 2)
        sl = slice(p * PAIR, (p + 1) * PAIR)
        mu = _segsum(O[n]) * (1.0 / HEAD_DIM)
        d = O[n] - mu
        var = _segsum(d * d) * (1.0 / HEAD_DIM)
        y = d * lax.rsqrt(var + RW_GN_EPS) * gng[:, sl] + gnb[:, sl]
        bonus = _segsum(R[n] * K2[n] * rk[:, sl]) * V[n]
        o_ref[nb, :, sl] = ((y + bonus) * pre[nb][7][:, sl]).astype(o_ref.dtype)


def _seq_specs(names, nbs, rows):
    specs = [pl.BlockSpec((nbs, rows, 512), lambda b, c, j=FCOL[n]: (b, c, j)) for n in names]
    return specs + [pl.BlockSpec((nbs, rows, SMALL_W), lambda b, c: (b, c, SMALL_BLK))]


def _rwkv(cf, B, T, mu3, musm, w0, w2, a0, a2, kk, ka, rk, gng, gnb):
    nbs = RW_SEQS if B % RW_SEQS == 0 else 1
    cf3 = cf.reshape(B, T, cf.shape[-1])
    full = lambda a: pl.BlockSpec(a.shape, lambda b, c: (0,) * a.ndim)
    ps = (mu3, musm, w0, w2, a0, a2, kk, ka, rk, gng, gnb)
    out = pl.pallas_call(
        _rwkv_kernel,
        grid=(B // nbs, T // RW_STEP_ROWS),
        in_specs=_seq_specs(('rw_r', 'rw_k', 'rw_v', 'rw_g'), nbs, RW_STEP_ROWS) + [full(a) for a in ps],
        out_specs=pl.BlockSpec((nbs, RW_STEP_ROWS, 512), lambda b, c: (b, c, 0)),
        out_shape=jax.ShapeDtypeStruct((B, T, D_GROUP), BF16),
        scratch_shapes=[pltpu.VMEM((nbs, TAIL + CHUNK, 3 * D_GROUP), F32),
                        pltpu.VMEM((nbs, TAIL + CHUNK, SMALL_W), F32),
                        pltpu.VMEM((nbs * N_HEADS // 2, HEAD_DIM, PAIR), F32)],
        compiler_params=_params(("parallel", "arbitrary")),
        name="rwkv7",
    )(cf3, cf3, cf3, cf3, cf3, *ps)
    return out.reshape(B * T, D_GROUP)


def _gdn_kernel(q_ref, k_ref, v_ref, z_ref, sm_ref, *rest):
    params, (o_ref, buf_ref, s_ref) = rest[:-3], rest[-3:]

    @pl.when(pl.program_id(1) == 0)
    def _():
        buf_ref[:, 0:TAIL, :] = jnp.zeros((buf_ref.shape[0], TAIL, 3 * D_GROUP), F32)
        s_ref[...] = jnp.zeros(s_ref.shape, F32)

    for sub in range(q_ref.shape[1] // CHUNK):
        q, k, v, z, sm, o = _chunk_views((q_ref, k_ref, v_ref, z_ref, sm_ref, o_ref), sub)
        _gdn_chunk(q, k, v, z, sm, *params, o, buf_ref, s_ref)


def _gdn_chunk(q_ref, k_ref, v_ref, z_ref, sm_ref, conv_ref, alog_ref, dtb_ref, ng_ref, o_ref, buf_ref, s_ref):
    C = CHUNK
    NB = q_ref.shape[0]
    conv = conv_ref[...]
    lane = _iota2((C, SMALL_W), 1)
    ri = _iota2((C, C), 0)
    ci = _iota2((C, C), 1)
    tril = ri >= ci
    strict = ri > ci
    ng = ng_ref[...]
    pre = []
    for nb in range(NB):
        buf_ref[nb, TAIL:TAIL + C, 0:512] = q_ref[nb]
        buf_ref[nb, TAIL:TAIL + C, 512:1024] = k_ref[nb]
        buf_ref[nb, TAIL:TAIL + C, 1024:1536] = v_ref[nb]
        acc = buf_ref[nb, TAIL:TAIL + C, :] * conv[GDN_CONV - 1:GDN_CONV, :]
        for i in range(GDN_CONV - 1):
            sh = GDN_CONV - 1 - i
            acc = acc + buf_ref[nb, TAIL - sh:TAIL - sh + C, :] * conv[i:i + 1, :]
        buf_ref[nb, 0:TAIL, :] = buf_ref[nb, C:C + TAIL, :]
        qkv = _silu(acc)
        sm = sm_ref[nb]
        g_all = jnp.where((lane >= SM_A) & (lane < SM_A + N_HEADS),
                          -jnp.exp(alog_ref[...]) * _softplus(sm + dtb_ref[...]), 0.0)
        gam_all = _dot(tril.astype(F32), g_all, HI)
        pre.append((qkv[:, 0:512], qkv[:, 512:1024], qkv[:, 1024:1536], _sigmoid(sm), gam_all, _silu(z_ref[nb])))

    H = range(NB * N_HEADS // 2)
    hd = lambda n: divmod(n, N_HEADS // 2)
    lo = _lo_mask(C)
    tril2, strict2 = _tri_masks2(C)
    diag2 = (_iota2((C, PAIR), 1) & (HEAD_DIM - 1)) == _iota2((C, PAIR), 0)

    def per_head(n, field, col):
        nb, p = hd(n)
        t = pre[nb][field]
        return jnp.where(lo, t[:, col + 2 * p:col + 2 * p + 1], t[:, col + 2 * p + 1:col + 2 * p + 2])

    Q = [t * lax.rsqrt(_segsum(t * t) + 1e-6) * SCALE for p in pre for t in _pairs(p[0])]
    K = [t * lax.rsqrt(_segsum(t * t) + 1e-6) for p in pre for t in _pairs(p[1])]
    V = [t for p in pre for t in _pairs(p[2])]
    BETA = [per_head(n, 3, SM_BETA) for n in H]
    GAM = [per_head(n, 4, SM_A) for n in H]
    GROW = [jnp.sum(jnp.where(diag2, t, 0.0), 0, keepdims=True) for t in GAM]
    DECAY = [jnp.exp(jnp.where(tril2, GAM[n] - GROW[n], NEG_INF)) for n in H]
    EG = [jnp.exp(t) for t in GAM]
    KB = [K[h] * BETA[h] for h in H]
    KKQ = [_pdot_nt(jnp.concatenate([KB[h], Q[h]], axis=0), K[h]) for h in H]
    QK = [KKQ[h][C:] * DECAY[h] for h in H]
    TM = _inv_unit_lower([-jnp.where(strict2, KKQ[h][:C] * DECAY[h], 0.0) for h in H])
    UU = [_pdot(TM[h], V[h] * BETA[h]) for h in H]
    W = [_pdot(TM[h], KB[h] * EG[h]) for h in H]
    S = [s_ref[h] for h in H]
    WS = [_pdot(jnp.concatenate([W[h], Q[h] * EG[h]], axis=0), S[h]) for h in H]
    VN = [UU[h] - WS[h][:C] for h in H]
    O = [WS[h][C:] + _pdot(QK[h], VN[h]) for h in H]
    for h in H:
        g_last = GAM[h][C - 1:C, :]
        s_ref[h] = S[h] * jnp.exp(g_last) + _pdot_tn(K[h] * jnp.exp(g_last - GAM[h]), VN[h])
    for n in H:
        nb, p = hd(n)
        sl = slice(p * PAIR, (p + 1) * PAIR)
        o = O[n] * lax.rsqrt(_segsum(O[n] * O[n]) * (1.0 / HEAD_DIM) + GDN_EPS) * ng
        o_ref[nb, :, sl] = (o * pre[nb][5][:, sl]).astype(o_ref.dtype)


def _gdn(cf, B, T, conv, alog_sm, dtb_sm, ng):
    nbs = GDN_SEQS if B % GDN_SEQS == 0 else 1
    cf3 = cf.reshape(B, T, cf.shape[-1])
    full = lambda a: pl.BlockSpec(a.shape, lambda b, c: (0,) * a.ndim)
    ps = (conv, alog_sm, dtb_sm, ng)
    out = pl.pallas_call(
        _gdn_kernel,
        grid=(B // nbs, T // GDN_STEP_ROWS),
        in_specs=_seq_specs(('gdn_q', 'gdn_k', 'gdn_v', 'gdn_g'), nbs, GDN_STEP_ROWS) + [full(a) for a in ps],
        out_specs=pl.BlockSpec((nbs, GDN_STEP_ROWS, 512), lambda b, c: (b, c, 0)),
        out_shape=jax.ShapeDtypeStruct((B, T, D_GROUP), BF16),
        scratch_shapes=[pltpu.VMEM((nbs, TAIL + CHUNK, 3 * D_GROUP), F32),
                        pltpu.VMEM((nbs * N_HEADS // 2, HEAD_DIM, PAIR), F32)],
        compiler_params=_params(("parallel", "arbitrary")),
        name="gdn",
    )(cf3, cf3, cf3, cf3, cf3, *ps)
    return out.reshape(B * T, D_GROUP)


def _store_heads(o_ref, g, ot, gate, bgate, gate_col):
    aq = o_ref.shape[0]
    for j in range(HPG):
        h = g * HPG + j
        sl = slice(h * HEAD_DIM, (h + 1) * HEAD_DIM)
        oh = ot[:, j * aq:(j + 1) * aq].T
        if bgate is not None:
            oh = oh * _sigmoid(bgate[:, gate_col + h:gate_col + h + 1])
        o_ref[:, sl] = oh * _silu(gate[:, sl])


def _attn_kernel(*refs, window, use_sel, use_sink, gate_col, tk):
    it = iter(refs)
    q_ref, k_ref, v_ref, g_ref = next(it), next(it), next(it), next(it)
    sm_ref = next(it) if gate_col is not None else None
    sel_ref = next(it) if use_sel else None
    sink_ref = next(it) if use_sink else None
    o_ref = next(it)
    vt_ref = next(it)
    selx_ref = next(it) if use_sel else None
    i = pl.program_id(1)
    aq = q_ref.shape[0]
    W = HPG * aq

    @pl.when(i == 0)
    def _():
        eye = (_iota2((KV_W, KV_W), 0) == _iota2((KV_W, KV_W), 1)).astype(BF16)
        ones = jnp.ones((VT_ROWS - HEAD_DIM, tk), BF16)
        for c in range(vt_ref.shape[0]):
            vt = _dot_nt(eye, v_ref[c * tk:(c + 1) * tk, :]).astype(BF16)
            for g in range(KV_HEADS):
                vt_ref[c, g] = jnp.concatenate([vt[g * HEAD_DIM:(g + 1) * HEAD_DIM], ones], axis=0)

    span = tk if window is None else window + aq
    qpos = i * aq + (_iota2((span, W), 1) & (aq - 1))
    krow = _iota2((span, W), 0)
    gate = g_ref
    if gate_col is not None:
        bgate = sm_ref

    G = range(KV_HEADS)
    QG = [jnp.concatenate([q_ref[:, h * HEAD_DIM:(h + 1) * HEAD_DIM] for h in range(g * HPG, (g + 1) * HPG)],
                          axis=0) * (SCALE * LOG2E) for g in G]
    if use_sel:
        for g in G:
            selx_ref[g] = jnp.concatenate([sel_ref[0, g]] * HPG, axis=1)
        nb = tk // NSA_SEL_BLOCK

    if window is not None:
        nt = span // tk
        t0 = jnp.maximum(i * aq - window, 0) // tk
        start = pl.multiple_of(t0 * tk, tk)
        rel = qpos - (start + krow)
        pbias = jnp.where((rel >= 0) & (rel < window), 0.0, NEG_INF)
        for g in G:
            ksl = slice(g * HEAD_DIM, (g + 1) * HEAD_DIM)
            s = _dot_nt(k_ref[pl.ds(start, span), ksl], QG[g]) + pbias
            m = jnp.max(s, 0, keepdims=True)
            if use_sink:
                sink = jnp.concatenate([jnp.zeros((1, aq), F32) + sink_ref[0:1, h:h + 1] * LOG2E
                                        for h in range(g * HPG, (g + 1) * HPG)], axis=1)
                m = jnp.maximum(m, sink)
            p = jnp.exp2((s - m).astype(BF16))
            vt = jnp.concatenate([vt_ref[t0 + j, g] for j in range(nt)], axis=1)
            pv = _dot(vt, p)
            l = pv[HEAD_DIM:HEAD_DIM + 1]
            if use_sink:
                l = l + jnp.exp2(sink - m)
            _store_heads(o_ref, g, pv[:HEAD_DIM] / l, gate, bgate if gate_col is not None else None, gate_col)
        return

    hi = (i * aq + aq + tk - 1) // tk

    def body(kt, carry, masked):
        off = pl.multiple_of(kt * tk, tk)
        if masked:
            pbias = jnp.where(qpos - (kt * tk + krow) >= 0, 0.0, NEG_INF)
        out = []
        for g in G:
            m, acc = carry[2 * g:2 * g + 2]
            ksl = slice(g * HEAD_DIM, (g + 1) * HEAD_DIM)
            s = _dot_nt(k_ref[pl.ds(off, tk), ksl], QG[g])
            if masked:
                s = s + pbias
            if use_sel:
                s = s + jnp.concatenate(
                    [jnp.broadcast_to(selx_ref[g, pl.ds(kt * nb + j, 1), :], (NSA_SEL_BLOCK, W))
                     for j in range(nb)], axis=0)
            m_new = jnp.maximum(m, jnp.max(s, 0, keepdims=True))
            a = jnp.exp2(m - m_new)
            p = jnp.exp2((s - m_new).astype(BF16))
            out += [m_new, a * acc + _dot(vt_ref[kt, g], p)]
        return tuple(out)

    init = []
    for g in G:
        init += [jnp.full((1, W), NEG_INF, F32), jnp.zeros((VT_ROWS, W), F32)]
    res = lax.fori_loop(0, hi - 1, functools.partial(body, masked=False), tuple(init))
    res = body(hi - 1, res, True)
    for g in G:
        acc = res[2 * g + 1]
        _store_heads(o_ref, g, acc[:HEAD_DIM] / acc[HEAD_DIM:HEAD_DIM + 1], gate,
                     bgate if gate_col is not None else None, gate_col)


def _attn(cb, cf, B, T, qname, kname, vname, gname, *, window=None, sel=None, sinks=None, gate_col=None, tk=TQ,
          aq=AQ_BAND):
    nq = T // aq
    in_specs = [pl.BlockSpec((aq, 512), lambda b, i, j=BF_Q[qname]: (b * nq + i, j)),
                pl.BlockSpec((T, KV_W), lambda b, i, j=BF_KV[kname]: (b, j)),
                pl.BlockSpec((T, KV_W), lambda b, i, j=BF_KV[vname]: (b, j)),
                pl.BlockSpec((aq, 512), lambda b, i, j=FCOL[gname]: (b * nq + i, j))]
    args = [cb, cb, cb, cf]
    if gate_col is not None:
        in_specs.append(pl.BlockSpec((aq, SMALL_W), lambda b, i: (b * nq + i, SMALL_BLK)))
        args.append(cf)
    scratch = [pltpu.VMEM((T // tk, KV_HEADS, VT_ROWS, tk), BF16)]
    if sel is not None:
        in_specs.append(pl.BlockSpec((1, KV_HEADS, sel.shape[2], aq), lambda b, i: (b, 0, 0, i)))
        args.append(sel)
        scratch.append(pltpu.VMEM((KV_HEADS, sel.shape[2], HPG * aq), F32))
    if sinks is not None:
        in_specs.append(pl.BlockSpec(sinks.shape, lambda b, i: (0, 0)))
        args.append(sinks)
    return pl.pallas_call(
        functools.partial(_attn_kernel, window=window, use_sel=sel is not None,
                          use_sink=sinks is not None, gate_col=gate_col, tk=tk),
        grid=(B, nq),
        in_specs=in_specs,
        out_specs=pl.BlockSpec((aq, 512), lambda b, i: (b * nq + i, 0)),
        out_shape=jax.ShapeDtypeStruct((B * T, D_GROUP), F32),
        scratch_shapes=scratch,
        compiler_params=_params(("parallel", "arbitrary")),
        name="attn_" + qname + ("_sel" if sel is not None else "_w%d" % window),
    )(*args)


def _gelu_tanh(x):
    return 0.5 * x * (1.0 + jnp.tanh(math.sqrt(2.0 / math.pi) * (x + 0.044715 * x * x * x)))


def _cmp_kernel(kc_ref, vc_ref, pe_ref, w1_ref, w2_ref, o_ref):
    nr = o_ref.shape[1]
    outs = []
    for s, ref in enumerate((kc_ref, vc_ref)):
        p1 = jnp.zeros((nr, 2 * NSA_CMP_HIDDEN), F32)
        p2 = jnp.zeros((nr, 2 * NSA_CMP_HIDDEN), F32)
        for l in range(NSA_CMP_STRIDE):
            x = ref[pl.ds(l, nr, stride=NSA_CMP_STRIDE), :]
            p1 = p1 + _dot((x + pe_ref[s, l]).astype(BF16), w1_ref[s, l])
            p2 = p2 + _dot((x + pe_ref[s, NSA_CMP_STRIDE + l]).astype(BF16), w1_ref[s, NSA_CMP_STRIDE + l])
        hid = _gelu_tanh(p1 + pltpu.roll(p2, nr - 1, 0))
        outs.append(_dot(hid.astype(BF16), w2_ref[s]))
    o_ref[0] = jnp.concatenate(outs, axis=1)


def _nsa_compress(cf, B, T, pe, w1, w2):
    nr = T // NSA_CMP_STRIDE
    full = lambda a: pl.BlockSpec(a.shape, lambda b: (0,) * a.ndim)
    return pl.pallas_call(
        _cmp_kernel,
        grid=(B,),
        in_specs=[pl.BlockSpec((T, KV_W), lambda b: (b, CMP_BLK)), pl.BlockSpec((T, KV_W), lambda b: (b, CMP_BLK + 1))]
                 + [full(a) for a in (pe, w1, w2)],
        out_specs=pl.BlockSpec((1, nr, 2 * KV_W), lambda b: (b, 0, 0)),
        out_shape=jax.ShapeDtypeStruct((B, nr, 2 * KV_W), F32),
        compiler_params=_params(("arbitrary",)),
        name="nsa_compress",
    )(cf, cf, pe, w1, w2)


def _cmpattn_kernel(q_ref, cmp_ref, g_ref, sm_ref, ov_ref, o_ref, sel_ref):
    i = pl.program_id(1)
    kv = cmp_ref[0]
    nr = kv.shape[0]
    ns = ov_ref.shape[0]
    tq = q_ref.shape[0]
    tpos = i * tq + _iota2((tq, nr), 0)
    cmask = _iota2((tq, nr), 1) * NSA_CMP_STRIDE + (NSA_CMP_BLOCK - 1) <= tpos
    gate = _silu(g_ref[...])
    bgate = _sigmoid(sm_ref[...])
    tblk = (i * tq + _iota2((ns, tq), 1)) // NSA_SEL_BLOCK
    jj = _iota2((ns, tq), 0)
    forced = (jj == 0) | (jj == tblk) | (jj == tblk - 1)
    causal = jj <= tblk
    cbias = jnp.where(cmask, 0.0, NEG_INF)
    for g in range(KV_HEADS):
        kc = kv[:, g * HEAD_DIM:(g + 1) * HEAD_DIM].astype(BF16)
        vc = kv[:, KV_W + g * HEAD_DIM:KV_W + (g + 1) * HEAD_DIM].astype(BF16)
        heads = range(g * HPG, (g + 1) * HPG)
        qg = jnp.concatenate([q_ref[:, h * HEAD_DIM:(h + 1) * HEAD_DIM] for h in heads], axis=0) * SCALE
        s = _dot_nt(qg, kc).reshape(HPG, tq, nr) + cbias[None]
        e = jnp.exp(s - jnp.max(s, -1, keepdims=True))
        p = jnp.where(cmask[None], e / jnp.sum(e, -1, keepdims=True), 0.0)
        psum = jnp.sum(p, axis=0)
        o = _dot(p.reshape(HPG * tq, nr).astype(BF16), vc)
        for j, h in enumerate(heads):
            sl = slice(h * HEAD_DIM, (h + 1) * HEAD_DIM)
            o_ref[:, sl] = o[j * tq:(j + 1) * tq] * bgate[:, SM_GATE + h:SM_GATE + h + 1] * gate[:, sl]
        imp = _dot_nt(ov_ref[...], psum, HI)
        imp = jnp.where(causal, jnp.where(forced, NSA_FORCE, imp), NEG_INF)
        rank = jnp.zeros((ns, tq), jnp.int32)
        for j in range(ns):
            row = imp[j:j + 1, :]
            rank = rank + ((row > imp) | ((row == imp) & (j < jj))).astype(jnp.int32)
        sel_ref[0, g] = jnp.where((rank < NSA_TOPN) & causal, 0.0, NEG_INF)


def _nsa_cmpattn(cb, cmp, cf, B, T, ov):
    tq = AQ_BAND
    nq = T // tq
    nr = cmp.shape[1]
    ns = T // NSA_SEL_BLOCK
    return pl.pallas_call(
        _cmpattn_kernel,
        grid=(B, nq),
        in_specs=[pl.BlockSpec((tq, 512), lambda b, i: (b * nq + i, BF_Q['nsa_q'])),
                  pl.BlockSpec((1, nr, 2 * KV_W), lambda b, i: (b, 0, 0)),
                  pl.BlockSpec((tq, 512), lambda b, i: (b * nq + i, FCOL['nsa_g'])),
                  pl.BlockSpec((tq, SMALL_W), lambda b, i: (b * nq + i, SMALL_BLK)),
                  pl.BlockSpec(ov.shape, lambda b, i: (0, 0))],
        out_specs=[pl.BlockSpec((tq, 512), lambda b, i: (b * nq + i, 0)),
                   pl.BlockSpec((1, KV_HEADS, ns, tq), lambda b, i: (b, 0, 0, i))],
        out_shape=[jax.ShapeDtypeStruct((B * T, D_GROUP), F32),
                   jax.ShapeDtypeStruct((B, KV_HEADS, ns, T), F32)],
        compiler_params=_params(("parallel", "arbitrary")),
        name="nsa_cmpattn",
    )(cb, cmp, cf, cf, ov)


def _out_kernel(ya_ref, yb_ref, c1_ref, c2_ref, c3_ref, yd_ref, x_ref, w_ref, lg_ref, lb_ref, o_ref, ob_ref):
    yc = c1_ref[...] + c2_ref[...] + c3_ref[...]
    acc = _dot(ya_ref[...], w_ref[0:512, :])
    acc = acc + _dot(yb_ref[...].astype(BF16), w_ref[512:1024, :])
    acc = acc + _dot(yc.astype(BF16), w_ref[1024:1536, :])
    acc = acc + _dot(yd_ref[...], w_ref[1536:2048, :])
    z = DEEPNORM_ALPHA * x_ref[...] + acc
    mu = jnp.mean(z, -1, keepdims=True)
    var = jnp.mean(jnp.square(z - mu), -1, keepdims=True)
    out = (z - mu) * lax.rsqrt(var + LN_EPS) * lg_ref[...] + lb_ref[...]
    o_ref[...] = out
    ob_ref[...] = out.astype(BF16)


def _out_proj(ys, x, w, layer, lg, lb, tm=512):
    m = x.shape[0]
    yspec = pl.BlockSpec((tm, 512), lambda i: (i, 0))
    xspec = pl.BlockSpec((tm, D_MODEL), lambda i: (i, 0))
    return pl.pallas_call(
        _out_kernel,
        grid=(m // tm,),
        in_specs=[yspec] * 6 + [xspec, pl.BlockSpec((None,) + w.shape[1:], lambda i: (layer, 0, 0),
                                                    pipeline_mode=pl.Buffered(1)),
                                pl.BlockSpec(lg.shape, lambda i: (0, 0)), pl.BlockSpec(lb.shape, lambda i: (0, 0))],
        out_specs=[xspec, xspec],
        out_shape=[jax.ShapeDtypeStruct((m, D_MODEL), F32), jax.ShapeDtypeStruct((m, D_MODEL), BF16)],
        compiler_params=_params(("parallel",)),
        name="out_proj_ln",
    )(*ys, x, w, lg, lb)


def _small_row(pieces):
    parts, pos = [], 0
    for off, vals in pieces:
        parts += [jnp.zeros((off - pos,), F32), vals.astype(F32)]
        pos = off + vals.shape[-1]
    return jnp.concatenate(parts + [jnp.zeros((SMALL_W - pos,), F32)]).reshape(1, SMALL_W)


def _overlap_matrix(T):
    nc = T // NSA_CMP_STRIDE
    ns = T // NSA_SEL_BLOCK
    cst = np.arange(nc) * NSA_CMP_STRIDE
    jst = np.arange(ns) * NSA_SEL_BLOCK
    ov = np.clip(np.minimum(cst[:, None] + NSA_CMP_BLOCK, jst[None, :] + NSA_SEL_BLOCK)
                 - np.maximum(cst[:, None], jst[None, :]), 0, None).astype(np.float32) / NSA_CMP_BLOCK
    return jnp.asarray(ov.T)


def _cmp_weights(pe_k, pe_v, k_w1, k_w2, v_w1, v_w2):
    eye = jnp.eye(KV_HEADS, dtype=F32)
    w1 = jnp.stack([k_w1, v_w1]).reshape(2, NSA_CMP_BLOCK, HEAD_DIM, NSA_CMP_HIDDEN)
    w1 = jnp.einsum('sldj,gh->slgdhj', w1, eye).reshape(2, NSA_CMP_BLOCK, KV_W, KV_HEADS * NSA_CMP_HIDDEN)
    w2 = jnp.einsum('sjd,gh->sgjhd', jnp.stack([k_w2, v_w2]), eye).reshape(2, KV_HEADS * NSA_CMP_HIDDEN, KV_W)
    pe = jnp.tile(jnp.stack([pe_k, pe_v])[:, :, None, :], (1, 1, 1, KV_HEADS))
    return pe, w1.astype(BF16), w2.astype(BF16)


_SMALL_OFF = len(F32_GROUPS) * 512
_F32_MAP = ([(FCOL[n] * 512, n) for n in F32_GROUPS]
            + [(_SMALL_OFF + o, n) for o, n in ((SM_WD, 'rw_wd'), (SM_AD, 'rw_ad'), (SM_GATE, 'nsa_gate'),
                                                (SM_BETA, 'gdn_beta'), (SM_A, 'gdn_a'))]
            + [(CMP_BLK * KV_W, 'nsa_kc'), ((CMP_BLK + 1) * KV_W, 'nsa_vc')])
_BF_MAP = ([(BF_Q[n] * 512, n) for n in BF_Q] + [(BF_KV[n] * KV_W, n) for n in BF_KV])


def _regroup_kernel(w_ref, of_ref, ob_ref):
    tc = w_ref.shape[1]
    rows = lambda name: w_ref[_OFF[name][0]:_OFF[name][0] + _OFF[name][1], :]
    for dst, name in _F32_MAP:
        if dst < _SMALL_OFF or dst >= _SMALL_OFF + SMALL_W:
            of_ref[dst:dst + _OFF[name][1], :] = rows(name).astype(BF16)
    parts, pos = [], 0
    for dst, name in _F32_MAP:
        if _SMALL_OFF <= dst < _SMALL_OFF + SMALL_W:
            parts += [jnp.zeros((dst - _SMALL_OFF - pos, tc), F32), rows(name)]
            pos = dst - _SMALL_OFF + _OFF[name][1]
    parts.append(jnp.zeros((SMALL_W - pos, tc), F32))
    of_ref[_SMALL_OFF:_SMALL_OFF + SMALL_W, :] = jnp.concatenate([p for p in parts if p.shape[0]], axis=0).astype(BF16)
    for dst, name in _BF_MAP:
        ob_ref[dst:dst + _OFF[name][1], :] = rows(name).astype(BF16)


def _projection_weights(w_in, tc=256):
    L, k, n = w_in.shape
    return pl.pallas_call(
        _regroup_kernel,
        grid=(L, k // tc),
        in_specs=[pl.BlockSpec((None, n, tc), lambda l, i: (l, 0, i))],
        out_specs=[pl.BlockSpec((None, N_F32, tc), lambda l, i: (l, 0, i)),
                   pl.BlockSpec((None, N_BF, tc), lambda l, i: (l, 0, i))],
        out_shape=[jax.ShapeDtypeStruct((L, N_F32, k), BF16), jax.ShapeDtypeStruct((L, N_BF, k), BF16)],
        compiler_params=_params(("parallel", "parallel")),
        name="regroup_w_in",
    )(jnp.transpose(w_in, (0, 2, 1)))


def _layer(x, xb, B, T, layer, w_f32, w_bf, w_out, ln_g, ln_b, rw_mu, rw_w0, rw_w2, rw_a0, rw_a2, rw_kk, rw_ka,
           rw_rk, rw_gn_g, rw_gn_b, swa_sinks, nsa_pe_k, nsa_pe_v, nsa_k_w1, nsa_k_w2, nsa_v_w1, nsa_v_w2,
           gdn_conv, gdn_A_log, gdn_dt_bias, gdn_norm_g):
    tm = min(2048, B * T)
    cf = _matmul(xb, w_f32, layer, F32, tm, 512)
    cb = _matmul(xb, w_bf, layer, BF16, min(1024, tm), N_BF)

    row = lambda a: a.reshape(1, -1).astype(F32)
    mu3 = row(rw_mu[:3 * D_GROUP])
    musm = _small_row([(SM_WD, rw_mu[3 * D_GROUP:3 * D_GROUP + RW_LORA]),
                       (SM_AD, rw_mu[3 * D_GROUP + RW_LORA:])])
    y_a = _rwkv(cf, B, T, mu3, musm, row(rw_w0), rw_w2, row(rw_a0), rw_a2, row(rw_kk), row(rw_ka),
                row(rw_rk), row(rw_gn_g), row(rw_gn_b))
    y_d = _gdn(cf, B, T, gdn_conv, _small_row([(SM_A, gdn_A_log)]), _small_row([(SM_A, gdn_dt_bias)]),
               jnp.tile(row(gdn_norm_g), (1, 2)))
    y_b = _attn(cb, cf, B, T, 'swa_q', 'swa_k', 'swa_v', 'swa_g', window=SWA_WINDOW, sinks=row(swa_sinks))
    cmp = _nsa_compress(cf, B, T, *_cmp_weights(nsa_pe_k, nsa_pe_v, nsa_k_w1, nsa_k_w2, nsa_v_w1, nsa_v_w2))
    c1, sel = _nsa_cmpattn(cb, cmp, cf, B, T, _overlap_matrix(T))
    c2 = _attn(cb, cf, B, T, 'nsa_q', 'nsa_ks', 'nsa_vs', 'nsa_g', sel=sel, gate_col=SM_GATE + N_HEADS, tk=4 * TQ,
               aq=AQ_SEL)
    c3 = _attn(cb, cf, B, T, 'nsa_q', 'nsa_kw', 'nsa_vw', 'nsa_g', window=NSA_WINDOW,
               gate_col=SM_GATE + 2 * N_HEADS)
    return _out_proj((y_a, y_b, c1, c2, c3, y_d), x, w_out, layer, row(ln_g), row(ln_b))


def kernel(x, w_in, w_out, ln_g, ln_b, rw_mu, rw_w0, rw_w2, rw_a0, rw_a2, rw_kk, rw_ka, rw_rk, rw_gn_g, rw_gn_b, swa_sinks, nsa_pe_k, nsa_pe_v, nsa_k_w1, nsa_k_w2, nsa_v_w1, nsa_v_w2, gdn_conv, gdn_A_log, gdn_dt_bias, gdn_norm_g):
    B, T, D = x.shape
    params = (ln_g, ln_b, rw_mu, rw_w0, rw_w2, rw_a0, rw_a2, rw_kk, rw_ka, rw_rk, rw_gn_g, rw_gn_b,
              swa_sinks, nsa_pe_k, nsa_pe_v, nsa_k_w1, nsa_k_w2, nsa_v_w1, nsa_v_w2, gdn_conv, gdn_A_log,
              gdn_dt_bias, gdn_norm_g)
    w_f32, w_bf = _projection_weights(w_in)
    w_out_b = w_out.astype(BF16)
    xf = x.reshape(B * T, D)
    xb = xf.astype(BF16)
    for i in range(w_in.shape[0]):
        xf, xb = _layer(xf, xb, B, T, i, w_f32, w_bf, w_out_b, *(p[i] for p in params))
    return xf.reshape(B, T, D)
```
